```python
import math
import jax, jax.numpy as jnp
from jax import lax
import numpy as np

D_MODEL = 1024
BATCH = 2
SEQ = 8192
DEPTH = 2

D_SSM = 384
SSM_GROUP = 16
N_SSM_GROUPS = D_SSM // SSM_GROUP
SSM_STATE = 64
POOL_WINDOWS = (2, 4, 8, 16)
N_POOL_GROUPS = len(POOL_WINDOWS)
POOL_GROUP = 64
D_POOL = N_POOL_GROUPS * POOL_GROUP
MAX_WINDOW = max(POOL_WINDOWS)
SGU_HEADS = 6
SGU_HEAD_DIM = 64
D_SGU = SGU_HEADS * SGU_HEAD_DIM
CHUNK = 128
D_MIX = D_SSM + D_POOL + D_SGU
D_IN = D_SSM + D_POOL + 2 * D_SGU
D_FF = ((8 * D_MODEL // 3 + 255) // 256) * 256
EPS = 1e-6

kernel_name = "hybrid_s5_pool_sgu_trunk"


def rms_norm(x, g):
    xf = x.astype(jnp.float32)
    y = xf * lax.rsqrt(jnp.mean(xf * xf, axis=-1, keepdims=True) + EPS)
    return (y * g.astype(jnp.float32)).astype(x.dtype)


def s5_mixer(u, A_re, A_im, log_dt, B_re, B_im, C_re, C_im, D_skip, w_glu, b_glu):
    f32 = jnp.float32
    bsz, seq, _ = u.shape
    uf = u.astype(f32).reshape(bsz, seq, N_SSM_GROUPS, SSM_GROUP)
    A_re = A_re.astype(f32); A_im = A_im.astype(f32)
    dt = jnp.exp(log_dt.astype(f32))[:, None]
    mag = jnp.exp(A_re * dt)
    ar = mag * jnp.cos(A_im * dt)
    ai = mag * jnp.sin(A_im * dt)
    den = A_re * A_re + A_im * A_im
    f_re = ((ar - 1.0) * A_re + ai * A_im) / den
    f_im = (ai * A_re - (ar - 1.0) * A_im) / den
    B_re = B_re.astype(f32); B_im = B_im.astype(f32)
    Bb_re = f_re[..., None] * B_re - f_im[..., None] * B_im
    Bb_im = f_re[..., None] * B_im + f_im[..., None] * B_re
    bu_re = jnp.einsum('bsgc,gnc->bsgn', uf, Bb_re)
    bu_im = jnp.einsum('bsgc,gnc->bsgn', uf, Bb_im)
    a_re = jnp.broadcast_to(ar, bu_re.shape)
    a_im = jnp.broadcast_to(ai, bu_re.shape)

    def combine(left, right):
        a1r, a1i, b1r, b1i = left
        a2r, a2i, b2r, b2i = right
        return (a1r * a2r - a1i * a2i,
                a1r * a2i + a1i * a2r,
                a2r * b1r - a2i * b1i + b2r,
                a2r * b1i + a2i * b1r + b2i)

    _, _, h_re, h_im = lax.associative_scan(combine, (a_re, a_im, bu_re, bu_im), axis=1)
    y = (jnp.einsum('bsgn,gcn->bsgc', h_re, C_re.astype(f32))
         - jnp.einsum('bsgn,gcn->bsgc', h_im, C_im.astype(f32)))
    y = y.reshape(bsz, seq, D_SSM) + D_skip.astype(f32) * uf.reshape(bsz, seq, D_SSM)
    g = jax.nn.gelu(y)
    out = g * jax.nn.sigmoid(g @ w_glu.astype(f32) + b_glu.astype(f32))
    return out.astype(u.dtype)


def pool_mixer(u, w_pool, pool_scale):
    f32 = jnp.float32
    bsz, seq, _ = u.shape
    uf = u.astype(f32).reshape(bsz, seq, N_POOL_GROUPS, POOL_GROUP)
    csum = jnp.cumsum(uf, axis=1)
    cpad = jnp.pad(csum, ((0, 0), (MAX_WINDOW, 0), (0, 0), (0, 0)))
    pos = jnp.arange(1, seq + 1)
    means = []
    for g, w in enumerate(POOL_WINDOWS):
        lagged = cpad[:, MAX_WINDOW - w:MAX_WINDOW - w + seq, g]
        count = jnp.minimum(pos, w).astype(f32)[None, :, None]
        means.append((csum[:, :, g] - lagged) / count)
    pooled = jnp.stack(means, axis=2) - uf
    mixed = jnp.einsum('bsgc,gcd->bsgd', pooled, w_pool.astype(f32))
    out = mixed.reshape(bsz, seq, D_POOL) * pool_scale.astype(f32)
    return out.astype(u.dtype)


def sgu_mixer(zu, zv, ln_g, ln_b, w_spatial, b_spatial):
    f32 = jnp.float32
    bsz, seq, _ = zu.shape
    n_chunks = seq // CHUNK
    u = jax.nn.gelu(zu.astype(f32))
    v = jax.nn.gelu(zv.astype(f32))
    mu = jnp.mean(v, axis=-1, keepdims=True)
    var = jnp.mean(jnp.square(v - mu), axis=-1, keepdims=True)
    v = (v - mu) * lax.rsqrt(var + EPS) * ln_g.astype(f32) + ln_b.astype(f32)
    vh = v.reshape(bsz, n_chunks, CHUNK, SGU_HEADS, SGU_HEAD_DIM)
    mask = jnp.tril(jnp.ones((CHUNK, CHUNK), dtype=bool))
    ws = jnp.where(mask[None], w_spatial.astype(f32), 0.0)
    mixed = jnp.einsum('hts,bnshd->bnthd', ws, vh)
    mixed = mixed + jnp.transpose(b_spatial.astype(f32))[None, None, :, :, None]
    out = u * mixed.reshape(bsz, seq, D_SGU)
    return out.astype(zu.dtype)


def setup_inputs(seed: int = 0) -> dict:
    key = jax.random.key(seed)
    ks = jax.random.split(key, 24)
    f32 = jnp.float32
    nrm = lambda k, shape, s: (jax.random.normal(k, shape, f32) * s)
    x = jax.random.normal(ks[0], (BATCH, SEQ, D_MODEL), f32)
    g_mix = 1.0 + nrm(ks[1], (DEPTH, D_MODEL), 0.02)
    w_in = nrm(ks[2], (DEPTH, D_MODEL, D_IN), D_MODEL ** -0.5)
    A_re = -0.5 + nrm(ks[3], (DEPTH, N_SSM_GROUPS, SSM_STATE), 0.01)
    A_im = (jnp.pi * jnp.arange(SSM_STATE, dtype=f32))[None, None, :] + nrm(ks[4], (DEPTH, N_SSM_GROUPS, SSM_STATE), 0.01)
    log_dt = jax.random.uniform(ks[5], (DEPTH, N_SSM_GROUPS), f32, math.log(1e-3), math.log(1e-1))
    B_re = nrm(ks[6], (DEPTH, N_SSM_GROUPS, SSM_STATE, SSM_GROUP), (2 * SSM_GROUP) ** -0.5)
    B_im = nrm(ks[7], (DEPTH, N_SSM_GROUPS, SSM_STATE, SSM_GROUP), (2 * SSM_GROUP) ** -0.5)
    C_re = nrm(ks[8], (DEPTH, N_SSM_GROUPS, SSM_GROUP, SSM_STATE), (2 * SSM_STATE) ** -0.5)
    C_im = nrm(ks[9], (DEPTH, N_SSM_GROUPS, SSM_GROUP, SSM_STATE), (2 * SSM_STATE) ** -0.5)
    D_skip = nrm(ks[10], (DEPTH, D_SSM), 1.0)
    w_glu = nrm(ks[11], (DEPTH, D_SSM, D_SSM), D_SSM ** -0.5)
    b_glu = nrm(ks[12], (DEPTH, D_SSM), 0.01)
    w_pool = nrm(ks[13], (DEPTH, N_POOL_GROUPS, POOL_GROUP, POOL_GROUP), POOL_GROUP ** -0.5)
    pool_scale = 1.0 + nrm(ks[14], (DEPTH, D_POOL), 0.02)
    sgu_ln_g = 1.0 + nrm(ks[15], (DEPTH, D_SGU), 0.02)
    sgu_ln_b = nrm(ks[16], (DEPTH, D_SGU), 0.01)
    w_spatial = nrm(ks[17], (DEPTH, SGU_HEADS, CHUNK, CHUNK), CHUNK ** -0.5)
    b_spatial = 1.0 + nrm(ks[18], (DEPTH, SGU_HEADS, CHUNK), 0.02)
    w_out = nrm(ks[19], (DEPTH, D_MIX, D_MODEL), D_MIX ** -0.5)
    g_ffn = 1.0 + nrm(ks[20], (DEPTH, D_MODEL), 0.02)
    kf = jax.random.split(ks[21], 3)
    w_gate = nrm(kf[0], (DEPTH, D_MODEL, D_FF), D_MODEL ** -0.5)
    w_up = nrm(kf[1], (DEPTH, D_MODEL, D_FF), D_MODEL ** -0.5)
    w_down = nrm(kf[2], (DEPTH, D_FF, D_MODEL), D_FF ** -0.5)
    g_final = 1.0 + nrm(ks[22], (D_MODEL,), 0.02)
    return {"x": x, "g_mix": g_mix, "w_in": w_in, "A_re": A_re, "A_im": A_im,
            "log_dt": log_dt, "B_re": B_re, "B_im": B_im, "C_re": C_re, "C_im": C_im,
            "D_skip": D_skip, "w_glu": w_glu, "b_glu": b_glu, "w_pool": w_pool,
            "pool_scale": pool_scale, "sgu_ln_g": sgu_ln_g, "sgu_ln_b": sgu_ln_b,
            "w_spatial": w_spatial, "b_spatial": b_spatial, "w_out": w_out,
            "g_ffn": g_ffn, "w_gate": w_gate, "w_up": w_up, "w_down": w_down,
            "g_final": g_final}


def reference(x, g_mix, w_in, A_re, A_im, log_dt, B_re, B_im, C_re, C_im, D_skip,
              w_glu, b_glu, w_pool, pool_scale, sgu_ln_g, sgu_ln_b, w_spatial, b_spatial,
              w_out, g_ffn, w_gate, w_up, w_down, g_final):
    split_points = (D_SSM, D_SSM + D_POOL, D_SSM + D_POOL + D_SGU)
    for l in range(DEPTH):
        h = rms_norm(x, g_mix[l])
        z = h @ w_in[l]
        z_a, z_b, z_u, z_v = jnp.split(z, split_points, axis=-1)
        y_a = s5_mixer(z_a, A_re[l], A_im[l], log_dt[l], B_re[l], B_im[l], C_re[l], C_im[l],
                       D_skip[l], w_glu[l], b_glu[l])
        y_b = pool_mixer(z_b, w_pool[l], pool_scale[l])
        y_c = sgu_mixer(z_u, z_v, sgu_ln_g[l], sgu_ln_b[l], w_spatial[l], b_spatial[l])
        y = jnp.concatenate([y_a, y_b, y_c], axis=-1) @ w_out[l]
        x = x + y
        h = rms_norm(x, g_ffn[l])
        x = x + (jax.nn.silu(h @ w_gate[l]) * (h @ w_up[l])) @ w_down[l]
    return rms_norm(x, g_final)
```

```python
import functools
import math

import jax
import jax.numpy as jnp
from jax import lax
from jax.experimental import pallas as pl
from jax.experimental.pallas import tpu as pltpu

D_MODEL = 1024
D_SSM = 384
SSM_GROUP = 16
N_SSM_GROUPS = D_SSM // SSM_GROUP
SSM_STATE = 64
POOL_WINDOWS = (2, 4, 8, 16)
POOL_GROUP = 64
D_POOL = len(POOL_WINDOWS) * POOL_GROUP
MAX_WINDOW = max(POOL_WINDOWS)
SGU_HEADS = 6
SGU_HEAD_DIM = 64
D_SGU = SGU_HEADS * SGU_HEAD_DIM
CHUNK = 128
D_IN = D_SSM + D_POOL + 2 * D_SGU
EPS = 1e-6

LANES = 128
SSM_BLOCK = 16
N_CLUSTERS = D_SSM // LANES
GROUPS_PER_CLUSTER = LANES // SSM_GROUP
CLUSTER_STATE = GROUPS_PER_CLUSTER * SSM_STATE
FLAT = SSM_BLOCK * LANES

TOKEN_TILE = 512
S5_ROW_TILE = 256
FF_CHUNKS = ((0, 1024), (1024, 2048), (2048, 2816))
VMEM_LIMIT = 56 * 1024 * 1024

F32 = jnp.float32
BF16 = jnp.bfloat16


def _gelu(x):
    c = math.sqrt(2.0 / math.pi)
    return 0.5 * x * (1.0 + jnp.tanh(c * (x + 0.044715 * (x * x * x))))


def _rms(x, g):
    ms = jnp.mean(x * x, axis=-1, keepdims=True)
    return x * lax.rsqrt(ms + EPS) * g


def _dot(a, b):
    return jnp.dot(a, b, preferred_element_type=F32)


def _mix_in_kernel(tiles_per_seq, x_ref, gmix_ref, win_ref, wpool_ref, pscale_ref,
                   lng_ref, lnb_ref, ws_ref, bsp_ref, za_ref, yb_ref, yc_ref, halo_ref):
    tm = x_ref.shape[0]
    i = pl.program_id(0)
    seq_tile = i % tiles_per_seq

    h = _rms(x_ref[...], gmix_ref[...]).astype(BF16)
    z = _dot(h, win_ref[...])

    for k in range(N_CLUSTERS):
        za_ref[k] = z[:, k * LANES:(k + 1) * LANES].astype(BF16)

    zb = z[:, D_SSM:D_SSM + D_POOL]
    halo = jnp.where(seq_tile == 0, 0.0, halo_ref[...])
    halo_ref[...] = zb[tm - MAX_WINDOW:, :]
    ext = jnp.concatenate([halo, zb], axis=0)
    lane = lax.broadcasted_iota(jnp.int32, (tm, LANES), 1)
    low = lane < POOL_GROUP
    pos1 = (seq_tile * tm + 1 + lax.broadcasted_iota(jnp.int32, (tm, LANES), 0)).astype(F32)

    e0 = ext[:, :LANES]
    s2 = e0 + pltpu.roll(e0, 1, 0)
    s4 = s2 + pltpu.roll(s2, 2, 0)
    sum0 = jnp.where(low, s2[MAX_WINDOW:], s4[MAX_WINDOW:])
    cnt0 = jnp.where(low, jnp.minimum(pos1, 2.0), jnp.minimum(pos1, 4.0))
    e1 = ext[:, LANES:]
    t2 = e1 + pltpu.roll(e1, 1, 0)
    t4 = t2 + pltpu.roll(t2, 2, 0)
    t8 = t4 + pltpu.roll(t4, 4, 0)
    t16 = t8 + pltpu.roll(t8, 8, 0)
    sum1 = jnp.where(low, t8[MAX_WINDOW:], t16[MAX_WINDOW:])
    cnt1 = jnp.where(low, jnp.minimum(pos1, 8.0), jnp.minimum(pos1, 16.0))
    pooled = jnp.concatenate([sum0 / cnt0, sum1 / cnt1], axis=1) - zb
    yb = _dot(pooled.astype(BF16), wpool_ref[...]) * pscale_ref[...]
    yb_ref[...] = yb.astype(BF16)

    zu = z[:, D_SSM + D_POOL:D_SSM + D_POOL + D_SGU]
    zv = z[:, D_SSM + D_POOL + D_SGU:]
    u = _gelu(zu)
    v = _gelu(zv)
    mu = jnp.mean(v, axis=-1, keepdims=True)
    vc = v - mu
    var = jnp.mean(vc * vc, axis=-1, keepdims=True)
    vn = (vc * lax.rsqrt(var + EPS) * lng_ref[...] + lnb_ref[...]).astype(BF16)
    lane_c = lax.broadcasted_iota(jnp.int32, (CHUNK, LANES), 1)
    low_c = lane_c < SGU_HEAD_DIM
    zero = jnp.zeros((), BF16)
    for c in range(tm // CHUNK):
        rows = slice(c * CHUNK, (c + 1) * CHUNK)
        parts = []
        for p in range(SGU_HEADS // 2):
            vp = vn[rows, p * LANES:(p + 1) * LANES]
            lo = jnp.where(low_c, vp, zero)
            hi = jnp.where(low_c, zero, vp)
            parts.append(_dot(ws_ref[2 * p], lo) + _dot(ws_ref[2 * p + 1], hi))
        mixed = jnp.concatenate(parts, axis=1) + bsp_ref[...]
        yc_ref[rows, :] = (u[rows, :] * mixed).astype(BF16)


def _mix_in(x2d, gmix, win, wpool_bd, pscale, lng, lnb, ws, bsp, seq_len):
    m = x2d.shape[0]
    tm = TOKEN_TILE
    const2 = lambda i: (0, 0)
    const3 = lambda i: (0, 0, 0)
    return pl.pallas_call(
        functools.partial(_mix_in_kernel, seq_len // tm),
        grid=(m // tm,),
        in_specs=[
            pl.BlockSpec((tm, D_MODEL), lambda i: (i, 0)),
            pl.BlockSpec((1, D_MODEL), const2),
            pl.BlockSpec((D_MODEL, D_IN), const2),
            pl.BlockSpec((D_POOL, D_POOL), const2),
            pl.BlockSpec((1, D_POOL), const2),
            pl.BlockSpec((1, D_SGU), const2),
            pl.BlockSpec((1, D_SGU), const2),
            pl.BlockSpec((SGU_HEADS, CHUNK, CHUNK), const3),
            pl.BlockSpec((CHUNK, D_SGU), const2),
        ],
        out_specs=[
            pl.BlockSpec((N_CLUSTERS, tm, LANES), lambda i: (0, i, 0)),
            pl.BlockSpec((tm, D_POOL), lambda i: (i, 0)),
            pl.BlockSpec((tm, D_SGU), lambda i: (i, 0)),
        ],
        out_shape=[
            jax.ShapeDtypeStruct((N_CLUSTERS, m, LANES), BF16),
            jax.ShapeDtypeStruct((m, D_POOL), BF16),
            jax.ShapeDtypeStruct((m, D_SGU), BF16),
        ],
        scratch_shapes=[pltpu.VMEM((MAX_WINDOW, D_POOL), F32)],
        compiler_params=pltpu.CompilerParams(
            dimension_semantics=("arbitrary",), vmem_limit_bytes=VMEM_LIMIT),
        name="mix_in",
    )(x2d, gmix, win, wpool_bd, pscale, lng, lnb, ws, bsp)


def _s5_kernel(tiles_per_seq, u_ref, lagk_ref, wend_ref, wread_ref, pw_ref, dskip_ref,
               g_ref, st_ref, carry_ref):
    rows = u_ref.shape[1]
    r = pl.program_id(1)

    @pl.when(r % tiles_per_seq == 0)
    def _():
        carry_ref[...] = jnp.zeros_like(carry_ref)

    u = u_ref[0]
    y = _dot(u, lagk_ref[0])
    st_ref[...] = _dot(u, wend_ref[0])

    pr = pw_ref[0, :, :CLUSTER_STATE]
    pi = pw_ref[0, :, CLUSTER_STATE:]

    def step(b, hc):
        hr, hi = hc
        e = st_ref[pl.ds(b, 1), :]
        st_ref[pl.ds(b, 1), :] = jnp.concatenate([hr, hi], axis=1)
        nr = pr * hr - pi * hi + e[:, :CLUSTER_STATE]
        ni = pr * hi + pi * hr + e[:, CLUSTER_STATE:]
        return nr, ni

    h0 = (carry_ref[:, :CLUSTER_STATE], carry_ref[:, CLUSTER_STATE:])
    hr, hi = lax.fori_loop(0, rows, step, h0, unroll=8)
    carry_ref[...] = jnp.concatenate([hr, hi], axis=1)

    y = y + _dot(st_ref[...].astype(BF16), wread_ref[0])
    y = y + dskip_ref[0] * u.astype(F32)
    g_ref[0] = _gelu(y)


def _s5(u_flat, lagk, wend, wread, pw, dskip, blocks_per_seq):
    nb = u_flat.shape[1]
    rt = S5_ROW_TILE
    per_cluster = lambda k, r: (k, 0, 0)
    return pl.pallas_call(
        functools.partial(_s5_kernel, blocks_per_seq // rt),
        grid=(N_CLUSTERS, nb // rt),
        in_specs=[
            pl.BlockSpec((1, rt, FLAT), lambda k, r: (k, r, 0)),
            pl.BlockSpec((1, FLAT, FLAT), per_cluster),
            pl.BlockSpec((1, FLAT, 2 * CLUSTER_STATE), per_cluster),
            pl.BlockSpec((1, 2 * CLUSTER_STATE, FLAT), per_cluster),
            pl.BlockSpec((1, 1, 2 * CLUSTER_STATE), per_cluster),
            pl.BlockSpec((1, 1, FLAT), per_cluster),
        ],
        out_specs=pl.BlockSpec((1, rt, FLAT), lambda k, r: (k, r, 0)),
        out_shape=jax.ShapeDtypeStruct((N_CLUSTERS, nb, FLAT), F32),
        scratch_shapes=[pltpu.VMEM((rt, 2 * CLUSTER_STATE), F32),
                        pltpu.VMEM((1, 2 * CLUSTER_STATE), F32)],
        compiler_params=pltpu.CompilerParams(
            dimension_semantics=("arbitrary", "arbitrary"), vmem_limit_bytes=VMEM_LIMIT),
        name="s5",
    )(u_flat, lagk, wend, wread, pw, dskip)


def _s5_weights(A_re, A_im, log_dt, B_re, B_im, C_re, C_im, D_skip):
    G, N, T = N_SSM_GROUPS, SSM_STATE, SSM_BLOCK
    dt = jnp.exp(log_dt)[:, None]
    mag = jnp.exp(A_re * dt)
    ar = mag * jnp.cos(A_im * dt)
    ai = mag * jnp.sin(A_im * dt)
    den = A_re * A_re + A_im * A_im
    f_re = ((ar - 1.0) * A_re + ai * A_im) / den
    f_im = (ai * A_re - (ar - 1.0) * A_im) / den
    bb_re = f_re[..., None] * B_re - f_im[..., None] * B_im
    bb_im = f_re[..., None] * B_im + f_im[..., None] * B_re
    lg = jnp.arange(T + 1, dtype=F32)[:, None, None]
    pmag = jnp.exp(lg * (A_re * dt)[None])
    pw_re = pmag * jnp.cos(lg * (A_im * dt)[None])
    pw_im = pmag * jnp.sin(lg * (A_im * dt)[None])
    x_re = pw_re[:T, :, :, None] * bb_re[None] - pw_im[:T, :, :, None] * bb_im[None]
    x_im = pw_re[:T, :, :, None] * bb_im[None] + pw_im[:T, :, :, None] * bb_re[None]
    klag = (jnp.einsum('lgnd,gcn->lgdc', x_re, C_re) - jnp.einsum('lgnd,gcn->lgdc', x_im, C_im))
    s_idx = jnp.arange(T)[:, None]
    t_idx = jnp.arange(T)[None, :]
    lag = jnp.clip(t_idx - s_idx, 0, T - 1)
    causal = (t_idx >= s_idx).astype(F32)
    kt = klag[lag] * causal[:, :, None, None, None]
    kt = kt.reshape(T, T, N_CLUSTERS, GROUPS_PER_CLUSTER, SSM_GROUP, SSM_GROUP)
    eye = jnp.eye(GROUPS_PER_CLUSTER, dtype=F32)
    lagk = jnp.einsum('stkgdc,gh->ksgdthc', kt, eye).reshape(N_CLUSTERS, FLAT, FLAT)
    xe = jnp.stack([x_re[::-1], x_im[::-1]], axis=0)
    xe = xe.reshape(2, T, N_CLUSTERS, GROUPS_PER_CLUSTER, N, SSM_GROUP)
    wend = jnp.einsum('rskgnd,gh->ksgdrhn', xe, eye).reshape(N_CLUSTERS, FLAT, 2 * CLUSTER_STATE)
    q_re = pw_re[1:, :, None, :]
    q_im = pw_im[1:, :, None, :]
    ca_re = C_re[None] * q_re - C_im[None] * q_im
    ca_im = C_re[None] * q_im + C_im[None] * q_re
    wc = jnp.stack([ca_re, -ca_im], axis=0)
    wc = wc.reshape(2, T, N_CLUSTERS, GROUPS_PER_CLUSTER, SSM_GROUP, N)
    wread = jnp.einsum('rtkgcn,gh->krgnthc', wc, eye).reshape(N_CLUSTERS, 2 * CLUSTER_STATE, FLAT)
    pw = jnp.stack([pw_re[T].reshape(N_CLUSTERS, CLUSTER_STATE),
                    pw_im[T].reshape(N_CLUSTERS, CLUSTER_STATE)], axis=1)
    pw = pw.reshape(N_CLUSTERS, 1, 2 * CLUSTER_STATE)
    dskip = jnp.tile(D_skip.reshape(N_CLUSTERS, 1, LANES), (1, 1, T))
    return lagk.astype(BF16), wend.astype(BF16), wread.astype(BF16), pw, dskip


def _mix_out_kernel(final, x_ref, g_ref, yb_ref, yc_ref, wglu_ref, bglu_ref, wout_ref,
                    gffn_ref, wg_ref, wu_ref, wd_ref, gfin_ref, o_ref):
    g = jnp.concatenate([g_ref[k] for k in range(N_CLUSTERS)], axis=1)
    ya = g * jax.nn.sigmoid(_dot(g.astype(BF16), wglu_ref[...]) + bglu_ref[...])
    ymix = jnp.concatenate([ya.astype(BF16), yb_ref[...], yc_ref[...]], axis=1)
    x1 = x_ref[...] + _dot(ymix, wout_ref[...])
    h = _rms(x1, gffn_ref[...]).astype(BF16)
    acc = x1
    for c0, c1 in FF_CHUNKS:
        gate = _dot(h, wg_ref[:, c0:c1])
        up = _dot(h, wu_ref[:, c0:c1])
        act = (gate * jax.nn.sigmoid(gate) * up).astype(BF16)
        acc = acc + _dot(act, wd_ref[c0:c1, :])
    if final:
        acc = _rms(acc, gfin_ref[...])
    o_ref[...] = acc


def _mix_out(x2d, g3, yb, yc, wglu, bglu, wout, gffn, wg, wu, wd, gfin, final):
    m = x2d.shape[0]
    tm = TOKEN_TILE
    d_ff = wg.shape[1]
    const2 = lambda i: (0, 0)
    resident = dict(pipeline_mode=pl.Buffered(1))
    return pl.pallas_call(
        functools.partial(_mix_out_kernel, final),
        grid=(m // tm,),
        in_specs=[
            pl.BlockSpec((tm, D_MODEL), lambda i: (i, 0)),
            pl.BlockSpec((N_CLUSTERS, tm, LANES), lambda i: (0, i, 0)),
            pl.BlockSpec((tm, D_POOL), lambda i: (i, 0)),
            pl.BlockSpec((tm, D_SGU), lambda i: (i, 0)),
            pl.BlockSpec((D_SSM, D_SSM), const2, **resident),
            pl.BlockSpec((1, D_SSM), const2),
            pl.BlockSpec((D_MODEL, D_MODEL), const2, **resident),
            pl.BlockSpec((1, D_MODEL), const2),
            pl.BlockSpec((D_MODEL, d_ff), const2, **resident),
            pl.BlockSpec((D_MODEL, d_ff), const2, **resident),
            pl.BlockSpec((d_ff, D_MODEL), const2, **resident),
            pl.BlockSpec((1, D_MODEL), const2),
        ],
        out_specs=pl.BlockSpec((tm, D_MODEL), lambda i: (i, 0)),
        out_shape=jax.ShapeDtypeStruct((m, D_MODEL), F32),
        compiler_params=pltpu.CompilerParams(
            dimension_semantics=("arbitrary",), vmem_limit_bytes=VMEM_LIMIT),
        name="mix_out",
    )(x2d, g3, yb, yc, wglu, bglu, wout, gffn, wg, wu, wd, gfin)


def kernel(x, g_mix, w_in, A_re, A_im, log_dt, B_re, B_im, C_re, C_im, D_skip, w_glu, b_glu,
           w_pool, pool_scale, sgu_ln_g, sgu_ln_b, w_spatial, b_spatial, w_out, g_ffn,
           w_gate, w_up, w_down, g_final):
    bsz, seq, d = x.shape
    depth = w_in.shape[0]
    m = bsz * seq
    assert d == D_MODEL and seq % TOKEN_TILE == 0 and TOKEN_TILE % CHUNK == 0
    assert (seq // SSM_BLOCK) % S5_ROW_TILE == 0
    x2d = x.reshape(m, D_MODEL)
    tril = jnp.tril(jnp.ones((CHUNK, CHUNK), dtype=bool))
    for l in range(depth):
        wpool_bd = jax.scipy.linalg.block_diag(*[w_pool[l, g] for g in range(len(POOL_WINDOWS))])
        ws = jnp.where(tril[None], w_spatial[l], 0.0).astype(BF16)
        bsp = jnp.repeat(jnp.transpose(b_spatial[l]), SGU_HEAD_DIM, axis=1)
        za, yb, yc = _mix_in(
            x2d, g_mix[l][None], w_in[l].astype(BF16), wpool_bd.astype(BF16),
            pool_scale[l][None], sgu_ln_g[l][None], sgu_ln_b[l][None], ws, bsp, seq)
        lagk, wend, wread, pw, dskip = _s5_weights(
            A_re[l], A_im[l], log_dt[l], B_re[l], B_im[l], C_re[l], C_im[l], D_skip[l])
        u_flat = za.reshape(N_CLUSTERS, m // SSM_BLOCK, FLAT)
        g3 = _s5(u_flat, lagk, wend, wread, pw, dskip, seq // SSM_BLOCK)
        g3 = g3.reshape(N_CLUSTERS, m, LANES)
        x2d = _mix_out(
            x2d, g3, yb, yc, w_glu[l].astype(BF16), b_glu[l][None], w_out[l].astype(BF16),
            g_ffn[l][None], w_gate[l].astype(BF16), w_up[l].astype(BF16),
            w_down[l].astype(BF16), g_final[None], final=(l == depth - 1))
    return x2d.reshape(bsz, seq, D_MODEL)
```

```python
import functools
import math

import jax
import jax.numpy as jnp
from jax import lax
from jax.experimental import pallas as pl
from jax.experimental.pallas import tpu as pltpu

D_MODEL = 1024
D_SSM = 384
SSM_GROUP = 16
N_SSM_GROUPS = D_SSM // SSM_GROUP
SSM_STATE = 64
POOL_WINDOWS = (2, 4, 8, 16)
POOL_GROUP = 64
D_POOL = len(POOL_WINDOWS) * POOL_GROUP
MAX_WINDOW = max(POOL_WINDOWS)
SGU_HEADS = 6
SGU_HEAD_DIM = 64
D_SGU = SGU_HEADS * SGU_HEAD_DIM
CHUNK = 128
D_IN = D_SSM + D_POOL + 2 * D_SGU
EPS = 1e-6

LANES = 128
SSM_BLOCK = 16
N_CLUSTERS = D_SSM // LANES
GROUPS_PER_CLUSTER = LANES // SSM_GROUP
CLUSTER_STATE = GROUPS_PER_CLUSTER * SSM_STATE
FLAT = SSM_BLOCK * LANES

TOKEN_TILE = 512
S5_ROW_TILE = 256
FF_CHUNKS = ((0, 1024), (1024, 2048), (2048, 2816))
VMEM_LIMIT = 56 * 1024 * 1024

F32 = jnp.float32
BF16 = jnp.bfloat16
NT_DIMS = (((1,), (1,)), ((), ()))


def _gelu(x):
    c = math.sqrt(2.0 / math.pi)
    return 0.5 * x * (1.0 + jnp.tanh(c * (x + 0.044715 * (x * x * x))))


def _rms(x, g):
    ms = jnp.mean(x * x, axis=-1, keepdims=True)
    return x * lax.rsqrt(ms + EPS) * g


def _dot(a, b):
    return jnp.dot(a, b, preferred_element_type=F32)


def _dot_nt(a, b):
    return lax.dot_general(a, b, NT_DIMS, preferred_element_type=F32)


def _mix_in_kernel(tiles_per_seq, x_ref, gmix_ref, win_ref, wpool_ref, pscale_ref,
                   lng_ref, lnb_ref, ws_ref, bsp_ref, za_ref, yb_ref, yc_ref, halo_ref):
    tm = x_ref.shape[0]
    i = pl.program_id(0)
    seq_tile = i % tiles_per_seq

    h = _rms(x_ref[...], gmix_ref[...]).astype(BF16)
    z = _dot(h, win_ref[...])

    for k in range(N_CLUSTERS):
        za_ref[k] = z[:, k * LANES:(k + 1) * LANES].astype(BF16)

    zb = z[:, D_SSM:D_SSM + D_POOL]
    halo = jnp.where(seq_tile == 0, 0.0, halo_ref[...])
    halo_ref[...] = zb[tm - MAX_WINDOW:, :]
    ext = jnp.concatenate([halo, zb], axis=0)
    lane = lax.broadcasted_iota(jnp.int32, (tm, LANES), 1)
    low = lane < POOL_GROUP
    pos1 = (seq_tile * tm + 1 + lax.broadcasted_iota(jnp.int32, (tm, LANES), 0)).astype(F32)

    e0 = ext[:, :LANES]
    s2 = e0 + pltpu.roll(e0, 1, 0)
    s4 = s2 + pltpu.roll(s2, 2, 0)
    sum0 = jnp.where(low, s2[MAX_WINDOW:], s4[MAX_WINDOW:])
    cnt0 = jnp.where(low, jnp.minimum(pos1, 2.0), jnp.minimum(pos1, 4.0))
    e1 = ext[:, LANES:]
    t2 = e1 + pltpu.roll(e1, 1, 0)
    t4 = t2 + pltpu.roll(t2, 2, 0)
    t8 = t4 + pltpu.roll(t4, 4, 0)
    t16 = t8 + pltpu.roll(t8, 8, 0)
    sum1 = jnp.where(low, t8[MAX_WINDOW:], t16[MAX_WINDOW:])
    cnt1 = jnp.where(low, jnp.minimum(pos1, 8.0), jnp.minimum(pos1, 16.0))
    pooled = jnp.concatenate([sum0 / cnt0, sum1 / cnt1], axis=1) - zb
    yb = _dot(pooled.astype(BF16), wpool_ref[...]) * pscale_ref[...]
    yb_ref[...] = yb.astype(BF16)

    zu = z[:, D_SSM + D_POOL:D_SSM + D_POOL + D_SGU]
    zv = z[:, D_SSM + D_POOL + D_SGU:]
    u = _gelu(zu)
    v = _gelu(zv)
    mu = jnp.mean(v, axis=-1, keepdims=True)
    vc = v - mu
    var = jnp.mean(vc * vc, axis=-1, keepdims=True)
    vn = (vc * lax.rsqrt(var + EPS) * lng_ref[...] + lnb_ref[...]).astype(BF16)
    lane_c = lax.broadcasted_iota(jnp.int32, (CHUNK, LANES), 1)
    low_c = lane_c < SGU_HEAD_DIM
    zero = jnp.zeros((), BF16)
    for c in range(tm // CHUNK):
        rows = slice(c * CHUNK, (c + 1) * CHUNK)
        parts = []
        for p in range(SGU_HEADS // 2):
            vp = vn[rows, p * LANES:(p + 1) * LANES]
            lo = jnp.where(low_c, vp, zero)
            hi = jnp.where(low_c, zero, vp)
            parts.append(_dot(ws_ref[2 * p], lo) + _dot(ws_ref[2 * p + 1], hi))
        mixed = jnp.concatenate(parts, axis=1) + bsp_ref[...]
        yc_ref[rows, :] = (u[rows, :] * mixed).astype(BF16)


def _mix_in(x2d, gmix, win, wpool_bd, pscale, lng, lnb, ws, bsp, seq_len):
    m = x2d.shape[0]
    tm = TOKEN_TILE
    const2 = lambda i: (0, 0)
    const3 = lambda i: (0, 0, 0)
    return pl.pallas_call(
        functools.partial(_mix_in_kernel, seq_len // tm),
        grid=(m // tm,),
        in_specs=[
            pl.BlockSpec((tm, D_MODEL), lambda i: (i, 0)),
            pl.BlockSpec((1, D_MODEL), const2),
            pl.BlockSpec((D_MODEL, D_IN), const2),
            pl.BlockSpec((D_POOL, D_POOL), const2),
            pl.BlockSpec((1, D_POOL), const2),
            pl.BlockSpec((1, D_SGU), const2),
            pl.BlockSpec((1, D_SGU), const2),
            pl.BlockSpec((SGU_HEADS, CHUNK, CHUNK), const3),
            pl.BlockSpec((CHUNK, D_SGU), const2),
        ],
        out_specs=[
            pl.BlockSpec((N_CLUSTERS, tm, LANES), lambda i: (0, i, 0)),
            pl.BlockSpec((tm, D_POOL), lambda i: (i, 0)),
            pl.BlockSpec((tm, D_SGU), lambda i: (i, 0)),
        ],
        out_shape=[
            jax.ShapeDtypeStruct((N_CLUSTERS, m, LANES), BF16),
            jax.ShapeDtypeStruct((m, D_POOL), BF16),
            jax.ShapeDtypeStruct((m, D_SGU), BF16),
        ],
        scratch_shapes=[pltpu.VMEM((MAX_WINDOW, D_POOL), F32)],
        compiler_params=pltpu.CompilerParams(
            dimension_semantics=("arbitrary",), vmem_limit_bytes=VMEM_LIMIT),
        name="mix_in",
    )(x2d, gmix, win, wpool_bd, pscale, lng, lnb, ws, bsp)


def _s5_build_operators(par_ref, bt_ref, ct_ref, wend_ref, wrd_ref, lagk_ref, pw_ref):
    a_re = par_ref[0, 0:1, :]
    a_im = par_ref[0, 1:2, :]
    dt = jnp.exp(par_ref[0, 2:3, :])
    l_re = a_re * dt
    l_im = a_im * dt
    mag = jnp.exp(l_re)
    ar = mag * jnp.cos(l_im)
    ai = mag * jnp.sin(l_im)
    den = a_re * a_re + a_im * a_im
    f_re = ((ar - 1.0) * a_re + ai * a_im) / den
    f_im = (ai * a_re - (ar - 1.0) * a_im) / den
    bt_re, bt_im = bt_ref[0, 0], bt_ref[0, 1]
    bb_re = f_re * bt_re - f_im * bt_im
    bb_im = f_re * bt_im + f_im * bt_re
    ct_re, ct_im = ct_ref[0, 0], ct_ref[0, 1]

    row_g = lax.broadcasted_iota(jnp.int32, (LANES, CLUSTER_STATE), 0) // SSM_GROUP
    col_g = lax.broadcasted_iota(jnp.int32, (LANES, CLUSTER_STATE), 1) // SSM_STATE
    same_group = row_g == col_g

    def power(l):
        m_l = jnp.exp(l * l_re)
        return m_l * jnp.cos(l * l_im), m_l * jnp.sin(l * l_im)

    def spread(v):
        return jnp.where(same_group, jnp.tile(v, (GROUPS_PER_CLUSTER, 1)), 0.0).astype(BF16)

    for s in range(SSM_BLOCK):
        p_re, p_im = power(float(SSM_BLOCK - 1 - s))
        rows = slice(s * LANES, (s + 1) * LANES)
        wend_ref[rows, :CLUSTER_STATE] = spread(p_re * bb_re - p_im * bb_im)
        wend_ref[rows, CLUSTER_STATE:] = spread(p_re * bb_im + p_im * bb_re)
        q_re, q_im = power(float(s + 1))
        wrd_ref[rows, :CLUSTER_STATE] = spread(q_re * ct_re - q_im * ct_im)
        wrd_ref[rows, CLUSTER_STATE:] = spread(-(q_re * ct_im + q_im * ct_re))

    wrd0 = jnp.concatenate([spread(ct_re), spread(-ct_im)], axis=1)
    lag_all = _dot_nt(wend_ref[...], wrd0).astype(BF16)
    for t in range(SSM_BLOCK):
        cols = slice(t * LANES, (t + 1) * LANES)
        top = (t + 1) * LANES
        lagk_ref[:top, cols] = lag_all[FLAT - top:, :]
        if top < FLAT:
            lagk_ref[top:, cols] = jnp.zeros((FLAT - top, LANES), BF16)

    p_re, p_im = power(float(SSM_BLOCK))
    pw_ref[...] = jnp.concatenate([p_re, p_im], axis=1)


def _s5_kernel(tiles_per_seq, u_ref, par_ref, bt_ref, ct_ref, dskip_ref, g_ref,
               wend_ref, wrd_ref, lagk_ref, pw_ref, st_ref, carry_ref):
    rows = u_ref.shape[1]
    r = pl.program_id(1)

    @pl.when(r == 0)
    def _():
        _s5_build_operators(par_ref, bt_ref, ct_ref, wend_ref, wrd_ref, lagk_ref, pw_ref)

    @pl.when(r % tiles_per_seq == 0)
    def _():
        carry_ref[...] = jnp.zeros_like(carry_ref)

    u = u_ref[0]
    y = _dot(u, lagk_ref[...])
    st_ref[...] = _dot(u, wend_ref[...])

    pr = pw_ref[:, :CLUSTER_STATE]
    pi = pw_ref[:, CLUSTER_STATE:]

    def step(b, hc):
        hr, hi = hc
        e = st_ref[pl.ds(b, 1), :]
        st_ref[pl.ds(b, 1), :] = jnp.concatenate([hr, hi], axis=1)
        nr = pr * hr - pi * hi + e[:, :CLUSTER_STATE]
        ni = pr * hi + pi * hr + e[:, CLUSTER_STATE:]
        return nr, ni

    h0 = (carry_ref[:, :CLUSTER_STATE], carry_ref[:, CLUSTER_STATE:])
    hr, hi = lax.fori_loop(0, rows, step, h0, unroll=8)
    carry_ref[...] = jnp.concatenate([hr, hi], axis=1)

    y = y + _dot_nt(st_ref[...].astype(BF16), wrd_ref[...])
    y = y + dskip_ref[0] * u.astype(F32)
    g_ref[0] = _gelu(y)


def _s5(u_flat, par, bt, ct, dskip, blocks_per_seq):
    nb = u_flat.shape[1]
    rt = S5_ROW_TILE
    return pl.pallas_call(
        functools.partial(_s5_kernel, blocks_per_seq // rt),
        grid=(N_CLUSTERS, nb // rt),
        in_specs=[
            pl.BlockSpec((1, rt, FLAT), lambda k, r: (k, r, 0)),
            pl.BlockSpec((1, 3, CLUSTER_STATE), lambda k, r: (k, 0, 0)),
            pl.BlockSpec((1, 2, SSM_GROUP, CLUSTER_STATE), lambda k, r: (k, 0, 0, 0)),
            pl.BlockSpec((1, 2, SSM_GROUP, CLUSTER_STATE), lambda k, r: (k, 0, 0, 0)),
            pl.BlockSpec((1, 1, FLAT), lambda k, r: (k, 0, 0)),
        ],
        out_specs=pl.BlockSpec((1, rt, FLAT), lambda k, r: (k, r, 0)),
        out_shape=jax.ShapeDtypeStruct((N_CLUSTERS, nb, FLAT), F32),
        scratch_shapes=[pltpu.VMEM((FLAT, 2 * CLUSTER_STATE), BF16),
                        pltpu.VMEM((FLAT, 2 * CLUSTER_STATE), BF16),
                        pltpu.VMEM((FLAT, FLAT), BF16),
                        pltpu.VMEM((1, 2 * CLUSTER_STATE), F32),
                        pltpu.VMEM((rt, 2 * CLUSTER_STATE), F32),
                        pltpu.VMEM((1, 2 * CLUSTER_STATE), F32)],
        compiler_params=pltpu.CompilerParams(
            dimension_semantics=("arbitrary", "arbitrary"), vmem_limit_bytes=VMEM_LIMIT),
        name="s5",
    )(u_flat, par, bt, ct, dskip)


def _s5_params(A_re, A_im, log_dt, B_re, B_im, C_re, C_im, D_skip):
    depth = A_re.shape[0]
    k, gpc, n, c = N_CLUSTERS, GROUPS_PER_CLUSTER, SSM_STATE, SSM_GROUP
    rows = lambda a: a.reshape(depth, k, 1, CLUSTER_STATE)
    ldt = jnp.broadcast_to(log_dt[..., None], A_re.shape)
    par = jnp.concatenate([rows(A_re), rows(A_im), rows(ldt)], axis=2)
    b = jnp.stack([B_re, B_im], axis=1).reshape(depth, 2, k, gpc, n, c)
    bt = b.transpose(0, 2, 1, 5, 3, 4).reshape(depth, k, 2, c, CLUSTER_STATE)
    cc = jnp.stack([C_re, C_im], axis=1).reshape(depth, 2, k, gpc, c, n)
    ct = cc.transpose(0, 2, 1, 4, 3, 5).reshape(depth, k, 2, c, CLUSTER_STATE)
    dskip = jnp.tile(D_skip.reshape(depth, k, 1, LANES), (1, 1, 1, SSM_BLOCK))
    return par, bt, ct, dskip


def _mix_out_kernel(final, x_ref, g_ref, yb_ref, yc_ref, wglu_ref, bglu_ref, wout_ref,
                    gffn_ref, wg_ref, wu_ref, wd_ref, gfin_ref, o_ref):
    g = jnp.concatenate([g_ref[k] for k in range(N_CLUSTERS)], axis=1)
    ya = g * jax.nn.sigmoid(_dot(g.astype(BF16), wglu_ref[...]) + bglu_ref[...])
    ymix = jnp.concatenate([ya.astype(BF16), yb_ref[...], yc_ref[...]], axis=1)
    x1 = x_ref[...] + _dot(ymix, wout_ref[...])
    h = _rms(x1, gffn_ref[...]).astype(BF16)
    acc = x1
    for c0, c1 in FF_CHUNKS:
        gate = _dot(h, wg_ref[:, c0:c1])
        up = _dot(h, wu_ref[:, c0:c1])
        act = (gate * jax.nn.sigmoid(gate) * up).astype(BF16)
        acc = acc + _dot(act, wd_ref[c0:c1, :])
    if final:
        acc = _rms(acc, gfin_ref[...])
    o_ref[...] = acc


def _mix_out(x2d, g3, yb, yc, wglu, bglu, wout, gffn, wg, wu, wd, gfin, final):
    m = x2d.shape[0]
    tm = TOKEN_TILE
    d_ff = wg.shape[1]
    const2 = lambda i: (0, 0)
    resident = dict(pipeline_mode=pl.Buffered(1))
    return pl.pallas_call(
        functools.partial(_mix_out_kernel, final),
        grid=(m // tm,),
        in_specs=[
            pl.BlockSpec((tm, D_MODEL), lambda i: (i, 0)),
            pl.BlockSpec((N_CLUSTERS, tm, LANES), lambda i: (0, i, 0)),
            pl.BlockSpec((tm, D_POOL), lambda i: (i, 0)),
            pl.BlockSpec((tm, D_SGU), lambda i: (i, 0)),
            pl.BlockSpec((D_SSM, D_SSM), const2, **resident),
            pl.BlockSpec((1, D_SSM), const2),
            pl.BlockSpec((D_MODEL, D_MODEL), const2, **resident),
            pl.BlockSpec((1, D_MODEL), const2),
            pl.BlockSpec((D_MODEL, d_ff), const2, **resident),
            pl.BlockSpec((D_MODEL, d_ff), const2, **resident),
            pl.BlockSpec((d_ff, D_MODEL), const2, **resident),
            pl.BlockSpec((1, D_MODEL), const2),
        ],
        out_specs=pl.BlockSpec((tm, D_MODEL), lambda i: (i, 0)),
        out_shape=jax.ShapeDtypeStruct((m, D_MODEL), F32),
        compiler_params=pltpu.CompilerParams(
            dimension_semantics=("arbitrary",), vmem_limit_bytes=VMEM_LIMIT),
        name="mix_out",
    )(x2d, g3, yb, yc, wglu, bglu, wout, gffn, wg, wu, wd, gfin)


def kernel(x, g_mix, w_in, A_re, A_im, log_dt, B_re, B_im, C_re, C_im, D_skip, w_glu, b_glu,
           w_pool, pool_scale, sgu_ln_g, sgu_ln_b, w_spatial, b_spatial, w_out, g_ffn,
           w_gate, w_up, w_down, g_final):
    bsz, seq, d = x.shape
    depth = w_in.shape[0]
    m = bsz * seq
    assert d == D_MODEL and seq % TOKEN_TILE == 0 and TOKEN_TILE % CHUNK == 0
    assert (seq // SSM_BLOCK) % S5_ROW_TILE == 0
    x2d = x.reshape(m, D_MODEL)
    tril = jnp.tril(jnp.ones((CHUNK, CHUNK), dtype=bool))
    par, bt, ct, dskip = _s5_params(A_re, A_im, log_dt, B_re, B_im, C_re, C_im, D_skip)
    for l in range(depth):
        wpool_bd = jax.scipy.linalg.block_diag(*[w_pool[l, g] for g in range(len(POOL_WINDOWS))])
        ws = jnp.where(tril[None], w_spatial[l], 0.0).astype(BF16)
        bsp = jnp.repeat(jnp.transpose(b_spatial[l]), SGU_HEAD_DIM, axis=1)
        za, yb, yc = _mix_in(
            x2d, g_mix[l][None], w_in[l].astype(BF16), wpool_bd.astype(BF16),
            pool_scale[l][None], sgu_ln_g[l][None], sgu_ln_b[l][None], ws, bsp, seq)
        u_flat = za.reshape(N_CLUSTERS, m // SSM_BLOCK, FLAT)
        g3 = _s5(u_flat, par[l], bt[l], ct[l], dskip[l], seq // SSM_BLOCK)
        g3 = g3.reshape(N_CLUSTERS, m, LANES)
        x2d = _mix_out(
            x2d, g3, yb, yc, w_glu[l].astype(BF16), b_glu[l][None], w_out[l].astype(BF16),
            g_ffn[l][None], w_gate[l].astype(BF16), w_up[l].astype(BF16),
            w_down[l].astype(BF16), g_final[None], final=(l == depth - 1))
    return x2d.reshape(bsz, seq, D_MODEL)
```

```python
import functools
import math

import jax
import jax.numpy as jnp
from jax import lax
from jax.experimental import pallas as pl
from jax.experimental.pallas import tpu as pltpu

D_MODEL = 1024
D_SSM = 384
SSM_GROUP = 16
N_SSM_GROUPS = D_SSM // SSM_GROUP
SSM_STATE = 64
POOL_WINDOWS = (2, 4, 8, 16)
POOL_GROUP = 64
D_POOL = len(POOL_WINDOWS) * POOL_GROUP
MAX_WINDOW = max(POOL_WINDOWS)
SGU_HEADS = 6
SGU_HEAD_DIM = 64
D_SGU = SGU_HEADS * SGU_HEAD_DIM
CHUNK = 128
D_IN = D_SSM + D_POOL + 2 * D_SGU
EPS = 1e-6

LANES = 128
SSM_BLOCK = 16
N_CLUSTERS = D_SSM // LANES
GROUPS_PER_CLUSTER = LANES // SSM_GROUP
CLUSTER_STATE = GROUPS_PER_CLUSTER * SSM_STATE
FLAT = SSM_BLOCK * LANES

TOKEN_TILE = 512
S5_ROW_TILE = 256
FF_CHUNKS = ((0, 1024), (1024, 2048), (2048, 2816))
VMEM_LIMIT = 56 * 1024 * 1024

F32 = jnp.float32
BF16 = jnp.bfloat16
NT_DIMS = (((1,), (1,)), ((), ()))


def _gelu(x):
    c = math.sqrt(2.0 / math.pi)
    return 0.5 * x * (1.0 + jnp.tanh(c * (x + 0.044715 * (x * x * x))))


def _rms(x, g):
    ms = jnp.mean(x * x, axis=-1, keepdims=True)
    return x * lax.rsqrt(ms + EPS) * g


def _dot(a, b):
    return jnp.dot(a, b, preferred_element_type=F32)


def _dot_nt(a, b):
    return lax.dot_general(a, b, NT_DIMS, preferred_element_type=F32)


def _mix_in_kernel(tiles_per_seq, x_ref, gmix_ref, win_ref, wpool_ref, pscale_ref,
                   lng_ref, lnb_ref, ws_ref, bsp_ref, za_ref, yb_ref, yc_ref, halo_ref, zs_ref):
    tm = x_ref.shape[0]
    i = pl.program_id(0)
    seq_tile = i % tiles_per_seq

    h = _rms(x_ref[...], gmix_ref[...]).astype(BF16)
    z = _dot(h, win_ref[...])

    for k in range(N_CLUSTERS):
        zs_ref[k] = z[:, k * LANES:(k + 1) * LANES]
    for k in range(N_CLUSTERS):
        for s in range(SSM_BLOCK):
            piece = zs_ref[k, pl.ds(s, tm // SSM_BLOCK, stride=SSM_BLOCK), :]
            za_ref[k, :, s * LANES:(s + 1) * LANES] = piece.astype(BF16)

    zb = z[:, D_SSM:D_SSM + D_POOL]
    halo = jnp.where(seq_tile == 0, 0.0, halo_ref[...])
    halo_ref[...] = zb[tm - MAX_WINDOW:, :]
    ext = jnp.concatenate([halo, zb], axis=0)
    lane = lax.broadcasted_iota(jnp.int32, (tm, LANES), 1)
    low = lane < POOL_GROUP
    pos1 = (seq_tile * tm + 1 + lax.broadcasted_iota(jnp.int32, (tm, LANES), 0)).astype(F32)

    e0 = ext[:, :LANES]
    s2 = e0 + pltpu.roll(e0, 1, 0)
    s4 = s2 + pltpu.roll(s2, 2, 0)
    sum0 = jnp.where(low, s2[MAX_WINDOW:], s4[MAX_WINDOW:])
    cnt0 = jnp.where(low, jnp.minimum(pos1, 2.0), jnp.minimum(pos1, 4.0))
    e1 = ext[:, LANES:]
    t2 = e1 + pltpu.roll(e1, 1, 0)
    t4 = t2 + pltpu.roll(t2, 2, 0)
    t8 = t4 + pltpu.roll(t4, 4, 0)
    t16 = t8 + pltpu.roll(t8, 8, 0)
    sum1 = jnp.where(low, t8[MAX_WINDOW:], t16[MAX_WINDOW:])
    cnt1 = jnp.where(low, jnp.minimum(pos1, 8.0), jnp.minimum(pos1, 16.0))
    pooled = jnp.concatenate([sum0 / cnt0, sum1 / cnt1], axis=1) - zb
    yb = _dot(pooled.astype(BF16), wpool_ref[...]) * pscale_ref[...]
    yb_ref[...] = yb.astype(BF16)

    zu = z[:, D_SSM + D_POOL:D_SSM + D_POOL + D_SGU]
    zv = z[:, D_SSM + D_POOL + D_SGU:]
    u = _gelu(zu)
    v = _gelu(zv)
    mu = jnp.mean(v, axis=-1, keepdims=True)
    vc = v - mu
    var = jnp.mean(vc * vc, axis=-1, keepdims=True)
    vn = (vc * lax.rsqrt(var + EPS) * lng_ref[...] + lnb_ref[...]).astype(BF16)
    lane_c = lax.broadcasted_iota(jnp.int32, (CHUNK, LANES), 1)
    low_c = lane_c < SGU_HEAD_DIM
    zero = jnp.zeros((), BF16)
    for c in range(tm // CHUNK):
        rows = slice(c * CHUNK, (c + 1) * CHUNK)
        parts = []
        for p in range(SGU_HEADS // 2):
            vp = vn[rows, p * LANES:(p + 1) * LANES]
            lo = jnp.where(low_c, vp, zero)
            hi = jnp.where(low_c, zero, vp)
            parts.append(_dot(ws_ref[2 * p], lo) + _dot(ws_ref[2 * p + 1], hi))
        mixed = jnp.concatenate(parts, axis=1) + bsp_ref[...]
        yc_ref[rows, :] = (u[rows, :] * mixed).astype(BF16)


def _layer_spec(layer, shape):
    zeros = (0,) * len(shape)
    return pl.BlockSpec((None,) + tuple(shape), lambda *_: (layer,) + zeros)


def _mix_in(layer, x2d, gmix, win, wpool_bd, pscale, lng, lnb, ws, bsp, seq_len):
    m = x2d.shape[0]
    tm = TOKEN_TILE
    spec = functools.partial(_layer_spec, layer)
    return pl.pallas_call(
        functools.partial(_mix_in_kernel, seq_len // tm),
        grid=(m // tm,),
        in_specs=[
            pl.BlockSpec((tm, D_MODEL), lambda i: (i, 0)),
            spec((1, D_MODEL)),
            spec((D_MODEL, D_IN)),
            spec((D_POOL, D_POOL)),
            spec((1, D_POOL)),
            spec((1, D_SGU)),
            spec((1, D_SGU)),
            spec((SGU_HEADS, CHUNK, CHUNK)),
            spec((CHUNK, D_SGU)),
        ],
        out_specs=[
            pl.BlockSpec((N_CLUSTERS, tm // SSM_BLOCK, FLAT), lambda i: (0, i, 0)),
            pl.BlockSpec((tm, D_POOL), lambda i: (i, 0)),
            pl.BlockSpec((tm, D_SGU), lambda i: (i, 0)),
        ],
        out_shape=[
            jax.ShapeDtypeStruct((N_CLUSTERS, m // SSM_BLOCK, FLAT), BF16),
            jax.ShapeDtypeStruct((m, D_POOL), BF16),
            jax.ShapeDtypeStruct((m, D_SGU), BF16),
        ],
        scratch_shapes=[pltpu.VMEM((MAX_WINDOW, D_POOL), F32),
                        pltpu.VMEM((N_CLUSTERS, tm, LANES), F32)],
        compiler_params=pltpu.CompilerParams(
            dimension_semantics=("arbitrary",), vmem_limit_bytes=VMEM_LIMIT),
        name="mix_in",
    )(x2d, gmix, win, wpool_bd, pscale, lng, lnb, ws, bsp)


def _s5_build_operators(par_ref, bt_ref, ct_ref, wend_ref, wrd_ref, lagk_ref, pw_ref):
    a_re = par_ref[0, 0:1, :]
    a_im = par_ref[0, 1:2, :]
    dt = jnp.exp(par_ref[0, 2:3, :])
    l_re = a_re * dt
    l_im = a_im * dt
    mag = jnp.exp(l_re)
    ar = mag * jnp.cos(l_im)
    ai = mag * jnp.sin(l_im)
    den = a_re * a_re + a_im * a_im
    f_re = ((ar - 1.0) * a_re + ai * a_im) / den
    f_im = (ai * a_re - (ar - 1.0) * a_im) / den
    bt_re, bt_im = bt_ref[0, 0], bt_ref[0, 1]
    bb_re = f_re * bt_re - f_im * bt_im
    bb_im = f_re * bt_im + f_im * bt_re
    ct_re, ct_im = ct_ref[0, 0], ct_ref[0, 1]

    row_g = lax.broadcasted_iota(jnp.int32, (LANES, CLUSTER_STATE), 0) // SSM_GROUP
    col_g = lax.broadcasted_iota(jnp.int32, (LANES, CLUSTER_STATE), 1) // SSM_STATE
    same_group = row_g == col_g

    def power(l):
        m_l = jnp.exp(l * l_re)
        return m_l * jnp.cos(l * l_im), m_l * jnp.sin(l * l_im)

    def spread(v):
        return jnp.where(same_group, jnp.tile(v, (GROUPS_PER_CLUSTER, 1)), 0.0).astype(BF16)

    for s in range(SSM_BLOCK):
        p_re, p_im = power(float(SSM_BLOCK - 1 - s))
        rows = slice(s * LANES, (s + 1) * LANES)
        wend_ref[rows, :CLUSTER_STATE] = spread(p_re * bb_re - p_im * bb_im)
        wend_ref[rows, CLUSTER_STATE:] = spread(p_re * bb_im + p_im * bb_re)
        q_re, q_im = power(float(s + 1))
        wrd_ref[rows, :CLUSTER_STATE] = spread(q_re * ct_re - q_im * ct_im)
        wrd_ref[rows, CLUSTER_STATE:] = spread(-(q_re * ct_im + q_im * ct_re))

    wrd0 = jnp.concatenate([spread(ct_re), spread(-ct_im)], axis=1)
    lag_all = _dot_nt(wend_ref[...], wrd0).astype(BF16)
    for t in range(SSM_BLOCK):
        cols = slice(t * LANES, (t + 1) * LANES)
        top = (t + 1) * LANES
        lagk_ref[:top, cols] = lag_all[FLAT - top:, :]
        if top < FLAT:
            lagk_ref[top:, cols] = jnp.zeros((FLAT - top, LANES), BF16)

    p_re, p_im = power(float(SSM_BLOCK))
    pw_ref[...] = jnp.concatenate([p_re, p_im], axis=1)


def _s5_kernel(tiles_per_seq, u_ref, par_ref, bt_ref, ct_ref, dskip_ref, g_ref,
               wend_ref, wrd_ref, lagk_ref, pw_ref, st_ref, carry_ref):
    rows = u_ref.shape[1]
    r = pl.program_id(1)

    @pl.when(r == 0)
    def _():
        _s5_build_operators(par_ref, bt_ref, ct_ref, wend_ref, wrd_ref, lagk_ref, pw_ref)

    @pl.when(r % tiles_per_seq == 0)
    def _():
        carry_ref[...] = jnp.zeros_like(carry_ref)

    u = u_ref[0]
    y = _dot(u, lagk_ref[...])
    st_ref[...] = _dot(u, wend_ref[...])

    pr = pw_ref[:, :CLUSTER_STATE]
    pi = pw_ref[:, CLUSTER_STATE:]

    def step(b, hc):
        hr, hi = hc
        e = st_ref[pl.ds(b, 1), :]
        st_ref[pl.ds(b, 1), :] = jnp.concatenate([hr, hi], axis=1)
        nr = pr * hr - pi * hi + e[:, :CLUSTER_STATE]
        ni = pr * hi + pi * hr + e[:, CLUSTER_STATE:]
        return nr, ni

    h0 = (carry_ref[:, :CLUSTER_STATE], carry_ref[:, CLUSTER_STATE:])
    hr, hi = lax.fori_loop(0, rows, step, h0, unroll=8)
    carry_ref[...] = jnp.concatenate([hr, hi], axis=1)

    y = y + _dot_nt(st_ref[...].astype(BF16), wrd_ref[...])
    y = y + dskip_ref[0] * u.astype(F32)
    g_ref[0] = _gelu(y)


def _s5(layer, u_flat, par, bt, ct, dskip, blocks_per_seq):
    nb = u_flat.shape[1]
    rt = S5_ROW_TILE
    return pl.pallas_call(
        functools.partial(_s5_kernel, blocks_per_seq // rt),
        grid=(N_CLUSTERS, nb // rt),
        in_specs=[
            pl.BlockSpec((1, rt, FLAT), lambda k, r: (k, r, 0)),
            pl.BlockSpec((None, 1, 3, CLUSTER_STATE), lambda k, r: (layer, k, 0, 0)),
            pl.BlockSpec((None, 1, 2, SSM_GROUP, CLUSTER_STATE), lambda k, r: (layer, k, 0, 0, 0)),
            pl.BlockSpec((None, 1, 2, SSM_GROUP, CLUSTER_STATE), lambda k, r: (layer, k, 0, 0, 0)),
            pl.BlockSpec((None, 1, 1, FLAT), lambda k, r: (layer, k, 0, 0)),
        ],
        out_specs=pl.BlockSpec((1, rt, FLAT), lambda k, r: (k, r, 0)),
        out_shape=jax.ShapeDtypeStruct((N_CLUSTERS, nb, FLAT), F32),
        scratch_shapes=[pltpu.VMEM((FLAT, 2 * CLUSTER_STATE), BF16),
                        pltpu.VMEM((FLAT, 2 * CLUSTER_STATE), BF16),
                        pltpu.VMEM((FLAT, FLAT), BF16),
                        pltpu.VMEM((1, 2 * CLUSTER_STATE), F32),
                        pltpu.VMEM((rt, 2 * CLUSTER_STATE), F32),
                        pltpu.VMEM((1, 2 * CLUSTER_STATE), F32)],
        compiler_params=pltpu.CompilerParams(
            dimension_semantics=("arbitrary", "arbitrary"), vmem_limit_bytes=VMEM_LIMIT),
        name="s5",
    )(u_flat, par, bt, ct, dskip)


def _s5_params(A_re, A_im, log_dt, B_re, B_im, C_re, C_im, D_skip):
    depth = A_re.shape[0]
    k, gpc, n, c = N_CLUSTERS, GROUPS_PER_CLUSTER, SSM_STATE, SSM_GROUP
    rows = lambda a: a.reshape(depth, k, 1, CLUSTER_STATE)
    ldt = jnp.broadcast_to(log_dt[..., None], A_re.shape)
    par = jnp.concatenate([rows(A_re), rows(A_im), rows(ldt)], axis=2)
    b = jnp.stack([B_re, B_im], axis=1).reshape(depth, 2, k, gpc, n, c)
    bt = b.transpose(0, 2, 1, 5, 3, 4).reshape(depth, k, 2, c, CLUSTER_STATE)
    cc = jnp.stack([C_re, C_im], axis=1).reshape(depth, 2, k, gpc, c, n)
    ct = cc.transpose(0, 2, 1, 4, 3, 5).reshape(depth, k, 2, c, CLUSTER_STATE)
    dskip = jnp.tile(D_skip.reshape(depth, k, 1, LANES), (1, 1, 1, SSM_BLOCK))
    return par, bt, ct, dskip


def _mix_out_kernel(final, x_ref, g_ref, yb_ref, yc_ref, wglu_ref, bglu_ref, wout_ref,
                    gffn_ref, wg_ref, wu_ref, wd_ref, gfin_ref, o_ref, gs_ref):
    tm = x_ref.shape[0]
    for k in range(N_CLUSTERS):
        for s in range(SSM_BLOCK):
            gs_ref[k, pl.ds(s, tm // SSM_BLOCK, stride=SSM_BLOCK), :] = (
                g_ref[k, :, s * LANES:(s + 1) * LANES])
    g = jnp.concatenate([gs_ref[k] for k in range(N_CLUSTERS)], axis=1)
    ya = g * jax.nn.sigmoid(_dot(g.astype(BF16), wglu_ref[...]) + bglu_ref[...])
    ymix = jnp.concatenate([ya.astype(BF16), yb_ref[...], yc_ref[...]], axis=1)
    x1 = x_ref[...] + _dot(ymix, wout_ref[...])
    h = _rms(x1, gffn_ref[...]).astype(BF16)
    acc = x1
    for c0, c1 in FF_CHUNKS:
        gate = _dot(h, wg_ref[:, c0:c1])
        up = _dot(h, wu_ref[:, c0:c1])
        act = (gate * jax.nn.sigmoid(gate) * up).astype(BF16)
        acc = acc + _dot(act, wd_ref[c0:c1, :])
    if final:
        acc = _rms(acc, gfin_ref[...])
    o_ref[...] = acc


def _mix_out(layer, x2d, g3, yb, yc, wglu, bglu, wout, gffn, wg, wu, wd, gfin, final):
    m = x2d.shape[0]
    tm = TOKEN_TILE
    d_ff = wg.shape[-1]

    def spec(shape, resident=False):
        zeros = (0,) * len(shape)
        mode = dict(pipeline_mode=pl.Buffered(1)) if resident else {}
        return pl.BlockSpec((None,) + tuple(shape), lambda i: (layer,) + zeros, **mode)

    return pl.pallas_call(
        functools.partial(_mix_out_kernel, final),
        grid=(m // tm,),
        in_specs=[
            pl.BlockSpec((tm, D_MODEL), lambda i: (i, 0)),
            pl.BlockSpec((N_CLUSTERS, tm // SSM_BLOCK, FLAT), lambda i: (0, i, 0)),
            pl.BlockSpec((tm, D_POOL), lambda i: (i, 0)),
            pl.BlockSpec((tm, D_SGU), lambda i: (i, 0)),
            spec((D_SSM, D_SSM), resident=True),
            spec((1, D_SSM)),
            spec((D_MODEL, D_MODEL), resident=True),
            spec((1, D_MODEL)),
            spec((D_MODEL, d_ff), resident=True),
            spec((D_MODEL, d_ff), resident=True),
            spec((d_ff, D_MODEL), resident=True),
            pl.BlockSpec((1, D_MODEL), lambda i: (0, 0)),
        ],
        out_specs=pl.BlockSpec((tm, D_MODEL), lambda i: (i, 0)),
        out_shape=jax.ShapeDtypeStruct((m, D_MODEL), F32),
        scratch_shapes=[pltpu.VMEM((N_CLUSTERS, tm, LANES), F32)],
        compiler_params=pltpu.CompilerParams(
            dimension_semantics=("arbitrary",), vmem_limit_bytes=VMEM_LIMIT),
        name="mix_out",
    )(x2d, g3, yb, yc, wglu, bglu, wout, gffn, wg, wu, wd, gfin)


def kernel(x, g_mix, w_in, A_re, A_im, log_dt, B_re, B_im, C_re, C_im, D_skip, w_glu, b_glu,
           w_pool, pool_scale, sgu_ln_g, sgu_ln_b, w_spatial, b_spatial, w_out, g_ffn,
           w_gate, w_up, w_down, g_final):
    bsz, seq, d = x.shape
    depth = w_in.shape[0]
    m = bsz * seq
    assert d == D_MODEL and seq % TOKEN_TILE == 0 and TOKEN_TILE % CHUNK == 0
    assert (seq // SSM_BLOCK) % S5_ROW_TILE == 0
    x2d = x.reshape(m, D_MODEL)
    row = lambda a: a[:, None, :]
    tril = jnp.tril(jnp.ones((CHUNK, CHUNK), dtype=bool))
    eye = jnp.eye(len(POOL_WINDOWS), dtype=F32)
    wpool_bd = jnp.einsum('dgij,gh->dgihj', w_pool, eye).reshape(depth, D_POOL, D_POOL).astype(BF16)
    ws = jnp.where(tril, w_spatial, 0.0).astype(BF16)
    bsp = jnp.repeat(jnp.swapaxes(b_spatial, 1, 2), SGU_HEAD_DIM, axis=2)
    par, bt, ct, dskip = _s5_params(A_re, A_im, log_dt, B_re, B_im, C_re, C_im, D_skip)
    win, wout, wglu = w_in.astype(BF16), w_out.astype(BF16), w_glu.astype(BF16)
    wg, wu, wd = w_gate.astype(BF16), w_up.astype(BF16), w_down.astype(BF16)
    for l in range(depth):
        u_flat, yb, yc = _mix_in(l, x2d, row(g_mix), win, wpool_bd, row(pool_scale),
                                 row(sgu_ln_g), row(sgu_ln_b), ws, bsp, seq)
        g3 = _s5(l, u_flat, par, bt, ct, dskip, seq // SSM_BLOCK)
        x2d = _mix_out(l, x2d, g3, yb, yc, wglu, row(b_glu), wout, row(g_ffn), wg, wu, wd,
                       g_final[None], final=(l == depth - 1))
    return x2d.reshape(bsz, seq, D_MODEL)
```

```python
import functools
import math

import jax
import jax.numpy as jnp
from jax import lax
from jax.experimental import pallas as pl
from jax.experimental.pallas import tpu as pltpu

D_MODEL = 1024
D_SSM = 384
SSM_GROUP = 16
N_SSM_GROUPS = D_SSM // SSM_GROUP
SSM_STATE = 64
POOL_WINDOWS = (2, 4, 8, 16)
POOL_GROUP = 64
D_POOL = len(POOL_WINDOWS) * POOL_GROUP
MAX_WINDOW = max(POOL_WINDOWS)
SGU_HEADS = 6
SGU_HEAD_DIM = 64
D_SGU = SGU_HEADS * SGU_HEAD_DIM
CHUNK = 128
D_IN = D_SSM + D_POOL + 2 * D_SGU
EPS = 1e-6

LANES = 128
SSM_BLOCK = 16
N_CLUSTERS = D_SSM // LANES
GROUPS_PER_CLUSTER = LANES // SSM_GROUP
CLUSTER_STATE = GROUPS_PER_CLUSTER * SSM_STATE
FLAT = SSM_BLOCK * LANES

MXU_WIDTH = 256
TOKEN_TILE = 512
S5_ROW_TILE = 512
FF_CHUNKS = ((0, 1024), (1024, 2048), (2048, 2816))
VMEM_LIMIT = 56 * 1024 * 1024

F32 = jnp.float32
BF16 = jnp.bfloat16
NT_DIMS = (((1,), (1,)), ((), ()))


def _gelu(x):
    c = math.sqrt(2.0 / math.pi)
    return 0.5 * x * (1.0 + jnp.tanh(c * (x + 0.044715 * (x * x * x))))


def _rms(x, g):
    ms = jnp.mean(x * x, axis=-1, keepdims=True)
    return x * lax.rsqrt(ms + EPS) * g


def _dot(a, b):
    return jnp.dot(a, b, preferred_element_type=F32)


def _dot_nt(a, b):
    return lax.dot_general(a, b, NT_DIMS, preferred_element_type=F32)


def _mix_in_kernel(tiles_per_seq, x_ref, gmix_ref, win_ref, wpool_ref, pscale_ref,
                   lng_ref, lnb_ref, ws_ref, bsp_ref, za_ref, yb_ref, yc_ref, halo_ref, zs_ref):
    tm = x_ref.shape[0]
    i = pl.program_id(0)
    seq_tile = i % tiles_per_seq

    h = _rms(x_ref[...], gmix_ref[...]).astype(BF16)
    z_sgu = _dot(h, win_ref[:, :2 * D_SGU])
    z_rest = _dot(h, win_ref[:, 2 * D_SGU:])
    zv = z_sgu[:, :D_SGU]
    zu = z_sgu[:, D_SGU:]
    zb = z_rest[:, D_SSM:]

    for k in range(N_CLUSTERS):
        zs_ref[k] = z_rest[:, k * LANES:(k + 1) * LANES]
    for k in range(N_CLUSTERS):
        for s in range(SSM_BLOCK):
            piece = zs_ref[k, pl.ds(s, tm // SSM_BLOCK, stride=SSM_BLOCK), :]
            za_ref[k, :, s * LANES:(s + 1) * LANES] = piece.astype(BF16)

    halo = jnp.where(seq_tile == 0, 0.0, halo_ref[...])
    halo_ref[...] = zb[tm - MAX_WINDOW:, :]
    ext = jnp.concatenate([halo, zb], axis=0)
    lane = lax.broadcasted_iota(jnp.int32, (tm, LANES), 1)
    low = lane < POOL_GROUP
    pos1 = (seq_tile * tm + 1 + lax.broadcasted_iota(jnp.int32, (tm, LANES), 0)).astype(F32)

    e0 = ext[:, :LANES]
    s2 = e0 + pltpu.roll(e0, 1, 0)
    s4 = s2 + pltpu.roll(s2, 2, 0)
    sum0 = jnp.where(low, s2[MAX_WINDOW:], s4[MAX_WINDOW:])
    cnt0 = jnp.where(low, jnp.minimum(pos1, 2.0), jnp.minimum(pos1, 4.0))
    e1 = ext[:, LANES:]
    t2 = e1 + pltpu.roll(e1, 1, 0)
    t4 = t2 + pltpu.roll(t2, 2, 0)
    t8 = t4 + pltpu.roll(t4, 4, 0)
    t16 = t8 + pltpu.roll(t8, 8, 0)
    sum1 = jnp.where(low, t8[MAX_WINDOW:], t16[MAX_WINDOW:])
    cnt1 = jnp.where(low, jnp.minimum(pos1, 8.0), jnp.minimum(pos1, 16.0))
    pooled = jnp.concatenate([sum0 / cnt0, sum1 / cnt1], axis=1) - zb
    yb = _dot(pooled.astype(BF16), wpool_ref[...]) * pscale_ref[...]
    yb_ref[...] = yb.astype(BF16)

    u = _gelu(zu)
    v = _gelu(zv)
    mu = jnp.mean(v, axis=-1, keepdims=True)
    vc = v - mu
    var = jnp.mean(vc * vc, axis=-1, keepdims=True)
    vn = (vc * lax.rsqrt(var + EPS) * lng_ref[...] + lnb_ref[...]).astype(BF16)
    lane_c = lax.broadcasted_iota(jnp.int32, (CHUNK, LANES), 1)
    low_c = lane_c < SGU_HEAD_DIM
    zero = jnp.zeros((), BF16)
    for c in range(tm // CHUNK):
        rows = slice(c * CHUNK, (c + 1) * CHUNK)
        parts = []
        for p in range(SGU_HEADS // 2):
            vp = vn[rows, p * LANES:(p + 1) * LANES]
            lo = jnp.where(low_c, vp, zero)
            hi = jnp.where(low_c, zero, vp)
            parts.append(_dot(ws_ref[2 * p], lo) + _dot(ws_ref[2 * p + 1], hi))
        mixed = jnp.concatenate(parts, axis=1) + bsp_ref[...]
        yc_ref[rows, :] = (u[rows, :] * mixed).astype(BF16)


def _layer_spec(layer, shape):
    zeros = (0,) * len(shape)
    return pl.BlockSpec((None,) + tuple(shape), lambda *_: (layer,) + zeros)


def _mix_in(layer, x2d, gmix, win, wpool_bd, pscale, lng, lnb, ws, bsp, seq_len):
    m = x2d.shape[0]
    tm = TOKEN_TILE
    spec = functools.partial(_layer_spec, layer)
    return pl.pallas_call(
        functools.partial(_mix_in_kernel, seq_len // tm),
        grid=(m // tm,),
        in_specs=[
            pl.BlockSpec((tm, D_MODEL), lambda i: (i, 0)),
            spec((1, D_MODEL)),
            spec((D_MODEL, D_IN)),
            spec((D_POOL, D_POOL)),
            spec((1, D_POOL)),
            spec((1, D_SGU)),
            spec((1, D_SGU)),
            spec((SGU_HEADS, CHUNK, CHUNK)),
            spec((CHUNK, D_SGU)),
        ],
        out_specs=[
            pl.BlockSpec((N_CLUSTERS, tm // SSM_BLOCK, FLAT), lambda i: (0, i, 0)),
            pl.BlockSpec((tm, D_POOL), lambda i: (i, 0)),
            pl.BlockSpec((tm, D_SGU), lambda i: (i, 0)),
        ],
        out_shape=[
            jax.ShapeDtypeStruct((N_CLUSTERS, m // SSM_BLOCK, FLAT), BF16),
            jax.ShapeDtypeStruct((m, D_POOL), BF16),
            jax.ShapeDtypeStruct((m, D_SGU), BF16),
        ],
        scratch_shapes=[pltpu.VMEM((MAX_WINDOW, D_POOL), F32),
                        pltpu.VMEM((N_CLUSTERS, tm, LANES), F32)],
        compiler_params=pltpu.CompilerParams(
            dimension_semantics=("arbitrary",), vmem_limit_bytes=VMEM_LIMIT),
        name="mix_in",
    )(x2d, gmix, win, wpool_bd, pscale, lng, lnb, ws, bsp)


def _s5_build_operators(par_ref, bt_ref, ct_ref, wend_ref, wrd_ref, lagk_ref, pw_ref):
    a_re = par_ref[0, 0:1, :]
    a_im = par_ref[0, 1:2, :]
    dt = jnp.exp(par_ref[0, 2:3, :])
    l_re = a_re * dt
    l_im = a_im * dt
    mag = jnp.exp(l_re)
    ar = mag * jnp.cos(l_im)
    ai = mag * jnp.sin(l_im)
    den = a_re * a_re + a_im * a_im
    f_re = ((ar - 1.0) * a_re + ai * a_im) / den
    f_im = (ai * a_re - (ar - 1.0) * a_im) / den
    bt_re, bt_im = bt_ref[0, 0], bt_ref[0, 1]
    bb_re = f_re * bt_re - f_im * bt_im
    bb_im = f_re * bt_im + f_im * bt_re
    ct_re, ct_im = ct_ref[0, 0], ct_ref[0, 1]

    row_g = lax.broadcasted_iota(jnp.int32, (LANES, CLUSTER_STATE), 0) // SSM_GROUP
    col_g = lax.broadcasted_iota(jnp.int32, (LANES, CLUSTER_STATE), 1) // SSM_STATE
    same_group = row_g == col_g

    def power(l):
        m_l = jnp.exp(l * l_re)
        return m_l * jnp.cos(l * l_im), m_l * jnp.sin(l * l_im)

    def spread(v):
        return jnp.where(same_group, jnp.tile(v, (GROUPS_PER_CLUSTER, 1)), 0.0).astype(BF16)

    for s in range(SSM_BLOCK):
        p_re, p_im = power(float(SSM_BLOCK - 1 - s))
        rows = slice(s * LANES, (s + 1) * LANES)
        wend_ref[rows, :CLUSTER_STATE] = spread(p_re * bb_re - p_im * bb_im)
        wend_ref[rows, CLUSTER_STATE:] = spread(p_re * bb_im + p_im * bb_re)
        q_re, q_im = power(float(s + 1))
        wrd_ref[rows, :CLUSTER_STATE] = spread(q_re * ct_re - q_im * ct_im)
        wrd_ref[rows, CLUSTER_STATE:] = spread(-(q_re * ct_im + q_im * ct_re))

    wrd0 = jnp.concatenate([spread(ct_re), spread(-ct_im)], axis=1)
    lag_all = _dot_nt(wend_ref[...], wrd0).astype(BF16)
    for t in range(SSM_BLOCK):
        cols = slice(t * LANES, (t + 1) * LANES)
        top = (t + 1) * LANES
        lagk_ref[:top, cols] = lag_all[FLAT - top:, :]
        if top < FLAT:
            lagk_ref[top:, cols] = jnp.zeros((FLAT - top, LANES), BF16)

    p_re, p_im = power(float(SSM_BLOCK))
    pw_ref[...] = jnp.concatenate([p_re, p_im], axis=1)


def _s5_kernel(tiles_per_seq, u_ref, par_ref, bt_ref, ct_ref, dskip_ref, g_ref,
               wend_ref, wrd_ref, lagk_ref, pw_ref, st_ref, carry_ref):
    rows = u_ref.shape[1]
    r = pl.program_id(1)

    @pl.when(r == 0)
    def _():
        _s5_build_operators(par_ref, bt_ref, ct_ref, wend_ref, wrd_ref, lagk_ref, pw_ref)

    @pl.when(r % tiles_per_seq == 0)
    def _():
        carry_ref[...] = jnp.zeros_like(carry_ref)

    u = u_ref[0]
    ys = []
    for j in range(FLAT // MXU_WIDTH):
        kdim = (j + 1) * MXU_WIDTH
        ys.append(_dot(u[:, :kdim], lagk_ref[:kdim, j * MXU_WIDTH:(j + 1) * MXU_WIDTH]))
    y = jnp.concatenate(ys, axis=1)
    st_ref[...] = _dot(u, wend_ref[...])

    pr = pw_ref[:, :CLUSTER_STATE]
    pi = pw_ref[:, CLUSTER_STATE:]

    def step(b, hc):
        hr, hi = hc
        e = st_ref[pl.ds(b, 1), :]
        st_ref[pl.ds(b, 1), :] = jnp.concatenate([hr, hi], axis=1)
        nr = pr * hr - pi * hi + e[:, :CLUSTER_STATE]
        ni = pr * hi + pi * hr + e[:, CLUSTER_STATE:]
        return nr, ni

    h0 = (carry_ref[:, :CLUSTER_STATE], carry_ref[:, CLUSTER_STATE:])
    hr, hi = lax.fori_loop(0, rows, step, h0, unroll=8)
    carry_ref[...] = jnp.concatenate([hr, hi], axis=1)

    y = y + _dot_nt(st_ref[...].astype(BF16), wrd_ref[...])
    y = y + dskip_ref[0] * u.astype(F32)
    g_ref[0] = _gelu(y)


def _s5(layer, u_flat, par, bt, ct, dskip, blocks_per_seq):
    nb = u_flat.shape[1]
    rt = S5_ROW_TILE
    return pl.pallas_call(
        functools.partial(_s5_kernel, blocks_per_seq // rt),
        grid=(N_CLUSTERS, nb // rt),
        in_specs=[
            pl.BlockSpec((1, rt, FLAT), lambda k, r: (k, r, 0)),
            pl.BlockSpec((None, 1, 3, CLUSTER_STATE), lambda k, r: (layer, k, 0, 0)),
            pl.BlockSpec((None, 1, 2, SSM_GROUP, CLUSTER_STATE), lambda k, r: (layer, k, 0, 0, 0)),
            pl.BlockSpec((None, 1, 2, SSM_GROUP, CLUSTER_STATE), lambda k, r: (layer, k, 0, 0, 0)),
            pl.BlockSpec((None, 1, 1, FLAT), lambda k, r: (layer, k, 0, 0)),
        ],
        out_specs=pl.BlockSpec((1, rt, FLAT), lambda k, r: (k, r, 0)),
        out_shape=jax.ShapeDtypeStruct((N_CLUSTERS, nb, FLAT), F32),
        scratch_shapes=[pltpu.VMEM((FLAT, 2 * CLUSTER_STATE), BF16),
                        pltpu.VMEM((FLAT, 2 * CLUSTER_STATE), BF16),
                        pltpu.VMEM((FLAT, FLAT), BF16),
                        pltpu.VMEM((1, 2 * CLUSTER_STATE), F32),
                        pltpu.VMEM((rt, 2 * CLUSTER_STATE), F32),
                        pltpu.VMEM((1, 2 * CLUSTER_STATE), F32)],
        compiler_params=pltpu.CompilerParams(
            dimension_semantics=("arbitrary", "arbitrary"), vmem_limit_bytes=VMEM_LIMIT),
        name="s5",
    )(u_flat, par, bt, ct, dskip)


def _s5_params(A_re, A_im, log_dt, B_re, B_im, C_re, C_im, D_skip):
    depth = A_re.shape[0]
    k, gpc, n, c = N_CLUSTERS, GROUPS_PER_CLUSTER, SSM_STATE, SSM_GROUP
    rows = lambda a: a.reshape(depth, k, 1, CLUSTER_STATE)
    ldt = jnp.broadcast_to(log_dt[..., None], A_re.shape)
    par = jnp.concatenate([rows(A_re), rows(A_im), rows(ldt)], axis=2)
    b = jnp.stack([B_re, B_im], axis=1).reshape(depth, 2, k, gpc, n, c)
    bt = b.transpose(0, 2, 1, 5, 3, 4).reshape(depth, k, 2, c, CLUSTER_STATE)
    cc = jnp.stack([C_re, C_im], axis=1).reshape(depth, 2, k, gpc, c, n)
    ct = cc.transpose(0, 2, 1, 4, 3, 5).reshape(depth, k, 2, c, CLUSTER_STATE)
    dskip = jnp.tile(D_skip.reshape(depth, k, 1, LANES), (1, 1, 1, SSM_BLOCK))
    return par, bt, ct, dskip


def _mix_out_kernel(final, x_ref, g_ref, yb_ref, yc_ref, wglu_ref, bglu_ref, wout_ref,
                    gffn_ref, wg_ref, wu_ref, wd_ref, gfin_ref, o_ref, gs_ref):
    tm = x_ref.shape[0]
    for k in range(N_CLUSTERS):
        for s in range(SSM_BLOCK):
            gs_ref[k, pl.ds(s, tm // SSM_BLOCK, stride=SSM_BLOCK), :] = (
                g_ref[k, :, s * LANES:(s + 1) * LANES])
    g = jnp.concatenate([gs_ref[k] for k in range(N_CLUSTERS)], axis=1)
    ya = g * jax.nn.sigmoid(_dot(g.astype(BF16), wglu_ref[...]) + bglu_ref[...])
    ymix = jnp.concatenate([ya.astype(BF16), yb_ref[...], yc_ref[...]], axis=1)
    x1 = x_ref[...] + _dot(ymix, wout_ref[...])
    h = _rms(x1, gffn_ref[...]).astype(BF16)
    acc = x1
    for c0, c1 in FF_CHUNKS:
        gate = _dot(h, wg_ref[:, c0:c1])
        up = _dot(h, wu_ref[:, c0:c1])
        act = (gate * jax.nn.sigmoid(gate) * up).astype(BF16)
        acc = acc + _dot(act, wd_ref[c0:c1, :])
    if final:
        acc = _rms(acc, gfin_ref[...])
    o_ref[...] = acc


def _mix_out(layer, x2d, g3, yb, yc, wglu, bglu, wout, gffn, wg, wu, wd, gfin, final):
    m = x2d.shape[0]
    tm = TOKEN_TILE
    d_ff = wg.shape[-1]

    def spec(shape, resident=False):
        zeros = (0,) * len(shape)
        mode = dict(pipeline_mode=pl.Buffered(1)) if resident else {}
        return pl.BlockSpec((None,) + tuple(shape), lambda i: (layer,) + zeros, **mode)

    return pl.pallas_call(
        functools.partial(_mix_out_kernel, final),
        grid=(m // tm,),
        in_specs=[
            pl.BlockSpec((tm, D_MODEL), lambda i: (i, 0)),
            pl.BlockSpec((N_CLUSTERS, tm // SSM_BLOCK, FLAT), lambda i: (0, i, 0)),
            pl.BlockSpec((tm, D_POOL), lambda i: (i, 0)),
            pl.BlockSpec((tm, D_SGU), lambda i: (i, 0)),
            spec((D_SSM, D_SSM), resident=True),
            spec((1, D_SSM)),
            spec((D_MODEL, D_MODEL), resident=True),
            spec((1, D_MODEL)),
            spec((D_MODEL, d_ff), resident=True),
            spec((D_MODEL, d_ff), resident=True),
            spec((d_ff, D_MODEL), resident=True),
            pl.BlockSpec((1, D_MODEL), lambda i: (0, 0)),
        ],
        out_specs=pl.BlockSpec((tm, D_MODEL), lambda i: (i, 0)),
        out_shape=jax.ShapeDtypeStruct((m, D_MODEL), F32),
        scratch_shapes=[pltpu.VMEM((N_CLUSTERS, tm, LANES), F32)],
        compiler_params=pltpu.CompilerParams(
            dimension_semantics=("arbitrary",), vmem_limit_bytes=VMEM_LIMIT),
        name="mix_out",
    )(x2d, g3, yb, yc, wglu, bglu, wout, gffn, wg, wu, wd, gfin)


def kernel(x, g_mix, w_in, A_re, A_im, log_dt, B_re, B_im, C_re, C_im, D_skip, w_glu, b_glu,
           w_pool, pool_scale, sgu_ln_g, sgu_ln_b, w_spatial, b_spatial, w_out, g_ffn,
           w_gate, w_up, w_down, g_final):
    bsz, seq, d = x.shape
    depth = w_in.shape[0]
    m = bsz * seq
    assert d == D_MODEL and seq % TOKEN_TILE == 0 and TOKEN_TILE % CHUNK == 0
    assert (seq // SSM_BLOCK) % S5_ROW_TILE == 0
    x2d = x.reshape(m, D_MODEL)
    row = lambda a: a[:, None, :]
    tril = jnp.tril(jnp.ones((CHUNK, CHUNK), dtype=bool))
    eye = jnp.eye(len(POOL_WINDOWS), dtype=F32)
    wpool_bd = jnp.einsum('dgij,gh->dgihj', w_pool, eye).reshape(depth, D_POOL, D_POOL).astype(BF16)
    ws = jnp.where(tril, w_spatial, 0.0).astype(BF16)
    bsp = jnp.repeat(jnp.swapaxes(b_spatial, 1, 2), SGU_HEAD_DIM, axis=2)
    par, bt, ct, dskip = _s5_params(A_re, A_im, log_dt, B_re, B_im, C_re, C_im, D_skip)
    a_end, b_end, u_end = D_SSM, D_SSM + D_POOL, D_SSM + D_POOL + D_SGU
    win = jnp.concatenate([w_in[..., u_end:], w_in[..., b_end:u_end], w_in[..., :b_end]],
                          axis=-1).astype(BF16)
    wout, wglu = w_out.astype(BF16), w_glu.astype(BF16)
    wg, wu, wd = w_gate.astype(BF16), w_up.astype(BF16), w_down.astype(BF16)
    for l in range(depth):
        u_flat, yb, yc = _mix_in(l, x2d, row(g_mix), win, wpool_bd, row(pool_scale),
                                 row(sgu_ln_g), row(sgu_ln_b), ws, bsp, seq)
        g3 = _s5(l, u_flat, par, bt, ct, dskip, seq // SSM_BLOCK)
        x2d = _mix_out(l, x2d, g3, yb, yc, wglu, row(b_glu), wout, row(g_ffn), wg, wu, wd,
                       g_final[None], final=(l == depth - 1))
    return x2d.reshape(bsz, seq, D_MODEL)
```

```python
import functools
import math

import jax
import jax.numpy as jnp
from jax import lax
from jax.experimental import pallas as pl
from jax.experimental.pallas import tpu as pltpu

D_MODEL = 1024
D_SSM = 384
SSM_GROUP = 16
N_SSM_GROUPS = D_SSM // SSM_GROUP
SSM_STATE = 64
POOL_WINDOWS = (2, 4, 8, 16)
POOL_GROUP = 64
D_POOL = len(POOL_WINDOWS) * POOL_GROUP
MAX_WINDOW = max(POOL_WINDOWS)
SGU_HEADS = 6
SGU_HEAD_DIM = 64
D_SGU = SGU_HEADS * SGU_HEAD_DIM
CHUNK = 128
D_IN = D_SSM + D_POOL + 2 * D_SGU
EPS = 1e-6

LANES = 128
SSM_BLOCK = 16
N_CLUSTERS = D_SSM // LANES
GROUPS_PER_CLUSTER = LANES // SSM_GROUP
CLUSTER_STATE = GROUPS_PER_CLUSTER * SSM_STATE
FLAT = SSM_BLOCK * LANES

MXU_WIDTH = 256
TOKEN_TILE = 512
S5_ROW_TILE = 512
FF_CHUNKS = ((0, 1024), (1024, 2048), (2048, 2816))
VMEM_LIMIT = 56 * 1024 * 1024

F32 = jnp.float32
BF16 = jnp.bfloat16
NT_DIMS = (((1,), (1,)), ((), ()))


def _gelu(x):
    c = math.sqrt(2.0 / math.pi)
    return 0.5 * x * (1.0 + jnp.tanh(c * (x + 0.044715 * (x * x * x))))


def _rms(x, g):
    ms = jnp.mean(x * x, axis=-1, keepdims=True)
    return x * lax.rsqrt(ms + EPS) * g


def _dot(a, b):
    return jnp.dot(a, b, preferred_element_type=F32)


def _dot_nt(a, b):
    return lax.dot_general(a, b, NT_DIMS, preferred_element_type=F32)


def _mix_in_kernel(tiles_per_seq, x_ref, gmix_ref, win_ref, wpool_ref, pscale_ref,
                   lng_ref, lnb_ref, ws_ref, bsp_ref, za_ref, yb_ref, yc_ref, halo_ref, zs_ref):
    tm = x_ref.shape[0]
    i = pl.program_id(0)
    seq_tile = i % tiles_per_seq

    h = _rms(x_ref[...], gmix_ref[...]).astype(BF16)
    z_sgu = _dot(h, win_ref[:, :2 * D_SGU])
    z_rest = _dot(h, win_ref[:, 2 * D_SGU:])
    zv = z_sgu[:, :D_SGU]
    zu = z_sgu[:, D_SGU:]
    zb = z_rest[:, D_SSM:]

    for k in range(N_CLUSTERS):
        zs_ref[k] = z_rest[:, k * LANES:(k + 1) * LANES]
    for k in range(N_CLUSTERS):
        for s in range(SSM_BLOCK):
            piece = zs_ref[k, pl.ds(s, tm // SSM_BLOCK, stride=SSM_BLOCK), :]
            za_ref[k, :, s * LANES:(s + 1) * LANES] = piece.astype(BF16)

    halo = jnp.where(seq_tile == 0, 0.0, halo_ref[...])
    halo_ref[...] = zb[tm - MAX_WINDOW:, :]
    ext = jnp.concatenate([halo, zb], axis=0)
    lane = lax.broadcasted_iota(jnp.int32, (tm, LANES), 1)
    low = lane < POOL_GROUP
    pos1 = (seq_tile * tm + 1 + lax.broadcasted_iota(jnp.int32, (tm, LANES), 0)).astype(F32)

    e0 = ext[:, :LANES]
    s2 = e0 + pltpu.roll(e0, 1, 0)
    s4 = s2 + pltpu.roll(s2, 2, 0)
    sum0 = jnp.where(low, s2[MAX_WINDOW:], s4[MAX_WINDOW:])
    cnt0 = jnp.where(low, jnp.minimum(pos1, 2.0), jnp.minimum(pos1, 4.0))
    e1 = ext[:, LANES:]
    t2 = e1 + pltpu.roll(e1, 1, 0)
    t4 = t2 + pltpu.roll(t2, 2, 0)
    t8 = t4 + pltpu.roll(t4, 4, 0)
    t16 = t8 + pltpu.roll(t8, 8, 0)
    sum1 = jnp.where(low, t8[MAX_WINDOW:], t16[MAX_WINDOW:])
    cnt1 = jnp.where(low, jnp.minimum(pos1, 8.0), jnp.minimum(pos1, 16.0))
    pooled = jnp.concatenate([sum0 / cnt0, sum1 / cnt1], axis=1) - zb
    yb = _dot(pooled.astype(BF16), wpool_ref[...]) * pscale_ref[...]
    yb_ref[...] = yb.astype(BF16)

    u = _gelu(zu)
    v = _gelu(zv)
    mu = jnp.mean(v, axis=-1, keepdims=True)
    vc = v - mu
    var = jnp.mean(vc * vc, axis=-1, keepdims=True)
    vn = (vc * lax.rsqrt(var + EPS) * lng_ref[...] + lnb_ref[...]).astype(BF16)
    lane_c = lax.broadcasted_iota(jnp.int32, (CHUNK, LANES), 1)
    low_c = lane_c < SGU_HEAD_DIM
    zero = jnp.zeros((), BF16)
    for c in range(tm // CHUNK):
        rows = slice(c * CHUNK, (c + 1) * CHUNK)
        parts = []
        for p in range(SGU_HEADS // 2):
            vp = vn[rows, p * LANES:(p + 1) * LANES]
            lo = jnp.where(low_c, vp, zero)
            hi = jnp.where(low_c, zero, vp)
            parts.append(_dot(ws_ref[2 * p], lo) + _dot(ws_ref[2 * p + 1], hi))
        mixed = jnp.concatenate(parts, axis=1) + bsp_ref[...]
        yc_ref[rows, :] = (u[rows, :] * mixed).astype(BF16)


def _layer_spec(layer, shape):
    zeros = (0,) * len(shape)
    return pl.BlockSpec((None,) + tuple(shape), lambda *_: (layer,) + zeros)


def _mix_in(layer, x2d, gmix, win, wpool_bd, pscale, lng, lnb, ws, bsp, seq_len):
    m = x2d.shape[0]
    tm = TOKEN_TILE
    spec = functools.partial(_layer_spec, layer)
    return pl.pallas_call(
        functools.partial(_mix_in_kernel, seq_len // tm),
        grid=(m // tm,),
        in_specs=[
            pl.BlockSpec((tm, D_MODEL), lambda i: (i, 0)),
            spec((1, D_MODEL)),
            spec((D_MODEL, D_IN)),
            spec((D_POOL, D_POOL)),
            spec((1, D_POOL)),
            spec((1, D_SGU)),
            spec((1, D_SGU)),
            spec((SGU_HEADS, CHUNK, CHUNK)),
            spec((CHUNK, D_SGU)),
        ],
        out_specs=[
            pl.BlockSpec((N_CLUSTERS, tm // SSM_BLOCK, FLAT), lambda i: (0, i, 0)),
            pl.BlockSpec((tm, D_POOL), lambda i: (i, 0)),
            pl.BlockSpec((tm, D_SGU), lambda i: (i, 0)),
        ],
        out_shape=[
            jax.ShapeDtypeStruct((N_CLUSTERS, m // SSM_BLOCK, FLAT), BF16),
            jax.ShapeDtypeStruct((m, D_POOL), BF16),
            jax.ShapeDtypeStruct((m, D_SGU), BF16),
        ],
        scratch_shapes=[pltpu.VMEM((MAX_WINDOW, D_POOL), F32),
                        pltpu.VMEM((N_CLUSTERS, tm, LANES), F32)],
        compiler_params=pltpu.CompilerParams(
            dimension_semantics=("arbitrary",), vmem_limit_bytes=VMEM_LIMIT),
        name="mix_in",
    )(x2d, gmix, win, wpool_bd, pscale, lng, lnb, ws, bsp)


def _s5_build_operators(par_ref, bt_ref, ct_ref, wend_ref, wrd_ref, lagk_ref, pw_ref):
    a_re = par_ref[0, 0:1, :]
    a_im = par_ref[0, 1:2, :]
    dt = jnp.exp(par_ref[0, 2:3, :])
    l_re = a_re * dt
    l_im = a_im * dt
    mag = jnp.exp(l_re)
    ar = mag * jnp.cos(l_im)
    ai = mag * jnp.sin(l_im)
    den = a_re * a_re + a_im * a_im
    f_re = ((ar - 1.0) * a_re + ai * a_im) / den
    f_im = (ai * a_re - (ar - 1.0) * a_im) / den
    bt_re, bt_im = bt_ref[0, 0], bt_ref[0, 1]
    bb_re = f_re * bt_re - f_im * bt_im
    bb_im = f_re * bt_im + f_im * bt_re
    ct_re, ct_im = ct_ref[0, 0], ct_ref[0, 1]

    row_g = lax.broadcasted_iota(jnp.int32, (LANES, CLUSTER_STATE), 0) // SSM_GROUP
    col_g = lax.broadcasted_iota(jnp.int32, (LANES, CLUSTER_STATE), 1) // SSM_STATE
    same_group = row_g == col_g

    def power(l):
        m_l = jnp.exp(l * l_re)
        return m_l * jnp.cos(l * l_im), m_l * jnp.sin(l * l_im)

    def spread(v):
        return jnp.where(same_group, jnp.tile(v, (GROUPS_PER_CLUSTER, 1)), 0.0).astype(BF16)

    for s in range(SSM_BLOCK):
        p_re, p_im = power(float(SSM_BLOCK - 1 - s))
        rows = slice(s * LANES, (s + 1) * LANES)
        wend_ref[rows, :CLUSTER_STATE] = spread(p_re * bb_re - p_im * bb_im)
        wend_ref[rows, CLUSTER_STATE:] = spread(p_re * bb_im + p_im * bb_re)
        q_re, q_im = power(float(s + 1))
        wrd_ref[rows, :CLUSTER_STATE] = spread(q_re * ct_re - q_im * ct_im)
        wrd_ref[rows, CLUSTER_STATE:] = spread(-(q_re * ct_im + q_im * ct_re))

    wrd0 = jnp.concatenate([spread(ct_re), spread(-ct_im)], axis=1)
    lag_all = _dot_nt(wend_ref[...], wrd0).astype(BF16)
    for t in range(SSM_BLOCK):
        cols = slice(t * LANES, (t + 1) * LANES)
        top = (t + 1) * LANES
        lagk_ref[:top, cols] = lag_all[FLAT - top:, :]
        if top < FLAT:
            lagk_ref[top:, cols] = jnp.zeros((FLAT - top, LANES), BF16)

    p_re, p_im = power(float(SSM_BLOCK))
    pw_ref[...] = jnp.concatenate([p_re, p_im], axis=1)


def _s5_kernel(tiles_per_seq, u_ref, par_ref, bt_ref, ct_ref, dskip_ref, g_ref,
               wend_ref, wrd_ref, lagk_ref, pw_ref, st_ref, carry_ref):
    rows = u_ref.shape[1]
    r = pl.program_id(1)

    @pl.when(r == 0)
    def _():
        _s5_build_operators(par_ref, bt_ref, ct_ref, wend_ref, wrd_ref, lagk_ref, pw_ref)

    @pl.when(r % tiles_per_seq == 0)
    def _():
        carry_ref[...] = jnp.zeros_like(carry_ref)

    u = u_ref[0]
    ys = []
    for j in range(FLAT // MXU_WIDTH):
        kdim = (j + 1) * MXU_WIDTH
        ys.append(_dot(u[:, :kdim], lagk_ref[:kdim, j * MXU_WIDTH:(j + 1) * MXU_WIDTH]))
    y = jnp.concatenate(ys, axis=1)
    st_ref[...] = _dot(u, wend_ref[...])

    pr = pw_ref[:, :CLUSTER_STATE]
    pi = pw_ref[:, CLUSTER_STATE:]

    def step(b, hc):
        hr, hi = hc
        e = st_ref[pl.ds(b, 1), :]
        st_ref[pl.ds(b, 1), :] = jnp.concatenate([hr, hi], axis=1)
        nr = pr * hr - pi * hi + e[:, :CLUSTER_STATE]
        ni = pr * hi + pi * hr + e[:, CLUSTER_STATE:]
        return nr, ni

    h0 = (carry_ref[:, :CLUSTER_STATE], carry_ref[:, CLUSTER_STATE:])
    hr, hi = lax.fori_loop(0, rows, step, h0, unroll=8)
    carry_ref[...] = jnp.concatenate([hr, hi], axis=1)

    y = y + _dot_nt(st_ref[...].astype(BF16), wrd_ref[...])
    y = y + dskip_ref[0] * u.astype(F32)
    g_ref[0] = _gelu(y)


def _s5(layer, u_flat, par, bt, ct, dskip, blocks_per_seq):
    nb = u_flat.shape[1]
    rt = S5_ROW_TILE
    return pl.pallas_call(
        functools.partial(_s5_kernel, blocks_per_seq // rt),
        grid=(N_CLUSTERS, nb // rt),
        in_specs=[
            pl.BlockSpec((1, rt, FLAT), lambda k, r: (k, r, 0)),
            pl.BlockSpec((None, 1, 3, CLUSTER_STATE), lambda k, r: (layer, k, 0, 0)),
            pl.BlockSpec((None, 1, 2, SSM_GROUP, CLUSTER_STATE), lambda k, r: (layer, k, 0, 0, 0)),
            pl.BlockSpec((None, 1, 2, SSM_GROUP, CLUSTER_STATE), lambda k, r: (layer, k, 0, 0, 0)),
            pl.BlockSpec((None, 1, 1, FLAT), lambda k, r: (layer, k, 0, 0)),
        ],
        out_specs=pl.BlockSpec((1, rt, FLAT), lambda k, r: (k, r, 0)),
        out_shape=jax.ShapeDtypeStruct((N_CLUSTERS, nb, FLAT), F32),
        scratch_shapes=[pltpu.VMEM((FLAT, 2 * CLUSTER_STATE), BF16),
                        pltpu.VMEM((FLAT, 2 * CLUSTER_STATE), BF16),
                        pltpu.VMEM((FLAT, FLAT), BF16),
                        pltpu.VMEM((1, 2 * CLUSTER_STATE), F32),
                        pltpu.VMEM((rt, 2 * CLUSTER_STATE), F32),
                        pltpu.VMEM((1, 2 * CLUSTER_STATE), F32)],
        compiler_params=pltpu.CompilerParams(
            dimension_semantics=("arbitrary", "arbitrary"), vmem_limit_bytes=VMEM_LIMIT),
        name="s5",
    )(u_flat, par, bt, ct, dskip)


def _s5_params(A_re, A_im, log_dt, B_re, B_im, C_re, C_im, D_skip):
    depth = A_re.shape[0]
    k, gpc, n, c = N_CLUSTERS, GROUPS_PER_CLUSTER, SSM_STATE, SSM_GROUP
    rows = lambda a: a.reshape(depth, k, 1, CLUSTER_STATE)
    ldt = jnp.broadcast_to(log_dt[..., None], A_re.shape)
    par = jnp.concatenate([rows(A_re), rows(A_im), rows(ldt)], axis=2)
    b = jnp.stack([B_re, B_im], axis=1).reshape(depth, 2, k, gpc, n, c)
    bt = b.transpose(0, 2, 1, 5, 3, 4).reshape(depth, k, 2, c, CLUSTER_STATE)
    cc = jnp.stack([C_re, C_im], axis=1).reshape(depth, 2, k, gpc, c, n)
    ct = cc.transpose(0, 2, 1, 4, 3, 5).reshape(depth, k, 2, c, CLUSTER_STATE)
    dskip = jnp.tile(D_skip.reshape(depth, k, 1, LANES), (1, 1, 1, SSM_BLOCK))
    return par, bt, ct, dskip


def _mix_out_kernel(final, n_tiles, x_ref, g_ref, yb_ref, yc_ref, wglu_ref, bglu_ref, wout_ref,
                    gffn_ref, wg_ref, wu_ref, wd_ref, gfin_ref, o_ref, gs_ref, x1_ref, h_ref):
    tm = x_ref.shape[0]
    i = pl.program_id(0)

    def mix_tail(slot):
        for k in range(N_CLUSTERS):
            for s in range(SSM_BLOCK):
                gs_ref[k, pl.ds(s, tm // SSM_BLOCK, stride=SSM_BLOCK), :] = (
                    g_ref[k, :, s * LANES:(s + 1) * LANES])
        g = jnp.concatenate([gs_ref[k] for k in range(N_CLUSTERS)], axis=1)
        ya = g * jax.nn.sigmoid(_dot(g.astype(BF16), wglu_ref[...]) + bglu_ref[...])
        ymix = jnp.concatenate([ya.astype(BF16), yb_ref[...], yc_ref[...]], axis=1)
        x1 = x_ref[...] + _dot(ymix, wout_ref[...])
        x1_ref[slot] = x1
        h_ref[slot] = _rms(x1, gffn_ref[...]).astype(BF16)

    def swiglu(slot):
        h = h_ref[slot]
        acc = x1_ref[slot]
        for c0, c1 in FF_CHUNKS:
            gate = _dot(h, wg_ref[:, c0:c1])
            up = _dot(h, wu_ref[:, c0:c1])
            act = (gate * jax.nn.sigmoid(gate) * up).astype(BF16)
            acc = acc + _dot(act, wd_ref[c0:c1, :])
        if final:
            acc = _rms(acc, gfin_ref[...])
        o_ref[...] = acc

    @pl.when(i == 0)
    def _():
        mix_tail(0)

    for parity in range(2):
        @pl.when(jnp.logical_and(jnp.logical_and(i > 0, i < n_tiles), i % 2 == parity))
        def _():
            swiglu(1 - parity)
            mix_tail(parity)

    @pl.when(i == n_tiles)
    def _():
        swiglu((n_tiles - 1) % 2)


def _mix_out(layer, x2d, g3, yb, yc, wglu, bglu, wout, gffn, wg, wu, wd, gfin, final):
    m = x2d.shape[0]
    tm = TOKEN_TILE
    d_ff = wg.shape[-1]

    def spec(shape, resident=False):
        zeros = (0,) * len(shape)
        mode = dict(pipeline_mode=pl.Buffered(1)) if resident else {}
        return pl.BlockSpec((None,) + tuple(shape), lambda i: (layer,) + zeros, **mode)

    n_tiles = m // tm
    cur = lambda i: jnp.minimum(i, n_tiles - 1)
    done = lambda i: jnp.maximum(i - 1, 0)
    return pl.pallas_call(
        functools.partial(_mix_out_kernel, final, n_tiles),
        grid=(n_tiles + 1,),
        in_specs=[
            pl.BlockSpec((tm, D_MODEL), lambda i: (cur(i), 0)),
            pl.BlockSpec((N_CLUSTERS, tm // SSM_BLOCK, FLAT), lambda i: (0, cur(i), 0)),
            pl.BlockSpec((tm, D_POOL), lambda i: (cur(i), 0)),
            pl.BlockSpec((tm, D_SGU), lambda i: (cur(i), 0)),
            spec((D_SSM, D_SSM), resident=True),
            spec((1, D_SSM)),
            spec((D_MODEL, D_MODEL), resident=True),
            spec((1, D_MODEL)),
            spec((D_MODEL, d_ff), resident=True),
            spec((D_MODEL, d_ff), resident=True),
            spec((d_ff, D_MODEL), resident=True),
            pl.BlockSpec((1, D_MODEL), lambda i: (0, 0)),
        ],
        out_specs=pl.BlockSpec((tm, D_MODEL), lambda i: (done(i), 0)),
        out_shape=jax.ShapeDtypeStruct((m, D_MODEL), F32),
        scratch_shapes=[pltpu.VMEM((N_CLUSTERS, tm, LANES), F32),
                        pltpu.VMEM((2, tm, D_MODEL), F32),
                        pltpu.VMEM((2, tm, D_MODEL), BF16)],
        compiler_params=pltpu.CompilerParams(
            dimension_semantics=("arbitrary",), vmem_limit_bytes=VMEM_LIMIT),
        name="mix_out",
    )(x2d, g3, yb, yc, wglu, bglu, wout, gffn, wg, wu, wd, gfin)


def kernel(x, g_mix, w_in, A_re, A_im, log_dt, B_re, B_im, C_re, C_im, D_skip, w_glu, b_glu,
           w_pool, pool_scale, sgu_ln_g, sgu_ln_b, w_spatial, b_spatial, w_out, g_ffn,
           w_gate, w_up, w_down, g_final):
    bsz, seq, d = x.shape
    depth = w_in.shape[0]
    m = bsz * seq
    assert d == D_MODEL and seq % TOKEN_TILE == 0 and TOKEN_TILE % CHUNK == 0
    assert (seq // SSM_BLOCK) % S5_ROW_TILE == 0
    x2d = x.reshape(m, D_MODEL)
    row = lambda a: a[:, None, :]
    tril = jnp.tril(jnp.ones((CHUNK, CHUNK), dtype=bool))
    eye = jnp.eye(len(POOL_WINDOWS), dtype=F32)
    wpool_bd = jnp.einsum('dgij,gh->dgihj', w_pool, eye).reshape(depth, D_POOL, D_POOL).astype(BF16)
    ws = jnp.where(tril, w_spatial, 0.0).astype(BF16)
    bsp = jnp.repeat(jnp.swapaxes(b_spatial, 1, 2), SGU_HEAD_DIM, axis=2)
    par, bt, ct, dskip = _s5_params(A_re, A_im, log_dt, B_re, B_im, C_re, C_im, D_skip)
    a_end, b_end, u_end = D_SSM, D_SSM + D_POOL, D_SSM + D_POOL + D_SGU
    win = jnp.concatenate([w_in[..., u_end:], w_in[..., b_end:u_end], w_in[..., :b_end]],
                          axis=-1).astype(BF16)
    wout, wglu = w_out.astype(BF16), w_glu.astype(BF16)
    wg, wu, wd = w_gate.astype(BF16), w_up.astype(BF16), w_down.astype(BF16)
    for l in range(depth):
        u_flat, yb, yc = _mix_in(l, x2d, row(g_mix), win, wpool_bd, row(pool_scale),
                                 row(sgu_ln_g), row(sgu_ln_b), ws, bsp, seq)
        g3 = _s5(l, u_flat, par, bt, ct, dskip, seq // SSM_BLOCK)
        x2d = _mix_out(l, x2d, g3, yb, yc, wglu, row(b_glu), wout, row(g_ffn), wg, wu, wd,
                       g_final[None], final=(l == depth - 1))
    return x2d.reshape(bsz, seq, D_MODEL)
```

```python
import functools
import math

import jax
import jax.numpy as jnp
from jax import lax
from jax.experimental import pallas as pl
from jax.experimental.pallas import tpu as pltpu

D_MODEL = 1024
D_SSM = 384
SSM_GROUP = 16
N_SSM_GROUPS = D_SSM // SSM_GROUP
SSM_STATE = 64
POOL_WINDOWS = (2, 4, 8, 16)
POOL_GROUP = 64
D_POOL = len(POOL_WINDOWS) * POOL_GROUP
MAX_WINDOW = max(POOL_WINDOWS)
SGU_HEADS = 6
SGU_HEAD_DIM = 64
D_SGU = SGU_HEADS * SGU_HEAD_DIM
CHUNK = 128
D_IN = D_SSM + D_POOL + 2 * D_SGU
EPS = 1e-6

LANES = 128
SSM_BLOCK = 16
N_CLUSTERS = D_SSM // LANES
GROUPS_PER_CLUSTER = LANES // SSM_GROUP
CLUSTER_STATE = GROUPS_PER_CLUSTER * SSM_STATE
FLAT = SSM_BLOCK * LANES

MXU_WIDTH = 256
TOKEN_TILE = 512
S5_ROW_TILE = 512
SUBLANES = 8
SCAN_SEGS = SUBLANES
SCAN_SEG_LEN = S5_ROW_TILE // SCAN_SEGS
SCAN_SEG_PITCH = SCAN_SEG_LEN + SUBLANES
SCAN_TILES = 2 * CLUSTER_STATE // LANES
SCAN_UNROLL = 8
FF_CHUNKS = ((0, 1024), (1024, 2048), (2048, 2816))
VMEM_LIMIT = 56 * 1024 * 1024

F32 = jnp.float32
BF16 = jnp.bfloat16
NT_DIMS = (((1,), (1,)), ((), ()))


def _gelu(x):
    c = math.sqrt(2.0 / math.pi)
    return 0.5 * x * (1.0 + jnp.tanh(c * (x + 0.044715 * (x * x * x))))


def _rms(x, g):
    ms = jnp.mean(x * x, axis=-1, keepdims=True)
    return x * lax.rsqrt(ms + EPS) * g


def _dot(a, b):
    return jnp.dot(a, b, preferred_element_type=F32)


def _dot_nt(a, b):
    return lax.dot_general(a, b, NT_DIMS, preferred_element_type=F32)


def _mix_in_kernel(tiles_per_seq, x_ref, gmix_ref, win_ref, wpool_ref, pscale_ref,
                   lng_ref, lnb_ref, ws_ref, bsp_ref, za_ref, yb_ref, yc_ref, halo_ref, zs_ref):
    tm = x_ref.shape[0]
    i = pl.program_id(0)
    seq_tile = i % tiles_per_seq

    h = _rms(x_ref[...], gmix_ref[...]).astype(BF16)
    z_sgu = _dot(h, win_ref[:, :2 * D_SGU])
    z_rest = _dot(h, win_ref[:, 2 * D_SGU:])
    zv = z_sgu[:, :D_SGU]
    zu = z_sgu[:, D_SGU:]
    zb = z_rest[:, D_SSM:]

    for k in range(N_CLUSTERS):
        zs_ref[k] = z_rest[:, k * LANES:(k + 1) * LANES]
    for k in range(N_CLUSTERS):
        for s in range(SSM_BLOCK):
            piece = zs_ref[k, pl.ds(s, tm // SSM_BLOCK, stride=SSM_BLOCK), :]
            za_ref[k, :, s * LANES:(s + 1) * LANES] = piece.astype(BF16)

    halo = jnp.where(seq_tile == 0, 0.0, halo_ref[...])
    halo_ref[...] = zb[tm - MAX_WINDOW:, :]
    ext = jnp.concatenate([halo, zb], axis=0)
    lane = lax.broadcasted_iota(jnp.int32, (tm, LANES), 1)
    low = lane < POOL_GROUP
    pos1 = (seq_tile * tm + 1 + lax.broadcasted_iota(jnp.int32, (tm, LANES), 0)).astype(F32)

    e0 = ext[:, :LANES]
    s2 = e0 + pltpu.roll(e0, 1, 0)
    s4 = s2 + pltpu.roll(s2, 2, 0)
    sum0 = jnp.where(low, s2[MAX_WINDOW:], s4[MAX_WINDOW:])
    cnt0 = jnp.where(low, jnp.minimum(pos1, 2.0), jnp.minimum(pos1, 4.0))
    e1 = ext[:, LANES:]
    t2 = e1 + pltpu.roll(e1, 1, 0)
    t4 = t2 + pltpu.roll(t2, 2, 0)
    t8 = t4 + pltpu.roll(t4, 4, 0)
    t16 = t8 + pltpu.roll(t8, 8, 0)
    sum1 = jnp.where(low, t8[MAX_WINDOW:], t16[MAX_WINDOW:])
    cnt1 = jnp.where(low, jnp.minimum(pos1, 8.0), jnp.minimum(pos1, 16.0))
    pooled = jnp.concatenate([sum0 / cnt0, sum1 / cnt1], axis=1) - zb
    yb = _dot(pooled.astype(BF16), wpool_ref[...]) * pscale_ref[...]
    yb_ref[...] = yb.astype(BF16)

    u = _gelu(zu)
    v = _gelu(zv)
    mu = jnp.mean(v, axis=-1, keepdims=True)
    vc = v - mu
    var = jnp.mean(vc * vc, axis=-1, keepdims=True)
    vn = (vc * lax.rsqrt(var + EPS) * lng_ref[...] + lnb_ref[...]).astype(BF16)
    lane_c = lax.broadcasted_iota(jnp.int32, (CHUNK, LANES), 1)
    low_c = lane_c < SGU_HEAD_DIM
    zero = jnp.zeros((), BF16)
    for c in range(tm // CHUNK):
        rows = slice(c * CHUNK, (c + 1) * CHUNK)
        parts = []
        for p in range(SGU_HEADS // 2):
            vp = vn[rows, p * LANES:(p + 1) * LANES]
            lo = jnp.where(low_c, vp, zero)
            hi = jnp.where(low_c, zero, vp)
            parts.append(_dot(ws_ref[2 * p], lo) + _dot(ws_ref[2 * p + 1], hi))
        mixed = jnp.concatenate(parts, axis=1) + bsp_ref[...]
        yc_ref[rows, :] = (u[rows, :] * mixed).astype(BF16)


def _layer_spec(layer, shape):
    zeros = (0,) * len(shape)
    return pl.BlockSpec((None,) + tuple(shape), lambda *_: (layer,) + zeros)


def _mix_in(layer, x2d, gmix, win, wpool_bd, pscale, lng, lnb, ws, bsp, seq_len):
    m = x2d.shape[0]
    tm = TOKEN_TILE
    spec = functools.partial(_layer_spec, layer)
    return pl.pallas_call(
        functools.partial(_mix_in_kernel, seq_len // tm),
        grid=(m // tm,),
        in_specs=[
            pl.BlockSpec((tm, D_MODEL), lambda i: (i, 0)),
            spec((1, D_MODEL)),
            spec((D_MODEL, D_IN)),
            spec((D_POOL, D_POOL)),
            spec((1, D_POOL)),
            spec((1, D_SGU)),
            spec((1, D_SGU)),
            spec((SGU_HEADS, CHUNK, CHUNK)),
            spec((CHUNK, D_SGU)),
        ],
        out_specs=[
            pl.BlockSpec((N_CLUSTERS, tm // SSM_BLOCK, FLAT), lambda i: (0, i, 0)),
            pl.BlockSpec((tm, D_POOL), lambda i: (i, 0)),
            pl.BlockSpec((tm, D_SGU), lambda i: (i, 0)),
        ],
        out_shape=[
            jax.ShapeDtypeStruct((N_CLUSTERS, m // SSM_BLOCK, FLAT), BF16),
            jax.ShapeDtypeStruct((m, D_POOL), BF16),
            jax.ShapeDtypeStruct((m, D_SGU), BF16),
        ],
        scratch_shapes=[pltpu.VMEM((MAX_WINDOW, D_POOL), F32),
                        pltpu.VMEM((N_CLUSTERS, tm, LANES), F32)],
        compiler_params=pltpu.CompilerParams(
            dimension_semantics=("arbitrary",), vmem_limit_bytes=VMEM_LIMIT),
        name="mix_in",
    )(x2d, gmix, win, wpool_bd, pscale, lng, lnb, ws, bsp)


def _s5_build_operators(par_ref, bt_ref, ct_ref, wend_ref, wrd_ref, lagk_ref, pw_ref):
    a_re = par_ref[0, 0:1, :]
    a_im = par_ref[0, 1:2, :]
    dt = jnp.exp(par_ref[0, 2:3, :])
    l_re = a_re * dt
    l_im = a_im * dt
    mag = jnp.exp(l_re)
    ar = mag * jnp.cos(l_im)
    ai = mag * jnp.sin(l_im)
    den = a_re * a_re + a_im * a_im
    f_re = ((ar - 1.0) * a_re + ai * a_im) / den
    f_im = (ai * a_re - (ar - 1.0) * a_im) / den
    bt_re, bt_im = bt_ref[0, 0], bt_ref[0, 1]
    bb_re = f_re * bt_re - f_im * bt_im
    bb_im = f_re * bt_im + f_im * bt_re
    ct_re, ct_im = ct_ref[0, 0], ct_ref[0, 1]

    row_g = lax.broadcasted_iota(jnp.int32, (LANES, CLUSTER_STATE), 0) // SSM_GROUP
    col_g = lax.broadcasted_iota(jnp.int32, (LANES, CLUSTER_STATE), 1) // SSM_STATE
    same_group = row_g == col_g

    def power(l):
        m_l = jnp.exp(l * l_re)
        return m_l * jnp.cos(l * l_im), m_l * jnp.sin(l * l_im)

    def spread(v):
        return jnp.where(same_group, jnp.tile(v, (GROUPS_PER_CLUSTER, 1)), 0.0).astype(BF16)

    for s in range(SSM_BLOCK):
        p_re, p_im = power(float(SSM_BLOCK - 1 - s))
        rows = slice(s * LANES, (s + 1) * LANES)
        wend_ref[rows, :CLUSTER_STATE] = spread(p_re * bb_re - p_im * bb_im)
        wend_ref[rows, CLUSTER_STATE:] = spread(p_re * bb_im + p_im * bb_re)
        q_re, q_im = power(float(s + 1))
        wrd_ref[rows, :CLUSTER_STATE] = spread(q_re * ct_re - q_im * ct_im)
        wrd_ref[rows, CLUSTER_STATE:] = spread(-(q_re * ct_im + q_im * ct_re))

    wrd0 = jnp.concatenate([spread(ct_re), spread(-ct_im)], axis=1)
    lag_all = _dot_nt(wend_ref[...], wrd0).astype(BF16)
    for t in range(SSM_BLOCK):
        cols = slice(t * LANES, (t + 1) * LANES)
        top = (t + 1) * LANES
        lagk_ref[:top, cols] = lag_all[FLAT - top:, :]
        if top < FLAT:
            lagk_ref[top:, cols] = jnp.zeros((FLAT - top, LANES), BF16)

    p_re, p_im = power(float(SSM_BLOCK))
    pw_ref[0:1, :] = jnp.concatenate([p_re, p_im], axis=1)
    for _ in range(SCAN_SEG_LEN.bit_length() - 1):
        p_re, p_im = p_re * p_re - p_im * p_im, 2.0 * (p_re * p_im)
    pw_ref[1:2, :] = jnp.concatenate([p_re, p_im], axis=1)


def _cmul_add(p_re, p_im, h_re, h_im, e_re, e_im):
    n_re = [pr * hr - pi * hi + er for pr, pi, hr, hi, er in zip(p_re, p_im, h_re, h_im, e_re)]
    n_im = [pr * hi + pi * hr + ei for pr, pi, hr, hi, ei in zip(p_re, p_im, h_re, h_im, e_im)]
    return n_re, n_im


def _s5_block_recurrence(st_ref, pw_ref):
    half = SCAN_TILES // 2

    def bcast(row):
        return [jnp.broadcast_to(row[:, q * LANES:(q + 1) * LANES], (SCAN_SEGS, LANES))
                for q in range(half)]

    p_re, p_im = bcast(pw_ref[0:1, :CLUSTER_STATE]), bcast(pw_ref[0:1, CLUSTER_STATE:])
    seg_rows = lambda i: pl.ds(i, SCAN_SEGS, stride=SCAN_SEG_PITCH)

    def gather(i):
        e = [st_ref[q, seg_rows(i), :] for q in range(SCAN_TILES)]
        return e[:half], e[half:]

    zeros = [jnp.zeros((SCAN_SEGS, LANES), F32)] * half

    def end_step(i, h):
        e_re, e_im = gather(i)
        return _cmul_add(p_re, p_im, h[0], h[1], e_re, e_im)

    l_re, l_im = lax.fori_loop(0, SCAN_SEG_LEN, end_step, (zeros, zeros), unroll=SCAN_UNROLL)

    s_re, s_im = bcast(pw_ref[1:2, :CLUSTER_STATE]), bcast(pw_ref[1:2, CLUSTER_STATE:])
    first = lax.broadcasted_iota(jnp.int32, (SCAN_SEGS, LANES), 0) == 0
    shift = lambda v: jnp.where(first, 0.0, pltpu.roll(v, 1, 0))
    c_re, c_im = zeros, zeros
    for _ in range(SCAN_SEGS - 1):
        n_re, n_im = _cmul_add(s_re, s_im, c_re, c_im, l_re, l_im)
        c_re, c_im = [shift(v) for v in n_re], [shift(v) for v in n_im]

    def scan_step(i, h):
        e_re, e_im = gather(i)
        for q in range(half):
            st_ref[q, seg_rows(i), :] = h[0][q]
            st_ref[half + q, seg_rows(i), :] = h[1][q]
        return _cmul_add(p_re, p_im, h[0], h[1], e_re, e_im)

    lax.fori_loop(0, SCAN_SEG_LEN, scan_step, (c_re, c_im), unroll=SCAN_UNROLL)


def _s5_kernel(u_ref, par_ref, bt_ref, ct_ref, dskip_ref, g_ref,
               wend_ref, wrd_ref, lagk_ref, pw_ref, st_ref):
    r = pl.program_id(1)

    @pl.when(r == 0)
    def _():
        _s5_build_operators(par_ref, bt_ref, ct_ref, wend_ref, wrd_ref, lagk_ref, pw_ref)

    u = u_ref[0]
    ys = []
    for j in range(FLAT // MXU_WIDTH):
        kdim = (j + 1) * MXU_WIDTH
        ys.append(_dot(u[:, :kdim], lagk_ref[:kdim, j * MXU_WIDTH:(j + 1) * MXU_WIDTH]))
    y = jnp.concatenate(ys, axis=1)

    e = _dot(u, wend_ref[...])
    for q in range(SCAN_TILES):
        for j in range(SCAN_SEGS):
            st_ref[q, j * SCAN_SEG_PITCH:j * SCAN_SEG_PITCH + SCAN_SEG_LEN, :] = (
                e[j * SCAN_SEG_LEN:(j + 1) * SCAN_SEG_LEN, q * LANES:(q + 1) * LANES])
    _s5_block_recurrence(st_ref, pw_ref)
    hprev = jnp.concatenate(
        [jnp.concatenate([st_ref[q, j * SCAN_SEG_PITCH:j * SCAN_SEG_PITCH + SCAN_SEG_LEN, :]
                          for q in range(SCAN_TILES)], axis=1)
         for j in range(SCAN_SEGS)], axis=0)

    y = y + _dot_nt(hprev.astype(BF16), wrd_ref[...])
    y = y + dskip_ref[0] * u.astype(F32)
    g_ref[0] = _gelu(y)


def _s5(layer, u_flat, par, bt, ct, dskip, blocks_per_seq):
    nb = u_flat.shape[1]
    rt = S5_ROW_TILE
    assert blocks_per_seq == rt and SCAN_SEG_LEN & (SCAN_SEG_LEN - 1) == 0
    return pl.pallas_call(
        _s5_kernel,
        grid=(N_CLUSTERS, nb // rt),
        in_specs=[
            pl.BlockSpec((1, rt, FLAT), lambda k, r: (k, r, 0)),
            pl.BlockSpec((None, 1, 3, CLUSTER_STATE), lambda k, r: (layer, k, 0, 0)),
            pl.BlockSpec((None, 1, 2, SSM_GROUP, CLUSTER_STATE), lambda k, r: (layer, k, 0, 0, 0)),
            pl.BlockSpec((None, 1, 2, SSM_GROUP, CLUSTER_STATE), lambda k, r: (layer, k, 0, 0, 0)),
            pl.BlockSpec((None, 1, 1, FLAT), lambda k, r: (layer, k, 0, 0)),
        ],
        out_specs=pl.BlockSpec((1, rt, FLAT), lambda k, r: (k, r, 0)),
        out_shape=jax.ShapeDtypeStruct((N_CLUSTERS, nb, FLAT), F32),
        scratch_shapes=[pltpu.VMEM((FLAT, 2 * CLUSTER_STATE), BF16),
                        pltpu.VMEM((FLAT, 2 * CLUSTER_STATE), BF16),
                        pltpu.VMEM((FLAT, FLAT), BF16),
                        pltpu.VMEM((2, 2 * CLUSTER_STATE), F32),
                        pltpu.VMEM((SCAN_TILES, SCAN_SEGS * SCAN_SEG_PITCH, LANES), F32)],
        compiler_params=pltpu.CompilerParams(
            dimension_semantics=("arbitrary", "arbitrary"), vmem_limit_bytes=VMEM_LIMIT),
        name="s5",
    )(u_flat, par, bt, ct, dskip)


def _s5_params(A_re, A_im, log_dt, B_re, B_im, C_re, C_im, D_skip):
    depth = A_re.shape[0]
    k, gpc, n, c = N_CLUSTERS, GROUPS_PER_CLUSTER, SSM_STATE, SSM_GROUP
    rows = lambda a: a.reshape(depth, k, 1, CLUSTER_STATE)
    ldt = jnp.broadcast_to(log_dt[..., None], A_re.shape)
    par = jnp.concatenate([rows(A_re), rows(A_im), rows(ldt)], axis=2)
    b = jnp.stack([B_re, B_im], axis=1).reshape(depth, 2, k, gpc, n, c)
    bt = b.transpose(0, 2, 1, 5, 3, 4).reshape(depth, k, 2, c, CLUSTER_STATE)
    cc = jnp.stack([C_re, C_im], axis=1).reshape(depth, 2, k, gpc, c, n)
    ct = cc.transpose(0, 2, 1, 4, 3, 5).reshape(depth, k, 2, c, CLUSTER_STATE)
    dskip = jnp.tile(D_skip.reshape(depth, k, 1, LANES), (1, 1, 1, SSM_BLOCK))
    return par, bt, ct, dskip


def _mix_out_kernel(final, x_ref, g_ref, yb_ref, yc_ref, wglu_ref, bglu_ref, wout_ref,
                    gffn_ref, wg_ref, wu_ref, wd_ref, gfin_ref, o_ref, gs_ref):
    tm = x_ref.shape[0]
    for k in range(N_CLUSTERS):
        for s in range(SSM_BLOCK):
            gs_ref[k, pl.ds(s, tm // SSM_BLOCK, stride=SSM_BLOCK), :] = (
                g_ref[k, :, s * LANES:(s + 1) * LANES])
    g = jnp.concatenate([gs_ref[k] for k in range(N_CLUSTERS)], axis=1)
    ya = g * jax.nn.sigmoid(_dot(g.astype(BF16), wglu_ref[...]) + bglu_ref[...])
    ymix = jnp.concatenate([ya.astype(BF16), yb_ref[...], yc_ref[...]], axis=1)
    x1 = x_ref[...] + _dot(ymix, wout_ref[...])
    h = _rms(x1, gffn_ref[...]).astype(BF16)
    acc = x1
    for c0, c1 in FF_CHUNKS:
        gate = _dot(h, wg_ref[:, c0:c1])
        up = _dot(h, wu_ref[:, c0:c1])
        act = (gate * jax.nn.sigmoid(gate) * up).astype(BF16)
        acc = acc + _dot(act, wd_ref[c0:c1, :])
    if final:
        acc = _rms(acc, gfin_ref[...])
    o_ref[...] = acc


def _mix_out(layer, x2d, g3, yb, yc, wglu, bglu, wout, gffn, wg, wu, wd, gfin, final):
    m = x2d.shape[0]
    tm = TOKEN_TILE
    d_ff = wg.shape[-1]

    def spec(shape, resident=False):
        zeros = (0,) * len(shape)
        mode = dict(pipeline_mode=pl.Buffered(1)) if resident else {}
        return pl.BlockSpec((None,) + tuple(shape), lambda i: (layer,) + zeros, **mode)

    return pl.pallas_call(
        functools.partial(_mix_out_kernel, final),
        grid=(m // tm,),
        in_specs=[
            pl.BlockSpec((tm, D_MODEL), lambda i: (i, 0)),
            pl.BlockSpec((N_CLUSTERS, tm // SSM_BLOCK, FLAT), lambda i: (0, i, 0)),
            pl.BlockSpec((tm, D_POOL), lambda i: (i, 0)),
            pl.BlockSpec((tm, D_SGU), lambda i: (i, 0)),
            spec((D_SSM, D_SSM), resident=True),
            spec((1, D_SSM)),
            spec((D_MODEL, D_MODEL), resident=True),
            spec((1, D_MODEL)),
            spec((D_MODEL, d_ff), resident=True),
            spec((D_MODEL, d_ff), resident=True),
            spec((d_ff, D_MODEL), resident=True),
            pl.BlockSpec((1, D_MODEL), lambda i: (0, 0)),
        ],
        out_specs=pl.BlockSpec((tm, D_MODEL), lambda i: (i, 0)),
        out_shape=jax.ShapeDtypeStruct((m, D_MODEL), F32),
        scratch_shapes=[pltpu.VMEM((N_CLUSTERS, tm, LANES), F32)],
        compiler_params=pltpu.CompilerParams(
            dimension_semantics=("arbitrary",), vmem_limit_bytes=VMEM_LIMIT),
        name="mix_out",
    )(x2d, g3, yb, yc, wglu, bglu, wout, gffn, wg, wu, wd, gfin)


def kernel(x, g_mix, w_in, A_re, A_im, log_dt, B_re, B_im, C_re, C_im, D_skip, w_glu, b_glu,
           w_pool, pool_scale, sgu_ln_g, sgu_ln_b, w_spatial, b_spatial, w_out, g_ffn,
           w_gate, w_up, w_down, g_final):
    bsz, seq, d = x.shape
    depth = w_in.shape[0]
    m = bsz * seq
    assert d == D_MODEL and seq % TOKEN_TILE == 0 and TOKEN_TILE % CHUNK == 0
    assert (seq // SSM_BLOCK) % S5_ROW_TILE == 0
    x2d = x.reshape(m, D_MODEL)
    row = lambda a: a[:, None, :]
    tril = jnp.tril(jnp.ones((CHUNK, CHUNK), dtype=bool))
    eye = jnp.eye(len(POOL_WINDOWS), dtype=F32)
    wpool_bd = jnp.einsum('dgij,gh->dgihj', w_pool, eye).reshape(depth, D_POOL, D_POOL).astype(BF16)
    ws = jnp.where(tril, w_spatial, 0.0).astype(BF16)
    bsp = jnp.repeat(jnp.swapaxes(b_spatial, 1, 2), SGU_HEAD_DIM, axis=2)
    par, bt, ct, dskip = _s5_params(A_re, A_im, log_dt, B_re, B_im, C_re, C_im, D_skip)
    a_end, b_end, u_end = D_SSM, D_SSM + D_POOL, D_SSM + D_POOL + D_SGU
    win = jnp.concatenate([w_in[..., u_end:], w_in[..., b_end:u_end], w_in[..., :b_end]],
                          axis=-1).astype(BF16)
    wout, wglu = w_out.astype(BF16), w_glu.astype(BF16)
    wg, wu, wd = w_gate.astype(BF16), w_up.astype(BF16), w_down.astype(BF16)
    for l in range(depth):
        u_flat, yb, yc = _mix_in(l, x2d, row(g_mix), win, wpool_bd, row(pool_scale),
                                 row(sgu_ln_g), row(sgu_ln_b), ws, bsp, seq)
        g3 = _s5(l, u_flat, par, bt, ct, dskip, seq // SSM_BLOCK)
        x2d = _mix_out(l, x2d, g3, yb, yc, wglu, row(b_glu), wout, row(g_ffn), wg, wu, wd,
                       g_final[None], final=(l == depth - 1))
    return x2d.reshape(bsz, seq, D_MODEL)
```

```python
import functools
import math

import jax
import jax.numpy as jnp
from jax import lax
from jax.experimental import pallas as pl
from jax.experimental.pallas import tpu as pltpu

D_MODEL = 1024
D_SSM = 384
SSM_GROUP = 16
N_SSM_GROUPS = D_SSM // SSM_GROUP
SSM_STATE = 64
POOL_WINDOWS = (2, 4, 8, 16)
POOL_GROUP = 64
D_POOL = len(POOL_WINDOWS) * POOL_GROUP
MAX_WINDOW = max(POOL_WINDOWS)
SGU_HEADS = 6
SGU_HEAD_DIM = 64
D_SGU = SGU_HEADS * SGU_HEAD_DIM
CHUNK = 128
D_IN = D_SSM + D_POOL + 2 * D_SGU
EPS = 1e-6

LANES = 128
SSM_BLOCK = 16
N_CLUSTERS = D_SSM // LANES
GROUPS_PER_CLUSTER = LANES // SSM_GROUP
CLUSTER_STATE = GROUPS_PER_CLUSTER * SSM_STATE
FLAT = SSM_BLOCK * LANES

MXU_WIDTH = 256
BF16_SUBLANES = 16
VEC_ROWS = 8
VEC_GMIX, VEC_GFFN, VEC_SGU, VEC_GLU, VEC_GFINAL = 0, 1, 2, 3, 4
TOKEN_TILE = 512
S5_ROW_TILE = 512
SUBLANES = 8
SCAN_SEGS = SUBLANES
SCAN_SEG_LEN = S5_ROW_TILE // SCAN_SEGS
SCAN_SEG_PITCH = SCAN_SEG_LEN + SUBLANES
SCAN_TILES = 2 * CLUSTER_STATE // LANES
SCAN_UNROLL = 8
FF_CHUNKS = ((0, 1024), (1024, 2048), (2048, 2816))
VMEM_LIMIT = 56 * 1024 * 1024

F32 = jnp.float32
BF16 = jnp.bfloat16
NT_DIMS = (((1,), (1,)), ((), ()))


def _gelu(x):
    c = math.sqrt(2.0 / math.pi)
    return 0.5 * x * (1.0 + jnp.tanh(c * (x + 0.044715 * (x * x * x))))


def _rms(x, g):
    ms = jnp.mean(x * x, axis=-1, keepdims=True)
    return x * lax.rsqrt(ms + EPS) * g


def _dot(a, b):
    return jnp.dot(a, b, preferred_element_type=F32)


def _dot_nt(a, b):
    return lax.dot_general(a, b, NT_DIMS, preferred_element_type=F32)


def _cast_riders(kinds, src_refs, dst_refs):
    for kind, src, dst in zip(kinds, src_refs, dst_refs):
        if kind == "w_in":
            b_end, u_end = D_SSM + D_POOL, D_SSM + D_POOL + D_SGU
            dst[:, :D_SGU] = src[:, u_end:].astype(BF16)
            dst[:, D_SGU:2 * D_SGU] = src[:, b_end:u_end].astype(BF16)
            dst[:, 2 * D_SGU:] = src[:, :b_end].astype(BF16)
        else:
            dst[...] = src[...].astype(BF16)


def _rider_specs(n_steps, layer, w):
    rows, cols = w.shape[1:]
    share = 1
    while (rows * share) % n_steps or (rows * share // n_steps) % BF16_SUBLANES:
        share *= 2
    slab = rows * share // n_steps
    src = pl.BlockSpec((None, slab, cols), lambda i: (layer, i // share, 0))
    dst = pl.BlockSpec((slab, cols), lambda i: (i // share, 0))
    return src, dst, jax.ShapeDtypeStruct((rows, cols), BF16)


def _mix_in_kernel(tiles_per_seq, rider_kinds, *refs):
    n_r = len(rider_kinds)
    x_ref, vec_ref, win_ref, wpool_ref, ws_ref, bsp_ref = refs[:6]
    rider_src = refs[6:6 + n_r]
    za_ref, yb_ref, yc_ref = refs[6 + n_r:9 + n_r]
    rider_dst = refs[9 + n_r:9 + 2 * n_r]
    halo_ref, zs_ref = refs[9 + 2 * n_r:]
    _cast_riders(rider_kinds, rider_src, rider_dst)

    tm = x_ref.shape[0]
    i = pl.program_id(0)
    seq_tile = i % tiles_per_seq
    gmix = vec_ref[VEC_GMIX:VEC_GMIX + 1, :]
    pscale = vec_ref[VEC_SGU:VEC_SGU + 1, :D_POOL]
    lng = vec_ref[VEC_SGU:VEC_SGU + 1, D_POOL:D_POOL + D_SGU]
    lnb = vec_ref[VEC_SGU:VEC_SGU + 1, D_POOL + D_SGU:]

    h = _rms(x_ref[...], gmix).astype(BF16)
    z_sgu = _dot(h, win_ref[:, :2 * D_SGU])
    z_rest = _dot(h, win_ref[:, 2 * D_SGU:])
    zv = z_sgu[:, :D_SGU]
    zu = z_sgu[:, D_SGU:]
    zb = z_rest[:, D_SSM:]

    for k in range(N_CLUSTERS):
        zs_ref[k] = z_rest[:, k * LANES:(k + 1) * LANES]
    for k in range(N_CLUSTERS):
        for s in range(SSM_BLOCK):
            piece = zs_ref[k, pl.ds(s, tm // SSM_BLOCK, stride=SSM_BLOCK), :]
            za_ref[k, :, s * LANES:(s + 1) * LANES] = piece.astype(BF16)

    halo = jnp.where(seq_tile == 0, 0.0, halo_ref[...])
    halo_ref[...] = zb[tm - MAX_WINDOW:, :]
    ext = jnp.concatenate([halo, zb], axis=0)
    lane = lax.broadcasted_iota(jnp.int32, (tm, LANES), 1)
    low = lane < POOL_GROUP
    pos1 = (seq_tile * tm + 1 + lax.broadcasted_iota(jnp.int32, (tm, LANES), 0)).astype(F32)

    e0 = ext[:, :LANES]
    s2 = e0 + pltpu.roll(e0, 1, 0)
    s4 = s2 + pltpu.roll(s2, 2, 0)
    sum0 = jnp.where(low, s2[MAX_WINDOW:], s4[MAX_WINDOW:])
    cnt0 = jnp.where(low, jnp.minimum(pos1, 2.0), jnp.minimum(pos1, 4.0))
    e1 = ext[:, LANES:]
    t2 = e1 + pltpu.roll(e1, 1, 0)
    t4 = t2 + pltpu.roll(t2, 2, 0)
    t8 = t4 + pltpu.roll(t4, 4, 0)
    t16 = t8 + pltpu.roll(t8, 8, 0)
    sum1 = jnp.where(low, t8[MAX_WINDOW:], t16[MAX_WINDOW:])
    cnt1 = jnp.where(low, jnp.minimum(pos1, 8.0), jnp.minimum(pos1, 16.0))
    pooled = jnp.concatenate([sum0 / cnt0, sum1 / cnt1], axis=1) - zb
    yb = _dot(pooled.astype(BF16), wpool_ref[...]) * pscale
    yb_ref[...] = yb.astype(BF16)

    u = _gelu(zu)
    v = _gelu(zv)
    mu = jnp.mean(v, axis=-1, keepdims=True)
    vc = v - mu
    var = jnp.mean(vc * vc, axis=-1, keepdims=True)
    vn = (vc * lax.rsqrt(var + EPS) * lng + lnb).astype(BF16)
    lane_c = lax.broadcasted_iota(jnp.int32, (CHUNK, LANES), 1)
    low_c = lane_c < SGU_HEAD_DIM
    zero = jnp.zeros((), BF16)
    for c in range(tm // CHUNK):
        rows = slice(c * CHUNK, (c + 1) * CHUNK)
        parts = []
        for p in range(SGU_HEADS // 2):
            vp = vn[rows, p * LANES:(p + 1) * LANES]
            lo = jnp.where(low_c, vp, zero)
            hi = jnp.where(low_c, zero, vp)
            parts.append(_dot(ws_ref[2 * p], lo) + _dot(ws_ref[2 * p + 1], hi))
        mixed = jnp.concatenate(parts, axis=1) + bsp_ref[...]
        yc_ref[rows, :] = (u[rows, :] * mixed).astype(BF16)


def _layer_spec(layer, shape):
    zeros = (0,) * len(shape)
    return pl.BlockSpec((None,) + tuple(shape), lambda *_: (layer,) + zeros)


def _mix_in(layer, x2d, vecs, win, wpool_bd, ws, bsp, seq_len, riders=()):
    m = x2d.shape[0]
    tm = TOKEN_TILE
    n_steps = m // tm
    spec = functools.partial(_layer_spec, layer)
    r_specs = [_rider_specs(n_steps, lyr, w) for _, w, lyr in riders]
    return pl.pallas_call(
        functools.partial(_mix_in_kernel, seq_len // tm, tuple(k for k, _, _ in riders)),
        grid=(n_steps,),
        in_specs=[
            pl.BlockSpec((tm, D_MODEL), lambda i: (i, 0)),
            spec((VEC_ROWS, D_MODEL)),
            pl.BlockSpec((D_MODEL, D_IN), lambda i: (0, 0)),
            spec((D_POOL, D_POOL)),
            spec((SGU_HEADS, CHUNK, CHUNK)),
            spec((CHUNK, D_SGU)),
        ] + [s for s, _, _ in r_specs],
        out_specs=[
            pl.BlockSpec((N_CLUSTERS, tm // SSM_BLOCK, FLAT), lambda i: (0, i, 0)),
            pl.BlockSpec((tm, D_POOL), lambda i: (i, 0)),
            pl.BlockSpec((tm, D_SGU), lambda i: (i, 0)),
        ] + [d for _, d, _ in r_specs],
        out_shape=[
            jax.ShapeDtypeStruct((N_CLUSTERS, m // SSM_BLOCK, FLAT), BF16),
            jax.ShapeDtypeStruct((m, D_POOL), BF16),
            jax.ShapeDtypeStruct((m, D_SGU), BF16),
        ] + [o for _, _, o in r_specs],
        scratch_shapes=[pltpu.VMEM((MAX_WINDOW, D_POOL), F32),
                        pltpu.VMEM((N_CLUSTERS, tm, LANES), F32)],
        compiler_params=pltpu.CompilerParams(
            dimension_semantics=("arbitrary",), vmem_limit_bytes=VMEM_LIMIT),
        name="mix_in",
    )(x2d, vecs, win, wpool_bd, ws, bsp, *[w for _, w, _ in riders])


def _s5_build_operators(par_ref, bt_ref, ct_ref, wend_ref, wrd_ref, lagk_ref, pw_ref):
    a_re = par_ref[0, 0:1, :]
    a_im = par_ref[0, 1:2, :]
    dt = jnp.exp(par_ref[0, 2:3, :])
    l_re = a_re * dt
    l_im = a_im * dt
    mag = jnp.exp(l_re)
    ar = mag * jnp.cos(l_im)
    ai = mag * jnp.sin(l_im)
    den = a_re * a_re + a_im * a_im
    f_re = ((ar - 1.0) * a_re + ai * a_im) / den
    f_im = (ai * a_re - (ar - 1.0) * a_im) / den
    bt_re, bt_im = bt_ref[0, 0], bt_ref[0, 1]
    bb_re = f_re * bt_re - f_im * bt_im
    bb_im = f_re * bt_im + f_im * bt_re
    ct_re, ct_im = ct_ref[0, 0], ct_ref[0, 1]

    row_g = lax.broadcasted_iota(jnp.int32, (LANES, CLUSTER_STATE), 0) // SSM_GROUP
    col_g = lax.broadcasted_iota(jnp.int32, (LANES, CLUSTER_STATE), 1) // SSM_STATE
    same_group = row_g == col_g

    def power(l):
        m_l = jnp.exp(l * l_re)
        return m_l * jnp.cos(l * l_im), m_l * jnp.sin(l * l_im)

    def spread(v):
        return jnp.where(same_group, jnp.tile(v, (GROUPS_PER_CLUSTER, 1)), 0.0).astype(BF16)

    for s in range(SSM_BLOCK):
        p_re, p_im = power(float(SSM_BLOCK - 1 - s))
        rows = slice(s * LANES, (s + 1) * LANES)
        wend_ref[rows, :CLUSTER_STATE] = spread(p_re * bb_re - p_im * bb_im)
        wend_ref[rows, CLUSTER_STATE:] = spread(p_re * bb_im + p_im * bb_re)
        q_re, q_im = power(float(s + 1))
        wrd_ref[rows, :CLUSTER_STATE] = spread(q_re * ct_re - q_im * ct_im)
        wrd_ref[rows, CLUSTER_STATE:] = spread(-(q_re * ct_im + q_im * ct_re))

    wrd0 = jnp.concatenate([spread(ct_re), spread(-ct_im)], axis=1)
    lag_all = _dot_nt(wend_ref[...], wrd0).astype(BF16)
    for t in range(SSM_BLOCK):
        cols = slice(t * LANES, (t + 1) * LANES)
        top = (t + 1) * LANES
        lagk_ref[:top, cols] = lag_all[FLAT - top:, :]
        if top < FLAT:
            lagk_ref[top:, cols] = jnp.zeros((FLAT - top, LANES), BF16)

    p_re, p_im = power(float(SSM_BLOCK))
    pw_ref[0:1, :] = jnp.concatenate([p_re, p_im], axis=1)
    for _ in range(SCAN_SEG_LEN.bit_length() - 1):
        p_re, p_im = p_re * p_re - p_im * p_im, 2.0 * (p_re * p_im)
    pw_ref[1:2, :] = jnp.concatenate([p_re, p_im], axis=1)


def _cmul_add(p_re, p_im, h_re, h_im, e_re, e_im):
    n_re = [pr * hr - pi * hi + er for pr, pi, hr, hi, er in zip(p_re, p_im, h_re, h_im, e_re)]
    n_im = [pr * hi + pi * hr + ei for pr, pi, hr, hi, ei in zip(p_re, p_im, h_re, h_im, e_im)]
    return n_re, n_im


def _s5_block_recurrence(st_ref, pw_ref):
    half = SCAN_TILES // 2

    def bcast(row):
        return [jnp.broadcast_to(row[:, q * LANES:(q + 1) * LANES], (SCAN_SEGS, LANES))
                for q in range(half)]

    p_re, p_im = bcast(pw_ref[0:1, :CLUSTER_STATE]), bcast(pw_ref[0:1, CLUSTER_STATE:])
    seg_rows = lambda i: pl.ds(i, SCAN_SEGS, stride=SCAN_SEG_PITCH)

    def gather(i):
        e = [st_ref[q, seg_rows(i), :] for q in range(SCAN_TILES)]
        return e[:half], e[half:]

    zeros = [jnp.zeros((SCAN_SEGS, LANES), F32)] * half

    def end_step(i, h):
        e_re, e_im = gather(i)
        return _cmul_add(p_re, p_im, h[0], h[1], e_re, e_im)

    l_re, l_im = lax.fori_loop(0, SCAN_SEG_LEN, end_step, (zeros, zeros), unroll=SCAN_UNROLL)

    s_re, s_im = bcast(pw_ref[1:2, :CLUSTER_STATE]), bcast(pw_ref[1:2, CLUSTER_STATE:])
    first = lax.broadcasted_iota(jnp.int32, (SCAN_SEGS, LANES), 0) == 0
    shift = lambda v: jnp.where(first, 0.0, pltpu.roll(v, 1, 0))
    c_re, c_im = zeros, zeros
    for _ in range(SCAN_SEGS - 1):
        n_re, n_im = _cmul_add(s_re, s_im, c_re, c_im, l_re, l_im)
        c_re, c_im = [shift(v) for v in n_re], [shift(v) for v in n_im]

    def scan_step(i, h):
        e_re, e_im = gather(i)
        for q in range(half):
            st_ref[q, seg_rows(i), :] = h[0][q]
            st_ref[half + q, seg_rows(i), :] = h[1][q]
        return _cmul_add(p_re, p_im, h[0], h[1], e_re, e_im)

    lax.fori_loop(0, SCAN_SEG_LEN, scan_step, (c_re, c_im), unroll=SCAN_UNROLL)


def _s5_kernel(u_ref, par_ref, bt_ref, ct_ref, dskip_ref, g_ref,
               wend_ref, wrd_ref, lagk_ref, pw_ref, st_ref):
    r = pl.program_id(1)

    @pl.when(r == 0)
    def _():
        _s5_build_operators(par_ref, bt_ref, ct_ref, wend_ref, wrd_ref, lagk_ref, pw_ref)

    u = u_ref[0]
    ys = []
    for j in range(FLAT // MXU_WIDTH):
        kdim = (j + 1) * MXU_WIDTH
        ys.append(_dot(u[:, :kdim], lagk_ref[:kdim, j * MXU_WIDTH:(j + 1) * MXU_WIDTH]))
    y = jnp.concatenate(ys, axis=1)

    e = _dot(u, wend_ref[...])
    for q in range(SCAN_TILES):
        for j in range(SCAN_SEGS):
            st_ref[q, j * SCAN_SEG_PITCH:j * SCAN_SEG_PITCH + SCAN_SEG_LEN, :] = (
                e[j * SCAN_SEG_LEN:(j + 1) * SCAN_SEG_LEN, q * LANES:(q + 1) * LANES])
    _s5_block_recurrence(st_ref, pw_ref)
    hprev = jnp.concatenate(
        [jnp.concatenate([st_ref[q, j * SCAN_SEG_PITCH:j * SCAN_SEG_PITCH + SCAN_SEG_LEN, :]
                          for q in range(SCAN_TILES)], axis=1)
         for j in range(SCAN_SEGS)], axis=0)

    y = y + _dot_nt(hprev.astype(BF16), wrd_ref[...])
    y = y + dskip_ref[0] * u.astype(F32)
    g_ref[0] = _gelu(y)


def _s5(layer, u_flat, par, bt, ct, dskip, blocks_per_seq):
    nb = u_flat.shape[1]
    rt = S5_ROW_TILE
    assert blocks_per_seq == rt and SCAN_SEG_LEN & (SCAN_SEG_LEN - 1) == 0
    return pl.pallas_call(
        _s5_kernel,
        grid=(N_CLUSTERS, nb // rt),
        in_specs=[
            pl.BlockSpec((1, rt, FLAT), lambda k, r: (k, r, 0)),
            pl.BlockSpec((None, 1, 3, CLUSTER_STATE), lambda k, r: (layer, k, 0, 0)),
            pl.BlockSpec((None, 1, 2, SSM_GROUP, CLUSTER_STATE), lambda k, r: (layer, k, 0, 0, 0)),
            pl.BlockSpec((None, 1, 2, SSM_GROUP, CLUSTER_STATE), lambda k, r: (layer, k, 0, 0, 0)),
            pl.BlockSpec((None, 1, 1, FLAT), lambda k, r: (layer, k, 0, 0)),
        ],
        out_specs=pl.BlockSpec((1, rt, FLAT), lambda k, r: (k, r, 0)),
        out_shape=jax.ShapeDtypeStruct((N_CLUSTERS, nb, FLAT), F32),
        scratch_shapes=[pltpu.VMEM((FLAT, 2 * CLUSTER_STATE), BF16),
                        pltpu.VMEM((FLAT, 2 * CLUSTER_STATE), BF16),
                        pltpu.VMEM((FLAT, FLAT), BF16),
                        pltpu.VMEM((2, 2 * CLUSTER_STATE), F32),
                        pltpu.VMEM((SCAN_TILES, SCAN_SEGS * SCAN_SEG_PITCH, LANES), F32)],
        compiler_params=pltpu.CompilerParams(
            dimension_semantics=("arbitrary", "arbitrary"), vmem_limit_bytes=VMEM_LIMIT),
        name="s5",
    )(u_flat, par, bt, ct, dskip)


def _s5_params(A_re, A_im, log_dt, B_re, B_im, C_re, C_im, D_skip):
    depth = A_re.shape[0]
    k, gpc, n, c = N_CLUSTERS, GROUPS_PER_CLUSTER, SSM_STATE, SSM_GROUP
    rows = lambda a: a.reshape(depth, k, 1, CLUSTER_STATE)
    ldt = jnp.broadcast_to(log_dt[..., None], A_re.shape)
    par = jnp.concatenate([rows(A_re), rows(A_im), rows(ldt)], axis=2)
    b = jnp.stack([B_re, B_im], axis=1).reshape(depth, 2, k, gpc, n, c)
    bt = b.transpose(0, 2, 1, 5, 3, 4).reshape(depth, k, 2, c, CLUSTER_STATE)
    cc = jnp.stack([C_re, C_im], axis=1).reshape(depth, 2, k, gpc, c, n)
    ct = cc.transpose(0, 2, 1, 4, 3, 5).reshape(depth, k, 2, c, CLUSTER_STATE)
    dskip = jnp.tile(D_skip.reshape(depth, k, 1, LANES), (1, 1, 1, SSM_BLOCK))
    return par, bt, ct, dskip


def _mix_out_kernel(final, rider_kinds, *refs):
    n_r = len(rider_kinds)
    x_ref, g_ref, yb_ref, yc_ref, vec_ref, wglu_ref, wout_ref, wg_ref, wu_ref, wd_ref = refs[:10]
    rider_src = refs[10:10 + n_r]
    o_ref = refs[10 + n_r]
    rider_dst = refs[11 + n_r:11 + 2 * n_r]
    (gs_ref,) = refs[11 + 2 * n_r:]
    _cast_riders(rider_kinds, rider_src, rider_dst)

    bglu = vec_ref[VEC_GLU:VEC_GLU + 1, :D_SSM]
    gffn = vec_ref[VEC_GFFN:VEC_GFFN + 1, :]
    tm = x_ref.shape[0]
    for k in range(N_CLUSTERS):
        for s in range(SSM_BLOCK):
            gs_ref[k, pl.ds(s, tm // SSM_BLOCK, stride=SSM_BLOCK), :] = (
                g_ref[k, :, s * LANES:(s + 1) * LANES])
    g = jnp.concatenate([gs_ref[k] for k in range(N_CLUSTERS)], axis=1)
    ya = g * jax.nn.sigmoid(_dot(g.astype(BF16), wglu_ref[...]) + bglu)
    ymix = jnp.concatenate([ya.astype(BF16), yb_ref[...], yc_ref[...]], axis=1)
    x1 = x_ref[...] + _dot(ymix, wout_ref[...])
    h = _rms(x1, gffn).astype(BF16)
    acc = x1
    for c0, c1 in FF_CHUNKS:
        gate = _dot(h, wg_ref[:, c0:c1])
        up = _dot(h, wu_ref[:, c0:c1])
        act = (gate * jax.nn.sigmoid(gate) * up).astype(BF16)
        acc = acc + _dot(act, wd_ref[c0:c1, :])
    if final:
        acc = _rms(acc, vec_ref[VEC_GFINAL:VEC_GFINAL + 1, :])
    o_ref[...] = acc


def _mix_out(layer, x2d, g3, yb, yc, vecs, wglu, wout, wg, wu, wd, final, riders=()):
    m = x2d.shape[0]
    tm = TOKEN_TILE
    n_steps = m // tm
    d_ff = wg.shape[-1]
    resident = lambda shape: pl.BlockSpec(shape, lambda i: (0, 0), pipeline_mode=pl.Buffered(1))
    r_specs = [_rider_specs(n_steps, lyr, w) for _, w, lyr in riders]
    return pl.pallas_call(
        functools.partial(_mix_out_kernel, final, tuple(k for k, _, _ in riders)),
        grid=(n_steps,),
        in_specs=[
            pl.BlockSpec((tm, D_MODEL), lambda i: (i, 0)),
            pl.BlockSpec((N_CLUSTERS, tm // SSM_BLOCK, FLAT), lambda i: (0, i, 0)),
            pl.BlockSpec((tm, D_POOL), lambda i: (i, 0)),
            pl.BlockSpec((tm, D_SGU), lambda i: (i, 0)),
            _layer_spec(layer, (VEC_ROWS, D_MODEL)),
            _layer_spec(layer, (D_SSM, D_SSM)),
            resident((D_MODEL, D_MODEL)),
            resident((D_MODEL, d_ff)),
            resident((D_MODEL, d_ff)),
            resident((d_ff, D_MODEL)),
        ] + [s for s, _, _ in r_specs],
        out_specs=[pl.BlockSpec((tm, D_MODEL), lambda i: (i, 0))] + [d for _, d, _ in r_specs],
        out_shape=[jax.ShapeDtypeStruct((m, D_MODEL), F32)] + [o for _, _, o in r_specs],
        scratch_shapes=[pltpu.VMEM((N_CLUSTERS, tm, LANES), F32)],
        compiler_params=pltpu.CompilerParams(
            dimension_semantics=("arbitrary",), vmem_limit_bytes=VMEM_LIMIT),
        name="mix_out",
    )(x2d, g3, yb, yc, vecs, wglu, wout, wg, wu, wd, *[w for _, w, _ in riders])


def kernel(x, g_mix, w_in, A_re, A_im, log_dt, B_re, B_im, C_re, C_im, D_skip, w_glu, b_glu,
           w_pool, pool_scale, sgu_ln_g, sgu_ln_b, w_spatial, b_spatial, w_out, g_ffn,
           w_gate, w_up, w_down, g_final):
    bsz, seq, d = x.shape
    depth = w_in.shape[0]
    m = bsz * seq
    assert d == D_MODEL and seq % TOKEN_TILE == 0 and TOKEN_TILE % CHUNK == 0
    assert (seq // SSM_BLOCK) % S5_ROW_TILE == 0
    x2d = x.reshape(m, D_MODEL)
    tril = jnp.tril(jnp.ones((CHUNK, CHUNK), dtype=bool))
    eye = jnp.eye(len(POOL_WINDOWS), dtype=F32)
    wpool_bd = jnp.einsum('dgij,gh->dgihj', w_pool, eye).reshape(depth, D_POOL, D_POOL).astype(BF16)
    ws = jnp.where(tril, w_spatial, 0.0).astype(BF16)
    bsp = jnp.repeat(jnp.swapaxes(b_spatial, 1, 2), SGU_HEAD_DIM, axis=2)
    par, bt, ct, dskip = _s5_params(A_re, A_im, log_dt, B_re, B_im, C_re, C_im, D_skip)
    vec_rows = [None] * VEC_ROWS
    vec_rows[VEC_GMIX] = g_mix
    vec_rows[VEC_GFFN] = g_ffn
    vec_rows[VEC_SGU] = jnp.concatenate([pool_scale, sgu_ln_g, sgu_ln_b], axis=-1)
    vec_rows[VEC_GLU] = jnp.pad(b_glu, ((0, 0), (0, D_MODEL - D_SSM)))
    vec_rows[VEC_GFINAL] = jnp.broadcast_to(g_final, (depth, D_MODEL))
    zero_row = jnp.zeros((depth, D_MODEL), F32)
    vecs = jnp.stack([zero_row if r is None else r for r in vec_rows], axis=1)
    wglu = w_glu.astype(BF16)
    b_end, u_end = D_SSM + D_POOL, D_SSM + D_POOL + D_SGU
    win = jnp.concatenate([w_in[0, :, u_end:], w_in[0, :, b_end:u_end], w_in[0, :, :b_end]],
                          axis=-1).astype(BF16)
    ffn_stacks = (w_out, w_gate, w_up, w_down)
    ffn = None
    for l in range(depth):
        riders = tuple(("plain", w, 0) for w in ffn_stacks) if l == 0 else ()
        u_flat, yb, yc, *cast = _mix_in(l, x2d, vecs, win, wpool_bd, ws, bsp, seq, riders)
        if l == 0:
            ffn = cast
        g3 = _s5(l, u_flat, par, bt, ct, dskip, seq // SSM_BLOCK)
        last = l == depth - 1
        riders = () if last else (("w_in", w_in, l + 1),) + tuple(
            ("plain", w, l + 1) for w in ffn_stacks)
        x2d, *cast = _mix_out(l, x2d, g3, yb, yc, vecs, wglu, *ffn, final=last, riders=riders)
        if not last:
            win, ffn = cast[0], cast[1:]
    return x2d.reshape(bsz, seq, D_MODEL)
```

```python
import functools
import math

import jax
import jax.numpy as jnp
from jax import lax
from jax.experimental import pallas as pl
from jax.experimental.pallas import tpu as pltpu

D_MODEL = 1024
D_SSM = 384
SSM_GROUP = 16
N_SSM_GROUPS = D_SSM // SSM_GROUP
SSM_STATE = 64
POOL_WINDOWS = (2, 4, 8, 16)
POOL_GROUP = 64
D_POOL = len(POOL_WINDOWS) * POOL_GROUP
MAX_WINDOW = max(POOL_WINDOWS)
SGU_HEADS = 6
SGU_HEAD_DIM = 64
D_SGU = SGU_HEADS * SGU_HEAD_DIM
CHUNK = 128
D_IN = D_SSM + D_POOL + 2 * D_SGU
EPS = 1e-6

LANES = 128
SSM_BLOCK = 16
N_CLUSTERS = D_SSM // LANES
GROUPS_PER_CLUSTER = LANES // SSM_GROUP
CLUSTER_STATE = GROUPS_PER_CLUSTER * SSM_STATE
FLAT = SSM_BLOCK * LANES

MXU_WIDTH = 256
BF16_SUBLANES = 16
VEC_ROWS = 8
VEC_GMIX, VEC_GFFN, VEC_SGU, VEC_GLU, VEC_GFINAL = 0, 1, 2, 3, 4
TOKEN_TILE = 512
MIX_IN_TILE = 1024
S5_ROW_TILE = 512
SUBLANES = 8
SCAN_SEGS = SUBLANES
SCAN_SEG_LEN = S5_ROW_TILE // SCAN_SEGS
SCAN_SEG_PITCH = SCAN_SEG_LEN + SUBLANES
SCAN_TILES = 2 * CLUSTER_STATE // LANES
SCAN_UNROLL = 8
FF_CHUNKS = ((0, 512), (512, 1024), (1024, 1536), (1536, 2048), (2048, 2560), (2560, 2816))
VMEM_LIMIT = 56 * 1024 * 1024

F32 = jnp.float32
BF16 = jnp.bfloat16
NT_DIMS = (((1,), (1,)), ((), ()))


def _gelu(x):
    c = math.sqrt(2.0 / math.pi)
    return 0.5 * x * (1.0 + jnp.tanh(c * (x + 0.044715 * (x * x * x))))


def _rms(x, g):
    ms = jnp.mean(x * x, axis=-1, keepdims=True)
    return x * lax.rsqrt(ms + EPS) * g


def _dot(a, b):
    return jnp.dot(a, b, preferred_element_type=F32)


def _dot_nt(a, b):
    return lax.dot_general(a, b, NT_DIMS, preferred_element_type=F32)


def _cast_riders(kinds, src_refs, dst_refs):
    for kind, src, dst in zip(kinds, src_refs, dst_refs):
        if kind == "w_in":
            b_end, u_end = D_SSM + D_POOL, D_SSM + D_POOL + D_SGU
            dst[:, :D_SGU] = src[:, u_end:].astype(BF16)
            dst[:, D_SGU:2 * D_SGU] = src[:, b_end:u_end].astype(BF16)
            dst[:, 2 * D_SGU:] = src[:, :b_end].astype(BF16)
        else:
            dst[...] = src[...].astype(BF16)


def _rider_specs(n_steps, layer, w):
    rows, cols = w.shape[1:]
    share = 1
    while (rows * share) % n_steps or (rows * share // n_steps) % BF16_SUBLANES:
        share *= 2
    slab = rows * share // n_steps
    src = pl.BlockSpec((None, slab, cols), lambda i: (layer, i // share, 0))
    dst = pl.BlockSpec((slab, cols), lambda i: (i // share, 0))
    return src, dst, jax.ShapeDtypeStruct((rows, cols), BF16)


def _mix_in_kernel(tiles_per_seq, rider_kinds, *refs):
    n_r = len(rider_kinds)
    x_ref, vec_ref, win_ref, wpool_ref, ws_ref, bsp_ref = refs[:6]
    rider_src = refs[6:6 + n_r]
    za_ref, yb_ref, yc_ref = refs[6 + n_r:9 + n_r]
    rider_dst = refs[9 + n_r:9 + 2 * n_r]
    halo_ref, zs_ref = refs[9 + 2 * n_r:]
    _cast_riders(rider_kinds, rider_src, rider_dst)

    tm = x_ref.shape[0]
    i = pl.program_id(0)
    seq_tile = i % tiles_per_seq
    gmix = vec_ref[VEC_GMIX:VEC_GMIX + 1, :]
    pscale = vec_ref[VEC_SGU:VEC_SGU + 1, :D_POOL]
    lng = vec_ref[VEC_SGU:VEC_SGU + 1, D_POOL:D_POOL + D_SGU]
    lnb = vec_ref[VEC_SGU:VEC_SGU + 1, D_POOL + D_SGU:]

    h = _rms(x_ref[...], gmix).astype(BF16)
    z_sgu = _dot(h, win_ref[:, :2 * D_SGU])
    z_rest = _dot(h, win_ref[:, 2 * D_SGU:])
    zv = z_sgu[:, :D_SGU]
    zu = z_sgu[:, D_SGU:]
    zb = z_rest[:, D_SSM:]

    for k in range(N_CLUSTERS):
        zs_ref[k] = z_rest[:, k * LANES:(k + 1) * LANES]
    for k in range(N_CLUSTERS):
        for s in range(SSM_BLOCK):
            piece = zs_ref[k, pl.ds(s, tm // SSM_BLOCK, stride=SSM_BLOCK), :]
            za_ref[k, :, s * LANES:(s + 1) * LANES] = piece.astype(BF16)

    halo = jnp.where(seq_tile == 0, 0.0, halo_ref[...])
    halo_ref[...] = zb[tm - MAX_WINDOW:, :]
    ext = jnp.concatenate([halo, zb], axis=0)
    lane = lax.broadcasted_iota(jnp.int32, (tm, LANES), 1)
    low = lane < POOL_GROUP
    pos1 = (seq_tile * tm + 1 + lax.broadcasted_iota(jnp.int32, (tm, LANES), 0)).astype(F32)

    e0 = ext[:, :LANES]
    s2 = e0 + pltpu.roll(e0, 1, 0)
    s4 = s2 + pltpu.roll(s2, 2, 0)
    sum0 = jnp.where(low, s2[MAX_WINDOW:], s4[MAX_WINDOW:])
    cnt0 = jnp.where(low, jnp.minimum(pos1, 2.0), jnp.minimum(pos1, 4.0))
    e1 = ext[:, LANES:]
    t2 = e1 + pltpu.roll(e1, 1, 0)
    t4 = t2 + pltpu.roll(t2, 2, 0)
    t8 = t4 + pltpu.roll(t4, 4, 0)
    t16 = t8 + pltpu.roll(t8, 8, 0)
    sum1 = jnp.where(low, t8[MAX_WINDOW:], t16[MAX_WINDOW:])
    cnt1 = jnp.where(low, jnp.minimum(pos1, 8.0), jnp.minimum(pos1, 16.0))
    pooled = jnp.concatenate([sum0 / cnt0, sum1 / cnt1], axis=1) - zb
    yb = _dot(pooled.astype(BF16), wpool_ref[...]) * pscale
    yb_ref[...] = yb.astype(BF16)

    u = _gelu(zu)
    v = _gelu(zv)
    mu = jnp.mean(v, axis=-1, keepdims=True)
    vc = v - mu
    var = jnp.mean(vc * vc, axis=-1, keepdims=True)
    vn = (vc * lax.rsqrt(var + EPS) * lng + lnb).astype(BF16)
    lane_c = lax.broadcasted_iota(jnp.int32, (CHUNK, LANES), 1)
    low_c = lane_c < SGU_HEAD_DIM
    zero = jnp.zeros((), BF16)
    for c in range(0, tm // CHUNK, 2):
        rows_a = slice(c * CHUNK, (c + 1) * CHUNK)
        rows_b = slice((c + 1) * CHUNK, (c + 2) * CHUNK)
        parts_a, parts_b = [], []
        for p in range(SGU_HEADS // 2):
            va = vn[rows_a, p * LANES:(p + 1) * LANES]
            vb = vn[rows_b, p * LANES:(p + 1) * LANES]
            top = jnp.concatenate([jnp.where(low_c, va, zero), jnp.where(low_c, vb, zero)], axis=1)
            bot = jnp.concatenate([jnp.where(low_c, zero, va), jnp.where(low_c, zero, vb)], axis=1)
            out = _dot(ws_ref[p], jnp.concatenate([top, bot], axis=0))
            parts_a.append(out[:, :LANES])
            parts_b.append(out[:, LANES:])
        for rows, parts in ((rows_a, parts_a), (rows_b, parts_b)):
            mixed = jnp.concatenate(parts, axis=1) + bsp_ref[...]
            yc_ref[rows, :] = (u[rows, :] * mixed).astype(BF16)


def _layer_spec(layer, shape):
    zeros = (0,) * len(shape)
    return pl.BlockSpec((None,) + tuple(shape), lambda *_: (layer,) + zeros)


def _mix_in(layer, x2d, vecs, win, wpool_bd, ws, bsp, seq_len, riders=()):
    m = x2d.shape[0]
    tm = MIX_IN_TILE
    n_steps = m // tm
    spec = functools.partial(_layer_spec, layer)
    r_specs = [_rider_specs(n_steps, lyr, w) for _, w, lyr in riders]
    return pl.pallas_call(
        functools.partial(_mix_in_kernel, seq_len // tm, tuple(k for k, _, _ in riders)),
        grid=(n_steps,),
        in_specs=[
            pl.BlockSpec((tm, D_MODEL), lambda i: (i, 0)),
            spec((VEC_ROWS, D_MODEL)),
            pl.BlockSpec((D_MODEL, D_IN), lambda i: (0, 0)),
            spec((D_POOL, D_POOL)),
            spec((SGU_HEADS // 2, CHUNK, 2 * CHUNK)),
            spec((CHUNK, D_SGU)),
        ] + [s for s, _, _ in r_specs],
        out_specs=[
            pl.BlockSpec((N_CLUSTERS, tm // SSM_BLOCK, FLAT), lambda i: (0, i, 0)),
            pl.BlockSpec((tm, D_POOL), lambda i: (i, 0)),
            pl.BlockSpec((tm, D_SGU), lambda i: (i, 0)),
        ] + [d for _, d, _ in r_specs],
        out_shape=[
            jax.ShapeDtypeStruct((N_CLUSTERS, m // SSM_BLOCK, FLAT), BF16),
            jax.ShapeDtypeStruct((m, D_POOL), BF16),
            jax.ShapeDtypeStruct((m, D_SGU), BF16),
        ] + [o for _, _, o in r_specs],
        scratch_shapes=[pltpu.VMEM((MAX_WINDOW, D_POOL), F32),
                        pltpu.VMEM((N_CLUSTERS, tm, LANES), F32)],
        compiler_params=pltpu.CompilerParams(
            dimension_semantics=("arbitrary",), vmem_limit_bytes=VMEM_LIMIT),
        name="mix_in",
    )(x2d, vecs, win, wpool_bd, ws, bsp, *[w for _, w, _ in riders])


def _s5_build_operators(par_ref, bt_ref, ct_ref, wend_ref, wrd_ref, lagk_ref, pw_ref):
    a_re = par_ref[0, 0:1, :]
    a_im = par_ref[0, 1:2, :]
    dt = jnp.exp(par_ref[0, 2:3, :])
    l_re = a_re * dt
    l_im = a_im * dt
    mag = jnp.exp(l_re)
    ar = mag * jnp.cos(l_im)
    ai = mag * jnp.sin(l_im)
    den = a_re * a_re + a_im * a_im
    f_re = ((ar - 1.0) * a_re + ai * a_im) / den
    f_im = (ai * a_re - (ar - 1.0) * a_im) / den
    bt_re, bt_im = bt_ref[0, 0], bt_ref[0, 1]
    bb_re = f_re * bt_re - f_im * bt_im
    bb_im = f_re * bt_im + f_im * bt_re
    ct_re, ct_im = ct_ref[0, 0], ct_ref[0, 1]

    row_g = lax.broadcasted_iota(jnp.int32, (LANES, CLUSTER_STATE), 0) // SSM_GROUP
    col_g = lax.broadcasted_iota(jnp.int32, (LANES, CLUSTER_STATE), 1) // SSM_STATE
    same_group = row_g == col_g

    def power(l):
        m_l = jnp.exp(l * l_re)
        return m_l * jnp.cos(l * l_im), m_l * jnp.sin(l * l_im)

    def spread(v):
        return jnp.where(same_group, jnp.tile(v, (GROUPS_PER_CLUSTER, 1)), 0.0).astype(BF16)

    for s in range(SSM_BLOCK):
        p_re, p_im = power(float(SSM_BLOCK - 1 - s))
        rows = slice(s * LANES, (s + 1) * LANES)
        wend_ref[rows, :CLUSTER_STATE] = spread(p_re * bb_re - p_im * bb_im)
        wend_ref[rows, CLUSTER_STATE:] = spread(p_re * bb_im + p_im * bb_re)
        q_re, q_im = power(float(s + 1))
        wrd_ref[rows, :CLUSTER_STATE] = spread(q_re * ct_re - q_im * ct_im)
        wrd_ref[rows, CLUSTER_STATE:] = spread(-(q_re * ct_im + q_im * ct_re))

    wrd0 = jnp.concatenate([spread(ct_re), spread(-ct_im)], axis=1)
    lag_all = _dot_nt(wend_ref[...], wrd0).astype(BF16)
    for t in range(SSM_BLOCK):
        cols = slice(t * LANES, (t + 1) * LANES)
        top = (t + 1) * LANES
        lagk_ref[:top, cols] = lag_all[FLAT - top:, :]
        if top < FLAT:
            lagk_ref[top:, cols] = jnp.zeros((FLAT - top, LANES), BF16)

    p_re, p_im = power(float(SSM_BLOCK))
    pw_ref[0:1, :] = jnp.concatenate([p_re, p_im], axis=1)
    for _ in range(SCAN_SEG_LEN.bit_length() - 1):
        p_re, p_im = p_re * p_re - p_im * p_im, 2.0 * (p_re * p_im)
    pw_ref[1:2, :] = jnp.concatenate([p_re, p_im], axis=1)


def _cmul_add(p_re, p_im, h_re, h_im, e_re, e_im):
    n_re = [pr * hr - pi * hi + er for pr, pi, hr, hi, er in zip(p_re, p_im, h_re, h_im, e_re)]
    n_im = [pr * hi + pi * hr + ei for pr, pi, hr, hi, ei in zip(p_re, p_im, h_re, h_im, e_im)]
    return n_re, n_im


def _s5_block_recurrence(st_ref, pw_ref):
    half = SCAN_TILES // 2

    def bcast(row):
        return [jnp.broadcast_to(row[:, q * LANES:(q + 1) * LANES], (SCAN_SEGS, LANES))
                for q in range(half)]

    p_re, p_im = bcast(pw_ref[0:1, :CLUSTER_STATE]), bcast(pw_ref[0:1, CLUSTER_STATE:])
    seg_rows = lambda i: pl.ds(i, SCAN_SEGS, stride=SCAN_SEG_PITCH)

    def gather(i):
        e = [st_ref[q, seg_rows(i), :] for q in range(SCAN_TILES)]
        return e[:half], e[half:]

    zeros = [jnp.zeros((SCAN_SEGS, LANES), F32)] * half

    def end_step(i, h):
        e_re, e_im = gather(i)
        return _cmul_add(p_re, p_im, h[0], h[1], e_re, e_im)

    l_re, l_im = lax.fori_loop(0, SCAN_SEG_LEN, end_step, (zeros, zeros), unroll=SCAN_UNROLL)

    s_re, s_im = bcast(pw_ref[1:2, :CLUSTER_STATE]), bcast(pw_ref[1:2, CLUSTER_STATE:])
    first = lax.broadcasted_iota(jnp.int32, (SCAN_SEGS, LANES), 0) == 0
    shift = lambda v: jnp.where(first, 0.0, pltpu.roll(v, 1, 0))
    c_re, c_im = zeros, zeros
    for _ in range(SCAN_SEGS - 1):
        n_re, n_im = _cmul_add(s_re, s_im, c_re, c_im, l_re, l_im)
        c_re, c_im = [shift(v) for v in n_re], [shift(v) for v in n_im]

    def scan_step(i, h):
        e_re, e_im = gather(i)
        for q in range(half):
            st_ref[q, seg_rows(i), :] = h[0][q]
            st_ref[half + q, seg_rows(i), :] = h[1][q]
        return _cmul_add(p_re, p_im, h[0], h[1], e_re, e_im)

    lax.fori_loop(0, SCAN_SEG_LEN, scan_step, (c_re, c_im), unroll=SCAN_UNROLL)


def _s5_kernel(u_ref, par_ref, bt_ref, ct_ref, dskip_ref, g_ref,
               wend_ref, wrd_ref, lagk_ref, pw_ref, st_ref):
    r = pl.program_id(1)

    @pl.when(r == 0)
    def _():
        _s5_build_operators(par_ref, bt_ref, ct_ref, wend_ref, wrd_ref, lagk_ref, pw_ref)

    u = u_ref[0]
    ys = []
    for j in range(FLAT // MXU_WIDTH):
        kdim = (j + 1) * MXU_WIDTH
        ys.append(_dot(u[:, :kdim], lagk_ref[:kdim, j * MXU_WIDTH:(j + 1) * MXU_WIDTH]))
    y = jnp.concatenate(ys, axis=1)

    e = _dot(u, wend_ref[...])
    for q in range(SCAN_TILES):
        for j in range(SCAN_SEGS):
            st_ref[q, j * SCAN_SEG_PITCH:j * SCAN_SEG_PITCH + SCAN_SEG_LEN, :] = (
                e[j * SCAN_SEG_LEN:(j + 1) * SCAN_SEG_LEN, q * LANES:(q + 1) * LANES])
    _s5_block_recurrence(st_ref, pw_ref)
    hprev = jnp.concatenate(
        [jnp.concatenate([st_ref[q, j * SCAN_SEG_PITCH:j * SCAN_SEG_PITCH + SCAN_SEG_LEN, :]
                          for q in range(SCAN_TILES)], axis=1)
         for j in range(SCAN_SEGS)], axis=0)

    y = y + _dot_nt(hprev.astype(BF16), wrd_ref[...])
    y = y + dskip_ref[0] * u.astype(F32)
    g_ref[0] = _gelu(y)


def _s5(layer, u_flat, par, bt, ct, dskip, blocks_per_seq):
    nb = u_flat.shape[1]
    rt = S5_ROW_TILE
    assert blocks_per_seq == rt and SCAN_SEG_LEN & (SCAN_SEG_LEN - 1) == 0
    return pl.pallas_call(
        _s5_kernel,
        grid=(N_CLUSTERS, nb // rt),
        in_specs=[
            pl.BlockSpec((1, rt, FLAT), lambda k, r: (k, r, 0)),
            pl.BlockSpec((None, 1, 3, CLUSTER_STATE), lambda k, r: (layer, k, 0, 0)),
            pl.BlockSpec((None, 1, 2, SSM_GROUP, CLUSTER_STATE), lambda k, r: (layer, k, 0, 0, 0)),
            pl.BlockSpec((None, 1, 2, SSM_GROUP, CLUSTER_STATE), lambda k, r: (layer, k, 0, 0, 0)),
            pl.BlockSpec((None, 1, 1, FLAT), lambda k, r: (layer, k, 0, 0)),
        ],
        out_specs=pl.BlockSpec((1, rt, FLAT), lambda k, r: (k, r, 0)),
        out_shape=jax.ShapeDtypeStruct((N_CLUSTERS, nb, FLAT), F32),
        scratch_shapes=[pltpu.VMEM((FLAT, 2 * CLUSTER_STATE), BF16),
                        pltpu.VMEM((FLAT, 2 * CLUSTER_STATE), BF16),
                        pltpu.VMEM((FLAT, FLAT), BF16),
                        pltpu.VMEM((2, 2 * CLUSTER_STATE), F32),
                        pltpu.VMEM((SCAN_TILES, SCAN_SEGS * SCAN_SEG_PITCH, LANES), F32)],
        compiler_params=pltpu.CompilerParams(
            dimension_semantics=("arbitrary", "arbitrary"), vmem_limit_bytes=VMEM_LIMIT),
        name="s5",
    )(u_flat, par, bt, ct, dskip)


def _s5_params(A_re, A_im, log_dt, B_re, B_im, C_re, C_im, D_skip):
    depth = A_re.shape[0]
    k, gpc, n, c = N_CLUSTERS, GROUPS_PER_CLUSTER, SSM_STATE, SSM_GROUP
    rows = lambda a: a.reshape(depth, k, 1, CLUSTER_STATE)
    ldt = jnp.broadcast_to(log_dt[..., None], A_re.shape)
    par = jnp.concatenate([rows(A_re), rows(A_im), rows(ldt)], axis=2)
    b = jnp.stack([B_re, B_im], axis=1).reshape(depth, 2, k, gpc, n, c)
    bt = b.transpose(0, 2, 1, 5, 3, 4).reshape(depth, k, 2, c, CLUSTER_STATE)
    cc = jnp.stack([C_re, C_im], axis=1).reshape(depth, 2, k, gpc, c, n)
    ct = cc.transpose(0, 2, 1, 4, 3, 5).reshape(depth, k, 2, c, CLUSTER_STATE)
    dskip = jnp.tile(D_skip.reshape(depth, k, 1, LANES), (1, 1, 1, SSM_BLOCK))
    return par, bt, ct, dskip


def _mix_out_kernel(final, rider_kinds, *refs):
    n_r = len(rider_kinds)
    x_ref, g_ref, yb_ref, yc_ref, vec_ref, wglu_ref, wout_ref, wg_ref, wu_ref, wd_ref = refs[:10]
    rider_src = refs[10:10 + n_r]
    o_ref = refs[10 + n_r]
    rider_dst = refs[11 + n_r:11 + 2 * n_r]
    (gs_ref,) = refs[11 + 2 * n_r:]
    _cast_riders(rider_kinds, rider_src, rider_dst)

    bglu = vec_ref[VEC_GLU:VEC_GLU + 1, :D_SSM]
    gffn = vec_ref[VEC_GFFN:VEC_GFFN + 1, :]
    tm = x_ref.shape[0]
    for k in range(N_CLUSTERS):
        for s in range(SSM_BLOCK):
            gs_ref[k, pl.ds(s, tm // SSM_BLOCK, stride=SSM_BLOCK), :] = (
                g_ref[k, :, s * LANES:(s + 1) * LANES])
    g = jnp.concatenate([gs_ref[k] for k in range(N_CLUSTERS)], axis=1)
    ya = g * jax.nn.sigmoid(_dot(g.astype(BF16), wglu_ref[...]) + bglu)
    ymix = jnp.concatenate([ya.astype(BF16), yb_ref[...], yc_ref[...]], axis=1)
    x1 = x_ref[...] + _dot(ymix, wout_ref[...])
    h = _rms(x1, gffn).astype(BF16)
    acc = x1
    for c0, c1 in FF_CHUNKS:
        gate = _dot(h, wg_ref[:, c0:c1])
        up = _dot(h, wu_ref[:, c0:c1])
        act = (gate * jax.nn.sigmoid(gate) * up).astype(BF16)
        acc = acc + _dot(act, wd_ref[c0:c1, :])
    if final:
        acc = _rms(acc, vec_ref[VEC_GFINAL:VEC_GFINAL + 1, :])
    o_ref[...] = acc


def _mix_out(layer, x2d, g3, yb, yc, vecs, wglu, wout, wg, wu, wd, final, riders=()):
    m = x2d.shape[0]
    tm = TOKEN_TILE
    n_steps = m // tm
    d_ff = wg.shape[-1]
    resident = lambda shape: pl.BlockSpec(shape, lambda i: (0, 0), pipeline_mode=pl.Buffered(1))
    r_specs = [_rider_specs(n_steps, lyr, w) for _, w, lyr in riders]
    return pl.pallas_call(
        functools.partial(_mix_out_kernel, final, tuple(k for k, _, _ in riders)),
        grid=(n_steps,),
        in_specs=[
            pl.BlockSpec((tm, D_MODEL), lambda i: (i, 0)),
            pl.BlockSpec((N_CLUSTERS, tm // SSM_BLOCK, FLAT), lambda i: (0, i, 0)),
            pl.BlockSpec((tm, D_POOL), lambda i: (i, 0)),
            pl.BlockSpec((tm, D_SGU), lambda i: (i, 0)),
            _layer_spec(layer, (VEC_ROWS, D_MODEL)),
            _layer_spec(layer, (D_SSM, D_SSM)),
            resident((D_MODEL, D_MODEL)),
            resident((D_MODEL, d_ff)),
            resident((D_MODEL, d_ff)),
            resident((d_ff, D_MODEL)),
        ] + [s for s, _, _ in r_specs],
        out_specs=[pl.BlockSpec((tm, D_MODEL), lambda i: (i, 0))] + [d for _, d, _ in r_specs],
        out_shape=[jax.ShapeDtypeStruct((m, D_MODEL), F32)] + [o for _, _, o in r_specs],
        scratch_shapes=[pltpu.VMEM((N_CLUSTERS, tm, LANES), F32)],
        compiler_params=pltpu.CompilerParams(
            dimension_semantics=("arbitrary",), vmem_limit_bytes=VMEM_LIMIT),
        name="mix_out",
    )(x2d, g3, yb, yc, vecs, wglu, wout, wg, wu, wd, *[w for _, w, _ in riders])


def kernel(x, g_mix, w_in, A_re, A_im, log_dt, B_re, B_im, C_re, C_im, D_skip, w_glu, b_glu,
           w_pool, pool_scale, sgu_ln_g, sgu_ln_b, w_spatial, b_spatial, w_out, g_ffn,
           w_gate, w_up, w_down, g_final):
    bsz, seq, d = x.shape
    depth = w_in.shape[0]
    m = bsz * seq
    assert d == D_MODEL and seq % MIX_IN_TILE == 0 and MIX_IN_TILE % (2 * CHUNK) == 0
    assert m % TOKEN_TILE == 0
    assert (seq // SSM_BLOCK) % S5_ROW_TILE == 0
    x2d = x.reshape(m, D_MODEL)
    tril = jnp.tril(jnp.ones((CHUNK, CHUNK), dtype=bool))
    eye = jnp.eye(len(POOL_WINDOWS), dtype=F32)
    wpool_bd = jnp.einsum('dgij,gh->dgihj', w_pool, eye).reshape(depth, D_POOL, D_POOL).astype(BF16)
    ws = jnp.where(tril, w_spatial, 0.0).astype(BF16)
    ws = jnp.concatenate([ws[:, 0::2], ws[:, 1::2]], axis=-1)
    bsp = jnp.repeat(jnp.swapaxes(b_spatial, 1, 2), SGU_HEAD_DIM, axis=2)
    par, bt, ct, dskip = _s5_params(A_re, A_im, log_dt, B_re, B_im, C_re, C_im, D_skip)
    vec_rows = [None] * VEC_ROWS
    vec_rows[VEC_GMIX] = g_mix
    vec_rows[VEC_GFFN] = g_ffn
    vec_rows[VEC_SGU] = jnp.concatenate([pool_scale, sgu_ln_g, sgu_ln_b], axis=-1)
    vec_rows[VEC_GLU] = jnp.pad(b_glu, ((0, 0), (0, D_MODEL - D_SSM)))
    vec_rows[VEC_GFINAL] = jnp.broadcast_to(g_final, (depth, D_MODEL))
    zero_row = jnp.zeros((depth, D_MODEL), F32)
    vecs = jnp.stack([zero_row if r is None else r for r in vec_rows], axis=1)
    wglu = w_glu.astype(BF16)
    b_end, u_end = D_SSM + D_POOL, D_SSM + D_POOL + D_SGU
    win = jnp.concatenate([w_in[0, :, u_end:], w_in[0, :, b_end:u_end], w_in[0, :, :b_end]],
                          axis=-1).astype(BF16)
    ffn_stacks = (w_out, w_gate, w_up, w_down)
    ffn = None
    for l in range(depth):
        riders = tuple(("plain", w, 0) for w in ffn_stacks) if l == 0 else ()
        u_flat, yb, yc, *cast = _mix_in(l, x2d, vecs, win, wpool_bd, ws, bsp, seq, riders)
        if l == 0:
            ffn = cast
        g3 = _s5(l, u_flat, par, bt, ct, dskip, seq // SSM_BLOCK)
        last = l == depth - 1
        riders = () if last else (("w_in", w_in, l + 1),) + tuple(
            ("plain", w, l + 1) for w in ffn_stacks)
        x2d, *cast = _mix_out(l, x2d, g3, yb, yc, vecs, wglu, *ffn, final=last, riders=riders)
        if not last:
            win, ffn = cast[0], cast[1:]
    return x2d.reshape(bsz, seq, D_MODEL)
```

```python
import functools
import math

import jax
import jax.numpy as jnp
from jax import lax
from jax.experimental import pallas as pl
from jax.experimental.pallas import tpu as pltpu

D_MODEL = 1024
D_SSM = 384
SSM_GROUP = 16
N_SSM_GROUPS = D_SSM // SSM_GROUP
SSM_STATE = 64
POOL_WINDOWS = (2, 4, 8, 16)
POOL_GROUP = 64
D_POOL = len(POOL_WINDOWS) * POOL_GROUP
MAX_WINDOW = max(POOL_WINDOWS)
SGU_HEADS = 6
SGU_HEAD_DIM = 64
D_SGU = SGU_HEADS * SGU_HEAD_DIM
CHUNK = 128
D_IN = D_SSM + D_POOL + 2 * D_SGU
EPS = 1e-6

LANES = 128
SSM_BLOCK = 16
N_CLUSTERS = D_SSM // LANES
GROUPS_PER_CLUSTER = LANES // SSM_GROUP
CLUSTER_STATE = GROUPS_PER_CLUSTER * SSM_STATE
FLAT = SSM_BLOCK * LANES

MXU_WIDTH = 256
BF16_SUBLANES = 16
VEC_ROWS = 8
VEC_GMIX, VEC_GFFN, VEC_SGU, VEC_GLU, VEC_GFINAL = 0, 1, 2, 3, 4
TOKEN_TILE = 512
MIX_IN_TILE = 1024
MIX_SUB = 512
S5_ROW_TILE = 512
SUBLANES = 8
SCAN_SEGS = SUBLANES
SCAN_SEG_LEN = S5_ROW_TILE // SCAN_SEGS
SCAN_SEG_PITCH = SCAN_SEG_LEN + SUBLANES
SCAN_TILES = 2 * CLUSTER_STATE // LANES
SCAN_UNROLL = 8
FF_CHUNKS = ((0, 768), (768, 1536), (1536, 2304), (2304, 2816))
VMEM_LIMIT = 56 * 1024 * 1024

F32 = jnp.float32
BF16 = jnp.bfloat16
NT_DIMS = (((1,), (1,)), ((), ()))


def _gelu(x):
    c = math.sqrt(2.0 / math.pi)
    return x * (0.5 + 0.5 * jnp.tanh(x * (c + (c * 0.044715) * (x * x))))


def _rms(x, g):
    ms = jnp.mean(x * x, axis=-1, keepdims=True)
    return x * lax.rsqrt(ms + EPS) * g


def _dot(a, b):
    return jnp.dot(a, b, preferred_element_type=F32)


def _dot_nt(a, b):
    return lax.dot_general(a, b, NT_DIMS, preferred_element_type=F32)


def _cast_riders(kinds, src_refs, dst_refs):
    for kind, src, dst in zip(kinds, src_refs, dst_refs):
        if kind == "w_in":
            b_end, u_end = D_SSM + D_POOL, D_SSM + D_POOL + D_SGU
            dst[:, :D_SGU] = src[:, u_end:].astype(BF16)
            dst[:, D_SGU:2 * D_SGU] = src[:, b_end:u_end].astype(BF16)
            dst[:, 2 * D_SGU:] = src[:, :b_end].astype(BF16)
        else:
            dst[...] = src[...].astype(BF16)


def _rider_specs(n_steps, layer, w):
    rows, cols = w.shape[1:]
    share = 1
    while (rows * share) % n_steps or (rows * share // n_steps) % BF16_SUBLANES:
        share *= 2
    slab = rows * share // n_steps
    which = lambda i: jnp.minimum(i, n_steps - 1) // share
    src = pl.BlockSpec((None, slab, cols), lambda i: (layer, which(i), 0))
    dst = pl.BlockSpec((slab, cols), lambda i: (which(i), 0))
    return src, dst, jax.ShapeDtypeStruct((rows, cols), BF16)


def _mix_in_kernel(tiles_per_seq, rider_kinds, *refs):
    n_r = len(rider_kinds)
    x_ref, vec_ref, win_ref, wpool_ref, ws_ref, bsp_ref = refs[:6]
    rider_src = refs[6:6 + n_r]
    za_ref, yb_ref, yc_ref = refs[6 + n_r:9 + n_r]
    rider_dst = refs[9 + n_r:9 + 2 * n_r]
    halo_ref, zsgu_ref, zs5_ref, zb_ref = refs[9 + 2 * n_r:]
    _cast_riders(rider_kinds, rider_src, rider_dst)
    i = pl.program_id(0)
    tm = x_ref.shape[0]
    sub_blocks = [slice(r, r + MIX_SUB) for r in range(0, tm, MIX_SUB)]

    def project(rows):
        h = _rms(x_ref[rows, :], vec_ref[VEC_GMIX:VEC_GMIX + 1, :]).astype(BF16)
        zsgu_ref[rows, :] = _dot(h, win_ref[:, :2 * D_SGU])
        z_rest = _dot(h, win_ref[:, 2 * D_SGU:])
        for k in range(N_CLUSTERS):
            zs5_ref[k, rows, :] = z_rest[:, k * LANES:(k + 1) * LANES]
        zb_ref[rows, :] = z_rest[:, D_SSM:]

    project(sub_blocks[0])
    for j, rows in enumerate(sub_blocks):
        if j + 1 < len(sub_blocks):
            project(sub_blocks[j + 1])
        _mix_rows(tiles_per_seq, i, rows, refs, n_r)


def _mix_rows(tiles_per_seq, tile, rows, refs, n_r):
    vec_ref, wpool_ref, ws_ref, bsp_ref = refs[1], refs[3], refs[4], refs[5]
    za_ref, yb_ref, yc_ref = refs[6 + n_r:9 + n_r]
    halo_ref, zsgu_ref, zs5_ref, zb_ref = refs[9 + 2 * n_r:]
    tile_rows = zb_ref.shape[0]
    tm = MIX_SUB
    row0 = rows.start
    pscale = vec_ref[VEC_SGU:VEC_SGU + 1, :D_POOL]
    lng = vec_ref[VEC_SGU:VEC_SGU + 1, D_POOL:D_POOL + D_SGU]
    lnb = vec_ref[VEC_SGU:VEC_SGU + 1, D_POOL + D_SGU:]
    zv = zsgu_ref[rows, :D_SGU]
    zu = zsgu_ref[rows, D_SGU:]
    zb = zb_ref[rows, :]

    blocks = slice(row0 // SSM_BLOCK, (row0 + tm) // SSM_BLOCK)
    for k in range(N_CLUSTERS):
        for s in range(SSM_BLOCK):
            piece = zs5_ref[k, pl.ds(row0 + s, tm // SSM_BLOCK, stride=SSM_BLOCK), :]
            za_ref[k, blocks, s * LANES:(s + 1) * LANES] = piece.astype(BF16)

    halo = halo_ref[...]
    if row0 == 0:
        halo = jnp.where(tile % tiles_per_seq == 0, 0.0, halo)
    halo_ref[...] = zb[tm - MAX_WINDOW:, :]
    ext = jnp.concatenate([halo, zb], axis=0)
    lane = lax.broadcasted_iota(jnp.int32, (tm, LANES), 1)
    low = lane < POOL_GROUP
    seq_pos = (tile % tiles_per_seq) * tile_rows + row0
    pos1 = (seq_pos + 1 + lax.broadcasted_iota(jnp.int32, (tm, LANES), 0)).astype(F32)

    e0 = ext[:, :LANES]
    s2 = e0 + pltpu.roll(e0, 1, 0)
    s4 = s2 + pltpu.roll(s2, 2, 0)
    sum0 = jnp.where(low, s2[MAX_WINDOW:], s4[MAX_WINDOW:])
    cnt0 = jnp.where(low, jnp.minimum(pos1, 2.0), jnp.minimum(pos1, 4.0))
    e1 = ext[:, LANES:]
    t2 = e1 + pltpu.roll(e1, 1, 0)
    t4 = t2 + pltpu.roll(t2, 2, 0)
    t8 = t4 + pltpu.roll(t4, 4, 0)
    t16 = t8 + pltpu.roll(t8, 8, 0)
    sum1 = jnp.where(low, t8[MAX_WINDOW:], t16[MAX_WINDOW:])
    cnt1 = jnp.where(low, jnp.minimum(pos1, 8.0), jnp.minimum(pos1, 16.0))
    pooled = jnp.concatenate([sum0 / cnt0, sum1 / cnt1], axis=1) - zb
    yb = _dot(pooled.astype(BF16), wpool_ref[...]) * pscale
    yb_ref[rows, :] = yb.astype(BF16)

    u = _gelu(zu)
    v = _gelu(zv)
    mu = jnp.mean(v, axis=-1, keepdims=True)
    vc = v - mu
    var = jnp.mean(vc * vc, axis=-1, keepdims=True)
    vn = (vc * lax.rsqrt(var + EPS) * lng + lnb).astype(BF16)
    lane_c = lax.broadcasted_iota(jnp.int32, (CHUNK, LANES), 1)
    low_c = lane_c < SGU_HEAD_DIM
    zero = jnp.zeros((), BF16)
    for c in range(0, tm // CHUNK, 2):
        rows_a = slice(c * CHUNK, (c + 1) * CHUNK)
        rows_b = slice((c + 1) * CHUNK, (c + 2) * CHUNK)
        parts_a, parts_b = [], []
        for p in range(SGU_HEADS // 2):
            va = vn[rows_a, p * LANES:(p + 1) * LANES]
            vb = vn[rows_b, p * LANES:(p + 1) * LANES]
            top = jnp.concatenate([jnp.where(low_c, va, zero), jnp.where(low_c, vb, zero)], axis=1)
            bot = jnp.concatenate([jnp.where(low_c, zero, va), jnp.where(low_c, zero, vb)], axis=1)
            out = _dot(ws_ref[p], jnp.concatenate([top, bot], axis=0))
            parts_a.append(out[:, :LANES])
            parts_b.append(out[:, LANES:])
        for sub, parts in ((rows_a, parts_a), (rows_b, parts_b)):
            mixed = jnp.concatenate(parts, axis=1) + bsp_ref[...]
            yc_ref[row0 + sub.start:row0 + sub.stop, :] = (u[sub, :] * mixed).astype(BF16)


def _layer_spec(layer, shape):
    zeros = (0,) * len(shape)
    return pl.BlockSpec((None,) + tuple(shape), lambda *_: (layer,) + zeros)


def _mix_in(layer, x2d, vecs, win, wpool_bd, ws, bsp, seq_len, riders=()):
    m = x2d.shape[0]
    tm = MIX_IN_TILE
    n_steps = m // tm
    spec = functools.partial(_layer_spec, layer)
    r_specs = [_rider_specs(n_steps, lyr, w) for _, w, lyr in riders]
    return pl.pallas_call(
        functools.partial(_mix_in_kernel, seq_len // tm, tuple(k for k, _, _ in riders)),
        grid=(n_steps,),
        in_specs=[
            pl.BlockSpec((tm, D_MODEL), lambda i: (i, 0)),
            spec((VEC_ROWS, D_MODEL)),
            pl.BlockSpec((D_MODEL, D_IN), lambda i: (0, 0)),
            spec((D_POOL, D_POOL)),
            spec((SGU_HEADS // 2, CHUNK, 2 * CHUNK)),
            spec((CHUNK, D_SGU)),
        ] + [s for s, _, _ in r_specs],
        out_specs=[
            pl.BlockSpec((N_CLUSTERS, tm // SSM_BLOCK, FLAT), lambda i: (0, i, 0)),
            pl.BlockSpec((tm, D_POOL), lambda i: (i, 0)),
            pl.BlockSpec((tm, D_SGU), lambda i: (i, 0)),
        ] + [d for _, d, _ in r_specs],
        out_shape=[
            jax.ShapeDtypeStruct((N_CLUSTERS, m // SSM_BLOCK, FLAT), BF16),
            jax.ShapeDtypeStruct((m, D_POOL), BF16),
            jax.ShapeDtypeStruct((m, D_SGU), BF16),
        ] + [o for _, _, o in r_specs],
        scratch_shapes=[pltpu.VMEM((MAX_WINDOW, D_POOL), F32),
                        pltpu.VMEM((tm, 2 * D_SGU), F32),
                        pltpu.VMEM((N_CLUSTERS, tm, LANES), F32),
                        pltpu.VMEM((tm, D_POOL), F32)],
        compiler_params=pltpu.CompilerParams(
            dimension_semantics=("arbitrary",), vmem_limit_bytes=VMEM_LIMIT),
        name="mix_in",
    )(x2d, vecs, win, wpool_bd, ws, bsp, *[w for _, w, _ in riders])


def _s5_build_operators(par_ref, bt_ref, ct_ref, wend_ref, wrd_ref, lagk_ref, pw_ref):
    a_re = par_ref[0, 0:1, :]
    a_im = par_ref[0, 1:2, :]
    dt = jnp.exp(par_ref[0, 2:3, :])
    l_re = a_re * dt
    l_im = a_im * dt
    mag = jnp.exp(l_re)
    ar = mag * jnp.cos(l_im)
    ai = mag * jnp.sin(l_im)
    den = a_re * a_re + a_im * a_im
    f_re = ((ar - 1.0) * a_re + ai * a_im) / den
    f_im = (ai * a_re - (ar - 1.0) * a_im) / den
    bt_re, bt_im = bt_ref[0, 0], bt_ref[0, 1]
    bb_re = f_re * bt_re - f_im * bt_im
    bb_im = f_re * bt_im + f_im * bt_re
    ct_re, ct_im = ct_ref[0, 0], ct_ref[0, 1]

    row_g = lax.broadcasted_iota(jnp.int32, (LANES, CLUSTER_STATE), 0) // SSM_GROUP
    col_g = lax.broadcasted_iota(jnp.int32, (LANES, CLUSTER_STATE), 1) // SSM_STATE
    same_group = row_g == col_g

    def power(l):
        m_l = jnp.exp(l * l_re)
        return m_l * jnp.cos(l * l_im), m_l * jnp.sin(l * l_im)

    def spread(v):
        return jnp.where(same_group, jnp.tile(v, (GROUPS_PER_CLUSTER, 1)), 0.0).astype(BF16)

    for s in range(SSM_BLOCK):
        p_re, p_im = power(float(SSM_BLOCK - 1 - s))
        rows = slice(s * LANES, (s + 1) * LANES)
        wend_ref[rows, :CLUSTER_STATE] = spread(p_re * bb_re - p_im * bb_im)
        wend_ref[rows, CLUSTER_STATE:] = spread(p_re * bb_im + p_im * bb_re)
        q_re, q_im = power(float(s + 1))
        wrd_ref[rows, :CLUSTER_STATE] = spread(q_re * ct_re - q_im * ct_im)
        wrd_ref[rows, CLUSTER_STATE:] = spread(-(q_re * ct_im + q_im * ct_re))

    wrd0 = jnp.concatenate([spread(ct_re), spread(-ct_im)], axis=1)
    lag_all = _dot_nt(wend_ref[...], wrd0).astype(BF16)
    for t in range(SSM_BLOCK):
        cols = slice(t * LANES, (t + 1) * LANES)
        top = (t + 1) * LANES
        lagk_ref[:top, cols] = lag_all[FLAT - top:, :]
        if top < FLAT:
            lagk_ref[top:, cols] = jnp.zeros((FLAT - top, LANES), BF16)

    p_re, p_im = power(float(SSM_BLOCK))
    pw_ref[0:1, :] = jnp.concatenate([p_re, p_im], axis=1)
    for _ in range(SCAN_SEG_LEN.bit_length() - 1):
        p_re, p_im = p_re * p_re - p_im * p_im, 2.0 * (p_re * p_im)
    pw_ref[1:2, :] = jnp.concatenate([p_re, p_im], axis=1)


def _cmul_add(p_re, p_im, h_re, h_im, e_re, e_im):
    n_re = [pr * hr - pi * hi + er for pr, pi, hr, hi, er in zip(p_re, p_im, h_re, h_im, e_re)]
    n_im = [pr * hi + pi * hr + ei for pr, pi, hr, hi, ei in zip(p_re, p_im, h_re, h_im, e_im)]
    return n_re, n_im


def _s5_block_recurrence(st_ref, pw_ref):
    half = SCAN_TILES // 2

    def bcast(row):
        return [jnp.broadcast_to(row[:, q * LANES:(q + 1) * LANES], (SCAN_SEGS, LANES))
                for q in range(half)]

    p_re, p_im = bcast(pw_ref[0:1, :CLUSTER_STATE]), bcast(pw_ref[0:1, CLUSTER_STATE:])
    seg_rows = lambda i: pl.ds(i, SCAN_SEGS, stride=SCAN_SEG_PITCH)

    def gather(i):
        e = [st_ref[q, seg_rows(i), :] for q in range(SCAN_TILES)]
        return e[:half], e[half:]

    zeros = [jnp.zeros((SCAN_SEGS, LANES), F32)] * half

    def end_step(i, h):
        e_re, e_im = gather(i)
        return _cmul_add(p_re, p_im, h[0], h[1], e_re, e_im)

    l_re, l_im = lax.fori_loop(0, SCAN_SEG_LEN, end_step, (zeros, zeros), unroll=SCAN_UNROLL)

    s_re, s_im = bcast(pw_ref[1:2, :CLUSTER_STATE]), bcast(pw_ref[1:2, CLUSTER_STATE:])
    first = lax.broadcasted_iota(jnp.int32, (SCAN_SEGS, LANES), 0) == 0
    shift = lambda v: jnp.where(first, 0.0, pltpu.roll(v, 1, 0))
    c_re, c_im = zeros, zeros
    for _ in range(SCAN_SEGS - 1):
        n_re, n_im = _cmul_add(s_re, s_im, c_re, c_im, l_re, l_im)
        c_re, c_im = [shift(v) for v in n_re], [shift(v) for v in n_im]

    def scan_step(i, h):
        e_re, e_im = gather(i)
        for q in range(half):
            st_ref[q, seg_rows(i), :] = h[0][q]
            st_ref[half + q, seg_rows(i), :] = h[1][q]
        return _cmul_add(p_re, p_im, h[0], h[1], e_re, e_im)

    lax.fori_loop(0, SCAN_SEG_LEN, scan_step, (c_re, c_im), unroll=SCAN_UNROLL)


def _s5_kernel(u_ref, par_ref, bt_ref, ct_ref, dskip_ref, g_ref,
               wend_ref, wrd_ref, lagk_ref, pw_ref, st_ref):
    r = pl.program_id(1)

    @pl.when(r == 0)
    def _():
        _s5_build_operators(par_ref, bt_ref, ct_ref, wend_ref, wrd_ref, lagk_ref, pw_ref)

    u = u_ref[0]
    ys = []
    for j in range(FLAT // MXU_WIDTH):
        kdim = (j + 1) * MXU_WIDTH
        ys.append(_dot(u[:, :kdim], lagk_ref[:kdim, j * MXU_WIDTH:(j + 1) * MXU_WIDTH]))
    y = jnp.concatenate(ys, axis=1)

    e = _dot(u, wend_ref[...])
    for q in range(SCAN_TILES):
        for j in range(SCAN_SEGS):
            st_ref[q, j * SCAN_SEG_PITCH:j * SCAN_SEG_PITCH + SCAN_SEG_LEN, :] = (
                e[j * SCAN_SEG_LEN:(j + 1) * SCAN_SEG_LEN, q * LANES:(q + 1) * LANES])
    _s5_block_recurrence(st_ref, pw_ref)
    hprev = jnp.concatenate(
        [jnp.concatenate([st_ref[q, j * SCAN_SEG_PITCH:j * SCAN_SEG_PITCH + SCAN_SEG_LEN, :]
                          for q in range(SCAN_TILES)], axis=1)
         for j in range(SCAN_SEGS)], axis=0)

    y = y + _dot_nt(hprev.astype(BF16), wrd_ref[...])
    y = y + dskip_ref[0] * u.astype(F32)
    g_ref[0] = _gelu(y)


def _s5(layer, u_flat, par, bt, ct, dskip, blocks_per_seq):
    nb = u_flat.shape[1]
    rt = S5_ROW_TILE
    assert blocks_per_seq == rt and SCAN_SEG_LEN & (SCAN_SEG_LEN - 1) == 0
    return pl.pallas_call(
        _s5_kernel,
        grid=(N_CLUSTERS, nb // rt),
        in_specs=[
            pl.BlockSpec((1, rt, FLAT), lambda k, r: (k, r, 0)),
            pl.BlockSpec((None, 1, 3, CLUSTER_STATE), lambda k, r: (layer, k, 0, 0)),
            pl.BlockSpec((None, 1, 2, SSM_GROUP, CLUSTER_STATE), lambda k, r: (layer, k, 0, 0, 0)),
            pl.BlockSpec((None, 1, 2, SSM_GROUP, CLUSTER_STATE), lambda k, r: (layer, k, 0, 0, 0)),
            pl.BlockSpec((None, 1, 1, FLAT), lambda k, r: (layer, k, 0, 0)),
        ],
        out_specs=pl.BlockSpec((1, rt, FLAT), lambda k, r: (k, r, 0)),
        out_shape=jax.ShapeDtypeStruct((N_CLUSTERS, nb, FLAT), F32),
        scratch_shapes=[pltpu.VMEM((FLAT, 2 * CLUSTER_STATE), BF16),
                        pltpu.VMEM((FLAT, 2 * CLUSTER_STATE), BF16),
                        pltpu.VMEM((FLAT, FLAT), BF16),
                        pltpu.VMEM((2, 2 * CLUSTER_STATE), F32),
                        pltpu.VMEM((SCAN_TILES, SCAN_SEGS * SCAN_SEG_PITCH, LANES), F32)],
        compiler_params=pltpu.CompilerParams(
            dimension_semantics=("arbitrary", "arbitrary"), vmem_limit_bytes=VMEM_LIMIT),
        name="s5",
    )(u_flat, par, bt, ct, dskip)


def _s5_params(A_re, A_im, log_dt, B_re, B_im, C_re, C_im, D_skip):
    depth = A_re.shape[0]
    k, gpc, n, c = N_CLUSTERS, GROUPS_PER_CLUSTER, SSM_STATE, SSM_GROUP
    rows = lambda a: a.reshape(depth, k, 1, CLUSTER_STATE)
    ldt = jnp.broadcast_to(log_dt[..., None], A_re.shape)
    par = jnp.concatenate([rows(A_re), rows(A_im), rows(ldt)], axis=2)
    b = jnp.stack([B_re, B_im], axis=1).reshape(depth, 2, k, gpc, n, c)
    bt = b.transpose(0, 2, 1, 5, 3, 4).reshape(depth, k, 2, c, CLUSTER_STATE)
    cc = jnp.stack([C_re, C_im], axis=1).reshape(depth, 2, k, gpc, c, n)
    ct = cc.transpose(0, 2, 1, 4, 3, 5).reshape(depth, k, 2, c, CLUSTER_STATE)
    dskip = jnp.tile(D_skip.reshape(depth, k, 1, LANES), (1, 1, 1, SSM_BLOCK))
    return par, bt, ct, dskip


def _mix_out_kernel(final, rider_kinds, *refs):
    n_r = len(rider_kinds)
    x_ref, g_ref, yb_ref, yc_ref, vec_ref, wglu_ref, wout_ref, wg_ref, wu_ref, wd_ref = refs[:10]
    rider_src = refs[10:10 + n_r]
    o_ref = refs[10 + n_r]
    rider_dst = refs[11 + n_r:11 + 2 * n_r]
    (gs_ref,) = refs[11 + 2 * n_r:]
    _cast_riders(rider_kinds, rider_src, rider_dst)

    bglu = vec_ref[VEC_GLU:VEC_GLU + 1, :D_SSM]
    gffn = vec_ref[VEC_GFFN:VEC_GFFN + 1, :]
    tm = x_ref.shape[0]
    for k in range(N_CLUSTERS):
        for s in range(SSM_BLOCK):
            gs_ref[k, pl.ds(s, tm // SSM_BLOCK, stride=SSM_BLOCK), :] = (
                g_ref[k, :, s * LANES:(s + 1) * LANES])
    g = jnp.concatenate([gs_ref[k] for k in range(N_CLUSTERS)], axis=1)
    ya = g * jax.nn.sigmoid(_dot(g.astype(BF16), wglu_ref[...]) + bglu)
    ymix = jnp.concatenate([ya.astype(BF16), yb_ref[...], yc_ref[...]], axis=1)
    x1 = x_ref[...] + _dot(ymix, wout_ref[...])
    h = _rms(x1, gffn).astype(BF16)
    acc = x1
    for c0, c1 in FF_CHUNKS:
        gate = _dot(h, wg_ref[:, c0:c1])
        up = _dot(h, wu_ref[:, c0:c1])
        act = (gate * jax.nn.sigmoid(gate) * up).astype(BF16)
        acc = acc + _dot(act, wd_ref[c0:c1, :])
    if final:
        acc = _rms(acc, vec_ref[VEC_GFINAL:VEC_GFINAL + 1, :])
    o_ref[...] = acc


def _mix_out(layer, x2d, g3, yb, yc, vecs, wglu, wout, wg, wu, wd, final, riders=()):
    m = x2d.shape[0]
    tm = TOKEN_TILE
    n_steps = m // tm
    d_ff = wg.shape[-1]
    resident = lambda shape: pl.BlockSpec(shape, lambda i: (0, 0), pipeline_mode=pl.Buffered(1))
    r_specs = [_rider_specs(n_steps, lyr, w) for _, w, lyr in riders]
    return pl.pallas_call(
        functools.partial(_mix_out_kernel, final, tuple(k for k, _, _ in riders)),
        grid=(n_steps,),
        in_specs=[
            pl.BlockSpec((tm, D_MODEL), lambda i: (i, 0)),
            pl.BlockSpec((N_CLUSTERS, tm // SSM_BLOCK, FLAT), lambda i: (0, i, 0)),
            pl.BlockSpec((tm, D_POOL), lambda i: (i, 0)),
            pl.BlockSpec((tm, D_SGU), lambda i: (i, 0)),
            _layer_spec(layer, (VEC_ROWS, D_MODEL)),
            _layer_spec(layer, (D_SSM, D_SSM)),
            resident((D_MODEL, D_MODEL)),
            resident((D_MODEL, d_ff)),
            resident((D_MODEL, d_ff)),
            resident((d_ff, D_MODEL)),
        ] + [s for s, _, _ in r_specs],
        out_specs=[pl.BlockSpec((tm, D_MODEL), lambda i: (i, 0))] + [d for _, d, _ in r_specs],
        out_shape=[jax.ShapeDtypeStruct((m, D_MODEL), F32)] + [o for _, _, o in r_specs],
        scratch_shapes=[pltpu.VMEM((N_CLUSTERS, tm, LANES), F32)],
        compiler_params=pltpu.CompilerParams(
            dimension_semantics=("arbitrary",), vmem_limit_bytes=VMEM_LIMIT),
        name="mix_out",
    )(x2d, g3, yb, yc, vecs, wglu, wout, wg, wu, wd, *[w for _, w, _ in riders])


def kernel(x, g_mix, w_in, A_re, A_im, log_dt, B_re, B_im, C_re, C_im, D_skip, w_glu, b_glu,
           w_pool, pool_scale, sgu_ln_g, sgu_ln_b, w_spatial, b_spatial, w_out, g_ffn,
           w_gate, w_up, w_down, g_final):
    bsz, seq, d = x.shape
    depth = w_in.shape[0]
    m = bsz * seq
    assert d == D_MODEL and seq % MIX_IN_TILE == 0 and MIX_IN_TILE % (2 * CHUNK) == 0
    assert m % TOKEN_TILE == 0
    assert (seq // SSM_BLOCK) % S5_ROW_TILE == 0
    x2d = x.reshape(m, D_MODEL)
    tril = jnp.tril(jnp.ones((CHUNK, CHUNK), dtype=bool))
    eye = jnp.eye(len(POOL_WINDOWS), dtype=F32)
    wpool_bd = jnp.einsum('dgij,gh->dgihj', w_pool, eye).reshape(depth, D_POOL, D_POOL).astype(BF16)
    ws = jnp.where(tril, w_spatial, 0.0).astype(BF16)
    ws = jnp.concatenate([ws[:, 0::2], ws[:, 1::2]], axis=-1)
    bsp = jnp.repeat(jnp.swapaxes(b_spatial, 1, 2), SGU_HEAD_DIM, axis=2)
    par, bt, ct, dskip = _s5_params(A_re, A_im, log_dt, B_re, B_im, C_re, C_im, D_skip)
    vec_rows = [None] * VEC_ROWS
    vec_rows[VEC_GMIX] = g_mix
    vec_rows[VEC_GFFN] = g_ffn
    vec_rows[VEC_SGU] = jnp.concatenate([pool_scale, sgu_ln_g, sgu_ln_b], axis=-1)
    vec_rows[VEC_GLU] = jnp.pad(b_glu, ((0, 0), (0, D_MODEL - D_SSM)))
    vec_rows[VEC_GFINAL] = jnp.broadcast_to(g_final, (depth, D_MODEL))
    zero_row = jnp.zeros((depth, D_MODEL), F32)
    vecs = jnp.stack([zero_row if r is None else r for r in vec_rows], axis=1)
    wglu = w_glu.astype(BF16)
    b_end, u_end = D_SSM + D_POOL, D_SSM + D_POOL + D_SGU
    win = jnp.concatenate([w_in[0, :, u_end:], w_in[0, :, b_end:u_end], w_in[0, :, :b_end]],
                          axis=-1).astype(BF16)
    ffn_stacks = (w_out, w_gate, w_up, w_down)
    ffn = None
    for l in range(depth):
        riders = tuple(("plain", w, 0) for w in ffn_stacks) if l == 0 else ()
        u_flat, yb, yc, *cast = _mix_in(l, x2d, vecs, win, wpool_bd, ws, bsp, seq, riders)
        if l == 0:
            ffn = cast
        g3 = _s5(l, u_flat, par, bt, ct, dskip, seq // SSM_BLOCK)
        last = l == depth - 1
        riders = () if last else (("w_in", w_in, l + 1),) + tuple(
            ("plain", w, l + 1) for w in ffn_stacks)
        x2d, *cast = _mix_out(l, x2d, g3, yb, yc, vecs, wglu, *ffn, final=last, riders=riders)
        if not last:
            win, ffn = cast[0], cast[1:]
    return x2d.reshape(bsz, seq, D_MODEL)
```

```python
import functools
import math

import jax
import jax.numpy as jnp
from jax import lax
from jax.experimental import pallas as pl
from jax.experimental.pallas import tpu as pltpu

D_MODEL = 1024
D_SSM = 384
SSM_GROUP = 16
N_SSM_GROUPS = D_SSM // SSM_GROUP
SSM_STATE = 64
POOL_WINDOWS = (2, 4, 8, 16)
POOL_GROUP = 64
D_POOL = len(POOL_WINDOWS) * POOL_GROUP
MAX_WINDOW = max(POOL_WINDOWS)
SGU_HEADS = 6
SGU_HEAD_DIM = 64
D_SGU = SGU_HEADS * SGU_HEAD_DIM
CHUNK = 128
D_IN = D_SSM + D_POOL + 2 * D_SGU
EPS = 1e-6

LANES = 128
SSM_BLOCK = 16
N_CLUSTERS = D_SSM // LANES
GROUPS_PER_CLUSTER = LANES // SSM_GROUP
CLUSTER_STATE = GROUPS_PER_CLUSTER * SSM_STATE
FLAT = SSM_BLOCK * LANES

MXU_WIDTH = 256
BF16_SUBLANES = 16
VEC_ROWS = 8
VEC_GMIX, VEC_GFFN, VEC_SGU, VEC_GLU, VEC_GFINAL = 0, 1, 2, 3, 4
TOKEN_TILE = 512
MIX_IN_TILE = 1024
S5_ROW_TILE = 512
SUBLANES = 8
SCAN_SEGS = SUBLANES
SCAN_SEG_LEN = S5_ROW_TILE // SCAN_SEGS
SCAN_SEG_PITCH = SCAN_SEG_LEN + SUBLANES
SCAN_TILES = 2 * CLUSTER_STATE // LANES
FF_CHUNKS = ((0, 768), (768, 1536), (1536, 2304), (2304, 2816))
VMEM_LIMIT = 56 * 1024 * 1024

F32 = jnp.float32
BF16 = jnp.bfloat16
NT_DIMS = (((1,), (1,)), ((), ()))


def _gelu(x):
    c = math.sqrt(2.0 / math.pi)
    return x * (0.5 + 0.5 * jnp.tanh(x * (c + (c * 0.044715) * (x * x))))


def _rms(x, g):
    ms = jnp.mean(x * x, axis=-1, keepdims=True)
    return x * lax.rsqrt(ms + EPS) * g


def _dot(a, b):
    return jnp.dot(a, b, preferred_element_type=F32)


def _dot_nt(a, b):
    return lax.dot_general(a, b, NT_DIMS, preferred_element_type=F32)


def _cast_riders(kinds, src_refs, dst_refs):
    for kind, src, dst in zip(kinds, src_refs, dst_refs):
        if kind == "w_in":
            b_end, u_end = D_SSM + D_POOL, D_SSM + D_POOL + D_SGU
            dst[:, :D_SGU] = src[:, u_end:].astype(BF16)
            dst[:, D_SGU:2 * D_SGU] = src[:, b_end:u_end].astype(BF16)
            dst[:, 2 * D_SGU:] = src[:, :b_end].astype(BF16)
        else:
            dst[...] = src[...].astype(BF16)


def _rider_specs(n_steps, layer, w):
    rows, cols = w.shape[1:]
    share = 1
    while (rows * share) % n_steps or (rows * share // n_steps) % BF16_SUBLANES:
        share *= 2
    slab = rows * share // n_steps
    which = lambda i: jnp.minimum(i, n_steps - 1) // share
    src = pl.BlockSpec((None, slab, cols), lambda i: (layer, which(i), 0))
    dst = pl.BlockSpec((slab, cols), lambda i: (which(i), 0))
    return src, dst, jax.ShapeDtypeStruct((rows, cols), BF16)


def _mix_in_kernel(tiles_per_seq, rider_kinds, *refs):
    n_r = len(rider_kinds)
    x_ref, vec_ref, win_ref, wpool_ref, ws_ref, bsp_ref = refs[:6]
    rider_src = refs[6:6 + n_r]
    za_ref, yb_ref, yc_ref = refs[6 + n_r:9 + n_r]
    rider_dst = refs[9 + n_r:9 + 2 * n_r]
    halo_ref, zs_ref = refs[9 + 2 * n_r:]
    _cast_riders(rider_kinds, rider_src, rider_dst)

    tm = x_ref.shape[0]
    i = pl.program_id(0)
    seq_tile = i % tiles_per_seq
    gmix = vec_ref[VEC_GMIX:VEC_GMIX + 1, :]
    pscale = vec_ref[VEC_SGU:VEC_SGU + 1, :D_POOL]
    lng = vec_ref[VEC_SGU:VEC_SGU + 1, D_POOL:D_POOL + D_SGU]
    lnb = vec_ref[VEC_SGU:VEC_SGU + 1, D_POOL + D_SGU:]

    h = _rms(x_ref[...], gmix).astype(BF16)
    z_sgu = _dot(h, win_ref[:, :2 * D_SGU])
    z_rest = _dot(h, win_ref[:, 2 * D_SGU:])
    zv = z_sgu[:, :D_SGU]
    zu = z_sgu[:, D_SGU:]
    zb = z_rest[:, D_SSM:]

    for k in range(N_CLUSTERS):
        zs_ref[k] = z_rest[:, k * LANES:(k + 1) * LANES]
    for k in range(N_CLUSTERS):
        for s in range(SSM_BLOCK):
            piece = zs_ref[k, pl.ds(s, tm // SSM_BLOCK, stride=SSM_BLOCK), :]
            za_ref[k, :, s * LANES:(s + 1) * LANES] = piece.astype(BF16)

    halo = jnp.where(seq_tile == 0, 0.0, halo_ref[...])
    halo_ref[...] = zb[tm - MAX_WINDOW:, :]
    ext = jnp.concatenate([halo, zb], axis=0)
    lane = lax.broadcasted_iota(jnp.int32, (tm, LANES), 1)
    low = lane < POOL_GROUP
    pos1 = (seq_tile * tm + 1 + lax.broadcasted_iota(jnp.int32, (tm, LANES), 0)).astype(F32)

    e0 = ext[:, :LANES]
    s2 = e0 + pltpu.roll(e0, 1, 0)
    s4 = s2 + pltpu.roll(s2, 2, 0)
    sum0 = jnp.where(low, s2[MAX_WINDOW:], s4[MAX_WINDOW:])
    cnt0 = jnp.where(low, jnp.minimum(pos1, 2.0), jnp.minimum(pos1, 4.0))
    e1 = ext[:, LANES:]
    t2 = e1 + pltpu.roll(e1, 1, 0)
    t4 = t2 + pltpu.roll(t2, 2, 0)
    t8 = t4 + pltpu.roll(t4, 4, 0)
    t16 = t8 + pltpu.roll(t8, 8, 0)
    sum1 = jnp.where(low, t8[MAX_WINDOW:], t16[MAX_WINDOW:])
    cnt1 = jnp.where(low, jnp.minimum(pos1, 8.0), jnp.minimum(pos1, 16.0))
    pooled = jnp.concatenate([sum0 / cnt0, sum1 / cnt1], axis=1) - zb
    yb = _dot(pooled.astype(BF16), wpool_ref[...]) * pscale
    yb_ref[...] = yb.astype(BF16)

    u = _gelu(zu)
    v = _gelu(zv)
    mu = jnp.mean(v, axis=-1, keepdims=True)
    vc = v - mu
    var = jnp.mean(vc * vc, axis=-1, keepdims=True)
    vn = (vc * lax.rsqrt(var + EPS) * lng + lnb).astype(BF16)
    lane_c = lax.broadcasted_iota(jnp.int32, (CHUNK, LANES), 1)
    low_c = lane_c < SGU_HEAD_DIM
    zero = jnp.zeros((), BF16)
    for c in range(0, tm // CHUNK, 2):
        rows_a = slice(c * CHUNK, (c + 1) * CHUNK)
        rows_b = slice((c + 1) * CHUNK, (c + 2) * CHUNK)
        parts_a, parts_b = [], []
        for p in range(SGU_HEADS // 2):
            va = vn[rows_a, p * LANES:(p + 1) * LANES]
            vb = vn[rows_b, p * LANES:(p + 1) * LANES]
            top = jnp.concatenate([jnp.where(low_c, va, zero), jnp.where(low_c, vb, zero)], axis=1)
            bot = jnp.concatenate([jnp.where(low_c, zero, va), jnp.where(low_c, zero, vb)], axis=1)
            out = _dot(ws_ref[p], jnp.concatenate([top, bot], axis=0))
            parts_a.append(out[:, :LANES])
            parts_b.append(out[:, LANES:])
        for rows, parts in ((rows_a, parts_a), (rows_b, parts_b)):
            mixed = jnp.concatenate(parts, axis=1) + bsp_ref[...]
            yc_ref[rows, :] = (u[rows, :] * mixed).astype(BF16)


def _layer_spec(layer, shape):
    zeros = (0,) * len(shape)
    return pl.BlockSpec((None,) + tuple(shape), lambda *_: (layer,) + zeros)


def _mix_in(layer, x2d, vecs, win, wpool_bd, ws, bsp, seq_len, riders=()):
    m = x2d.shape[0]
    tm = MIX_IN_TILE
    n_steps = m // tm
    spec = functools.partial(_layer_spec, layer)
    r_specs = [_rider_specs(n_steps, lyr, w) for _, w, lyr in riders]
    return pl.pallas_call(
        functools.partial(_mix_in_kernel, seq_len // tm, tuple(k for k, _, _ in riders)),
        grid=(n_steps,),
        in_specs=[
            pl.BlockSpec((tm, D_MODEL), lambda i: (i, 0)),
            spec((VEC_ROWS, D_MODEL)),
            pl.BlockSpec((D_MODEL, D_IN), lambda i: (0, 0)),
            spec((D_POOL, D_POOL)),
            spec((SGU_HEADS // 2, CHUNK, 2 * CHUNK)),
            spec((CHUNK, D_SGU)),
        ] + [s for s, _, _ in r_specs],
        out_specs=[
            pl.BlockSpec((N_CLUSTERS, tm // SSM_BLOCK, FLAT), lambda i: (0, i, 0)),
            pl.BlockSpec((tm, D_POOL), lambda i: (i, 0)),
            pl.BlockSpec((tm, D_SGU), lambda i: (i, 0)),
        ] + [d for _, d, _ in r_specs],
        out_shape=[
            jax.ShapeDtypeStruct((N_CLUSTERS, m // SSM_BLOCK, FLAT), BF16),
            jax.ShapeDtypeStruct((m, D_POOL), BF16),
            jax.ShapeDtypeStruct((m, D_SGU), BF16),
        ] + [o for _, _, o in r_specs],
        scratch_shapes=[pltpu.VMEM((MAX_WINDOW, D_POOL), F32),
                        pltpu.VMEM((N_CLUSTERS, tm, LANES), F32)],
        compiler_params=pltpu.CompilerParams(
            dimension_semantics=("arbitrary",), vmem_limit_bytes=VMEM_LIMIT),
        name="mix_in",
    )(x2d, vecs, win, wpool_bd, ws, bsp, *[w for _, w, _ in riders])


def _s5_build_operators(par_ref, bt_ref, ct_ref, wend_ref, wrd_ref, lagk_ref, pw_ref):
    a_re = par_ref[0, 0:1, :]
    a_im = par_ref[0, 1:2, :]
    dt = jnp.exp(par_ref[0, 2:3, :])
    l_re = a_re * dt
    l_im = a_im * dt
    mag = jnp.exp(l_re)
    ar = mag * jnp.cos(l_im)
    ai = mag * jnp.sin(l_im)
    den = a_re * a_re + a_im * a_im
    f_re = ((ar - 1.0) * a_re + ai * a_im) / den
    f_im = (ai * a_re - (ar - 1.0) * a_im) / den
    bt_re, bt_im = bt_ref[0, 0], bt_ref[0, 1]
    bb_re = f_re * bt_re - f_im * bt_im
    bb_im = f_re * bt_im + f_im * bt_re
    ct_re, ct_im = ct_ref[0, 0], ct_ref[0, 1]

    row_g = lax.broadcasted_iota(jnp.int32, (LANES, CLUSTER_STATE), 0) // SSM_GROUP
    col_g = lax.broadcasted_iota(jnp.int32, (LANES, CLUSTER_STATE), 1) // SSM_STATE
    same_group = row_g == col_g

    def power(l):
        m_l = jnp.exp(l * l_re)
        return m_l * jnp.cos(l * l_im), m_l * jnp.sin(l * l_im)

    def spread(v):
        return jnp.where(same_group, jnp.tile(v, (GROUPS_PER_CLUSTER, 1)), 0.0).astype(BF16)

    for s in range(SSM_BLOCK):
        p_re, p_im = power(float(SSM_BLOCK - 1 - s))
        rows = slice(s * LANES, (s + 1) * LANES)
        wend_ref[rows, :CLUSTER_STATE] = spread(p_re * bb_re - p_im * bb_im)
        wend_ref[rows, CLUSTER_STATE:] = spread(p_re * bb_im + p_im * bb_re)
        q_re, q_im = power(float(s + 1))
        wrd_ref[rows, :CLUSTER_STATE] = spread(q_re * ct_re - q_im * ct_im)
        wrd_ref[rows, CLUSTER_STATE:] = spread(-(q_re * ct_im + q_im * ct_re))

    wrd0 = jnp.concatenate([spread(ct_re), spread(-ct_im)], axis=1)
    lag_all = _dot_nt(wend_ref[...], wrd0).astype(BF16)
    for t in range(SSM_BLOCK):
        cols = slice(t * LANES, (t + 1) * LANES)
        top = (t + 1) * LANES
        lagk_ref[:top, cols] = lag_all[FLAT - top:, :]
        if top < FLAT:
            lagk_ref[top:, cols] = jnp.zeros((FLAT - top, LANES), BF16)

    p_re, p_im = power(float(SSM_BLOCK))
    pw_ref[0:1, :] = jnp.concatenate([p_re, p_im], axis=1)
    for _ in range(SCAN_SEG_LEN.bit_length() - 1):
        p_re, p_im = p_re * p_re - p_im * p_im, 2.0 * (p_re * p_im)
    pw_ref[1:2, :] = jnp.concatenate([p_re, p_im], axis=1)


def _cmul_add(p_re, p_im, h_re, h_im, e_re, e_im):
    n_re = [pr * hr - pi * hi + er for pr, pi, hr, hi, er in zip(p_re, p_im, h_re, h_im, e_re)]
    n_im = [pr * hi + pi * hr + ei for pr, pi, hr, hi, ei in zip(p_re, p_im, h_re, h_im, e_im)]
    return n_re, n_im


def _s5_block_recurrence(st_ref, pw_ref, fillers):
    fillers = list(fillers)
    total_cost = sum(c for c, _ in fillers)
    issued_cost = [0.0]
    ticks = [0]

    def tick():
        while fillers and issued_cost[0] * 2 * SCAN_SEG_LEN <= ticks[0] * total_cost:
            cost, thunk = fillers.pop(0)
            issued_cost[0] += cost
            thunk()
        ticks[0] += 1

    half = SCAN_TILES // 2

    def bcast(row):
        return [jnp.broadcast_to(row[:, q * LANES:(q + 1) * LANES], (SCAN_SEGS, LANES))
                for q in range(half)]

    p_re, p_im = bcast(pw_ref[0:1, :CLUSTER_STATE]), bcast(pw_ref[0:1, CLUSTER_STATE:])
    seg_rows = lambda i: pl.ds(i, SCAN_SEGS, stride=SCAN_SEG_PITCH)

    def gather(i):
        e = [st_ref[q, seg_rows(i), :] for q in range(SCAN_TILES)]
        return e[:half], e[half:]

    zeros = [jnp.zeros((SCAN_SEGS, LANES), F32)] * half

    def end_step(i, h):
        e_re, e_im = gather(i)
        return _cmul_add(p_re, p_im, h[0], h[1], e_re, e_im)

    h = (zeros, zeros)
    for i in range(SCAN_SEG_LEN):
        tick()
        h = end_step(i, h)
    l_re, l_im = h

    s_re, s_im = bcast(pw_ref[1:2, :CLUSTER_STATE]), bcast(pw_ref[1:2, CLUSTER_STATE:])
    first = lax.broadcasted_iota(jnp.int32, (SCAN_SEGS, LANES), 0) == 0
    shift = lambda v: jnp.where(first, 0.0, pltpu.roll(v, 1, 0))
    c_re, c_im = zeros, zeros
    for _ in range(SCAN_SEGS - 1):
        n_re, n_im = _cmul_add(s_re, s_im, c_re, c_im, l_re, l_im)
        c_re, c_im = [shift(v) for v in n_re], [shift(v) for v in n_im]

    def scan_step(i, h):
        e_re, e_im = gather(i)
        for q in range(half):
            st_ref[q, seg_rows(i), :] = h[0][q]
            st_ref[half + q, seg_rows(i), :] = h[1][q]
        return _cmul_add(p_re, p_im, h[0], h[1], e_re, e_im)

    h = (c_re, c_im)
    for i in range(SCAN_SEG_LEN):
        tick()
        h = scan_step(i, h)
    while fillers:
        fillers.pop(0)[1]()


def _s5_kernel(u_ref, par_ref, bt_ref, ct_ref, dskip_ref, g_ref,
               wend_ref, wrd_ref, lagk_ref, pw_ref, st_ref):
    r = pl.program_id(1)

    @pl.when(r == 0)
    def _():
        _s5_build_operators(par_ref, bt_ref, ct_ref, wend_ref, wrd_ref, lagk_ref, pw_ref)

    u = u_ref[0]
    e = _dot(u, wend_ref[...])
    for q in range(SCAN_TILES):
        for j in range(SCAN_SEGS):
            st_ref[q, j * SCAN_SEG_PITCH:j * SCAN_SEG_PITCH + SCAN_SEG_LEN, :] = (
                e[j * SCAN_SEG_LEN:(j + 1) * SCAN_SEG_LEN, q * LANES:(q + 1) * LANES])

    ys = [None] * (FLAT // MXU_WIDTH)

    def lag_product(j):
        kdim = (j + 1) * MXU_WIDTH
        ys[j] = _dot(u[:, :kdim], lagk_ref[:kdim, j * MXU_WIDTH:(j + 1) * MXU_WIDTH])

    _s5_block_recurrence(st_ref, pw_ref,
                         [(j + 1, functools.partial(lag_product, j)) for j in range(len(ys))])
    y = jnp.concatenate(ys, axis=1)
    hprev = jnp.concatenate(
        [jnp.concatenate([st_ref[q, j * SCAN_SEG_PITCH:j * SCAN_SEG_PITCH + SCAN_SEG_LEN, :]
                          for q in range(SCAN_TILES)], axis=1)
         for j in range(SCAN_SEGS)], axis=0)

    y = y + _dot_nt(hprev.astype(BF16), wrd_ref[...])
    y = y + dskip_ref[0] * u.astype(F32)
    g_ref[0] = _gelu(y)


def _s5(layer, u_flat, par, bt, ct, dskip, blocks_per_seq):
    nb = u_flat.shape[1]
    rt = S5_ROW_TILE
    assert blocks_per_seq == rt and SCAN_SEG_LEN & (SCAN_SEG_LEN - 1) == 0
    return pl.pallas_call(
        _s5_kernel,
        grid=(N_CLUSTERS, nb // rt),
        in_specs=[
            pl.BlockSpec((1, rt, FLAT), lambda k, r: (k, r, 0)),
            pl.BlockSpec((None, 1, 3, CLUSTER_STATE), lambda k, r: (layer, k, 0, 0)),
            pl.BlockSpec((None, 1, 2, SSM_GROUP, CLUSTER_STATE), lambda k, r: (layer, k, 0, 0, 0)),
            pl.BlockSpec((None, 1, 2, SSM_GROUP, CLUSTER_STATE), lambda k, r: (layer, k, 0, 0, 0)),
            pl.BlockSpec((None, 1, 1, FLAT), lambda k, r: (layer, k, 0, 0)),
        ],
        out_specs=pl.BlockSpec((1, rt, FLAT), lambda k, r: (k, r, 0)),
        out_shape=jax.ShapeDtypeStruct((N_CLUSTERS, nb, FLAT), F32),
        scratch_shapes=[pltpu.VMEM((FLAT, 2 * CLUSTER_STATE), BF16),
                        pltpu.VMEM((FLAT, 2 * CLUSTER_STATE), BF16),
                        pltpu.VMEM((FLAT, FLAT), BF16),
                        pltpu.VMEM((2, 2 * CLUSTER_STATE), F32),
                        pltpu.VMEM((SCAN_TILES, SCAN_SEGS * SCAN_SEG_PITCH, LANES), F32)],
        compiler_params=pltpu.CompilerParams(
            dimension_semantics=("arbitrary", "arbitrary"), vmem_limit_bytes=VMEM_LIMIT),
        name="s5",
    )(u_flat, par, bt, ct, dskip)


def _s5_params(A_re, A_im, log_dt, B_re, B_im, C_re, C_im, D_skip):
    depth = A_re.shape[0]
    k, gpc, n, c = N_CLUSTERS, GROUPS_PER_CLUSTER, SSM_STATE, SSM_GROUP
    rows = lambda a: a.reshape(depth, k, 1, CLUSTER_STATE)
    ldt = jnp.broadcast_to(log_dt[..., None], A_re.shape)
    par = jnp.concatenate([rows(A_re), rows(A_im), rows(ldt)], axis=2)
    b = jnp.stack([B_re, B_im], axis=1).reshape(depth, 2, k, gpc, n, c)
    bt = b.transpose(0, 2, 1, 5, 3, 4).reshape(depth, k, 2, c, CLUSTER_STATE)
    cc = jnp.stack([C_re, C_im], axis=1).reshape(depth, 2, k, gpc, c, n)
    ct = cc.transpose(0, 2, 1, 4, 3, 5).reshape(depth, k, 2, c, CLUSTER_STATE)
    dskip = jnp.tile(D_skip.reshape(depth, k, 1, LANES), (1, 1, 1, SSM_BLOCK))
    return par, bt, ct, dskip


def _mix_out_kernel(final, rider_kinds, *refs):
    n_r = len(rider_kinds)
    x_ref, g_ref, yb_ref, yc_ref, vec_ref, wglu_ref, wout_ref, wg_ref, wu_ref, wd_ref = refs[:10]
    rider_src = refs[10:10 + n_r]
    o_ref = refs[10 + n_r]
    rider_dst = refs[11 + n_r:11 + 2 * n_r]
    (gs_ref,) = refs[11 + 2 * n_r:]
    _cast_riders(rider_kinds, rider_src, rider_dst)

    bglu = vec_ref[VEC_GLU:VEC_GLU + 1, :D_SSM]
    gffn = vec_ref[VEC_GFFN:VEC_GFFN + 1, :]
    tm = x_ref.shape[0]
    for k in range(N_CLUSTERS):
        for s in range(SSM_BLOCK):
            gs_ref[k, pl.ds(s, tm // SSM_BLOCK, stride=SSM_BLOCK), :] = (
                g_ref[k, :, s * LANES:(s + 1) * LANES])
    g = jnp.concatenate([gs_ref[k] for k in range(N_CLUSTERS)], axis=1)
    ya = g * jax.nn.sigmoid(_dot(g.astype(BF16), wglu_ref[...]) + bglu)
    ymix = jnp.concatenate([ya.astype(BF16), yb_ref[...], yc_ref[...]], axis=1)
    x1 = x_ref[...] + _dot(ymix, wout_ref[...])
    h = _rms(x1, gffn).astype(BF16)
    acc = x1
    for c0, c1 in FF_CHUNKS:
        gate = _dot(h, wg_ref[:, c0:c1])
        up = _dot(h, wu_ref[:, c0:c1])
        act = (gate * jax.nn.sigmoid(gate) * up).astype(BF16)
        acc = acc + _dot(act, wd_ref[c0:c1, :])
    if final:
        acc = _rms(acc, vec_ref[VEC_GFINAL:VEC_GFINAL + 1, :])
    o_ref[...] = acc


def _mix_out(layer, x2d, g3, yb, yc, vecs, wglu, wout, wg, wu, wd, final, riders=()):
    m = x2d.shape[0]
    tm = TOKEN_TILE
    n_steps = m // tm
    d_ff = wg.shape[-1]
    resident = lambda shape: pl.BlockSpec(shape, lambda i: (0, 0), pipeline_mode=pl.Buffered(1))
    r_specs = [_rider_specs(n_steps, lyr, w) for _, w, lyr in riders]
    return pl.pallas_call(
        functools.partial(_mix_out_kernel, final, tuple(k for k, _, _ in riders)),
        grid=(n_steps,),
        in_specs=[
            pl.BlockSpec((tm, D_MODEL), lambda i: (i, 0)),
            pl.BlockSpec((N_CLUSTERS, tm // SSM_BLOCK, FLAT), lambda i: (0, i, 0)),
            pl.BlockSpec((tm, D_POOL), lambda i: (i, 0)),
            pl.BlockSpec((tm, D_SGU), lambda i: (i, 0)),
            _layer_spec(layer, (VEC_ROWS, D_MODEL)),
            _layer_spec(layer, (D_SSM, D_SSM)),
            resident((D_MODEL, D_MODEL)),
            resident((D_MODEL, d_ff)),
            resident((D_MODEL, d_ff)),
            resident((d_ff, D_MODEL)),
        ] + [s for s, _, _ in r_specs],
        out_specs=[pl.BlockSpec((tm, D_MODEL), lambda i: (i, 0))] + [d for _, d, _ in r_specs],
        out_shape=[jax.ShapeDtypeStruct((m, D_MODEL), F32)] + [o for _, _, o in r_specs],
        scratch_shapes=[pltpu.VMEM((N_CLUSTERS, tm, LANES), F32)],
        compiler_params=pltpu.CompilerParams(
            dimension_semantics=("arbitrary",), vmem_limit_bytes=VMEM_LIMIT),
        name="mix_out",
    )(x2d, g3, yb, yc, vecs, wglu, wout, wg, wu, wd, *[w for _, w, _ in riders])


def kernel(x, g_mix, w_in, A_re, A_im, log_dt, B_re, B_im, C_re, C_im, D_skip, w_glu, b_glu,
           w_pool, pool_scale, sgu_ln_g, sgu_ln_b, w_spatial, b_spatial, w_out, g_ffn,
           w_gate, w_up, w_down, g_final):
    bsz, seq, d = x.shape
    depth = w_in.shape[0]
    m = bsz * seq
    assert d == D_MODEL and seq % MIX_IN_TILE == 0 and MIX_IN_TILE % (2 * CHUNK) == 0
    assert m % TOKEN_TILE == 0
    assert (seq // SSM_BLOCK) % S5_ROW_TILE == 0
    x2d = x.reshape(m, D_MODEL)
    tril = jnp.tril(jnp.ones((CHUNK, CHUNK), dtype=bool))
    eye = jnp.eye(len(POOL_WINDOWS), dtype=F32)
    wpool_bd = jnp.einsum('dgij,gh->dgihj', w_pool, eye).reshape(depth, D_POOL, D_POOL).astype(BF16)
    ws = jnp.where(tril, w_spatial, 0.0).astype(BF16)
    ws = jnp.concatenate([ws[:, 0::2], ws[:, 1::2]], axis=-1)
    bsp = jnp.repeat(jnp.swapaxes(b_spatial, 1, 2), SGU_HEAD_DIM, axis=2)
    par, bt, ct, dskip = _s5_params(A_re, A_im, log_dt, B_re, B_im, C_re, C_im, D_skip)
    vec_rows = [None] * VEC_ROWS
    vec_rows[VEC_GMIX] = g_mix
    vec_rows[VEC_GFFN] = g_ffn
    vec_rows[VEC_SGU] = jnp.concatenate([pool_scale, sgu_ln_g, sgu_ln_b], axis=-1)
    vec_rows[VEC_GLU] = jnp.pad(b_glu, ((0, 0), (0, D_MODEL - D_SSM)))
    vec_rows[VEC_GFINAL] = jnp.broadcast_to(g_final, (depth, D_MODEL))
    zero_row = jnp.zeros((depth, D_MODEL), F32)
    vecs = jnp.stack([zero_row if r is None else r for r in vec_rows], axis=1)
    wglu = w_glu.astype(BF16)
    b_end, u_end = D_SSM + D_POOL, D_SSM + D_POOL + D_SGU
    win = jnp.concatenate([w_in[0, :, u_end:], w_in[0, :, b_end:u_end], w_in[0, :, :b_end]],
                          axis=-1).astype(BF16)
    ffn_stacks = (w_out, w_gate, w_up, w_down)
    ffn = None
    for l in range(depth):
        riders = tuple(("plain", w, 0) for w in ffn_stacks) if l == 0 else ()
        u_flat, yb, yc, *cast = _mix_in(l, x2d, vecs, win, wpool_bd, ws, bsp, seq, riders)
        if l == 0:
            ffn = cast
        g3 = _s5(l, u_flat, par, bt, ct, dskip, seq // SSM_BLOCK)
        last = l == depth - 1
        riders = () if last else (("w_in", w_in, l + 1),) + tuple(
            ("plain", w, l + 1) for w in ffn_stacks)
        x2d, *cast = _mix_out(l, x2d, g3, yb, yc, vecs, wglu, *ffn, final=last, riders=riders)
        if not last:
            win, ffn = cast[0], cast[1:]
    return x2d.reshape(bsz, seq, D_MODEL)
```

```python
import functools
import math

import jax
import jax.numpy as jnp
from jax import lax
from jax.experimental import pallas as pl
from jax.experimental.pallas import tpu as pltpu

D_MODEL = 1024
D_SSM = 384
SSM_GROUP = 16
N_SSM_GROUPS = D_SSM // SSM_GROUP
SSM_STATE = 64
POOL_WINDOWS = (2, 4, 8, 16)
POOL_GROUP = 64
D_POOL = len(POOL_WINDOWS) * POOL_GROUP
MAX_WINDOW = max(POOL_WINDOWS)
SGU_HEADS = 6
SGU_HEAD_DIM = 64
D_SGU = SGU_HEADS * SGU_HEAD_DIM
CHUNK = 128
D_IN = D_SSM + D_POOL + 2 * D_SGU
EPS = 1e-6

LANES = 128
SSM_BLOCK = 16
N_CLUSTERS = D_SSM // LANES
GROUPS_PER_CLUSTER = LANES // SSM_GROUP
CLUSTER_STATE = GROUPS_PER_CLUSTER * SSM_STATE
FLAT = SSM_BLOCK * LANES

MXU_WIDTH = 256
BF16_SUBLANES = 16
VEC_ROWS = 8
VEC_GMIX, VEC_GFFN, VEC_SGU, VEC_GLU, VEC_GFINAL = 0, 1, 2, 3, 4
TOKEN_TILE = 1024
MIX_IN_TILE = 1024
S5_ROW_TILE = 512
SUBLANES = 8
SCAN_SEGS = SUBLANES
SCAN_SEG_LEN = S5_ROW_TILE // SCAN_SEGS
SCAN_SEG_PITCH = SCAN_SEG_LEN + SUBLANES
SCAN_TILES = 2 * CLUSTER_STATE // LANES
FF_CHUNKS = ((0, 768), (768, 1536), (1536, 2304), (2304, 2816))
VMEM_LIMIT = 64 * 1024 * 1024

F32 = jnp.float32
BF16 = jnp.bfloat16
NT_DIMS = (((1,), (1,)), ((), ()))


def _gelu(x):
    c = math.sqrt(2.0 / math.pi)
    return x * (0.5 + 0.5 * jnp.tanh(x * (c + (c * 0.044715) * (x * x))))


def _rms(x, g):
    ms = jnp.mean(x * x, axis=-1, keepdims=True)
    return x * lax.rsqrt(ms + EPS) * g


def _dot(a, b):
    return jnp.dot(a, b, preferred_element_type=F32)


def _dot_nt(a, b):
    return lax.dot_general(a, b, NT_DIMS, preferred_element_type=F32)


def _cast_riders(kinds, src_refs, dst_refs):
    for kind, src, dst in zip(kinds, src_refs, dst_refs):
        if kind == "w_in":
            b_end, u_end = D_SSM + D_POOL, D_SSM + D_POOL + D_SGU
            dst[:, :D_SGU] = src[:, u_end:].astype(BF16)
            dst[:, D_SGU:2 * D_SGU] = src[:, b_end:u_end].astype(BF16)
            dst[:, 2 * D_SGU:] = src[:, :b_end].astype(BF16)
        else:
            dst[...] = src[...].astype(BF16)


def _rider_specs(n_steps, layer, w):
    rows, cols = w.shape[1:]
    share = 1
    while (rows * share) % n_steps or (rows * share // n_steps) % BF16_SUBLANES:
        share *= 2
    slab = rows * share // n_steps
    which = lambda i: jnp.minimum(i, n_steps - 1) // share
    src = pl.BlockSpec((None, slab, cols), lambda i: (layer, which(i), 0))
    dst = pl.BlockSpec((slab, cols), lambda i: (which(i), 0))
    return src, dst, jax.ShapeDtypeStruct((rows, cols), BF16)


def _mix_in_kernel(tiles_per_seq, rider_kinds, *refs):
    n_r = len(rider_kinds)
    x_ref, vec_ref, win_ref, wpool_ref, ws_ref, bsp_ref = refs[:6]
    rider_src = refs[6:6 + n_r]
    za_ref, yb_ref, yc_ref = refs[6 + n_r:9 + n_r]
    rider_dst = refs[9 + n_r:9 + 2 * n_r]
    halo_ref, zs_ref = refs[9 + 2 * n_r:]
    _cast_riders(rider_kinds, rider_src, rider_dst)

    tm = x_ref.shape[0]
    i = pl.program_id(0)
    seq_tile = i % tiles_per_seq
    gmix = vec_ref[VEC_GMIX:VEC_GMIX + 1, :]
    pscale = vec_ref[VEC_SGU:VEC_SGU + 1, :D_POOL]
    lng = vec_ref[VEC_SGU:VEC_SGU + 1, D_POOL:D_POOL + D_SGU]
    lnb = vec_ref[VEC_SGU:VEC_SGU + 1, D_POOL + D_SGU:]

    h = _rms(x_ref[...], gmix).astype(BF16)
    z_sgu = _dot(h, win_ref[:, :2 * D_SGU])
    z_rest = _dot(h, win_ref[:, 2 * D_SGU:])
    zv = z_sgu[:, :D_SGU]
    zu = z_sgu[:, D_SGU:]
    zb = z_rest[:, D_SSM:]

    for k in range(N_CLUSTERS):
        zs_ref[k] = z_rest[:, k * LANES:(k + 1) * LANES]
    for k in range(N_CLUSTERS):
        for s in range(SSM_BLOCK):
            piece = zs_ref[k, pl.ds(s, tm // SSM_BLOCK, stride=SSM_BLOCK), :]
            za_ref[k, :, s * LANES:(s + 1) * LANES] = piece.astype(BF16)

    halo = jnp.where(seq_tile == 0, 0.0, halo_ref[...])
    halo_ref[...] = zb[tm - MAX_WINDOW:, :]
    ext = jnp.concatenate([halo, zb], axis=0)
    lane = lax.broadcasted_iota(jnp.int32, (tm, LANES), 1)
    low = lane < POOL_GROUP
    pos1 = (seq_tile * tm + 1 + lax.broadcasted_iota(jnp.int32, (tm, LANES), 0)).astype(F32)

    e0 = ext[:, :LANES]
    s2 = e0 + pltpu.roll(e0, 1, 0)
    s4 = s2 + pltpu.roll(s2, 2, 0)
    sum0 = jnp.where(low, s2[MAX_WINDOW:], s4[MAX_WINDOW:])
    cnt0 = jnp.where(low, jnp.minimum(pos1, 2.0), jnp.minimum(pos1, 4.0))
    e1 = ext[:, LANES:]
    t2 = e1 + pltpu.roll(e1, 1, 0)
    t4 = t2 + pltpu.roll(t2, 2, 0)
    t8 = t4 + pltpu.roll(t4, 4, 0)
    t16 = t8 + pltpu.roll(t8, 8, 0)
    sum1 = jnp.where(low, t8[MAX_WINDOW:], t16[MAX_WINDOW:])
    cnt1 = jnp.where(low, jnp.minimum(pos1, 8.0), jnp.minimum(pos1, 16.0))
    pooled = jnp.concatenate([sum0 / cnt0, sum1 / cnt1], axis=1) - zb
    yb = _dot(pooled.astype(BF16), wpool_ref[...]) * pscale
    yb_ref[...] = yb.astype(BF16)

    u = _gelu(zu)
    v = _gelu(zv)
    mu = jnp.mean(v, axis=-1, keepdims=True)
    vc = v - mu
    var = jnp.mean(vc * vc, axis=-1, keepdims=True)
    vn = (vc * lax.rsqrt(var + EPS) * lng + lnb).astype(BF16)
    lane_c = lax.broadcasted_iota(jnp.int32, (CHUNK, LANES), 1)
    low_c = lane_c < SGU_HEAD_DIM
    zero = jnp.zeros((), BF16)
    for c in range(0, tm // CHUNK, 2):
        rows_a = slice(c * CHUNK, (c + 1) * CHUNK)
        rows_b = slice((c + 1) * CHUNK, (c + 2) * CHUNK)
        parts_a, parts_b = [], []
        for p in range(SGU_HEADS // 2):
            va = vn[rows_a, p * LANES:(p + 1) * LANES]
            vb = vn[rows_b, p * LANES:(p + 1) * LANES]
            top = jnp.concatenate([jnp.where(low_c, va, zero), jnp.where(low_c, vb, zero)], axis=1)
            bot = jnp.concatenate([jnp.where(low_c, zero, va), jnp.where(low_c, zero, vb)], axis=1)
            out = _dot(ws_ref[p], jnp.concatenate([top, bot], axis=0))
            parts_a.append(out[:, :LANES])
            parts_b.append(out[:, LANES:])
        for rows, parts in ((rows_a, parts_a), (rows_b, parts_b)):
            mixed = jnp.concatenate(parts, axis=1) + bsp_ref[...]
            yc_ref[rows, :] = (u[rows, :] * mixed).astype(BF16)


def _layer_spec(layer, shape):
    zeros = (0,) * len(shape)
    return pl.BlockSpec((None,) + tuple(shape), lambda *_: (layer,) + zeros)


def _mix_in(layer, x2d, vecs, win, wpool_bd, ws, bsp, seq_len, riders=()):
    m = x2d.shape[0]
    tm = MIX_IN_TILE
    n_steps = m // tm
    spec = functools.partial(_layer_spec, layer)
    r_specs = [_rider_specs(n_steps, lyr, w) for _, w, lyr in riders]
    return pl.pallas_call(
        functools.partial(_mix_in_kernel, seq_len // tm, tuple(k for k, _, _ in riders)),
        grid=(n_steps,),
        in_specs=[
            pl.BlockSpec((tm, D_MODEL), lambda i: (i, 0)),
            spec((VEC_ROWS, D_MODEL)),
            pl.BlockSpec((D_MODEL, D_IN), lambda i: (0, 0)),
            spec((D_POOL, D_POOL)),
            spec((SGU_HEADS // 2, CHUNK, 2 * CHUNK)),
            spec((CHUNK, D_SGU)),
        ] + [s for s, _, _ in r_specs],
        out_specs=[
            pl.BlockSpec((N_CLUSTERS, tm // SSM_BLOCK, FLAT), lambda i: (0, i, 0)),
            pl.BlockSpec((tm, D_POOL), lambda i: (i, 0)),
            pl.BlockSpec((tm, D_SGU), lambda i: (i, 0)),
        ] + [d for _, d, _ in r_specs],
        out_shape=[
            jax.ShapeDtypeStruct((N_CLUSTERS, m // SSM_BLOCK, FLAT), BF16),
            jax.ShapeDtypeStruct((m, D_POOL), BF16),
            jax.ShapeDtypeStruct((m, D_SGU), BF16),
        ] + [o for _, _, o in r_specs],
        scratch_shapes=[pltpu.VMEM((MAX_WINDOW, D_POOL), F32),
                        pltpu.VMEM((N_CLUSTERS, tm, LANES), F32)],
        compiler_params=pltpu.CompilerParams(
            dimension_semantics=("arbitrary",), vmem_limit_bytes=VMEM_LIMIT),
        name="mix_in",
    )(x2d, vecs, win, wpool_bd, ws, bsp, *[w for _, w, _ in riders])


def _s5_build_operators(par_ref, bt_ref, ct_ref, wend_ref, wrd_ref, lagk_ref, pw_ref):
    a_re = par_ref[0, 0:1, :]
    a_im = par_ref[0, 1:2, :]
    dt = jnp.exp(par_ref[0, 2:3, :])
    l_re = a_re * dt
    l_im = a_im * dt
    mag = jnp.exp(l_re)
    ar = mag * jnp.cos(l_im)
    ai = mag * jnp.sin(l_im)
    den = a_re * a_re + a_im * a_im
    f_re = ((ar - 1.0) * a_re + ai * a_im) / den
    f_im = (ai * a_re - (ar - 1.0) * a_im) / den
    bt_re, bt_im = bt_ref[0, 0], bt_ref[0, 1]
    bb_re = f_re * bt_re - f_im * bt_im
    bb_im = f_re * bt_im + f_im * bt_re
    ct_re, ct_im = ct_ref[0, 0], ct_ref[0, 1]

    row_g = lax.broadcasted_iota(jnp.int32, (LANES, CLUSTER_STATE), 0) // SSM_GROUP
    col_g = lax.broadcasted_iota(jnp.int32, (LANES, CLUSTER_STATE), 1) // SSM_STATE
    same_group = row_g == col_g

    def power(l):
        m_l = jnp.exp(l * l_re)
        return m_l * jnp.cos(l * l_im), m_l * jnp.sin(l * l_im)

    def spread(v):
        return jnp.where(same_group, jnp.tile(v, (GROUPS_PER_CLUSTER, 1)), 0.0).astype(BF16)

    for s in range(SSM_BLOCK):
        p_re, p_im = power(float(SSM_BLOCK - 1 - s))
        rows = slice(s * LANES, (s + 1) * LANES)
        wend_ref[rows, :CLUSTER_STATE] = spread(p_re * bb_re - p_im * bb_im)
        wend_ref[rows, CLUSTER_STATE:] = spread(p_re * bb_im + p_im * bb_re)
        q_re, q_im = power(float(s + 1))
        wrd_ref[rows, :CLUSTER_STATE] = spread(q_re * ct_re - q_im * ct_im)
        wrd_ref[rows, CLUSTER_STATE:] = spread(-(q_re * ct_im + q_im * ct_re))

    wrd0 = jnp.concatenate([spread(ct_re), spread(-ct_im)], axis=1)
    lag_all = _dot_nt(wend_ref[...], wrd0).astype(BF16)
    for t in range(SSM_BLOCK):
        cols = slice(t * LANES, (t + 1) * LANES)
        top = (t + 1) * LANES
        lagk_ref[:top, cols] = lag_all[FLAT - top:, :]
        if top < FLAT:
            lagk_ref[top:, cols] = jnp.zeros((FLAT - top, LANES), BF16)

    p_re, p_im = power(float(SSM_BLOCK))
    pw_ref[0:1, :] = jnp.concatenate([p_re, p_im], axis=1)
    for _ in range(SCAN_SEG_LEN.bit_length() - 1):
        p_re, p_im = p_re * p_re - p_im * p_im, 2.0 * (p_re * p_im)
    pw_ref[1:2, :] = jnp.concatenate([p_re, p_im], axis=1)


def _cmul_add(p_re, p_im, h_re, h_im, e_re, e_im):
    n_re = [pr * hr - pi * hi + er for pr, pi, hr, hi, er in zip(p_re, p_im, h_re, h_im, e_re)]
    n_im = [pr * hi + pi * hr + ei for pr, pi, hr, hi, ei in zip(p_re, p_im, h_re, h_im, e_im)]
    return n_re, n_im


def _s5_block_recurrence(st_ref, pw_ref, fillers):
    fillers = list(fillers)
    total_cost = sum(c for c, _ in fillers)
    issued_cost = [0.0]
    ticks = [0]

    def tick():
        while fillers and issued_cost[0] * 2 * SCAN_SEG_LEN <= ticks[0] * total_cost:
            cost, thunk = fillers.pop(0)
            issued_cost[0] += cost
            thunk()
        ticks[0] += 1

    half = SCAN_TILES // 2

    def bcast(row):
        return [jnp.broadcast_to(row[:, q * LANES:(q + 1) * LANES], (SCAN_SEGS, LANES))
                for q in range(half)]

    p_re, p_im = bcast(pw_ref[0:1, :CLUSTER_STATE]), bcast(pw_ref[0:1, CLUSTER_STATE:])
    seg_rows = lambda i: pl.ds(i, SCAN_SEGS, stride=SCAN_SEG_PITCH)

    def gather(i):
        e = [st_ref[q, seg_rows(i), :] for q in range(SCAN_TILES)]
        return e[:half], e[half:]

    zeros = [jnp.zeros((SCAN_SEGS, LANES), F32)] * half

    def end_step(i, h):
        e_re, e_im = gather(i)
        return _cmul_add(p_re, p_im, h[0], h[1], e_re, e_im)

    h = (zeros, zeros)
    for i in range(SCAN_SEG_LEN):
        tick()
        h = end_step(i, h)
    l_re, l_im = h

    s_re, s_im = bcast(pw_ref[1:2, :CLUSTER_STATE]), bcast(pw_ref[1:2, CLUSTER_STATE:])
    first = lax.broadcasted_iota(jnp.int32, (SCAN_SEGS, LANES), 0) == 0
    shift = lambda v: jnp.where(first, 0.0, pltpu.roll(v, 1, 0))
    c_re, c_im = zeros, zeros
    for _ in range(SCAN_SEGS - 1):
        n_re, n_im = _cmul_add(s_re, s_im, c_re, c_im, l_re, l_im)
        c_re, c_im = [shift(v) for v in n_re], [shift(v) for v in n_im]

    def scan_step(i, h):
        e_re, e_im = gather(i)
        for q in range(half):
            st_ref[q, seg_rows(i), :] = h[0][q]
            st_ref[half + q, seg_rows(i), :] = h[1][q]
        return _cmul_add(p_re, p_im, h[0], h[1], e_re, e_im)

    h = (c_re, c_im)
    for i in range(SCAN_SEG_LEN):
        tick()
        h = scan_step(i, h)
    while fillers:
        fillers.pop(0)[1]()


def _s5_kernel(u_ref, par_ref, bt_ref, ct_ref, dskip_ref, g_ref,
               wend_ref, wrd_ref, lagk_ref, pw_ref, st_ref):
    r = pl.program_id(1)

    @pl.when(r == 0)
    def _():
        _s5_build_operators(par_ref, bt_ref, ct_ref, wend_ref, wrd_ref, lagk_ref, pw_ref)

    u = u_ref[0]
    e = _dot(u, wend_ref[...])
    for q in range(SCAN_TILES):
        for j in range(SCAN_SEGS):
            st_ref[q, j * SCAN_SEG_PITCH:j * SCAN_SEG_PITCH + SCAN_SEG_LEN, :] = (
                e[j * SCAN_SEG_LEN:(j + 1) * SCAN_SEG_LEN, q * LANES:(q + 1) * LANES])

    ys = [None] * (FLAT // MXU_WIDTH)

    def lag_product(j):
        kdim = (j + 1) * MXU_WIDTH
        ys[j] = _dot(u[:, :kdim], lagk_ref[:kdim, j * MXU_WIDTH:(j + 1) * MXU_WIDTH])

    _s5_block_recurrence(st_ref, pw_ref,
                         [(j + 1, functools.partial(lag_product, j)) for j in range(len(ys))])
    y = jnp.concatenate(ys, axis=1)
    hprev = jnp.concatenate(
        [jnp.concatenate([st_ref[q, j * SCAN_SEG_PITCH:j * SCAN_SEG_PITCH + SCAN_SEG_LEN, :]
                          for q in range(SCAN_TILES)], axis=1)
         for j in range(SCAN_SEGS)], axis=0)

    y = y + _dot_nt(hprev.astype(BF16), wrd_ref[...])
    y = y + dskip_ref[0] * u.astype(F32)
    g_ref[0] = _gelu(y)


def _s5(layer, u_flat, par, bt, ct, dskip, blocks_per_seq):
    nb = u_flat.shape[1]
    rt = S5_ROW_TILE
    assert blocks_per_seq == rt and SCAN_SEG_LEN & (SCAN_SEG_LEN - 1) == 0
    return pl.pallas_call(
        _s5_kernel,
        grid=(N_CLUSTERS, nb // rt),
        in_specs=[
            pl.BlockSpec((1, rt, FLAT), lambda k, r: (k, r, 0)),
            pl.BlockSpec((None, 1, 3, CLUSTER_STATE), lambda k, r: (layer, k, 0, 0)),
            pl.BlockSpec((None, 1, 2, SSM_GROUP, CLUSTER_STATE), lambda k, r: (layer, k, 0, 0, 0)),
            pl.BlockSpec((None, 1, 2, SSM_GROUP, CLUSTER_STATE), lambda k, r: (layer, k, 0, 0, 0)),
            pl.BlockSpec((None, 1, 1, FLAT), lambda k, r: (layer, k, 0, 0)),
        ],
        out_specs=pl.BlockSpec((1, rt, FLAT), lambda k, r: (k, r, 0)),
        out_shape=jax.ShapeDtypeStruct((N_CLUSTERS, nb, FLAT), F32),
        scratch_shapes=[pltpu.VMEM((FLAT, 2 * CLUSTER_STATE), BF16),
                        pltpu.VMEM((FLAT, 2 * CLUSTER_STATE), BF16),
                        pltpu.VMEM((FLAT, FLAT), BF16),
                        pltpu.VMEM((2, 2 * CLUSTER_STATE), F32),
                        pltpu.VMEM((SCAN_TILES, SCAN_SEGS * SCAN_SEG_PITCH, LANES), F32)],
        compiler_params=pltpu.CompilerParams(
            dimension_semantics=("arbitrary", "arbitrary"), vmem_limit_bytes=VMEM_LIMIT),
        name="s5",
    )(u_flat, par, bt, ct, dskip)


def _s5_params(A_re, A_im, log_dt, B_re, B_im, C_re, C_im, D_skip):
    depth = A_re.shape[0]
    k, gpc, n, c = N_CLUSTERS, GROUPS_PER_CLUSTER, SSM_STATE, SSM_GROUP
    rows = lambda a: a.reshape(depth, k, 1, CLUSTER_STATE)
    ldt = jnp.broadcast_to(log_dt[..., None], A_re.shape)
    par = jnp.concatenate([rows(A_re), rows(A_im), rows(ldt)], axis=2)
    b = jnp.stack([B_re, B_im], axis=1).reshape(depth, 2, k, gpc, n, c)
    bt = b.transpose(0, 2, 1, 5, 3, 4).reshape(depth, k, 2, c, CLUSTER_STATE)
    cc = jnp.stack([C_re, C_im], axis=1).reshape(depth, 2, k, gpc, c, n)
    ct = cc.transpose(0, 2, 1, 4, 3, 5).reshape(depth, k, 2, c, CLUSTER_STATE)
    dskip = jnp.tile(D_skip.reshape(depth, k, 1, LANES), (1, 1, 1, SSM_BLOCK))
    return par, bt, ct, dskip


def _mix_out_kernel(final, rider_kinds, *refs):
    n_r = len(rider_kinds)
    x_ref, g_ref, yb_ref, yc_ref, vec_ref, wglu_ref, wout_ref, wg_ref, wu_ref, wd_ref = refs[:10]
    rider_src = refs[10:10 + n_r]
    o_ref = refs[10 + n_r]
    rider_dst = refs[11 + n_r:11 + 2 * n_r]
    (gs_ref,) = refs[11 + 2 * n_r:]
    _cast_riders(rider_kinds, rider_src, rider_dst)

    bglu = vec_ref[VEC_GLU:VEC_GLU + 1, :D_SSM]
    gffn = vec_ref[VEC_GFFN:VEC_GFFN + 1, :]
    tm = x_ref.shape[0]
    for k in range(N_CLUSTERS):
        for s in range(SSM_BLOCK):
            gs_ref[k, pl.ds(s, tm // SSM_BLOCK, stride=SSM_BLOCK), :] = (
                g_ref[k, :, s * LANES:(s + 1) * LANES])
    g = jnp.concatenate([gs_ref[k] for k in range(N_CLUSTERS)], axis=1)
    ya = g * jax.nn.sigmoid(_dot(g.astype(BF16), wglu_ref[...]) + bglu)
    ymix = jnp.concatenate([ya.astype(BF16), yb_ref[...], yc_ref[...]], axis=1)
    x1 = x_ref[...] + _dot(ymix, wout_ref[...])
    h = _rms(x1, gffn).astype(BF16)
    acc = x1
    for c0, c1 in FF_CHUNKS:
        gate = _dot(h, wg_ref[:, c0:c1])
        up = _dot(h, wu_ref[:, c0:c1])
        act = (gate * jax.nn.sigmoid(gate) * up).astype(BF16)
        acc = acc + _dot(act, wd_ref[c0:c1, :])
    if final:
        acc = _rms(acc, vec_ref[VEC_GFINAL:VEC_GFINAL + 1, :])
    o_ref[...] = acc


def _mix_out(layer, x2d, g3, yb, yc, vecs, wglu, wout, wg, wu, wd, final, riders=()):
    m = x2d.shape[0]
    tm = TOKEN_TILE
    n_steps = m // tm
    d_ff = wg.shape[-1]
    resident = lambda shape: pl.BlockSpec(shape, lambda i: (0, 0), pipeline_mode=pl.Buffered(1))
    r_specs = [_rider_specs(n_steps, lyr, w) for _, w, lyr in riders]
    return pl.pallas_call(
        functools.partial(_mix_out_kernel, final, tuple(k for k, _, _ in riders)),
        grid=(n_steps,),
        in_specs=[
            pl.BlockSpec((tm, D_MODEL), lambda i: (i, 0)),
            pl.BlockSpec((N_CLUSTERS, tm // SSM_BLOCK, FLAT), lambda i: (0, i, 0)),
            pl.BlockSpec((tm, D_POOL), lambda i: (i, 0)),
            pl.BlockSpec((tm, D_SGU), lambda i: (i, 0)),
            _layer_spec(layer, (VEC_ROWS, D_MODEL)),
            _layer_spec(layer, (D_SSM, D_SSM)),
            resident((D_MODEL, D_MODEL)),
            resident((D_MODEL, d_ff)),
            resident((D_MODEL, d_ff)),
            resident((d_ff, D_MODEL)),
        ] + [s for s, _, _ in r_specs],
        out_specs=[pl.BlockSpec((tm, D_MODEL), lambda i: (i, 0))] + [d for _, d, _ in r_specs],
        out_shape=[jax.ShapeDtypeStruct((m, D_MODEL), F32)] + [o for _, _, o in r_specs],
        scratch_shapes=[pltpu.VMEM((N_CLUSTERS, tm, LANES), F32)],
        compiler_params=pltpu.CompilerParams(
            dimension_semantics=("arbitrary",), vmem_limit_bytes=VMEM_LIMIT),
        name="mix_out",
    )(x2d, g3, yb, yc, vecs, wglu, wout, wg, wu, wd, *[w for _, w, _ in riders])


def kernel(x, g_mix, w_in, A_re, A_im, log_dt, B_re, B_im, C_re, C_im, D_skip, w_glu, b_glu,
           w_pool, pool_scale, sgu_ln_g, sgu_ln_b, w_spatial, b_spatial, w_out, g_ffn,
           w_gate, w_up, w_down, g_final):
    bsz, seq, d = x.shape
    depth = w_in.shape[0]
    m = bsz * seq
    assert d == D_MODEL and seq % MIX_IN_TILE == 0 and MIX_IN_TILE % (2 * CHUNK) == 0
    assert m % TOKEN_TILE == 0
    assert (seq // SSM_BLOCK) % S5_ROW_TILE == 0
    x2d = x.reshape(m, D_MODEL)
    tril = jnp.tril(jnp.ones((CHUNK, CHUNK), dtype=bool))
    eye = jnp.eye(len(POOL_WINDOWS), dtype=F32)
    wpool_bd = jnp.einsum('dgij,gh->dgihj', w_pool, eye).reshape(depth, D_POOL, D_POOL).astype(BF16)
    ws = jnp.where(tril, w_spatial, 0.0).astype(BF16)
    ws = jnp.concatenate([ws[:, 0::2], ws[:, 1::2]], axis=-1)
    bsp = jnp.repeat(jnp.swapaxes(b_spatial, 1, 2), SGU_HEAD_DIM, axis=2)
    par, bt, ct, dskip = _s5_params(A_re, A_im, log_dt, B_re, B_im, C_re, C_im, D_skip)
    vec_rows = [None] * VEC_ROWS
    vec_rows[VEC_GMIX] = g_mix
    vec_rows[VEC_GFFN] = g_ffn
    vec_rows[VEC_SGU] = jnp.concatenate([pool_scale, sgu_ln_g, sgu_ln_b], axis=-1)
    vec_rows[VEC_GLU] = jnp.pad(b_glu, ((0, 0), (0, D_MODEL - D_SSM)))
    vec_rows[VEC_GFINAL] = jnp.broadcast_to(g_final, (depth, D_MODEL))
    zero_row = jnp.zeros((depth, D_MODEL), F32)
    vecs = jnp.stack([zero_row if r is None else r for r in vec_rows], axis=1)
    wglu = w_glu.astype(BF16)
    b_end, u_end = D_SSM + D_POOL, D_SSM + D_POOL + D_SGU
    win = jnp.concatenate([w_in[0, :, u_end:], w_in[0, :, b_end:u_end], w_in[0, :, :b_end]],
                          axis=-1).astype(BF16)
    ffn_stacks = (w_out, w_gate, w_up, w_down)
    ffn = None
    for l in range(depth):
        riders = tuple(("plain", w, 0) for w in ffn_stacks) if l == 0 else ()
        u_flat, yb, yc, *cast = _mix_in(l, x2d, vecs, win, wpool_bd, ws, bsp, seq, riders)
        if l == 0:
            ffn = cast
        g3 = _s5(l, u_flat, par, bt, ct, dskip, seq // SSM_BLOCK)
        last = l == depth - 1
        riders = () if last else (("w_in", w_in, l + 1),) + tuple(
            ("plain", w, l + 1) for w in ffn_stacks)
        x2d, *cast = _mix_out(l, x2d, g3, yb, yc, vecs, wglu, *ffn, final=last, riders=riders)
        if not last:
            win, ffn = cast[0], cast[1:]
    return x2d.reshape(bsz, seq, D_MODEL)
```

```python
import functools
import math

import jax
import jax.numpy as jnp
from jax import lax
from jax.experimental import pallas as pl
from jax.experimental.pallas import tpu as pltpu

D_MODEL = 1024
D_SSM = 384
SSM_GROUP = 16
N_SSM_GROUPS = D_SSM // SSM_GROUP
SSM_STATE = 64
POOL_WINDOWS = (2, 4, 8, 16)
POOL_GROUP = 64
D_POOL = len(POOL_WINDOWS) * POOL_GROUP
MAX_WINDOW = max(POOL_WINDOWS)
SGU_HEADS = 6
SGU_HEAD_DIM = 64
D_SGU = SGU_HEADS * SGU_HEAD_DIM
CHUNK = 128
D_IN = D_SSM + D_POOL + 2 * D_SGU
EPS = 1e-6

LANES = 128
SSM_BLOCK = 16
N_CLUSTERS = D_SSM // LANES
GROUPS_PER_CLUSTER = LANES // SSM_GROUP
CLUSTER_STATE = GROUPS_PER_CLUSTER * SSM_STATE
FLAT = SSM_BLOCK * LANES

MXU_WIDTH = 256
BF16_SUBLANES = 16
VEC_ROWS = 8
VEC_GMIX, VEC_GFFN, VEC_SGU, VEC_GLU, VEC_GFINAL = 0, 1, 2, 3, 4
TOKEN_TILE = 1024
MIX_IN_TILE = 1024
S5_ROW_TILE = 512
SUBLANES = 8
SCAN_SEGS = SUBLANES
SCAN_SEG_LEN = S5_ROW_TILE // SCAN_SEGS
SCAN_SEG_PITCH = SCAN_SEG_LEN + SUBLANES
SCAN_TILES = 2 * CLUSTER_STATE // LANES
FF_CHUNKS = ((0, 768), (768, 1536), (1536, 2304), (2304, 2816))
MIB = 1024 * 1024
V7X_VMEM_BYTES = 64 * MIB
VMEM_MIX_IN = 40 * MIB
VMEM_S5 = 44 * MIB
VMEM_MIX_OUT = V7X_VMEM_BYTES

F32 = jnp.float32
BF16 = jnp.bfloat16
NT_DIMS = (((1,), (1,)), ((), ()))


def _gelu(x):
    c = math.sqrt(2.0 / math.pi)
    return x * (0.5 + 0.5 * jnp.tanh(x * (c + (c * 0.044715) * (x * x))))


def _rms(x, g):
    ms = jnp.mean(x * x, axis=-1, keepdims=True)
    return x * lax.rsqrt(ms + EPS) * g


def _dot(a, b):
    return jnp.dot(a, b, preferred_element_type=F32)


def _dot_nt(a, b):
    return lax.dot_general(a, b, NT_DIMS, preferred_element_type=F32)


def _cast_riders(kinds, src_refs, dst_refs):
    for kind, src, dst in zip(kinds, src_refs, dst_refs):
        if kind == "w_in":
            b_end, u_end = D_SSM + D_POOL, D_SSM + D_POOL + D_SGU
            dst[:, :D_SGU] = src[:, u_end:].astype(BF16)
            dst[:, D_SGU:2 * D_SGU] = src[:, b_end:u_end].astype(BF16)
            dst[:, 2 * D_SGU:] = src[:, :b_end].astype(BF16)
        else:
            dst[...] = src[...].astype(BF16)


def _rider_specs(n_steps, layer, w):
    rows, cols = w.shape[1:]
    share = 1
    while (rows * share) % n_steps or (rows * share // n_steps) % BF16_SUBLANES:
        share *= 2
    slab = rows * share // n_steps
    which = lambda i: jnp.minimum(i, n_steps - 1) // share
    src = pl.BlockSpec((None, slab, cols), lambda i: (layer, which(i), 0))
    dst = pl.BlockSpec((slab, cols), lambda i: (which(i), 0))
    return src, dst, jax.ShapeDtypeStruct((rows, cols), BF16)


def _mix_in_kernel(tiles_per_seq, rider_kinds, *refs):
    n_r = len(rider_kinds)
    x_ref, vec_ref, win_ref, wpool_ref, ws_ref, bsp_ref = refs[:6]
    rider_src = refs[6:6 + n_r]
    za_ref, yb_ref, yc_ref = refs[6 + n_r:9 + n_r]
    rider_dst = refs[9 + n_r:9 + 2 * n_r]
    halo_ref, zs_ref = refs[9 + 2 * n_r:]
    _cast_riders(rider_kinds, rider_src, rider_dst)

    tm = x_ref.shape[0]
    i = pl.program_id(0)
    seq_tile = i % tiles_per_seq
    gmix = vec_ref[VEC_GMIX:VEC_GMIX + 1, :]
    pscale = vec_ref[VEC_SGU:VEC_SGU + 1, :D_POOL]
    lng = vec_ref[VEC_SGU:VEC_SGU + 1, D_POOL:D_POOL + D_SGU]
    lnb = vec_ref[VEC_SGU:VEC_SGU + 1, D_POOL + D_SGU:]

    h = _rms(x_ref[...], gmix).astype(BF16)
    z_sgu = _dot(h, win_ref[:, :2 * D_SGU])
    z_rest = _dot(h, win_ref[:, 2 * D_SGU:])
    zv = z_sgu[:, :D_SGU]
    zu = z_sgu[:, D_SGU:]
    zb = z_rest[:, D_SSM:]

    for k in range(N_CLUSTERS):
        zs_ref[k] = z_rest[:, k * LANES:(k + 1) * LANES]
    for k in range(N_CLUSTERS):
        for s in range(SSM_BLOCK):
            piece = zs_ref[k, pl.ds(s, tm // SSM_BLOCK, stride=SSM_BLOCK), :]
            za_ref[k, :, s * LANES:(s + 1) * LANES] = piece.astype(BF16)

    halo = jnp.where(seq_tile == 0, 0.0, halo_ref[...])
    halo_ref[...] = zb[tm - MAX_WINDOW:, :]
    ext = jnp.concatenate([halo, zb], axis=0)
    lane = lax.broadcasted_iota(jnp.int32, (tm, LANES), 1)
    low = lane < POOL_GROUP
    low_h = lax.broadcasted_iota(jnp.int32, (MAX_WINDOW, LANES), 1) < POOL_GROUP
    low_r = lax.broadcasted_iota(jnp.int32, (1, LANES), 1) < POOL_GROUP
    pos1 = (seq_tile * tm + 1
            + lax.broadcasted_iota(jnp.int32, (MAX_WINDOW, LANES), 0)).astype(F32)

    def window_mean(total, w_low, w_high):
        cnt = jnp.where(low_h, jnp.minimum(pos1, w_low), jnp.minimum(pos1, w_high))
        inv_w = jnp.where(low_r, 1.0 / w_low, 1.0 / w_high)
        return jnp.concatenate([total[:MAX_WINDOW] / cnt, total[MAX_WINDOW:] * inv_w], axis=0)

    e0 = ext[:, :LANES]
    s2 = e0 + pltpu.roll(e0, 1, 0)
    s4 = s2 + pltpu.roll(s2, 2, 0)
    sum0 = jnp.where(low, s2[MAX_WINDOW:], s4[MAX_WINDOW:])
    e1 = ext[:, LANES:]
    t2 = e1 + pltpu.roll(e1, 1, 0)
    t4 = t2 + pltpu.roll(t2, 2, 0)
    t8 = t4 + pltpu.roll(t4, 4, 0)
    t16 = t8 + pltpu.roll(t8, 8, 0)
    sum1 = jnp.where(low, t8[MAX_WINDOW:], t16[MAX_WINDOW:])
    pooled = jnp.concatenate([window_mean(sum0, 2.0, 4.0), window_mean(sum1, 8.0, 16.0)],
                             axis=1) - zb
    yb = _dot(pooled.astype(BF16), wpool_ref[...]) * pscale
    yb_ref[...] = yb.astype(BF16)

    u = _gelu(zu)
    v = _gelu(zv)
    mu = jnp.mean(v, axis=-1, keepdims=True)
    vc = v - mu
    var = jnp.mean(vc * vc, axis=-1, keepdims=True)
    vn = (vc * lax.rsqrt(var + EPS) * lng + lnb).astype(BF16)
    lane_c = lax.broadcasted_iota(jnp.int32, (CHUNK, LANES), 1)
    low_c = lane_c < SGU_HEAD_DIM
    zero = jnp.zeros((), BF16)
    for c in range(0, tm // CHUNK, 2):
        rows_a = slice(c * CHUNK, (c + 1) * CHUNK)
        rows_b = slice((c + 1) * CHUNK, (c + 2) * CHUNK)
        parts_a, parts_b = [], []
        for p in range(SGU_HEADS // 2):
            va = vn[rows_a, p * LANES:(p + 1) * LANES]
            vb = vn[rows_b, p * LANES:(p + 1) * LANES]
            top = jnp.concatenate([jnp.where(low_c, va, zero), jnp.where(low_c, vb, zero)], axis=1)
            bot = jnp.concatenate([jnp.where(low_c, zero, va), jnp.where(low_c, zero, vb)], axis=1)
            out = _dot(ws_ref[p], jnp.concatenate([top, bot], axis=0))
            parts_a.append(out[:, :LANES])
            parts_b.append(out[:, LANES:])
        for rows, parts in ((rows_a, parts_a), (rows_b, parts_b)):
            mixed = jnp.concatenate(parts, axis=1) + bsp_ref[...]
            yc_ref[rows, :] = (u[rows, :] * mixed).astype(BF16)


def _layer_spec(layer, shape):
    zeros = (0,) * len(shape)
    return pl.BlockSpec((None,) + tuple(shape), lambda *_: (layer,) + zeros)


def _mix_in(layer, x2d, vecs, win, wpool_bd, ws, bsp, seq_len, riders=()):
    m = x2d.shape[0]
    tm = MIX_IN_TILE
    n_steps = m // tm
    spec = functools.partial(_layer_spec, layer)
    r_specs = [_rider_specs(n_steps, lyr, w) for _, w, lyr in riders]
    return pl.pallas_call(
        functools.partial(_mix_in_kernel, seq_len // tm, tuple(k for k, _, _ in riders)),
        grid=(n_steps,),
        in_specs=[
            pl.BlockSpec((tm, D_MODEL), lambda i: (i, 0)),
            spec((VEC_ROWS, D_MODEL)),
            pl.BlockSpec((D_MODEL, D_IN), lambda i: (0, 0)),
            spec((D_POOL, D_POOL)),
            spec((SGU_HEADS // 2, CHUNK, 2 * CHUNK)),
            spec((CHUNK, D_SGU)),
        ] + [s for s, _, _ in r_specs],
        out_specs=[
            pl.BlockSpec((N_CLUSTERS, tm // SSM_BLOCK, FLAT), lambda i: (0, i, 0)),
            pl.BlockSpec((tm, D_POOL), lambda i: (i, 0)),
            pl.BlockSpec((tm, D_SGU), lambda i: (i, 0)),
        ] + [d for _, d, _ in r_specs],
        out_shape=[
            jax.ShapeDtypeStruct((N_CLUSTERS, m // SSM_BLOCK, FLAT), BF16),
            jax.ShapeDtypeStruct((m, D_POOL), BF16),
            jax.ShapeDtypeStruct((m, D_SGU), BF16),
        ] + [o for _, _, o in r_specs],
        scratch_shapes=[pltpu.VMEM((MAX_WINDOW, D_POOL), F32),
                        pltpu.VMEM((N_CLUSTERS, tm, LANES), F32)],
        compiler_params=pltpu.CompilerParams(
            dimension_semantics=("arbitrary",), vmem_limit_bytes=VMEM_MIX_IN),
        name="mix_in",
    )(x2d, vecs, win, wpool_bd, ws, bsp, *[w for _, w, _ in riders])


def _s5_build_operators(par_ref, bt_ref, ct_ref, wend_ref, wrd_ref, lagk_ref, pw_ref):
    a_re = par_ref[0, 0:1, :]
    a_im = par_ref[0, 1:2, :]
    dt = jnp.exp(par_ref[0, 2:3, :])
    l_re = a_re * dt
    l_im = a_im * dt
    mag = jnp.exp(l_re)
    ar = mag * jnp.cos(l_im)
    ai = mag * jnp.sin(l_im)
    den = a_re * a_re + a_im * a_im
    f_re = ((ar - 1.0) * a_re + ai * a_im) / den
    f_im = (ai * a_re - (ar - 1.0) * a_im) / den
    bt_re, bt_im = bt_ref[0, 0], bt_ref[0, 1]
    bb_re = f_re * bt_re - f_im * bt_im
    bb_im = f_re * bt_im + f_im * bt_re
    ct_re, ct_im = ct_ref[0, 0], ct_ref[0, 1]

    row_g = lax.broadcasted_iota(jnp.int32, (LANES, CLUSTER_STATE), 0) // SSM_GROUP
    col_g = lax.broadcasted_iota(jnp.int32, (LANES, CLUSTER_STATE), 1) // SSM_STATE
    same_group = row_g == col_g

    def power(l):
        m_l = jnp.exp(l * l_re)
        return m_l * jnp.cos(l * l_im), m_l * jnp.sin(l * l_im)

    def spread(v):
        return jnp.where(same_group, jnp.tile(v, (GROUPS_PER_CLUSTER, 1)), 0.0).astype(BF16)

    for s in range(SSM_BLOCK):
        p_re, p_im = power(float(SSM_BLOCK - 1 - s))
        rows = slice(s * LANES, (s + 1) * LANES)
        wend_ref[rows, :CLUSTER_STATE] = spread(p_re * bb_re - p_im * bb_im)
        wend_ref[rows, CLUSTER_STATE:] = spread(p_re * bb_im + p_im * bb_re)
        q_re, q_im = power(float(s + 1))
        wrd_ref[rows, :CLUSTER_STATE] = spread(q_re * ct_re - q_im * ct_im)
        wrd_ref[rows, CLUSTER_STATE:] = spread(-(q_re * ct_im + q_im * ct_re))

    wrd0 = jnp.concatenate([spread(ct_re), spread(-ct_im)], axis=1)
    lag_all = _dot_nt(wend_ref[...], wrd0).astype(BF16)
    for t in range(SSM_BLOCK):
        cols = slice(t * LANES, (t + 1) * LANES)
        top = (t + 1) * LANES
        lagk_ref[:top, cols] = lag_all[FLAT - top:, :]
        if top < FLAT:
            lagk_ref[top:, cols] = jnp.zeros((FLAT - top, LANES), BF16)

    p_re, p_im = power(float(SSM_BLOCK))
    pw_ref[0:1, :] = jnp.concatenate([p_re, p_im], axis=1)
    for _ in range(SCAN_SEG_LEN.bit_length() - 1):
        p_re, p_im = p_re * p_re - p_im * p_im, 2.0 * (p_re * p_im)
    pw_ref[1:2, :] = jnp.concatenate([p_re, p_im], axis=1)


def _cmul_add(p_re, p_im, h_re, h_im, e_re, e_im):
    n_re = [pr * hr - pi * hi + er for pr, pi, hr, hi, er in zip(p_re, p_im, h_re, h_im, e_re)]
    n_im = [pr * hi + pi * hr + ei for pr, pi, hr, hi, ei in zip(p_re, p_im, h_re, h_im, e_im)]
    return n_re, n_im


def _s5_block_recurrence(st_ref, pw_ref, fillers):
    fillers = list(fillers)
    total_cost = sum(c for c, _ in fillers)
    issued_cost = [0.0]
    ticks = [0]

    def tick():
        while fillers and issued_cost[0] * 2 * SCAN_SEG_LEN <= ticks[0] * total_cost:
            cost, thunk = fillers.pop(0)
            issued_cost[0] += cost
            thunk()
        ticks[0] += 1

    half = SCAN_TILES // 2

    def bcast(row):
        return [jnp.broadcast_to(row[:, q * LANES:(q + 1) * LANES], (SCAN_SEGS, LANES))
                for q in range(half)]

    p_re, p_im = bcast(pw_ref[0:1, :CLUSTER_STATE]), bcast(pw_ref[0:1, CLUSTER_STATE:])
    seg_rows = lambda i: pl.ds(i, SCAN_SEGS, stride=SCAN_SEG_PITCH)

    def gather(i):
        e = [st_ref[q, seg_rows(i), :] for q in range(SCAN_TILES)]
        return e[:half], e[half:]

    zeros = [jnp.zeros((SCAN_SEGS, LANES), F32)] * half

    def end_step(i, h):
        e_re, e_im = gather(i)
        return _cmul_add(p_re, p_im, h[0], h[1], e_re, e_im)

    h = (zeros, zeros)
    for i in range(SCAN_SEG_LEN):
        tick()
        h = end_step(i, h)
    l_re, l_im = h

    s_re, s_im = bcast(pw_ref[1:2, :CLUSTER_STATE]), bcast(pw_ref[1:2, CLUSTER_STATE:])
    first = lax.broadcasted_iota(jnp.int32, (SCAN_SEGS, LANES), 0) == 0
    shift = lambda v: jnp.where(first, 0.0, pltpu.roll(v, 1, 0))
    c_re, c_im = zeros, zeros
    for _ in range(SCAN_SEGS - 1):
        n_re, n_im = _cmul_add(s_re, s_im, c_re, c_im, l_re, l_im)
        c_re, c_im = [shift(v) for v in n_re], [shift(v) for v in n_im]

    def scan_step(i, h):
        e_re, e_im = gather(i)
        for q in range(half):
            st_ref[q, seg_rows(i), :] = h[0][q]
            st_ref[half + q, seg_rows(i), :] = h[1][q]
        return _cmul_add(p_re, p_im, h[0], h[1], e_re, e_im)

    h = (c_re, c_im)
    for i in range(SCAN_SEG_LEN):
        tick()
        h = scan_step(i, h)
    while fillers:
        fillers.pop(0)[1]()


def _s5_kernel(u_ref, par_ref, bt_ref, ct_ref, dskip_ref, g_ref,
               wend_ref, wrd_ref, lagk_ref, pw_ref, st_ref):
    r = pl.program_id(1)

    @pl.when(r == 0)
    def _():
        _s5_build_operators(par_ref, bt_ref, ct_ref, wend_ref, wrd_ref, lagk_ref, pw_ref)

    u = u_ref[0]
    e = _dot(u, wend_ref[...])
    for q in range(SCAN_TILES):
        for j in range(SCAN_SEGS):
            st_ref[q, j * SCAN_SEG_PITCH:j * SCAN_SEG_PITCH + SCAN_SEG_LEN, :] = (
                e[j * SCAN_SEG_LEN:(j + 1) * SCAN_SEG_LEN, q * LANES:(q + 1) * LANES])

    ys = [None] * (FLAT // MXU_WIDTH)

    def lag_product(j):
        kdim = (j + 1) * MXU_WIDTH
        ys[j] = _dot(u[:, :kdim], lagk_ref[:kdim, j * MXU_WIDTH:(j + 1) * MXU_WIDTH])

    _s5_block_recurrence(st_ref, pw_ref,
                         [(j + 1, functools.partial(lag_product, j)) for j in range(len(ys))])
    y = jnp.concatenate(ys, axis=1)
    hprev = jnp.concatenate(
        [jnp.concatenate([st_ref[q, j * SCAN_SEG_PITCH:j * SCAN_SEG_PITCH + SCAN_SEG_LEN, :]
                          for q in range(SCAN_TILES)], axis=1)
         for j in range(SCAN_SEGS)], axis=0)

    y = y + _dot_nt(hprev.astype(BF16), wrd_ref[...])
    y = y + dskip_ref[0] * u.astype(F32)
    g_ref[0] = _gelu(y)


def _s5(layer, u_flat, par, bt, ct, dskip, blocks_per_seq):
    nb = u_flat.shape[1]
    rt = S5_ROW_TILE
    assert blocks_per_seq == rt and SCAN_SEG_LEN & (SCAN_SEG_LEN - 1) == 0
    return pl.pallas_call(
        _s5_kernel,
        grid=(N_CLUSTERS, nb // rt),
        in_specs=[
            pl.BlockSpec((1, rt, FLAT), lambda k, r: (k, r, 0)),
            pl.BlockSpec((None, 1, 3, CLUSTER_STATE), lambda k, r: (layer, k, 0, 0)),
            pl.BlockSpec((None, 1, 2, SSM_GROUP, CLUSTER_STATE), lambda k, r: (layer, k, 0, 0, 0)),
            pl.BlockSpec((None, 1, 2, SSM_GROUP, CLUSTER_STATE), lambda k, r: (layer, k, 0, 0, 0)),
            pl.BlockSpec((None, 1, 1, FLAT), lambda k, r: (layer, k, 0, 0)),
        ],
        out_specs=pl.BlockSpec((1, rt, FLAT), lambda k, r: (k, r, 0)),
        out_shape=jax.ShapeDtypeStruct((N_CLUSTERS, nb, FLAT), F32),
        scratch_shapes=[pltpu.VMEM((FLAT, 2 * CLUSTER_STATE), BF16),
                        pltpu.VMEM((FLAT, 2 * CLUSTER_STATE), BF16),
                        pltpu.VMEM((FLAT, FLAT), BF16),
                        pltpu.VMEM((2, 2 * CLUSTER_STATE), F32),
                        pltpu.VMEM((SCAN_TILES, SCAN_SEGS * SCAN_SEG_PITCH, LANES), F32)],
        compiler_params=pltpu.CompilerParams(
            dimension_semantics=("arbitrary", "arbitrary"), vmem_limit_bytes=VMEM_S5),
        name="s5",
    )(u_flat, par, bt, ct, dskip)


def _s5_params(A_re, A_im, log_dt, B_re, B_im, C_re, C_im, D_skip):
    depth = A_re.shape[0]
    k, gpc, n, c = N_CLUSTERS, GROUPS_PER_CLUSTER, SSM_STATE, SSM_GROUP
    rows = lambda a: a.reshape(depth, k, 1, CLUSTER_STATE)
    ldt = jnp.broadcast_to(log_dt[..., None], A_re.shape)
    par = jnp.concatenate([rows(A_re), rows(A_im), rows(ldt)], axis=2)
    b = jnp.stack([B_re, B_im], axis=1).reshape(depth, 2, k, gpc, n, c)
    bt = b.transpose(0, 2, 1, 5, 3, 4).reshape(depth, k, 2, c, CLUSTER_STATE)
    cc = jnp.stack([C_re, C_im], axis=1).reshape(depth, 2, k, gpc, c, n)
    ct = cc.transpose(0, 2, 1, 4, 3, 5).reshape(depth, k, 2, c, CLUSTER_STATE)
    dskip = jnp.tile(D_skip.reshape(depth, k, 1, LANES), (1, 1, 1, SSM_BLOCK))
    return par, bt, ct, dskip


def _mix_out_kernel(final, rider_kinds, *refs):
    n_r = len(rider_kinds)
    x_ref, g_ref, yb_ref, yc_ref, vec_ref, wglu_ref, wout_ref, wg_ref, wu_ref, wd_ref = refs[:10]
    rider_src = refs[10:10 + n_r]
    o_ref = refs[10 + n_r]
    rider_dst = refs[11 + n_r:11 + 2 * n_r]
    (gs_ref,) = refs[11 + 2 * n_r:]
    _cast_riders(rider_kinds, rider_src, rider_dst)

    bglu = vec_ref[VEC_GLU:VEC_GLU + 1, :D_SSM]
    gffn = vec_ref[VEC_GFFN:VEC_GFFN + 1, :]
    tm = x_ref.shape[0]
    for k in range(N_CLUSTERS):
        for s in range(SSM_BLOCK):
            gs_ref[k, pl.ds(s, tm // SSM_BLOCK, stride=SSM_BLOCK), :] = (
                g_ref[k, :, s * LANES:(s + 1) * LANES])
    g = jnp.concatenate([gs_ref[k] for k in range(N_CLUSTERS)], axis=1)
    ya = g * jax.nn.sigmoid(_dot(g.astype(BF16), wglu_ref[...]) + bglu)
    ymix = jnp.concatenate([ya.astype(BF16), yb_ref[...], yc_ref[...]], axis=1)
    x1 = x_ref[...] + _dot(ymix, wout_ref[...])
    h = _rms(x1, gffn).astype(BF16)
    acc = x1
    for c0, c1 in FF_CHUNKS:
        gate = _dot(h, wg_ref[:, c0:c1])
        up = _dot(h, wu_ref[:, c0:c1])
        act = (gate * jax.nn.sigmoid(gate) * up).astype(BF16)
        acc = acc + _dot(act, wd_ref[c0:c1, :])
    if final:
        acc = _rms(acc, vec_ref[VEC_GFINAL:VEC_GFINAL + 1, :])
    o_ref[...] = acc


def _mix_out(layer, x2d, g3, yb, yc, vecs, wglu, wout, wg, wu, wd, final, riders=()):
    m = x2d.shape[0]
    tm = TOKEN_TILE
    n_steps = m // tm
    d_ff = wg.shape[-1]
    resident = lambda shape: pl.BlockSpec(shape, lambda i: (0, 0), pipeline_mode=pl.Buffered(1))
    r_specs = [_rider_specs(n_steps, lyr, w) for _, w, lyr in riders]
    return pl.pallas_call(
        functools.partial(_mix_out_kernel, final, tuple(k for k, _, _ in riders)),
        grid=(n_steps,),
        in_specs=[
            pl.BlockSpec((tm, D_MODEL), lambda i: (i, 0)),
            pl.BlockSpec((N_CLUSTERS, tm // SSM_BLOCK, FLAT), lambda i: (0, i, 0)),
            pl.BlockSpec((tm, D_POOL), lambda i: (i, 0)),
            pl.BlockSpec((tm, D_SGU), lambda i: (i, 0)),
            _layer_spec(layer, (VEC_ROWS, D_MODEL)),
            _layer_spec(layer, (D_SSM, D_SSM)),
            resident((D_MODEL, D_MODEL)),
            resident((D_MODEL, d_ff)),
            resident((D_MODEL, d_ff)),
            resident((d_ff, D_MODEL)),
        ] + [s for s, _, _ in r_specs],
        out_specs=[pl.BlockSpec((tm, D_MODEL), lambda i: (i, 0))] + [d for _, d, _ in r_specs],
        out_shape=[jax.ShapeDtypeStruct((m, D_MODEL), F32)] + [o for _, _, o in r_specs],
        scratch_shapes=[pltpu.VMEM((N_CLUSTERS, tm, LANES), F32)],
        compiler_params=pltpu.CompilerParams(
            dimension_semantics=("arbitrary",), vmem_limit_bytes=VMEM_MIX_OUT),
        name="mix_out",
    )(x2d, g3, yb, yc, vecs, wglu, wout, wg, wu, wd, *[w for _, w, _ in riders])


def kernel(x, g_mix, w_in, A_re, A_im, log_dt, B_re, B_im, C_re, C_im, D_skip, w_glu, b_glu,
           w_pool, pool_scale, sgu_ln_g, sgu_ln_b, w_spatial, b_spatial, w_out, g_ffn,
           w_gate, w_up, w_down, g_final):
    bsz, seq, d = x.shape
    depth = w_in.shape[0]
    m = bsz * seq
    assert d == D_MODEL and seq % MIX_IN_TILE == 0 and MIX_IN_TILE % (2 * CHUNK) == 0
    assert m % TOKEN_TILE == 0
    assert (seq // SSM_BLOCK) % S5_ROW_TILE == 0
    x2d = x.reshape(m, D_MODEL)
    tril = jnp.tril(jnp.ones((CHUNK, CHUNK), dtype=bool))
    eye = jnp.eye(len(POOL_WINDOWS), dtype=F32)
    wpool_bd = jnp.einsum('dgij,gh->dgihj', w_pool, eye).reshape(depth, D_POOL, D_POOL).astype(BF16)
    ws = jnp.where(tril, w_spatial, 0.0).astype(BF16)
    ws = jnp.concatenate([ws[:, 0::2], ws[:, 1::2]], axis=-1)
    bsp = jnp.repeat(jnp.swapaxes(b_spatial, 1, 2), SGU_HEAD_DIM, axis=2)
    par, bt, ct, dskip = _s5_params(A_re, A_im, log_dt, B_re, B_im, C_re, C_im, D_skip)
    vec_rows = [None] * VEC_ROWS
    vec_rows[VEC_GMIX] = g_mix
    vec_rows[VEC_GFFN] = g_ffn
    vec_rows[VEC_SGU] = jnp.concatenate([pool_scale, sgu_ln_g, sgu_ln_b], axis=-1)
    vec_rows[VEC_GLU] = jnp.pad(b_glu, ((0, 0), (0, D_MODEL - D_SSM)))
    vec_rows[VEC_GFINAL] = jnp.broadcast_to(g_final, (depth, D_MODEL))
    zero_row = jnp.zeros((depth, D_MODEL), F32)
    vecs = jnp.stack([zero_row if r is None else r for r in vec_rows], axis=1)
    wglu = w_glu.astype(BF16)
    b_end, u_end = D_SSM + D_POOL, D_SSM + D_POOL + D_SGU
    win = jnp.concatenate([w_in[0, :, u_end:], w_in[0, :, b_end:u_end], w_in[0, :, :b_end]],
                          axis=-1).astype(BF16)
    ffn_stacks = (w_out, w_gate, w_up, w_down)
    ffn = None
    for l in range(depth):
        riders = tuple(("plain", w, 0) for w in ffn_stacks) if l == 0 else ()
        u_flat, yb, yc, *cast = _mix_in(l, x2d, vecs, win, wpool_bd, ws, bsp, seq, riders)
        if l == 0:
            ffn = cast
        g3 = _s5(l, u_flat, par, bt, ct, dskip, seq // SSM_BLOCK)
        last = l == depth - 1
        riders = () if last else (("w_in", w_in, l + 1),) + tuple(
            ("plain", w, l + 1) for w in ffn_stacks)
        x2d, *cast = _mix_out(l, x2d, g3, yb, yc, vecs, wglu, *ffn, final=last, riders=riders)
        if not last:
            win, ffn = cast[0], cast[1:]
    return x2d.reshape(bsz, seq, D_MODEL)
```

```python
import functools
import math

import jax
import jax.numpy as jnp
from jax import lax
from jax.experimental import pallas as pl
from jax.experimental.pallas import tpu as pltpu

D_MODEL = 1024
D_SSM = 384
SSM_GROUP = 16
N_SSM_GROUPS = D_SSM // SSM_GROUP
SSM_STATE = 64
POOL_WINDOWS = (2, 4, 8, 16)
POOL_GROUP = 64
D_POOL = len(POOL_WINDOWS) * POOL_GROUP
MAX_WINDOW = max(POOL_WINDOWS)
SGU_HEADS = 6
SGU_HEAD_DIM = 64
D_SGU = SGU_HEADS * SGU_HEAD_DIM
CHUNK = 128
D_IN = D_SSM + D_POOL + 2 * D_SGU
EPS = 1e-6

LANES = 128
SSM_BLOCK = 16
N_CLUSTERS = D_SSM // LANES
GROUPS_PER_CLUSTER = LANES // SSM_GROUP
CLUSTER_STATE = GROUPS_PER_CLUSTER * SSM_STATE
FLAT = SSM_BLOCK * LANES

MXU_WIDTH = 256
BF16_SUBLANES = 16
VEC_ROWS = 8
VEC_GMIX, VEC_GFFN, VEC_SGU, VEC_GLU, VEC_GFINAL = 0, 1, 2, 3, 4
TOKEN_TILE = 1024
MIX_IN_TILE = 1024
S5_ROW_TILE = 512
SUBLANES = 8
SCAN_SEGS = SUBLANES
SCAN_SEG_LEN = S5_ROW_TILE // SCAN_SEGS
SCAN_SEG_PITCH = SCAN_SEG_LEN + SUBLANES
SCAN_TILES = 2 * CLUSTER_STATE // LANES
FF_CHUNKS = ((0, 768), (768, 1536), (1536, 2304), (2304, 2816))
V7X_VMEM_BYTES = 64 * 1024 * 1024
VMEM_S5 = VMEM_MIX_OUT = V7X_VMEM_BYTES
VMEM_MIX_IN = 40 * 1024 * 1024

F32 = jnp.float32
BF16 = jnp.bfloat16
NT_DIMS = (((1,), (1,)), ((), ()))


def _gelu(x):
    c = math.sqrt(2.0 / math.pi)
    return x * (0.5 + 0.5 * jnp.tanh(x * (c + (c * 0.044715) * (x * x))))


def _rms(x, g):
    ms = jnp.mean(x * x, axis=-1, keepdims=True)
    return x * lax.rsqrt(ms + EPS) * g


def _dot(a, b):
    return jnp.dot(a, b, preferred_element_type=F32)


def _dot_nt(a, b):
    return lax.dot_general(a, b, NT_DIMS, preferred_element_type=F32)


def _cast_riders(kinds, src_refs, dst_refs):
    for kind, src, dst in zip(kinds, src_refs, dst_refs):
        if kind == "w_in":
            b_end, u_end = D_SSM + D_POOL, D_SSM + D_POOL + D_SGU
            dst[:, :D_SGU] = src[:, u_end:].astype(BF16)
            dst[:, D_SGU:2 * D_SGU] = src[:, b_end:u_end].astype(BF16)
            dst[:, 2 * D_SGU:] = src[:, :b_end].astype(BF16)
        else:
            dst[...] = src[...].astype(BF16)


def _rider_specs(n_steps, layer, w):
    rows, cols = w.shape[1:]
    share = 1
    while (rows * share) % n_steps or (rows * share // n_steps) % BF16_SUBLANES:
        share *= 2
    slab = rows * share // n_steps
    which = lambda i: jnp.minimum(i, n_steps - 1) // share
    src = pl.BlockSpec((None, slab, cols), lambda i: (layer, which(i), 0))
    dst = pl.BlockSpec((slab, cols), lambda i: (which(i), 0))
    return src, dst, jax.ShapeDtypeStruct((rows, cols), BF16)


def _mix_in_kernel(tiles_per_seq, rider_kinds, *refs):
    n_r = len(rider_kinds)
    x_ref, vec_ref, win_ref, wpool_ref, ws_ref, bsp_ref = refs[:6]
    rider_src = refs[6:6 + n_r]
    za_ref, yb_ref, yc_ref = refs[6 + n_r:9 + n_r]
    rider_dst = refs[9 + n_r:9 + 2 * n_r]
    halo_ref, zs_ref = refs[9 + 2 * n_r:]
    _cast_riders(rider_kinds, rider_src, rider_dst)

    tm = x_ref.shape[0]
    i = pl.program_id(0)
    seq_tile = i % tiles_per_seq
    gmix = vec_ref[VEC_GMIX:VEC_GMIX + 1, :]
    pscale = vec_ref[VEC_SGU:VEC_SGU + 1, :D_POOL]
    lng = vec_ref[VEC_SGU:VEC_SGU + 1, D_POOL:D_POOL + D_SGU]
    lnb = vec_ref[VEC_SGU:VEC_SGU + 1, D_POOL + D_SGU:]

    h = _rms(x_ref[...], gmix).astype(BF16)
    z_sgu = _dot(h, win_ref[:, :2 * D_SGU])
    z_rest = _dot(h, win_ref[:, 2 * D_SGU:])
    zv = z_sgu[:, :D_SGU]
    zu = z_sgu[:, D_SGU:]
    zb = z_rest[:, D_SSM:]

    for k in range(N_CLUSTERS):
        zs_ref[k] = z_rest[:, k * LANES:(k + 1) * LANES]
    for k in range(N_CLUSTERS):
        for s in range(SSM_BLOCK):
            piece = zs_ref[k, pl.ds(s, tm // SSM_BLOCK, stride=SSM_BLOCK), :]
            za_ref[k, :, s * LANES:(s + 1) * LANES] = piece.astype(BF16)

    halo = jnp.where(seq_tile == 0, 0.0, halo_ref[...])
    halo_ref[...] = zb[tm - MAX_WINDOW:, :]
    ext = jnp.concatenate([halo, zb], axis=0)
    lane = lax.broadcasted_iota(jnp.int32, (tm, LANES), 1)
    low = lane < POOL_GROUP
    low_h = lax.broadcasted_iota(jnp.int32, (MAX_WINDOW, LANES), 1) < POOL_GROUP
    low_r = lax.broadcasted_iota(jnp.int32, (1, LANES), 1) < POOL_GROUP
    pos1 = (seq_tile * tm + 1
            + lax.broadcasted_iota(jnp.int32, (MAX_WINDOW, LANES), 0)).astype(F32)

    def window_mean(total, w_low, w_high):
        cnt = jnp.where(low_h, jnp.minimum(pos1, w_low), jnp.minimum(pos1, w_high))
        inv_w = jnp.where(low_r, 1.0 / w_low, 1.0 / w_high)
        return jnp.concatenate([total[:MAX_WINDOW] / cnt, total[MAX_WINDOW:] * inv_w], axis=0)

    e0 = ext[:, :LANES]
    s2 = e0 + pltpu.roll(e0, 1, 0)
    s4 = s2 + pltpu.roll(s2, 2, 0)
    sum0 = jnp.where(low, s2[MAX_WINDOW:], s4[MAX_WINDOW:])
    e1 = ext[:, LANES:]
    t2 = e1 + pltpu.roll(e1, 1, 0)
    t4 = t2 + pltpu.roll(t2, 2, 0)
    t8 = t4 + pltpu.roll(t4, 4, 0)
    t16 = t8 + pltpu.roll(t8, 8, 0)
    sum1 = jnp.where(low, t8[MAX_WINDOW:], t16[MAX_WINDOW:])
    pooled = jnp.concatenate([window_mean(sum0, 2.0, 4.0), window_mean(sum1, 8.0, 16.0)],
                             axis=1) - zb
    yb = _dot(pooled.astype(BF16), wpool_ref[...]) * pscale
    yb_ref[...] = yb.astype(BF16)

    u = _gelu(zu)
    v = _gelu(zv)
    mu = jnp.mean(v, axis=-1, keepdims=True)
    vc = v - mu
    var = jnp.mean(vc * vc, axis=-1, keepdims=True)
    vn = (vc * lax.rsqrt(var + EPS) * lng + lnb).astype(BF16)
    lane_c = lax.broadcasted_iota(jnp.int32, (CHUNK, LANES), 1)
    low_c = lane_c < SGU_HEAD_DIM
    zero = jnp.zeros((), BF16)
    for c in range(0, tm // CHUNK, 2):
        rows_a = slice(c * CHUNK, (c + 1) * CHUNK)
        rows_b = slice((c + 1) * CHUNK, (c + 2) * CHUNK)
        parts_a, parts_b = [], []
        for p in range(SGU_HEADS // 2):
            va = vn[rows_a, p * LANES:(p + 1) * LANES]
            vb = vn[rows_b, p * LANES:(p + 1) * LANES]
            top = jnp.concatenate([jnp.where(low_c, va, zero), jnp.where(low_c, vb, zero)], axis=1)
            bot = jnp.concatenate([jnp.where(low_c, zero, va), jnp.where(low_c, zero, vb)], axis=1)
            out = _dot(ws_ref[p], jnp.concatenate([top, bot], axis=0))
            parts_a.append(out[:, :LANES])
            parts_b.append(out[:, LANES:])
        for rows, parts in ((rows_a, parts_a), (rows_b, parts_b)):
            mixed = jnp.concatenate(parts, axis=1) + bsp_ref[...]
            yc_ref[rows, :] = (u[rows, :] * mixed).astype(BF16)


def _layer_spec(layer, shape):
    zeros = (0,) * len(shape)
    return pl.BlockSpec((None,) + tuple(shape), lambda *_: (layer,) + zeros)


def _mix_in(layer, x2d, vecs, win, wpool_bd, ws, bsp, seq_len, riders=()):
    m = x2d.shape[0]
    tm = MIX_IN_TILE
    n_steps = m // tm
    spec = functools.partial(_layer_spec, layer)
    r_specs = [_rider_specs(n_steps, lyr, w) for _, w, lyr in riders]
    return pl.pallas_call(
        functools.partial(_mix_in_kernel, seq_len // tm, tuple(k for k, _, _ in riders)),
        grid=(n_steps,),
        in_specs=[
            pl.BlockSpec((tm, D_MODEL), lambda i: (i, 0)),
            spec((VEC_ROWS, D_MODEL)),
            pl.BlockSpec((D_MODEL, D_IN), lambda i: (0, 0)),
            spec((D_POOL, D_POOL)),
            spec((SGU_HEADS // 2, CHUNK, 2 * CHUNK)),
            spec((CHUNK, D_SGU)),
        ] + [s for s, _, _ in r_specs],
        out_specs=[
            pl.BlockSpec((N_CLUSTERS, tm // SSM_BLOCK, FLAT), lambda i: (0, i, 0)),
            pl.BlockSpec((tm, D_POOL), lambda i: (i, 0)),
            pl.BlockSpec((tm, D_SGU), lambda i: (i, 0)),
        ] + [d for _, d, _ in r_specs],
        out_shape=[
            jax.ShapeDtypeStruct((N_CLUSTERS, m // SSM_BLOCK, FLAT), BF16),
            jax.ShapeDtypeStruct((m, D_POOL), BF16),
            jax.ShapeDtypeStruct((m, D_SGU), BF16),
        ] + [o for _, _, o in r_specs],
        scratch_shapes=[pltpu.VMEM((MAX_WINDOW, D_POOL), F32),
                        pltpu.VMEM((N_CLUSTERS, tm, LANES), F32)],
        compiler_params=pltpu.CompilerParams(
            dimension_semantics=("arbitrary",), vmem_limit_bytes=VMEM_MIX_IN),
        name="mix_in",
    )(x2d, vecs, win, wpool_bd, ws, bsp, *[w for _, w, _ in riders])


def _s5_build_operators(par_ref, bt_ref, ct_ref, wend_ref, wrd_ref, lagk_ref, pw_ref):
    a_re = par_ref[0, 0:1, :]
    a_im = par_ref[0, 1:2, :]
    dt = jnp.exp(par_ref[0, 2:3, :])
    l_re = a_re * dt
    l_im = a_im * dt
    mag = jnp.exp(l_re)
    ar = mag * jnp.cos(l_im)
    ai = mag * jnp.sin(l_im)
    den = a_re * a_re + a_im * a_im
    f_re = ((ar - 1.0) * a_re + ai * a_im) / den
    f_im = (ai * a_re - (ar - 1.0) * a_im) / den
    bt_re, bt_im = bt_ref[0, 0], bt_ref[0, 1]
    bb_re = f_re * bt_re - f_im * bt_im
    bb_im = f_re * bt_im + f_im * bt_re
    ct_re, ct_im = ct_ref[0, 0], ct_ref[0, 1]

    row_g = lax.broadcasted_iota(jnp.int32, (LANES, CLUSTER_STATE), 0) // SSM_GROUP
    col_g = lax.broadcasted_iota(jnp.int32, (LANES, CLUSTER_STATE), 1) // SSM_STATE
    same_group = row_g == col_g

    def power(l):
        m_l = jnp.exp(l * l_re)
        return m_l * jnp.cos(l * l_im), m_l * jnp.sin(l * l_im)

    def spread(v):
        return jnp.where(same_group, jnp.tile(v, (GROUPS_PER_CLUSTER, 1)), 0.0).astype(BF16)

    for s in range(SSM_BLOCK):
        p_re, p_im = power(float(SSM_BLOCK - 1 - s))
        rows = slice(s * LANES, (s + 1) * LANES)
        wend_ref[rows, :CLUSTER_STATE] = spread(p_re * bb_re - p_im * bb_im)
        wend_ref[rows, CLUSTER_STATE:] = spread(p_re * bb_im + p_im * bb_re)
        q_re, q_im = power(float(s + 1))
        wrd_ref[rows, :CLUSTER_STATE] = spread(q_re * ct_re - q_im * ct_im)
        wrd_ref[rows, CLUSTER_STATE:] = spread(-(q_re * ct_im + q_im * ct_re))

    wrd0 = jnp.concatenate([spread(ct_re), spread(-ct_im)], axis=1)
    lag_all = _dot_nt(wend_ref[...], wrd0).astype(BF16)
    for t in range(SSM_BLOCK):
        cols = slice(t * LANES, (t + 1) * LANES)
        top = (t + 1) * LANES
        lagk_ref[:top, cols] = lag_all[FLAT - top:, :]
        if top < FLAT:
            lagk_ref[top:, cols] = jnp.zeros((FLAT - top, LANES), BF16)

    p_re, p_im = power(float(SSM_BLOCK))
    pw_ref[0:1, :] = jnp.concatenate([p_re, p_im], axis=1)
    for _ in range(SCAN_SEG_LEN.bit_length() - 1):
        p_re, p_im = p_re * p_re - p_im * p_im, 2.0 * (p_re * p_im)
    pw_ref[1:2, :] = jnp.concatenate([p_re, p_im], axis=1)


def _cmul_add(p_re, p_im, h_re, h_im, e_re, e_im):
    n_re = [pr * hr - pi * hi + er for pr, pi, hr, hi, er in zip(p_re, p_im, h_re, h_im, e_re)]
    n_im = [pr * hi + pi * hr + ei for pr, pi, hr, hi, ei in zip(p_re, p_im, h_re, h_im, e_im)]
    return n_re, n_im


def _s5_block_recurrence(st_ref, pw_ref, fillers):
    fillers = list(fillers)
    total_cost = sum(c for c, _ in fillers)
    issued_cost = [0.0]
    ticks = [0]

    def tick():
        while fillers and issued_cost[0] * 2 * SCAN_SEG_LEN <= ticks[0] * total_cost:
            cost, thunk = fillers.pop(0)
            issued_cost[0] += cost
            thunk()
        ticks[0] += 1

    half = SCAN_TILES // 2

    def bcast(row):
        return [jnp.broadcast_to(row[:, q * LANES:(q + 1) * LANES], (SCAN_SEGS, LANES))
                for q in range(half)]

    p_re, p_im = bcast(pw_ref[0:1, :CLUSTER_STATE]), bcast(pw_ref[0:1, CLUSTER_STATE:])
    seg_rows = lambda i: pl.ds(i, SCAN_SEGS, stride=SCAN_SEG_PITCH)

    def gather(i):
        e = [st_ref[q, seg_rows(i), :] for q in range(SCAN_TILES)]
        return e[:half], e[half:]

    zeros = [jnp.zeros((SCAN_SEGS, LANES), F32)] * half

    def end_step(i, h):
        e_re, e_im = gather(i)
        return _cmul_add(p_re, p_im, h[0], h[1], e_re, e_im)

    h = (zeros, zeros)
    for i in range(SCAN_SEG_LEN):
        tick()
        h = end_step(i, h)
    l_re, l_im = h

    s_re, s_im = bcast(pw_ref[1:2, :CLUSTER_STATE]), bcast(pw_ref[1:2, CLUSTER_STATE:])
    first = lax.broadcasted_iota(jnp.int32, (SCAN_SEGS, LANES), 0) == 0
    shift = lambda v: jnp.where(first, 0.0, pltpu.roll(v, 1, 0))
    c_re, c_im = zeros, zeros
    for _ in range(SCAN_SEGS - 1):
        n_re, n_im = _cmul_add(s_re, s_im, c_re, c_im, l_re, l_im)
        c_re, c_im = [shift(v) for v in n_re], [shift(v) for v in n_im]

    def scan_step(i, h):
        e_re, e_im = gather(i)
        for q in range(half):
            st_ref[q, seg_rows(i), :] = h[0][q]
            st_ref[half + q, seg_rows(i), :] = h[1][q]
        return _cmul_add(p_re, p_im, h[0], h[1], e_re, e_im)

    h = (c_re, c_im)
    for i in range(SCAN_SEG_LEN):
        tick()
        h = scan_step(i, h)
    while fillers:
        fillers.pop(0)[1]()


def _s5_kernel(u_ref, par_ref, bt_ref, ct_ref, dskip_ref, g_ref,
               wend_ref, wrd_ref, lagk_ref, pw_ref, st_ref):
    r = pl.program_id(1)

    @pl.when(r == 0)
    def _():
        _s5_build_operators(par_ref, bt_ref, ct_ref, wend_ref, wrd_ref, lagk_ref, pw_ref)

    u = u_ref[0]
    e = _dot(u, wend_ref[...])
    for q in range(SCAN_TILES):
        for j in range(SCAN_SEGS):
            st_ref[q, j * SCAN_SEG_PITCH:j * SCAN_SEG_PITCH + SCAN_SEG_LEN, :] = (
                e[j * SCAN_SEG_LEN:(j + 1) * SCAN_SEG_LEN, q * LANES:(q + 1) * LANES])

    ys = [None] * (FLAT // MXU_WIDTH)

    def lag_product(j):
        kdim = (j + 1) * MXU_WIDTH
        ys[j] = _dot(u[:, :kdim], lagk_ref[:kdim, j * MXU_WIDTH:(j + 1) * MXU_WIDTH])

    _s5_block_recurrence(st_ref, pw_ref,
                         [(j + 1, functools.partial(lag_product, j)) for j in range(len(ys))])
    y = jnp.concatenate(ys, axis=1)
    hprev = jnp.concatenate(
        [jnp.concatenate([st_ref[q, j * SCAN_SEG_PITCH:j * SCAN_SEG_PITCH + SCAN_SEG_LEN, :]
                          for q in range(SCAN_TILES)], axis=1)
         for j in range(SCAN_SEGS)], axis=0)

    y = y + _dot_nt(hprev.astype(BF16), wrd_ref[...])
    y = y + dskip_ref[0] * u.astype(F32)
    g_ref[0] = _gelu(y)


def _s5(layer, u_flat, par, bt, ct, dskip, blocks_per_seq):
    nb = u_flat.shape[1]
    rt = S5_ROW_TILE
    assert blocks_per_seq == rt and SCAN_SEG_LEN & (SCAN_SEG_LEN - 1) == 0
    return pl.pallas_call(
        _s5_kernel,
        grid=(N_CLUSTERS, nb // rt),
        in_specs=[
            pl.BlockSpec((1, rt, FLAT), lambda k, r: (k, r, 0)),
            pl.BlockSpec((None, 1, 3, CLUSTER_STATE), lambda k, r: (layer, k, 0, 0)),
            pl.BlockSpec((None, 1, 2, SSM_GROUP, CLUSTER_STATE), lambda k, r: (layer, k, 0, 0, 0)),
            pl.BlockSpec((None, 1, 2, SSM_GROUP, CLUSTER_STATE), lambda k, r: (layer, k, 0, 0, 0)),
            pl.BlockSpec((None, 1, 1, FLAT), lambda k, r: (layer, k, 0, 0)),
        ],
        out_specs=pl.BlockSpec((1, rt, FLAT), lambda k, r: (k, r, 0)),
        out_shape=jax.ShapeDtypeStruct((N_CLUSTERS, nb, FLAT), F32),
        scratch_shapes=[pltpu.VMEM((FLAT, 2 * CLUSTER_STATE), BF16),
                        pltpu.VMEM((FLAT, 2 * CLUSTER_STATE), BF16),
                        pltpu.VMEM((FLAT, FLAT), BF16),
                        pltpu.VMEM((2, 2 * CLUSTER_STATE), F32),
                        pltpu.VMEM((SCAN_TILES, SCAN_SEGS * SCAN_SEG_PITCH, LANES), F32)],
        compiler_params=pltpu.CompilerParams(
            dimension_semantics=("arbitrary", "arbitrary"), vmem_limit_bytes=VMEM_S5),
        name="s5",
    )(u_flat, par, bt, ct, dskip)


def _s5_params(A_re, A_im, log_dt, B_re, B_im, C_re, C_im, D_skip):
    depth = A_re.shape[0]
    k, gpc, n, c = N_CLUSTERS, GROUPS_PER_CLUSTER, SSM_STATE, SSM_GROUP
    rows = lambda a: a.reshape(depth, k, 1, CLUSTER_STATE)
    ldt = jnp.broadcast_to(log_dt[..., None], A_re.shape)
    par = jnp.concatenate([rows(A_re), rows(A_im), rows(ldt)], axis=2)
    b = jnp.stack([B_re, B_im], axis=1).reshape(depth, 2, k, gpc, n, c)
    bt = b.transpose(0, 2, 1, 5, 3, 4).reshape(depth, k, 2, c, CLUSTER_STATE)
    cc = jnp.stack([C_re, C_im], axis=1).reshape(depth, 2, k, gpc, c, n)
    ct = cc.transpose(0, 2, 1, 4, 3, 5).reshape(depth, k, 2, c, CLUSTER_STATE)
    dskip = jnp.tile(D_skip.reshape(depth, k, 1, LANES), (1, 1, 1, SSM_BLOCK))
    return par, bt, ct, dskip


def _mix_out_kernel(final, rider_kinds, *refs):
    n_r = len(rider_kinds)
    x_ref, g_ref, yb_ref, yc_ref, vec_ref, wglu_ref, wout_ref, wg_ref, wu_ref, wd_ref = refs[:10]
    rider_src = refs[10:10 + n_r]
    o_ref = refs[10 + n_r]
    rider_dst = refs[11 + n_r:11 + 2 * n_r]
    (gs_ref,) = refs[11 + 2 * n_r:]
    _cast_riders(rider_kinds, rider_src, rider_dst)

    bglu = vec_ref[VEC_GLU:VEC_GLU + 1, :D_SSM]
    gffn = vec_ref[VEC_GFFN:VEC_GFFN + 1, :]
    tm = x_ref.shape[0]
    for k in range(N_CLUSTERS):
        for s in range(SSM_BLOCK):
            gs_ref[k, pl.ds(s, tm // SSM_BLOCK, stride=SSM_BLOCK), :] = (
                g_ref[k, :, s * LANES:(s + 1) * LANES])
    g = jnp.concatenate([gs_ref[k] for k in range(N_CLUSTERS)], axis=1)
    ya = g * jax.nn.sigmoid(_dot(g.astype(BF16), wglu_ref[...]) + bglu)
    ymix = jnp.concatenate([ya.astype(BF16), yb_ref[...], yc_ref[...]], axis=1)
    x1 = x_ref[...] + _dot(ymix, wout_ref[...])
    h = _rms(x1, gffn).astype(BF16)
    acc = x1
    for c0, c1 in FF_CHUNKS:
        gate = _dot(h, wg_ref[:, c0:c1])
        up = _dot(h, wu_ref[:, c0:c1])
        act = (gate * jax.nn.sigmoid(gate) * up).astype(BF16)
        acc = acc + _dot(act, wd_ref[c0:c1, :])
    if final:
        acc = _rms(acc, vec_ref[VEC_GFINAL:VEC_GFINAL + 1, :])
    o_ref[...] = acc


def _mix_out(layer, x2d, g3, yb, yc, vecs, wglu, wout, wg, wu, wd, final, riders=()):
    m = x2d.shape[0]
    tm = TOKEN_TILE
    n_steps = m // tm
    d_ff = wg.shape[-1]
    resident = lambda shape: pl.BlockSpec(shape, lambda i: (0, 0), pipeline_mode=pl.Buffered(1))
    r_specs = [_rider_specs(n_steps, lyr, w) for _, w, lyr in riders]
    return pl.pallas_call(
        functools.partial(_mix_out_kernel, final, tuple(k for k, _, _ in riders)),
        grid=(n_steps,),
        in_specs=[
            pl.BlockSpec((tm, D_MODEL), lambda i: (i, 0)),
            pl.BlockSpec((N_CLUSTERS, tm // SSM_BLOCK, FLAT), lambda i: (0, i, 0)),
            pl.BlockSpec((tm, D_POOL), lambda i: (i, 0)),
            pl.BlockSpec((tm, D_SGU), lambda i: (i, 0)),
            _layer_spec(layer, (VEC_ROWS, D_MODEL)),
            _layer_spec(layer, (D_SSM, D_SSM)),
            resident((D_MODEL, D_MODEL)),
            resident((D_MODEL, d_ff)),
            resident((D_MODEL, d_ff)),
            resident((d_ff, D_MODEL)),
        ] + [s for s, _, _ in r_specs],
        out_specs=[pl.BlockSpec((tm, D_MODEL), lambda i: (i, 0))] + [d for _, d, _ in r_specs],
        out_shape=[jax.ShapeDtypeStruct((m, D_MODEL), F32)] + [o for _, _, o in r_specs],
        scratch_shapes=[pltpu.VMEM((N_CLUSTERS, tm, LANES), F32)],
        compiler_params=pltpu.CompilerParams(
            dimension_semantics=("arbitrary",), vmem_limit_bytes=VMEM_MIX_OUT),
        name="mix_out",
    )(x2d, g3, yb, yc, vecs, wglu, wout, wg, wu, wd, *[w for _, w, _ in riders])


def kernel(x, g_mix, w_in, A_re, A_im, log_dt, B_re, B_im, C_re, C_im, D_skip, w_glu, b_glu,
           w_pool, pool_scale, sgu_ln_g, sgu_ln_b, w_spatial, b_spatial, w_out, g_ffn,
           w_gate, w_up, w_down, g_final):
    bsz, seq, d = x.shape
    depth = w_in.shape[0]
    m = bsz * seq
    assert d == D_MODEL and seq % MIX_IN_TILE == 0 and MIX_IN_TILE % (2 * CHUNK) == 0
    assert m % TOKEN_TILE == 0
    assert (seq // SSM_BLOCK) % S5_ROW_TILE == 0
    x2d = x.reshape(m, D_MODEL)
    tril = jnp.tril(jnp.ones((CHUNK, CHUNK), dtype=bool))
    eye = jnp.eye(len(POOL_WINDOWS), dtype=F32)
    wpool_bd = jnp.einsum('dgij,gh->dgihj', w_pool, eye).reshape(depth, D_POOL, D_POOL).astype(BF16)
    ws = jnp.where(tril, w_spatial, 0.0).astype(BF16)
    ws = jnp.concatenate([ws[:, 0::2], ws[:, 1::2]], axis=-1)
    bsp = jnp.repeat(jnp.swapaxes(b_spatial, 1, 2), SGU_HEAD_DIM, axis=2)
    par, bt, ct, dskip = _s5_params(A_re, A_im, log_dt, B_re, B_im, C_re, C_im, D_skip)
    vec_rows = [None] * VEC_ROWS
    vec_rows[VEC_GMIX] = g_mix
    vec_rows[VEC_GFFN] = g_ffn
    vec_rows[VEC_SGU] = jnp.concatenate([pool_scale, sgu_ln_g, sgu_ln_b], axis=-1)
    vec_rows[VEC_GLU] = jnp.pad(b_glu, ((0, 0), (0, D_MODEL - D_SSM)))
    vec_rows[VEC_GFINAL] = jnp.broadcast_to(g_final, (depth, D_MODEL))
    zero_row = jnp.zeros((depth, D_MODEL), F32)
    vecs = jnp.stack([zero_row if r is None else r for r in vec_rows], axis=1)
    wglu = w_glu.astype(BF16)
    b_end, u_end = D_SSM + D_POOL, D_SSM + D_POOL + D_SGU
    win = jnp.concatenate([w_in[0, :, u_end:], w_in[0, :, b_end:u_end], w_in[0, :, :b_end]],
                          axis=-1).astype(BF16)
    ffn_stacks = (w_out, w_gate, w_up, w_down)
    ffn = None
    for l in range(depth):
        riders = tuple(("plain", w, 0) for w in ffn_stacks) if l == 0 else ()
        u_flat, yb, yc, *cast = _mix_in(l, x2d, vecs, win, wpool_bd, ws, bsp, seq, riders)
        if l == 0:
            ffn = cast
        g3 = _s5(l, u_flat, par, bt, ct, dskip, seq // SSM_BLOCK)
        last = l == depth - 1
        riders = () if last else (("w_in", w_in, l + 1),) + tuple(
            ("plain", w, l + 1) for w in ffn_stacks)
        x2d, *cast = _mix_out(l, x2d, g3, yb, yc, vecs, wglu, *ffn, final=last, riders=riders)
        if not last:
            win, ffn = cast[0], cast[1:]
    return x2d.reshape(bsz, seq, D_MODEL)
```

```python
import functools
import math

import jax
import jax.numpy as jnp
from jax import lax
from jax.experimental import pallas as pl
from jax.experimental.pallas import tpu as pltpu

D_MODEL = 1024
D_SSM = 384
SSM_GROUP = 16
N_SSM_GROUPS = D_SSM // SSM_GROUP
SSM_STATE = 64
POOL_WINDOWS = (2, 4, 8, 16)
POOL_GROUP = 64
D_POOL = len(POOL_WINDOWS) * POOL_GROUP
MAX_WINDOW = max(POOL_WINDOWS)
SGU_HEADS = 6
SGU_HEAD_DIM = 64
D_SGU = SGU_HEADS * SGU_HEAD_DIM
CHUNK = 128
D_IN = D_SSM + D_POOL + 2 * D_SGU
EPS = 1e-6

LANES = 128
SSM_BLOCK = 16
N_CLUSTERS = D_SSM // LANES
GROUPS_PER_CLUSTER = LANES // SSM_GROUP
CLUSTER_STATE = GROUPS_PER_CLUSTER * SSM_STATE
FLAT = SSM_BLOCK * LANES
GROUP_FLAT = SSM_BLOCK * SSM_GROUP
GROUP_PAIRS = GROUPS_PER_CLUSTER // 2

MXU_WIDTH = 256
BF16_SUBLANES = 16
VEC_ROWS = 8
VEC_GMIX, VEC_GFFN, VEC_SGU, VEC_GLU, VEC_GFINAL = 0, 1, 2, 3, 4
TOKEN_TILE = 1024
MIX_IN_TILE = 1024
S5_ROW_TILE = 512
SUBLANES = 8
SCAN_SEGS = SUBLANES
SCAN_SEG_LEN = S5_ROW_TILE // SCAN_SEGS
SCAN_SEG_PITCH = SCAN_SEG_LEN + SUBLANES
SCAN_TILES = 2 * CLUSTER_STATE // LANES
FF_CHUNKS = ((0, 768), (768, 1536), (1536, 2304), (2304, 2816))
V7X_VMEM_BYTES = 64 * 1024 * 1024
VMEM_MIX_IN = VMEM_S5 = VMEM_MIX_OUT = V7X_VMEM_BYTES

F32 = jnp.float32
BF16 = jnp.bfloat16
NT_DIMS = (((1,), (1,)), ((), ()))


def _gelu(x):
    c = math.sqrt(2.0 / math.pi)
    return x * (0.5 + 0.5 * jnp.tanh(x * (c + (c * 0.044715) * (x * x))))


def _rms(x, g):
    ms = jnp.mean(x * x, axis=-1, keepdims=True)
    return x * lax.rsqrt(ms + EPS) * g


def _dot(a, b):
    return jnp.dot(a, b, preferred_element_type=F32)


def _dot_nt(a, b):
    return lax.dot_general(a, b, NT_DIMS, preferred_element_type=F32)


def _chunk_transpose(tiles):
    n = len(tiles)
    chunk = LANES // n
    v = list(tiles)
    cidx = lax.broadcasted_iota(jnp.int32, v[0].shape, 1) // chunk
    k = 1
    while k < n:
        keep = (cidx & k) == 0
        nv = list(v)
        for a in range(n):
            if a & k:
                continue
            lo, hi = v[a], v[a + k]
            nv[a] = jnp.where(keep, lo, pltpu.roll(hi, k * chunk, 1))
            nv[a + k] = jnp.where(keep, pltpu.roll(lo, LANES - k * chunk, 1), hi)
        v = nv
        k *= 2
    return v


def _cast_riders(kinds, src_refs, dst_refs):
    for kind, src, dst in zip(kinds, src_refs, dst_refs):
        if kind == "w_in":
            b_end, u_end = D_SSM + D_POOL, D_SSM + D_POOL + D_SGU
            dst[:, :D_SGU] = src[:, u_end:].astype(BF16)
            dst[:, D_SGU:2 * D_SGU] = src[:, b_end:u_end].astype(BF16)
            dst[:, 2 * D_SGU:] = src[:, :b_end].astype(BF16)
        else:
            dst[...] = src[...].astype(BF16)


def _rider_specs(n_steps, layer, w):
    rows, cols = w.shape[1:]
    share = 1
    while (rows * share) % n_steps or (rows * share // n_steps) % BF16_SUBLANES:
        share *= 2
    slab = rows * share // n_steps
    which = lambda i: jnp.minimum(i, n_steps - 1) // share
    src = pl.BlockSpec((None, slab, cols), lambda i: (layer, which(i), 0))
    dst = pl.BlockSpec((slab, cols), lambda i: (which(i), 0))
    return src, dst, jax.ShapeDtypeStruct((rows, cols), BF16)


def _mix_in_kernel(tiles_per_seq, rider_kinds, *refs):
    n_r = len(rider_kinds)
    x_ref, vec_ref, win_ref, wpool_ref, ws_ref, bsp_ref = refs[:6]
    rider_src = refs[6:6 + n_r]
    za_ref, yb_ref, yc_ref = refs[6 + n_r:9 + n_r]
    rider_dst = refs[9 + n_r:9 + 2 * n_r]
    halo_ref, zs_ref = refs[9 + 2 * n_r:]
    _cast_riders(rider_kinds, rider_src, rider_dst)

    tm = x_ref.shape[0]
    i = pl.program_id(0)
    seq_tile = i % tiles_per_seq
    gmix = vec_ref[VEC_GMIX:VEC_GMIX + 1, :]
    pscale = vec_ref[VEC_SGU:VEC_SGU + 1, :D_POOL]
    lng = vec_ref[VEC_SGU:VEC_SGU + 1, D_POOL:D_POOL + D_SGU]
    lnb = vec_ref[VEC_SGU:VEC_SGU + 1, D_POOL + D_SGU:]

    h = _rms(x_ref[...], gmix).astype(BF16)
    z_sgu = _dot(h, win_ref[:, :2 * D_SGU])
    z_rest = _dot(h, win_ref[:, 2 * D_SGU:])
    zv = z_sgu[:, :D_SGU]
    zu = z_sgu[:, D_SGU:]
    zb = z_rest[:, D_SSM:]

    for k in range(N_CLUSTERS):
        zs_ref[k] = z_rest[:, k * LANES:(k + 1) * LANES]
    half_block = SSM_BLOCK // 2
    for k in range(N_CLUSTERS):
        for hf in range(2):
            slots = [zs_ref[k, pl.ds(hf * half_block + i, tm // SSM_BLOCK, stride=SSM_BLOCK), :]
                     for i in range(half_block)]
            for g, tile in enumerate(_chunk_transpose(slots)):
                col = g * GROUP_FLAT + hf * LANES
                za_ref[k, :, col:col + LANES] = tile.astype(BF16)

    halo = jnp.where(seq_tile == 0, 0.0, halo_ref[...])
    halo_ref[...] = zb[tm - MAX_WINDOW:, :]
    ext = jnp.concatenate([halo, zb], axis=0)
    lane = lax.broadcasted_iota(jnp.int32, (tm, LANES), 1)
    low = lane < POOL_GROUP
    low_h = lax.broadcasted_iota(jnp.int32, (MAX_WINDOW, LANES), 1) < POOL_GROUP
    low_r = lax.broadcasted_iota(jnp.int32, (1, LANES), 1) < POOL_GROUP
    pos1 = (seq_tile * tm + 1
            + lax.broadcasted_iota(jnp.int32, (MAX_WINDOW, LANES), 0)).astype(F32)

    def window_mean(total, w_low, w_high):
        cnt = jnp.where(low_h, jnp.minimum(pos1, w_low), jnp.minimum(pos1, w_high))
        inv_w = jnp.where(low_r, 1.0 / w_low, 1.0 / w_high)
        return jnp.concatenate([total[:MAX_WINDOW] / cnt, total[MAX_WINDOW:] * inv_w], axis=0)

    e0 = ext[:, :LANES]
    s2 = e0 + pltpu.roll(e0, 1, 0)
    s4 = s2 + pltpu.roll(s2, 2, 0)
    sum0 = jnp.where(low, s2[MAX_WINDOW:], s4[MAX_WINDOW:])
    e1 = ext[:, LANES:]
    t2 = e1 + pltpu.roll(e1, 1, 0)
    t4 = t2 + pltpu.roll(t2, 2, 0)
    t8 = t4 + pltpu.roll(t4, 4, 0)
    t16 = t8 + pltpu.roll(t8, 8, 0)
    sum1 = jnp.where(low, t8[MAX_WINDOW:], t16[MAX_WINDOW:])
    pooled = jnp.concatenate([window_mean(sum0, 2.0, 4.0), window_mean(sum1, 8.0, 16.0)],
                             axis=1) - zb
    yb = _dot(pooled.astype(BF16), wpool_ref[...]) * pscale
    yb_ref[...] = yb.astype(BF16)

    u = _gelu(zu)
    v = _gelu(zv)
    mu = jnp.mean(v, axis=-1, keepdims=True)
    vc = v - mu
    var = jnp.mean(vc * vc, axis=-1, keepdims=True)
    vn = (vc * lax.rsqrt(var + EPS) * lng + lnb).astype(BF16)
    lane_c = lax.broadcasted_iota(jnp.int32, (CHUNK, LANES), 1)
    low_c = lane_c < SGU_HEAD_DIM
    zero = jnp.zeros((), BF16)
    for c in range(0, tm // CHUNK, 2):
        rows_a = slice(c * CHUNK, (c + 1) * CHUNK)
        rows_b = slice((c + 1) * CHUNK, (c + 2) * CHUNK)
        parts_a, parts_b = [], []
        for p in range(SGU_HEADS // 2):
            va = vn[rows_a, p * LANES:(p + 1) * LANES]
            vb = vn[rows_b, p * LANES:(p + 1) * LANES]
            top = jnp.concatenate([jnp.where(low_c, va, zero), jnp.where(low_c, vb, zero)], axis=1)
            bot = jnp.concatenate([jnp.where(low_c, zero, va), jnp.where(low_c, zero, vb)], axis=1)
            out = _dot(ws_ref[p], jnp.concatenate([top, bot], axis=0))
            parts_a.append(out[:, :LANES])
            parts_b.append(out[:, LANES:])
        for rows, parts in ((rows_a, parts_a), (rows_b, parts_b)):
            mixed = jnp.concatenate(parts, axis=1) + bsp_ref[...]
            yc_ref[rows, :] = (u[rows, :] * mixed).astype(BF16)


def _layer_spec(layer, shape):
    zeros = (0,) * len(shape)
    return pl.BlockSpec((None,) + tuple(shape), lambda *_: (layer,) + zeros)


def _mix_in(layer, x2d, vecs, win, wpool_bd, ws, bsp, seq_len, riders=()):
    m = x2d.shape[0]
    tm = MIX_IN_TILE
    n_steps = m // tm
    spec = functools.partial(_layer_spec, layer)
    r_specs = [_rider_specs(n_steps, lyr, w) for _, w, lyr in riders]
    return pl.pallas_call(
        functools.partial(_mix_in_kernel, seq_len // tm, tuple(k for k, _, _ in riders)),
        grid=(n_steps,),
        in_specs=[
            pl.BlockSpec((tm, D_MODEL), lambda i: (i, 0)),
            spec((VEC_ROWS, D_MODEL)),
            pl.BlockSpec((D_MODEL, D_IN), lambda i: (0, 0)),
            spec((D_POOL, D_POOL)),
            spec((SGU_HEADS // 2, CHUNK, 2 * CHUNK)),
            spec((CHUNK, D_SGU)),
        ] + [s for s, _, _ in r_specs],
        out_specs=[
            pl.BlockSpec((N_CLUSTERS, tm // SSM_BLOCK, FLAT), lambda i: (0, i, 0)),
            pl.BlockSpec((tm, D_POOL), lambda i: (i, 0)),
            pl.BlockSpec((tm, D_SGU), lambda i: (i, 0)),
        ] + [d for _, d, _ in r_specs],
        out_shape=[
            jax.ShapeDtypeStruct((N_CLUSTERS, m // SSM_BLOCK, FLAT), BF16),
            jax.ShapeDtypeStruct((m, D_POOL), BF16),
            jax.ShapeDtypeStruct((m, D_SGU), BF16),
        ] + [o for _, _, o in r_specs],
        scratch_shapes=[pltpu.VMEM((MAX_WINDOW, D_POOL), F32),
                        pltpu.VMEM((N_CLUSTERS, tm, LANES), F32)],
        compiler_params=pltpu.CompilerParams(
            dimension_semantics=("arbitrary",), vmem_limit_bytes=VMEM_MIX_IN),
        name="mix_in",
    )(x2d, vecs, win, wpool_bd, ws, bsp, *[w for _, w, _ in riders])


def _s5_build_operators(par_ref, bt_ref, ct_ref, wend_ref, wrd_ref, lag0_ref, lagk_ref, pw_ref):
    a_re = par_ref[0, 0:1, :]
    a_im = par_ref[0, 1:2, :]
    dt = jnp.exp(par_ref[0, 2:3, :])
    l_re = a_re * dt
    l_im = a_im * dt
    mag = jnp.exp(l_re)
    ar = mag * jnp.cos(l_im)
    ai = mag * jnp.sin(l_im)
    den = a_re * a_re + a_im * a_im
    f_re = ((ar - 1.0) * a_re + ai * a_im) / den
    f_im = (ai * a_re - (ar - 1.0) * a_im) / den
    bt_re, bt_im = bt_ref[0, 0], bt_ref[0, 1]
    bb_re = f_re * bt_re - f_im * bt_im
    bb_im = f_re * bt_im + f_im * bt_re
    ct_re, ct_im = ct_ref[0, 0], ct_ref[0, 1]

    lane_q = lax.broadcasted_iota(jnp.int32, (SSM_GROUP, LANES), 1) // SSM_STATE

    def power(l):
        m_l = jnp.exp(l * l_re)
        return m_l * jnp.cos(l * l_im), m_l * jnp.sin(l * l_im)

    def store_pairs(ref, slot, v_re, v_im):
        for p in range(GROUP_PAIRS):
            t_re = v_re[:, p * LANES:(p + 1) * LANES]
            t_im = v_im[:, p * LANES:(p + 1) * LANES]
            for q in range(2):
                rows = slice(q * GROUP_FLAT + slot * SSM_GROUP, q * GROUP_FLAT + (slot + 1) * SSM_GROUP)
                ref[p, rows, :LANES] = jnp.where(lane_q == q, t_re, 0.0).astype(BF16)
                ref[p, rows, LANES:] = jnp.where(lane_q == q, t_im, 0.0).astype(BF16)

    for s in range(SSM_BLOCK):
        p_re, p_im = power(float(SSM_BLOCK - 1 - s))
        store_pairs(wend_ref, s, p_re * bb_re - p_im * bb_im, p_re * bb_im + p_im * bb_re)
        q_re, q_im = power(float(s + 1))
        store_pairs(wrd_ref, s, q_re * ct_re - q_im * ct_im, -(q_re * ct_im + q_im * ct_re))
        r_re, r_im = power(float(s))
        store_pairs(lag0_ref, s, r_re * ct_re - r_im * ct_im, -(r_re * ct_im + r_im * ct_re))

    last = (SSM_BLOCK - 1) * SSM_GROUP
    col = lax.broadcasted_iota(jnp.int32, (SSM_GROUP, GROUP_FLAT), 1)
    for p in range(GROUP_PAIRS):
        bbar = jnp.concatenate([wend_ref[p, q * GROUP_FLAT + last:q * GROUP_FLAT + last + SSM_GROUP, :]
                                for q in range(2)], axis=0)
        lags = _dot_nt(bbar, lag0_ref[p])
        for q in range(2):
            k_row = lags[q * SSM_GROUP:(q + 1) * SSM_GROUP, q * GROUP_FLAT:(q + 1) * GROUP_FLAT]
            for s in range(SSM_BLOCK):
                shifted = pltpu.roll(k_row, s * SSM_GROUP, 1) if s else k_row
                lagk_ref[2 * p + q, s * SSM_GROUP:(s + 1) * SSM_GROUP, :] = jnp.where(
                    col >= s * SSM_GROUP, shifted, 0.0).astype(BF16)

    p_re, p_im = power(float(SSM_BLOCK))
    pw_ref[0:1, :] = jnp.concatenate([p_re, p_im], axis=1)
    for _ in range(SCAN_SEG_LEN.bit_length() - 1):
        p_re, p_im = p_re * p_re - p_im * p_im, 2.0 * (p_re * p_im)
    pw_ref[1:2, :] = jnp.concatenate([p_re, p_im], axis=1)


def _cmul_add(p_re, p_im, h_re, h_im, e_re, e_im):
    n_re = [pr * hr - pi * hi + er for pr, pi, hr, hi, er in zip(p_re, p_im, h_re, h_im, e_re)]
    n_im = [pr * hi + pi * hr + ei for pr, pi, hr, hi, ei in zip(p_re, p_im, h_re, h_im, e_im)]
    return n_re, n_im


def _s5_block_recurrence(st_ref, pw_ref, fillers):
    fillers = list(fillers)
    total_cost = sum(c for c, _ in fillers)
    issued_cost = [0.0]
    ticks = [0]

    def tick():
        while fillers and issued_cost[0] * 2 * SCAN_SEG_LEN <= ticks[0] * total_cost:
            cost, thunk = fillers.pop(0)
            issued_cost[0] += cost
            thunk()
        ticks[0] += 1

    half = SCAN_TILES // 2

    def bcast(row):
        return [jnp.broadcast_to(row[:, q * LANES:(q + 1) * LANES], (SCAN_SEGS, LANES))
                for q in range(half)]

    p_re, p_im = bcast(pw_ref[0:1, :CLUSTER_STATE]), bcast(pw_ref[0:1, CLUSTER_STATE:])
    seg_rows = lambda i: pl.ds(i, SCAN_SEGS, stride=SCAN_SEG_PITCH)

    def gather(i):
        e = [st_ref[q, seg_rows(i), :] for q in range(SCAN_TILES)]
        return e[:half], e[half:]

    zeros = [jnp.zeros((SCAN_SEGS, LANES), F32)] * half

    def end_step(i, h):
        e_re, e_im = gather(i)
        return _cmul_add(p_re, p_im, h[0], h[1], e_re, e_im)

    h = (zeros, zeros)
    for i in range(SCAN_SEG_LEN):
        tick()
        h = end_step(i, h)
    l_re, l_im = h

    s_re, s_im = bcast(pw_ref[1:2, :CLUSTER_STATE]), bcast(pw_ref[1:2, CLUSTER_STATE:])
    first = lax.broadcasted_iota(jnp.int32, (SCAN_SEGS, LANES), 0) == 0
    shift = lambda v: jnp.where(first, 0.0, pltpu.roll(v, 1, 0))
    c_re, c_im = zeros, zeros
    for _ in range(SCAN_SEGS - 1):
        n_re, n_im = _cmul_add(s_re, s_im, c_re, c_im, l_re, l_im)
        c_re, c_im = [shift(v) for v in n_re], [shift(v) for v in n_im]

    def scan_step(i, h):
        e_re, e_im = gather(i)
        for q in range(half):
            st_ref[q, seg_rows(i), :] = h[0][q]
            st_ref[half + q, seg_rows(i), :] = h[1][q]
        return _cmul_add(p_re, p_im, h[0], h[1], e_re, e_im)

    h = (c_re, c_im)
    for i in range(SCAN_SEG_LEN):
        tick()
        h = scan_step(i, h)
    while fillers:
        fillers.pop(0)[1]()


def _s5_kernel(u_ref, par_ref, bt_ref, ct_ref, dskip_ref, g_ref,
               wend_ref, wrd_ref, lag0_ref, lagk_ref, pw_ref, st_ref):
    r = pl.program_id(1)

    @pl.when(r == 0)
    def _():
        _s5_build_operators(par_ref, bt_ref, ct_ref, wend_ref, wrd_ref, lag0_ref, lagk_ref, pw_ref)

    u = u_ref[0]
    pair_cols = lambda p: slice(2 * p * GROUP_FLAT, 2 * (p + 1) * GROUP_FLAT)
    seg_rows = lambda j: slice(j * SCAN_SEG_PITCH, j * SCAN_SEG_PITCH + SCAN_SEG_LEN)

    for p in range(GROUP_PAIRS):
        e = _dot(u[:, pair_cols(p)], wend_ref[p])
        for tile, cols in ((p, slice(0, LANES)), (GROUP_PAIRS + p, slice(LANES, 2 * LANES))):
            for j in range(SCAN_SEGS):
                st_ref[tile, seg_rows(j), :] = e[j * SCAN_SEG_LEN:(j + 1) * SCAN_SEG_LEN, cols]

    ys = [None] * GROUPS_PER_CLUSTER

    def lag_product(g):
        ys[g] = _dot(u[:, g * GROUP_FLAT:(g + 1) * GROUP_FLAT], lagk_ref[g])

    _s5_block_recurrence(st_ref, pw_ref,
                         [(1, functools.partial(lag_product, g)) for g in range(len(ys))])
    y = jnp.concatenate(ys, axis=1)

    outs = []
    for p in range(GROUP_PAIRS):
        hprev = jnp.concatenate(
            [jnp.concatenate([st_ref[p, seg_rows(j), :], st_ref[GROUP_PAIRS + p, seg_rows(j), :]],
                             axis=1) for j in range(SCAN_SEGS)], axis=0)
        outs.append(_dot_nt(hprev.astype(BF16), wrd_ref[p]))
    y = y + jnp.concatenate(outs, axis=1)
    y = y + dskip_ref[0] * u.astype(F32)
    g_ref[0] = _gelu(y)


def _s5(layer, u_flat, par, bt, ct, dskip, blocks_per_seq):
    nb = u_flat.shape[1]
    rt = S5_ROW_TILE
    assert blocks_per_seq == rt and SCAN_SEG_LEN & (SCAN_SEG_LEN - 1) == 0
    return pl.pallas_call(
        _s5_kernel,
        grid=(N_CLUSTERS, nb // rt),
        in_specs=[
            pl.BlockSpec((1, rt, FLAT), lambda k, r: (k, r, 0)),
            pl.BlockSpec((None, 1, 3, CLUSTER_STATE), lambda k, r: (layer, k, 0, 0)),
            pl.BlockSpec((None, 1, 2, SSM_GROUP, CLUSTER_STATE), lambda k, r: (layer, k, 0, 0, 0)),
            pl.BlockSpec((None, 1, 2, SSM_GROUP, CLUSTER_STATE), lambda k, r: (layer, k, 0, 0, 0)),
            pl.BlockSpec((None, 1, 1, FLAT), lambda k, r: (layer, k, 0, 0)),
        ],
        out_specs=pl.BlockSpec((1, rt, FLAT), lambda k, r: (k, r, 0)),
        out_shape=jax.ShapeDtypeStruct((N_CLUSTERS, nb, FLAT), F32),
        scratch_shapes=[pltpu.VMEM((GROUP_PAIRS, 2 * GROUP_FLAT, 2 * LANES), BF16),
                        pltpu.VMEM((GROUP_PAIRS, 2 * GROUP_FLAT, 2 * LANES), BF16),
                        pltpu.VMEM((GROUP_PAIRS, 2 * GROUP_FLAT, 2 * LANES), BF16),
                        pltpu.VMEM((GROUPS_PER_CLUSTER, GROUP_FLAT, GROUP_FLAT), BF16),
                        pltpu.VMEM((2, 2 * CLUSTER_STATE), F32),
                        pltpu.VMEM((SCAN_TILES, SCAN_SEGS * SCAN_SEG_PITCH, LANES), F32)],
        compiler_params=pltpu.CompilerParams(
            dimension_semantics=("arbitrary", "arbitrary"), vmem_limit_bytes=VMEM_S5),
        name="s5",
    )(u_flat, par, bt, ct, dskip)


def _s5_params(A_re, A_im, log_dt, B_re, B_im, C_re, C_im, D_skip):
    depth = A_re.shape[0]
    k, gpc, n, c = N_CLUSTERS, GROUPS_PER_CLUSTER, SSM_STATE, SSM_GROUP
    rows = lambda a: a.reshape(depth, k, 1, CLUSTER_STATE)
    ldt = jnp.broadcast_to(log_dt[..., None], A_re.shape)
    par = jnp.concatenate([rows(A_re), rows(A_im), rows(ldt)], axis=2)
    b = jnp.stack([B_re, B_im], axis=1).reshape(depth, 2, k, gpc, n, c)
    bt = b.transpose(0, 2, 1, 5, 3, 4).reshape(depth, k, 2, c, CLUSTER_STATE)
    cc = jnp.stack([C_re, C_im], axis=1).reshape(depth, 2, k, gpc, c, n)
    ct = cc.transpose(0, 2, 1, 4, 3, 5).reshape(depth, k, 2, c, CLUSTER_STATE)
    dskip = jnp.broadcast_to(D_skip.reshape(depth, k, gpc, 1, c), (depth, k, gpc, SSM_BLOCK, c))
    return par, bt, ct, dskip.reshape(depth, k, 1, FLAT)


def _mix_out_kernel(final, rider_kinds, *refs):
    n_r = len(rider_kinds)
    x_ref, g_ref, yb_ref, yc_ref, vec_ref, wglu_ref, wout_ref, wg_ref, wu_ref, wd_ref = refs[:10]
    rider_src = refs[10:10 + n_r]
    o_ref = refs[10 + n_r]
    rider_dst = refs[11 + n_r:11 + 2 * n_r]
    (gs_ref,) = refs[11 + 2 * n_r:]
    _cast_riders(rider_kinds, rider_src, rider_dst)

    bglu = vec_ref[VEC_GLU:VEC_GLU + 1, :D_SSM]
    gffn = vec_ref[VEC_GFFN:VEC_GFFN + 1, :]
    tm = x_ref.shape[0]
    half_block = SSM_BLOCK // 2
    for k in range(N_CLUSTERS):
        for hf in range(2):
            groups = [g_ref[k, :, g * GROUP_FLAT + hf * LANES:g * GROUP_FLAT + (hf + 1) * LANES]
                      for g in range(GROUPS_PER_CLUSTER)]
            for i, tile in enumerate(_chunk_transpose(groups)):
                gs_ref[k, pl.ds(hf * half_block + i, tm // SSM_BLOCK, stride=SSM_BLOCK), :] = tile
    g = jnp.concatenate([gs_ref[k] for k in range(N_CLUSTERS)], axis=1)
    ya = g * jax.nn.sigmoid(_dot(g.astype(BF16), wglu_ref[...]) + bglu)
    ymix = jnp.concatenate([ya.astype(BF16), yb_ref[...], yc_ref[...]], axis=1)
    x1 = x_ref[...] + _dot(ymix, wout_ref[...])
    h = _rms(x1, gffn).astype(BF16)
    acc = x1
    for c0, c1 in FF_CHUNKS:
        gate = _dot(h, wg_ref[:, c0:c1])
        up = _dot(h, wu_ref[:, c0:c1])
        act = (gate * jax.nn.sigmoid(gate) * up).astype(BF16)
        acc = acc + _dot(act, wd_ref[c0:c1, :])
    if final:
        acc = _rms(acc, vec_ref[VEC_GFINAL:VEC_GFINAL + 1, :])
    o_ref[...] = acc


def _mix_out(layer, x2d, g3, yb, yc, vecs, wglu, wout, wg, wu, wd, final, riders=()):
    m = x2d.shape[0]
    tm = TOKEN_TILE
    n_steps = m // tm
    d_ff = wg.shape[-1]
    resident = lambda shape: pl.BlockSpec(shape, lambda i: (0, 0), pipeline_mode=pl.Buffered(1))
    r_specs = [_rider_specs(n_steps, lyr, w) for _, w, lyr in riders]
    return pl.pallas_call(
        functools.partial(_mix_out_kernel, final, tuple(k for k, _, _ in riders)),
        grid=(n_steps,),
        in_specs=[
            pl.BlockSpec((tm, D_MODEL), lambda i: (i, 0)),
            pl.BlockSpec((N_CLUSTERS, tm // SSM_BLOCK, FLAT), lambda i: (0, i, 0)),
            pl.BlockSpec((tm, D_POOL), lambda i: (i, 0)),
            pl.BlockSpec((tm, D_SGU), lambda i: (i, 0)),
            _layer_spec(layer, (VEC_ROWS, D_MODEL)),
            _layer_spec(layer, (D_SSM, D_SSM)),
            resident((D_MODEL, D_MODEL)),
            resident((D_MODEL, d_ff)),
            resident((D_MODEL, d_ff)),
            resident((d_ff, D_MODEL)),
        ] + [s for s, _, _ in r_specs],
        out_specs=[pl.BlockSpec((tm, D_MODEL), lambda i: (i, 0))] + [d for _, d, _ in r_specs],
        out_shape=[jax.ShapeDtypeStruct((m, D_MODEL), F32)] + [o for _, _, o in r_specs],
        scratch_shapes=[pltpu.VMEM((N_CLUSTERS, tm, LANES), F32)],
        compiler_params=pltpu.CompilerParams(
            dimension_semantics=("arbitrary",), vmem_limit_bytes=VMEM_MIX_OUT),
        name="mix_out",
    )(x2d, g3, yb, yc, vecs, wglu, wout, wg, wu, wd, *[w for _, w, _ in riders])


def kernel(x, g_mix, w_in, A_re, A_im, log_dt, B_re, B_im, C_re, C_im, D_skip, w_glu, b_glu,
           w_pool, pool_scale, sgu_ln_g, sgu_ln_b, w_spatial, b_spatial, w_out, g_ffn,
           w_gate, w_up, w_down, g_final):
    bsz, seq, d = x.shape
    depth = w_in.shape[0]
    m = bsz * seq
    assert d == D_MODEL and seq % MIX_IN_TILE == 0 and MIX_IN_TILE % (2 * CHUNK) == 0
    assert m % TOKEN_TILE == 0
    assert (seq // SSM_BLOCK) % S5_ROW_TILE == 0
    x2d = x.reshape(m, D_MODEL)
    tril = jnp.tril(jnp.ones((CHUNK, CHUNK), dtype=bool))
    eye = jnp.eye(len(POOL_WINDOWS), dtype=F32)
    wpool_bd = jnp.einsum('dgij,gh->dgihj', w_pool, eye).reshape(depth, D_POOL, D_POOL).astype(BF16)
    ws = jnp.where(tril, w_spatial, 0.0).astype(BF16)
    ws = jnp.concatenate([ws[:, 0::2], ws[:, 1::2]], axis=-1)
    bsp = jnp.repeat(jnp.swapaxes(b_spatial, 1, 2), SGU_HEAD_DIM, axis=2)
    par, bt, ct, dskip = _s5_params(A_re, A_im, log_dt, B_re, B_im, C_re, C_im, D_skip)
    vec_rows = [None] * VEC_ROWS
    vec_rows[VEC_GMIX] = g_mix
    vec_rows[VEC_GFFN] = g_ffn
    vec_rows[VEC_SGU] = jnp.concatenate([pool_scale, sgu_ln_g, sgu_ln_b], axis=-1)
    vec_rows[VEC_GLU] = jnp.pad(b_glu, ((0, 0), (0, D_MODEL - D_SSM)))
    vec_rows[VEC_GFINAL] = jnp.broadcast_to(g_final, (depth, D_MODEL))
    zero_row = jnp.zeros((depth, D_MODEL), F32)
    vecs = jnp.stack([zero_row if r is None else r for r in vec_rows], axis=1)
    wglu = w_glu.astype(BF16)
    b_end, u_end = D_SSM + D_POOL, D_SSM + D_POOL + D_SGU
    win = jnp.concatenate([w_in[0, :, u_end:], w_in[0, :, b_end:u_end], w_in[0, :, :b_end]],
                          axis=-1).astype(BF16)
    ffn_stacks = (w_out, w_gate, w_up, w_down)
    ffn = None
    for l in range(depth):
        riders = tuple(("plain", w, 0) for w in ffn_stacks) if l == 0 else ()
        u_flat, yb, yc, *cast = _mix_in(l, x2d, vecs, win, wpool_bd, ws, bsp, seq, riders)
        if l == 0:
            ffn = cast
        g3 = _s5(l, u_flat, par, bt, ct, dskip, seq // SSM_BLOCK)
        last = l == depth - 1
        riders = () if last else (("w_in", w_in, l + 1),) + tuple(
            ("plain", w, l + 1) for w in ffn_stacks)
        x2d, *cast = _mix_out(l, x2d, g3, yb, yc, vecs, wglu, *ffn, final=last, riders=riders)
        if not last:
            win, ffn = cast[0], cast[1:]
    return x2d.reshape(bsz, seq, D_MODEL)
```

```python
import functools
import math

import jax
import jax.numpy as jnp
from jax import lax
from jax.experimental import pallas as pl
from jax.experimental.pallas import tpu as pltpu

D_MODEL = 1024
D_SSM = 384
SSM_GROUP = 16
N_SSM_GROUPS = D_SSM // SSM_GROUP
SSM_STATE = 64
POOL_WINDOWS = (2, 4, 8, 16)
POOL_GROUP = 64
D_POOL = len(POOL_WINDOWS) * POOL_GROUP
MAX_WINDOW = max(POOL_WINDOWS)
SGU_HEADS = 6
SGU_HEAD_DIM = 64
D_SGU = SGU_HEADS * SGU_HEAD_DIM
CHUNK = 128
D_IN = D_SSM + D_POOL + 2 * D_SGU
EPS = 1e-6

LANES = 128
SSM_BLOCK = 16
N_CLUSTERS = D_SSM // LANES
GROUPS_PER_CLUSTER = LANES // SSM_GROUP
CLUSTER_STATE = GROUPS_PER_CLUSTER * SSM_STATE
FLAT = SSM_BLOCK * LANES
GROUP_FLAT = SSM_BLOCK * SSM_GROUP
GROUP_PAIRS = GROUPS_PER_CLUSTER // 2

MXU_WIDTH = 256
BF16_SUBLANES = 16
VEC_ROWS = 8
VEC_GMIX, VEC_GFFN, VEC_SGU, VEC_GLU, VEC_GFINAL = 0, 1, 2, 3, 4
TOKEN_TILE = 512
MIX_IN_TILE = 1024
S5_ROW_TILE = 512
SUBLANES = 8
SCAN_SEGS = SUBLANES
SCAN_SEG_LEN = S5_ROW_TILE // SCAN_SEGS
SCAN_SEG_PITCH = SCAN_SEG_LEN + SUBLANES
SCAN_TILES = 2 * CLUSTER_STATE // LANES
FF_CHUNKS = ((0, 768), (768, 1536), (1536, 2304), (2304, 2816))
V7X_VMEM_BYTES = 64 * 1024 * 1024
VMEM_MIX_IN = VMEM_S5 = VMEM_MIX_OUT = V7X_VMEM_BYTES

F32 = jnp.float32
BF16 = jnp.bfloat16
NT_DIMS = (((1,), (1,)), ((), ()))


def _gelu(x):
    c = math.sqrt(2.0 / math.pi)
    return x * (0.5 + 0.5 * jnp.tanh(x * (c + (c * 0.044715) * (x * x))))


def _rms(x, g):
    ms = jnp.mean(x * x, axis=-1, keepdims=True)
    return x * lax.rsqrt(ms + EPS) * g


def _dot(a, b):
    return jnp.dot(a, b, preferred_element_type=F32)


def _dot_nt(a, b):
    return lax.dot_general(a, b, NT_DIMS, preferred_element_type=F32)


def _chunk_transpose(tiles):
    n = len(tiles)
    chunk = LANES // n
    v = list(tiles)
    cidx = lax.broadcasted_iota(jnp.int32, v[0].shape, 1) // chunk
    k = 1
    while k < n:
        keep = (cidx & k) == 0
        nv = list(v)
        for a in range(n):
            if a & k:
                continue
            lo, hi = v[a], v[a + k]
            nv[a] = jnp.where(keep, lo, pltpu.roll(hi, k * chunk, 1))
            nv[a + k] = jnp.where(keep, pltpu.roll(lo, LANES - k * chunk, 1), hi)
        v = nv
        k *= 2
    return v


def _cast_riders(kinds, src_refs, dst_refs):
    for kind, src, dst in zip(kinds, src_refs, dst_refs):
        if kind == "w_in":
            b_end, u_end = D_SSM + D_POOL, D_SSM + D_POOL + D_SGU
            dst[:, :D_SGU] = src[:, u_end:].astype(BF16)
            dst[:, D_SGU:2 * D_SGU] = src[:, b_end:u_end].astype(BF16)
            dst[:, 2 * D_SGU:] = src[:, :b_end].astype(BF16)
        else:
            dst[...] = src[...].astype(BF16)


def _rider_specs(n_steps, layer, w):
    rows, cols = w.shape[1:]
    share = 1
    while (rows * share) % n_steps or (rows * share // n_steps) % BF16_SUBLANES:
        share *= 2
    slab = rows * share // n_steps
    which = lambda i: jnp.minimum(i, n_steps - 1) // share
    src = pl.BlockSpec((None, slab, cols), lambda i: (layer, which(i), 0))
    dst = pl.BlockSpec((slab, cols), lambda i: (which(i), 0))
    return src, dst, jax.ShapeDtypeStruct((rows, cols), BF16)


def _mix_in_kernel(tiles_per_seq, rider_kinds, *refs):
    n_r = len(rider_kinds)
    x_ref, vec_ref, win_ref, wpool_ref, ws_ref, bsp_ref = refs[:6]
    rider_src = refs[6:6 + n_r]
    za_ref, yb_ref, yc_ref = refs[6 + n_r:9 + n_r]
    rider_dst = refs[9 + n_r:9 + 2 * n_r]
    halo_ref, zs_ref = refs[9 + 2 * n_r:]
    _cast_riders(rider_kinds, rider_src, rider_dst)

    tm = x_ref.shape[0]
    i = pl.program_id(0)
    seq_tile = i % tiles_per_seq
    gmix = vec_ref[VEC_GMIX:VEC_GMIX + 1, :]
    pscale = vec_ref[VEC_SGU:VEC_SGU + 1, :D_POOL]
    lng = vec_ref[VEC_SGU:VEC_SGU + 1, D_POOL:D_POOL + D_SGU]
    lnb = vec_ref[VEC_SGU:VEC_SGU + 1, D_POOL + D_SGU:]

    h = _rms(x_ref[...], gmix).astype(BF16)
    z_sgu = _dot(h, win_ref[:, :2 * D_SGU])
    z_rest = _dot(h, win_ref[:, 2 * D_SGU:])
    zv = z_sgu[:, :D_SGU]
    zu = z_sgu[:, D_SGU:]
    zb = z_rest[:, D_SSM:]

    for k in range(N_CLUSTERS):
        zs_ref[k] = z_rest[:, k * LANES:(k + 1) * LANES]
    half_block = SSM_BLOCK // 2
    for k in range(N_CLUSTERS):
        for hf in range(2):
            slots = [zs_ref[k, pl.ds(hf * half_block + i, tm // SSM_BLOCK, stride=SSM_BLOCK), :]
                     for i in range(half_block)]
            for g, tile in enumerate(_chunk_transpose(slots)):
                col = g * GROUP_FLAT + hf * LANES
                za_ref[k, :, col:col + LANES] = tile.astype(BF16)

    halo = jnp.where(seq_tile == 0, 0.0, halo_ref[...])
    halo_ref[...] = zb[tm - MAX_WINDOW:, :]
    ext = jnp.concatenate([halo, zb], axis=0)
    lane = lax.broadcasted_iota(jnp.int32, (tm, LANES), 1)
    low = lane < POOL_GROUP
    low_h = lax.broadcasted_iota(jnp.int32, (MAX_WINDOW, LANES), 1) < POOL_GROUP
    low_r = lax.broadcasted_iota(jnp.int32, (1, LANES), 1) < POOL_GROUP
    pos1 = (seq_tile * tm + 1
            + lax.broadcasted_iota(jnp.int32, (MAX_WINDOW, LANES), 0)).astype(F32)

    def window_mean(total, w_low, w_high):
        cnt = jnp.where(low_h, jnp.minimum(pos1, w_low), jnp.minimum(pos1, w_high))
        inv_w = jnp.where(low_r, 1.0 / w_low, 1.0 / w_high)
        return jnp.concatenate([total[:MAX_WINDOW] / cnt, total[MAX_WINDOW:] * inv_w], axis=0)

    e0 = ext[:, :LANES]
    s2 = e0 + pltpu.roll(e0, 1, 0)
    s4 = s2 + pltpu.roll(s2, 2, 0)
    sum0 = jnp.where(low, s2[MAX_WINDOW:], s4[MAX_WINDOW:])
    e1 = ext[:, LANES:]
    t2 = e1 + pltpu.roll(e1, 1, 0)
    t4 = t2 + pltpu.roll(t2, 2, 0)
    t8 = t4 + pltpu.roll(t4, 4, 0)
    t16 = t8 + pltpu.roll(t8, 8, 0)
    sum1 = jnp.where(low, t8[MAX_WINDOW:], t16[MAX_WINDOW:])
    pooled = jnp.concatenate([window_mean(sum0, 2.0, 4.0), window_mean(sum1, 8.0, 16.0)],
                             axis=1) - zb
    yb = _dot(pooled.astype(BF16), wpool_ref[...]) * pscale
    yb_ref[...] = yb.astype(BF16)

    u = _gelu(zu)
    v = _gelu(zv)
    mu = jnp.mean(v, axis=-1, keepdims=True)
    vc = v - mu
    var = jnp.mean(vc * vc, axis=-1, keepdims=True)
    vn = (vc * lax.rsqrt(var + EPS) * lng + lnb).astype(BF16)
    lane_c = lax.broadcasted_iota(jnp.int32, (CHUNK, LANES), 1)
    low_c = lane_c < SGU_HEAD_DIM
    zero = jnp.zeros((), BF16)
    for c in range(0, tm // CHUNK, 2):
        rows_a = slice(c * CHUNK, (c + 1) * CHUNK)
        rows_b = slice((c + 1) * CHUNK, (c + 2) * CHUNK)
        parts_a, parts_b = [], []
        for p in range(SGU_HEADS // 2):
            va = vn[rows_a, p * LANES:(p + 1) * LANES]
            vb = vn[rows_b, p * LANES:(p + 1) * LANES]
            top = jnp.concatenate([jnp.where(low_c, va, zero), jnp.where(low_c, vb, zero)], axis=1)
            bot = jnp.concatenate([jnp.where(low_c, zero, va), jnp.where(low_c, zero, vb)], axis=1)
            out = _dot(ws_ref[p], jnp.concatenate([top, bot], axis=0))
            parts_a.append(out[:, :LANES])
            parts_b.append(out[:, LANES:])
        for rows, parts in ((rows_a, parts_a), (rows_b, parts_b)):
            mixed = jnp.concatenate(parts, axis=1) + bsp_ref[...]
            yc_ref[rows, :] = (u[rows, :] * mixed).astype(BF16)


def _layer_spec(layer, shape):
    zeros = (0,) * len(shape)
    return pl.BlockSpec((None,) + tuple(shape), lambda *_: (layer,) + zeros)


def _mix_in(layer, x2d, vecs, win, wpool_bd, ws, bsp, seq_len, riders=()):
    m = x2d.shape[0]
    tm = MIX_IN_TILE
    n_steps = m // tm
    spec = functools.partial(_layer_spec, layer)
    r_specs = [_rider_specs(n_steps, lyr, w) for _, w, lyr in riders]
    return pl.pallas_call(
        functools.partial(_mix_in_kernel, seq_len // tm, tuple(k for k, _, _ in riders)),
        grid=(n_steps,),
        in_specs=[
            pl.BlockSpec((tm, D_MODEL), lambda i: (i, 0)),
            spec((VEC_ROWS, D_MODEL)),
            pl.BlockSpec((D_MODEL, D_IN), lambda i: (0, 0)),
            spec((D_POOL, D_POOL)),
            spec((SGU_HEADS // 2, CHUNK, 2 * CHUNK)),
            spec((CHUNK, D_SGU)),
        ] + [s for s, _, _ in r_specs],
        out_specs=[
            pl.BlockSpec((N_CLUSTERS, tm // SSM_BLOCK, FLAT), lambda i: (0, i, 0)),
            pl.BlockSpec((tm, D_POOL), lambda i: (i, 0)),
            pl.BlockSpec((tm, D_SGU), lambda i: (i, 0)),
        ] + [d for _, d, _ in r_specs],
        out_shape=[
            jax.ShapeDtypeStruct((N_CLUSTERS, m // SSM_BLOCK, FLAT), BF16),
            jax.ShapeDtypeStruct((m, D_POOL), BF16),
            jax.ShapeDtypeStruct((m, D_SGU), BF16),
        ] + [o for _, _, o in r_specs],
        scratch_shapes=[pltpu.VMEM((MAX_WINDOW, D_POOL), F32),
                        pltpu.VMEM((N_CLUSTERS, tm, LANES), F32)],
        compiler_params=pltpu.CompilerParams(
            dimension_semantics=("arbitrary",), vmem_limit_bytes=VMEM_MIX_IN),
        name="mix_in",
    )(x2d, vecs, win, wpool_bd, ws, bsp, *[w for _, w, _ in riders])


def _s5_build_operators(par_ref, bt_ref, ct_ref, wend_ref, wrd_ref, lag0_ref, lagk_ref, pw_ref):
    a_re = par_ref[0, 0:1, :]
    a_im = par_ref[0, 1:2, :]
    dt = jnp.exp(par_ref[0, 2:3, :])
    l_re = a_re * dt
    l_im = a_im * dt
    mag = jnp.exp(l_re)
    ar = mag * jnp.cos(l_im)
    ai = mag * jnp.sin(l_im)
    den = a_re * a_re + a_im * a_im
    f_re = ((ar - 1.0) * a_re + ai * a_im) / den
    f_im = (ai * a_re - (ar - 1.0) * a_im) / den
    bt_re, bt_im = bt_ref[0, 0], bt_ref[0, 1]
    bb_re = f_re * bt_re - f_im * bt_im
    bb_im = f_re * bt_im + f_im * bt_re
    ct_re, ct_im = ct_ref[0, 0], ct_ref[0, 1]

    lane_q = lax.broadcasted_iota(jnp.int32, (SSM_GROUP, LANES), 1) // SSM_STATE

    def power(l):
        m_l = jnp.exp(l * l_re)
        return m_l * jnp.cos(l * l_im), m_l * jnp.sin(l * l_im)

    def store_pairs(ref, slot, v_re, v_im):
        for p in range(GROUP_PAIRS):
            t_re = v_re[:, p * LANES:(p + 1) * LANES]
            t_im = v_im[:, p * LANES:(p + 1) * LANES]
            for q in range(2):
                rows = slice(q * GROUP_FLAT + slot * SSM_GROUP, q * GROUP_FLAT + (slot + 1) * SSM_GROUP)
                ref[p, rows, :LANES] = jnp.where(lane_q == q, t_re, 0.0).astype(BF16)
                ref[p, rows, LANES:] = jnp.where(lane_q == q, t_im, 0.0).astype(BF16)

    for s in range(SSM_BLOCK):
        p_re, p_im = power(float(SSM_BLOCK - 1 - s))
        store_pairs(wend_ref, s, p_re * bb_re - p_im * bb_im, p_re * bb_im + p_im * bb_re)
        q_re, q_im = power(float(s + 1))
        store_pairs(wrd_ref, s, q_re * ct_re - q_im * ct_im, -(q_re * ct_im + q_im * ct_re))
        r_re, r_im = power(float(s))
        store_pairs(lag0_ref, s, r_re * ct_re - r_im * ct_im, -(r_re * ct_im + r_im * ct_re))

    last = (SSM_BLOCK - 1) * SSM_GROUP
    col = lax.broadcasted_iota(jnp.int32, (SSM_GROUP, GROUP_FLAT), 1)
    for p in range(GROUP_PAIRS):
        bbar = jnp.concatenate([wend_ref[p, q * GROUP_FLAT + last:q * GROUP_FLAT + last + SSM_GROUP, :]
                                for q in range(2)], axis=0)
        lags = _dot_nt(bbar, lag0_ref[p])
        for q in range(2):
            k_row = lags[q * SSM_GROUP:(q + 1) * SSM_GROUP, q * GROUP_FLAT:(q + 1) * GROUP_FLAT]
            for s in range(SSM_BLOCK):
                shifted = pltpu.roll(k_row, s * SSM_GROUP, 1) if s else k_row
                lagk_ref[2 * p + q, s * SSM_GROUP:(s + 1) * SSM_GROUP, :] = jnp.where(
                    col >= s * SSM_GROUP, shifted, 0.0).astype(BF16)

    p_re, p_im = power(float(SSM_BLOCK))
    pw_ref[0:1, :] = jnp.concatenate([p_re, p_im], axis=1)
    for _ in range(SCAN_SEG_LEN.bit_length() - 1):
        p_re, p_im = p_re * p_re - p_im * p_im, 2.0 * (p_re * p_im)
    pw_ref[1:2, :] = jnp.concatenate([p_re, p_im], axis=1)


def _cmul_add(p_re, p_im, h_re, h_im, e_re, e_im):
    n_re = [pr * hr - pi * hi + er for pr, pi, hr, hi, er in zip(p_re, p_im, h_re, h_im, e_re)]
    n_im = [pr * hi + pi * hr + ei for pr, pi, hr, hi, ei in zip(p_re, p_im, h_re, h_im, e_im)]
    return n_re, n_im


def _s5_block_recurrence(st_ref, pw_ref, fillers):
    fillers = list(fillers)
    total_cost = sum(c for c, _ in fillers)
    issued_cost = [0.0]
    ticks = [0]

    def tick():
        while fillers and issued_cost[0] * 2 * SCAN_SEG_LEN <= ticks[0] * total_cost:
            cost, thunk = fillers.pop(0)
            issued_cost[0] += cost
            thunk()
        ticks[0] += 1

    half = SCAN_TILES // 2

    def bcast(row):
        return [jnp.broadcast_to(row[:, q * LANES:(q + 1) * LANES], (SCAN_SEGS, LANES))
                for q in range(half)]

    p_re, p_im = bcast(pw_ref[0:1, :CLUSTER_STATE]), bcast(pw_ref[0:1, CLUSTER_STATE:])
    seg_rows = lambda i: pl.ds(i, SCAN_SEGS, stride=SCAN_SEG_PITCH)

    def gather(i):
        e = [st_ref[q, seg_rows(i), :] for q in range(SCAN_TILES)]
        return e[:half], e[half:]

    zeros = [jnp.zeros((SCAN_SEGS, LANES), F32)] * half

    def end_step(i, h):
        e_re, e_im = gather(i)
        return _cmul_add(p_re, p_im, h[0], h[1], e_re, e_im)

    h = (zeros, zeros)
    for i in range(SCAN_SEG_LEN):
        tick()
        h = end_step(i, h)
    l_re, l_im = h

    s_re, s_im = bcast(pw_ref[1:2, :CLUSTER_STATE]), bcast(pw_ref[1:2, CLUSTER_STATE:])
    first = lax.broadcasted_iota(jnp.int32, (SCAN_SEGS, LANES), 0) == 0
    shift = lambda v: jnp.where(first, 0.0, pltpu.roll(v, 1, 0))
    c_re, c_im = zeros, zeros
    for _ in range(SCAN_SEGS - 1):
        n_re, n_im = _cmul_add(s_re, s_im, c_re, c_im, l_re, l_im)
        c_re, c_im = [shift(v) for v in n_re], [shift(v) for v in n_im]

    def scan_step(i, h):
        e_re, e_im = gather(i)
        for q in range(half):
            st_ref[q, seg_rows(i), :] = h[0][q]
            st_ref[half + q, seg_rows(i), :] = h[1][q]
        return _cmul_add(p_re, p_im, h[0], h[1], e_re, e_im)

    h = (c_re, c_im)
    for i in range(SCAN_SEG_LEN):
        tick()
        h = scan_step(i, h)
    while fillers:
        fillers.pop(0)[1]()


def _s5_kernel(u_ref, par_ref, bt_ref, ct_ref, dskip_ref, g_ref,
               wend_ref, wrd_ref, lag0_ref, lagk_ref, pw_ref, st_ref):
    r = pl.program_id(1)

    @pl.when(r == 0)
    def _():
        _s5_build_operators(par_ref, bt_ref, ct_ref, wend_ref, wrd_ref, lag0_ref, lagk_ref, pw_ref)

    u = u_ref[0]
    pair_cols = lambda p: slice(2 * p * GROUP_FLAT, 2 * (p + 1) * GROUP_FLAT)
    seg_rows = lambda j: slice(j * SCAN_SEG_PITCH, j * SCAN_SEG_PITCH + SCAN_SEG_LEN)

    for p in range(GROUP_PAIRS):
        e = _dot(u[:, pair_cols(p)], wend_ref[p])
        for tile, cols in ((p, slice(0, LANES)), (GROUP_PAIRS + p, slice(LANES, 2 * LANES))):
            for j in range(SCAN_SEGS):
                st_ref[tile, seg_rows(j), :] = e[j * SCAN_SEG_LEN:(j + 1) * SCAN_SEG_LEN, cols]

    ys = [None] * GROUPS_PER_CLUSTER

    def lag_product(g):
        ys[g] = _dot(u[:, g * GROUP_FLAT:(g + 1) * GROUP_FLAT], lagk_ref[g])

    _s5_block_recurrence(st_ref, pw_ref,
                         [(1, functools.partial(lag_product, g)) for g in range(len(ys))])
    y = jnp.concatenate(ys, axis=1)

    outs = []
    for p in range(GROUP_PAIRS):
        hprev = jnp.concatenate(
            [jnp.concatenate([st_ref[p, seg_rows(j), :], st_ref[GROUP_PAIRS + p, seg_rows(j), :]],
                             axis=1) for j in range(SCAN_SEGS)], axis=0)
        outs.append(_dot_nt(hprev.astype(BF16), wrd_ref[p]))
    y = y + jnp.concatenate(outs, axis=1)
    y = y + dskip_ref[0] * u.astype(F32)
    g_ref[0] = _gelu(y)


def _s5(layer, u_flat, par, bt, ct, dskip, blocks_per_seq):
    nb = u_flat.shape[1]
    rt = S5_ROW_TILE
    assert blocks_per_seq == rt and SCAN_SEG_LEN & (SCAN_SEG_LEN - 1) == 0
    return pl.pallas_call(
        _s5_kernel,
        grid=(N_CLUSTERS, nb // rt),
        in_specs=[
            pl.BlockSpec((1, rt, FLAT), lambda k, r: (k, r, 0)),
            pl.BlockSpec((None, 1, 3, CLUSTER_STATE), lambda k, r: (layer, k, 0, 0)),
            pl.BlockSpec((None, 1, 2, SSM_GROUP, CLUSTER_STATE), lambda k, r: (layer, k, 0, 0, 0)),
            pl.BlockSpec((None, 1, 2, SSM_GROUP, CLUSTER_STATE), lambda k, r: (layer, k, 0, 0, 0)),
            pl.BlockSpec((None, 1, 1, FLAT), lambda k, r: (layer, k, 0, 0)),
        ],
        out_specs=pl.BlockSpec((1, rt, FLAT), lambda k, r: (k, r, 0)),
        out_shape=jax.ShapeDtypeStruct((N_CLUSTERS, nb, FLAT), F32),
        scratch_shapes=[pltpu.VMEM((GROUP_PAIRS, 2 * GROUP_FLAT, 2 * LANES), BF16),
                        pltpu.VMEM((GROUP_PAIRS, 2 * GROUP_FLAT, 2 * LANES), BF16),
                        pltpu.VMEM((GROUP_PAIRS, 2 * GROUP_FLAT, 2 * LANES), BF16),
                        pltpu.VMEM((GROUPS_PER_CLUSTER, GROUP_FLAT, GROUP_FLAT), BF16),
                        pltpu.VMEM((2, 2 * CLUSTER_STATE), F32),
                        pltpu.VMEM((SCAN_TILES, SCAN_SEGS * SCAN_SEG_PITCH, LANES), F32)],
        compiler_params=pltpu.CompilerParams(
            dimension_semantics=("arbitrary", "arbitrary"), vmem_limit_bytes=VMEM_S5),
        name="s5",
    )(u_flat, par, bt, ct, dskip)


def _s5_params(A_re, A_im, log_dt, B_re, B_im, C_re, C_im, D_skip):
    depth = A_re.shape[0]
    k, gpc, n, c = N_CLUSTERS, GROUPS_PER_CLUSTER, SSM_STATE, SSM_GROUP
    rows = lambda a: a.reshape(depth, k, 1, CLUSTER_STATE)
    ldt = jnp.broadcast_to(log_dt[..., None], A_re.shape)
    par = jnp.concatenate([rows(A_re), rows(A_im), rows(ldt)], axis=2)
    b = jnp.stack([B_re, B_im], axis=1).reshape(depth, 2, k, gpc, n, c)
    bt = b.transpose(0, 2, 1, 5, 3, 4).reshape(depth, k, 2, c, CLUSTER_STATE)
    cc = jnp.stack([C_re, C_im], axis=1).reshape(depth, 2, k, gpc, c, n)
    ct = cc.transpose(0, 2, 1, 4, 3, 5).reshape(depth, k, 2, c, CLUSTER_STATE)
    dskip = jnp.broadcast_to(D_skip.reshape(depth, k, gpc, 1, c), (depth, k, gpc, SSM_BLOCK, c))
    return par, bt, ct, dskip.reshape(depth, k, 1, FLAT)


def _mix_out_kernel(final, rider_kinds, *refs):
    n_r = len(rider_kinds)
    (x_ref, g0_ref, gnext_ref, yb_ref, yc_ref, vec_ref, wglu_ref, wout_ref, wg_ref, wu_ref,
     wd_ref) = refs[:11]
    rider_src = refs[11:11 + n_r]
    o_ref = refs[11 + n_r]
    rider_dst = refs[12 + n_r:12 + 2 * n_r]
    gs_ref, ya_ref = refs[12 + 2 * n_r:]
    _cast_riders(rider_kinds, rider_src, rider_dst)

    bglu = vec_ref[VEC_GLU:VEC_GLU + 1, :D_SSM]
    gffn = vec_ref[VEC_GFFN:VEC_GFFN + 1, :]
    tm = x_ref.shape[0]
    half_block = SSM_BLOCK // 2

    def glu_branch(src_ref):
        for k in range(N_CLUSTERS):
            for hf in range(2):
                groups = [src_ref[k, :, g * GROUP_FLAT + hf * LANES:g * GROUP_FLAT + (hf + 1) * LANES]
                          for g in range(GROUPS_PER_CLUSTER)]
                for i, tile in enumerate(_chunk_transpose(groups)):
                    rows = pl.ds(hf * half_block + i, tm // SSM_BLOCK, stride=SSM_BLOCK)
                    gs_ref[k, rows, :] = tile
        g = jnp.concatenate([gs_ref[k] for k in range(N_CLUSTERS)], axis=1)
        ya = g * jax.nn.sigmoid(_dot(g.astype(BF16), wglu_ref[...]) + bglu)
        ya_ref[...] = ya.astype(BF16)

    @pl.when(pl.program_id(0) == 0)
    def _():
        glu_branch(g0_ref)

    ymix = jnp.concatenate([ya_ref[...], yb_ref[...], yc_ref[...]], axis=1)
    x1 = x_ref[...] + _dot(ymix, wout_ref[...])
    h = _rms(x1, gffn).astype(BF16)
    acc = x1
    for n, (c0, c1) in enumerate(FF_CHUNKS):
        gate = _dot(h, wg_ref[:, c0:c1])
        up = _dot(h, wu_ref[:, c0:c1])
        act = (gate * jax.nn.sigmoid(gate) * up).astype(BF16)
        acc = acc + _dot(act, wd_ref[c0:c1, :])
        if n == 0:
            glu_branch(gnext_ref)
    if final:
        acc = _rms(acc, vec_ref[VEC_GFINAL:VEC_GFINAL + 1, :])
    o_ref[...] = acc


def _mix_out(layer, x2d, g3, yb, yc, vecs, wglu, wout, wg, wu, wd, final, riders=()):
    m = x2d.shape[0]
    tm = TOKEN_TILE
    n_steps = m // tm
    d_ff = wg.shape[-1]
    resident = lambda shape: pl.BlockSpec(shape, lambda i: (0, 0), pipeline_mode=pl.Buffered(1))
    r_specs = [_rider_specs(n_steps, lyr, w) for _, w, lyr in riders]
    return pl.pallas_call(
        functools.partial(_mix_out_kernel, final, tuple(k for k, _, _ in riders)),
        grid=(n_steps,),
        in_specs=[
            pl.BlockSpec((tm, D_MODEL), lambda i: (i, 0)),
            pl.BlockSpec((N_CLUSTERS, tm // SSM_BLOCK, FLAT), lambda i: (0, 0, 0),
                         pipeline_mode=pl.Buffered(1)),
            pl.BlockSpec((N_CLUSTERS, tm // SSM_BLOCK, FLAT),
                         lambda i: (0, jnp.minimum(i + 1, n_steps - 1), 0)),
            pl.BlockSpec((tm, D_POOL), lambda i: (i, 0)),
            pl.BlockSpec((tm, D_SGU), lambda i: (i, 0)),
            _layer_spec(layer, (VEC_ROWS, D_MODEL)),
            _layer_spec(layer, (D_SSM, D_SSM)),
            resident((D_MODEL, D_MODEL)),
            resident((D_MODEL, d_ff)),
            resident((D_MODEL, d_ff)),
            resident((d_ff, D_MODEL)),
        ] + [s for s, _, _ in r_specs],
        out_specs=[pl.BlockSpec((tm, D_MODEL), lambda i: (i, 0))] + [d for _, d, _ in r_specs],
        out_shape=[jax.ShapeDtypeStruct((m, D_MODEL), F32)] + [o for _, _, o in r_specs],
        scratch_shapes=[pltpu.VMEM((N_CLUSTERS, tm, LANES), F32),
                        pltpu.VMEM((tm, D_SSM), BF16)],
        compiler_params=pltpu.CompilerParams(
            dimension_semantics=("arbitrary",), vmem_limit_bytes=VMEM_MIX_OUT),
        name="mix_out",
    )(x2d, g3, g3, yb, yc, vecs, wglu, wout, wg, wu, wd, *[w for _, w, _ in riders])


def kernel(x, g_mix, w_in, A_re, A_im, log_dt, B_re, B_im, C_re, C_im, D_skip, w_glu, b_glu,
           w_pool, pool_scale, sgu_ln_g, sgu_ln_b, w_spatial, b_spatial, w_out, g_ffn,
           w_gate, w_up, w_down, g_final):
    bsz, seq, d = x.shape
    depth = w_in.shape[0]
    m = bsz * seq
    assert d == D_MODEL and seq % MIX_IN_TILE == 0 and MIX_IN_TILE % (2 * CHUNK) == 0
    assert m % TOKEN_TILE == 0
    assert (seq // SSM_BLOCK) % S5_ROW_TILE == 0
    x2d = x.reshape(m, D_MODEL)
    tril = jnp.tril(jnp.ones((CHUNK, CHUNK), dtype=bool))
    eye = jnp.eye(len(POOL_WINDOWS), dtype=F32)
    wpool_bd = jnp.einsum('dgij,gh->dgihj', w_pool, eye).reshape(depth, D_POOL, D_POOL).astype(BF16)
    ws = jnp.where(tril, w_spatial, 0.0).astype(BF16)
    ws = jnp.concatenate([ws[:, 0::2], ws[:, 1::2]], axis=-1)
    bsp = jnp.repeat(jnp.swapaxes(b_spatial, 1, 2), SGU_HEAD_DIM, axis=2)
    par, bt, ct, dskip = _s5_params(A_re, A_im, log_dt, B_re, B_im, C_re, C_im, D_skip)
    vec_rows = [None] * VEC_ROWS
    vec_rows[VEC_GMIX] = g_mix
    vec_rows[VEC_GFFN] = g_ffn
    vec_rows[VEC_SGU] = jnp.concatenate([pool_scale, sgu_ln_g, sgu_ln_b], axis=-1)
    vec_rows[VEC_GLU] = jnp.pad(b_glu, ((0, 0), (0, D_MODEL - D_SSM)))
    vec_rows[VEC_GFINAL] = jnp.broadcast_to(g_final, (depth, D_MODEL))
    zero_row = jnp.zeros((depth, D_MODEL), F32)
    vecs = jnp.stack([zero_row if r is None else r for r in vec_rows], axis=1)
    wglu = w_glu.astype(BF16)
    b_end, u_end = D_SSM + D_POOL, D_SSM + D_POOL + D_SGU
    win = jnp.concatenate([w_in[0, :, u_end:], w_in[0, :, b_end:u_end], w_in[0, :, :b_end]],
                          axis=-1).astype(BF16)
    ffn_stacks = (w_out, w_gate, w_up, w_down)
    ffn = None
    for l in range(depth):
        riders = tuple(("plain", w, 0) for w in ffn_stacks) if l == 0 else ()
        u_flat, yb, yc, *cast = _mix_in(l, x2d, vecs, win, wpool_bd, ws, bsp, seq, riders)
        if l == 0:
            ffn = cast
        g3 = _s5(l, u_flat, par, bt, ct, dskip, seq // SSM_BLOCK)
        last = l == depth - 1
        riders = () if last else (("w_in", w_in, l + 1),) + tuple(
            ("plain", w, l + 1) for w in ffn_stacks)
        x2d, *cast = _mix_out(l, x2d, g3, yb, yc, vecs, wglu, *ffn, final=last, riders=riders)
        if not last:
            win, ffn = cast[0], cast[1:]
    return x2d.reshape(bsz, seq, D_MODEL)
```

```python
import functools
import math

import jax
import jax.numpy as jnp
from jax import lax
from jax.experimental import pallas as pl
from jax.experimental.pallas import tpu as pltpu

D_MODEL = 1024
D_SSM = 384
SSM_GROUP = 16
N_SSM_GROUPS = D_SSM // SSM_GROUP
SSM_STATE = 64
POOL_WINDOWS = (2, 4, 8, 16)
POOL_GROUP = 64
D_POOL = len(POOL_WINDOWS) * POOL_GROUP
MAX_WINDOW = max(POOL_WINDOWS)
SGU_HEADS = 6
SGU_HEAD_DIM = 64
D_SGU = SGU_HEADS * SGU_HEAD_DIM
CHUNK = 128
D_IN = D_SSM + D_POOL + 2 * D_SGU
EPS = 1e-6

LANES = 128
SSM_BLOCK = 16
N_CLUSTERS = D_SSM // LANES
GROUPS_PER_CLUSTER = LANES // SSM_GROUP
CLUSTER_STATE = GROUPS_PER_CLUSTER * SSM_STATE
FLAT = SSM_BLOCK * LANES
GROUP_FLAT = SSM_BLOCK * SSM_GROUP
GROUP_PAIRS = GROUPS_PER_CLUSTER // 2

MXU_WIDTH = 256
BF16_SUBLANES = 16
VEC_ROWS = 8
VEC_GMIX, VEC_GFFN, VEC_SGU, VEC_GLU, VEC_GFINAL = 0, 1, 2, 3, 4
TOKEN_TILE = 512
TOKEN_TILE_NO_RIDERS = 1024
MIX_IN_TILE = 1024
S5_ROW_TILE = 512
SUBLANES = 8
SCAN_SEGS = SUBLANES
SCAN_SEG_LEN = S5_ROW_TILE // SCAN_SEGS
SCAN_SEG_PITCH = SCAN_SEG_LEN + SUBLANES
SCAN_TILES = 2 * CLUSTER_STATE // LANES
FF_CHUNKS = ((0, 768), (768, 1536), (1536, 2304), (2304, 2816))
V7X_VMEM_BYTES = 64 * 1024 * 1024
VMEM_MIX_IN = VMEM_S5 = VMEM_MIX_OUT = V7X_VMEM_BYTES

F32 = jnp.float32
BF16 = jnp.bfloat16
NT_DIMS = (((1,), (1,)), ((), ()))


def _gelu(x):
    c = math.sqrt(2.0 / math.pi)
    return x * (0.5 + 0.5 * jnp.tanh(x * (c + (c * 0.044715) * (x * x))))


def _rms(x, g):
    ms = jnp.mean(x * x, axis=-1, keepdims=True)
    return x * lax.rsqrt(ms + EPS) * g


def _dot(a, b):
    return jnp.dot(a, b, preferred_element_type=F32)


def _dot_nt(a, b):
    return lax.dot_general(a, b, NT_DIMS, preferred_element_type=F32)


def _chunk_transpose(tiles):
    n = len(tiles)
    chunk = LANES // n
    v = list(tiles)
    cidx = lax.broadcasted_iota(jnp.int32, v[0].shape, 1) // chunk
    k = 1
    while k < n:
        keep = (cidx & k) == 0
        nv = list(v)
        for a in range(n):
            if a & k:
                continue
            lo, hi = v[a], v[a + k]
            nv[a] = jnp.where(keep, lo, pltpu.roll(hi, k * chunk, 1))
            nv[a + k] = jnp.where(keep, pltpu.roll(lo, LANES - k * chunk, 1), hi)
        v = nv
        k *= 2
    return v


def _cast_riders(kinds, src_refs, dst_refs):
    for kind, src, dst in zip(kinds, src_refs, dst_refs):
        if kind == "w_in":
            b_end, u_end = D_SSM + D_POOL, D_SSM + D_POOL + D_SGU
            dst[:, :D_SGU] = src[:, u_end:].astype(BF16)
            dst[:, D_SGU:2 * D_SGU] = src[:, b_end:u_end].astype(BF16)
            dst[:, 2 * D_SGU:] = src[:, :b_end].astype(BF16)
        else:
            dst[...] = src[...].astype(BF16)


def _rider_specs(n_steps, layer, w):
    rows, cols = w.shape[1:]
    share = 1
    while (rows * share) % n_steps or (rows * share // n_steps) % BF16_SUBLANES:
        share *= 2
    slab = rows * share // n_steps
    which = lambda i: jnp.minimum(i, n_steps - 1) // share
    src = pl.BlockSpec((None, slab, cols), lambda i: (layer, which(i), 0))
    dst = pl.BlockSpec((slab, cols), lambda i: (which(i), 0))
    return src, dst, jax.ShapeDtypeStruct((rows, cols), BF16)


def _mix_in_kernel(tiles_per_seq, rider_kinds, *refs):
    n_r = len(rider_kinds)
    x_ref, vec_ref, win_ref, wpool_ref, ws_ref, bsp_ref = refs[:6]
    rider_src = refs[6:6 + n_r]
    za_ref, yb_ref, yc_ref = refs[6 + n_r:9 + n_r]
    rider_dst = refs[9 + n_r:9 + 2 * n_r]
    halo_ref, zs_ref = refs[9 + 2 * n_r:]
    _cast_riders(rider_kinds, rider_src, rider_dst)

    tm = x_ref.shape[0]
    i = pl.program_id(0)
    seq_tile = i % tiles_per_seq
    gmix = vec_ref[VEC_GMIX:VEC_GMIX + 1, :]
    pscale = vec_ref[VEC_SGU:VEC_SGU + 1, :D_POOL]
    lng = vec_ref[VEC_SGU:VEC_SGU + 1, D_POOL:D_POOL + D_SGU]
    lnb = vec_ref[VEC_SGU:VEC_SGU + 1, D_POOL + D_SGU:]

    h = _rms(x_ref[...], gmix).astype(BF16)
    z_sgu = _dot(h, win_ref[:, :2 * D_SGU])
    z_rest = _dot(h, win_ref[:, 2 * D_SGU:])
    zv = z_sgu[:, :D_SGU]
    zu = z_sgu[:, D_SGU:]
    zb = z_rest[:, D_SSM:]

    for k in range(N_CLUSTERS):
        zs_ref[k] = z_rest[:, k * LANES:(k + 1) * LANES]
    half_block = SSM_BLOCK // 2
    for k in range(N_CLUSTERS):
        for hf in range(2):
            slots = [zs_ref[k, pl.ds(hf * half_block + i, tm // SSM_BLOCK, stride=SSM_BLOCK), :]
                     for i in range(half_block)]
            for g, tile in enumerate(_chunk_transpose(slots)):
                col = g * GROUP_FLAT + hf * LANES
                za_ref[k, :, col:col + LANES] = tile.astype(BF16)

    halo = jnp.where(seq_tile == 0, 0.0, halo_ref[...])
    halo_ref[...] = zb[tm - MAX_WINDOW:, :]
    ext = jnp.concatenate([halo, zb], axis=0)
    lane = lax.broadcasted_iota(jnp.int32, (tm, LANES), 1)
    low = lane < POOL_GROUP
    low_h = lax.broadcasted_iota(jnp.int32, (MAX_WINDOW, LANES), 1) < POOL_GROUP
    low_r = lax.broadcasted_iota(jnp.int32, (1, LANES), 1) < POOL_GROUP
    pos1 = (seq_tile * tm + 1
            + lax.broadcasted_iota(jnp.int32, (MAX_WINDOW, LANES), 0)).astype(F32)

    def window_mean(total, w_low, w_high):
        cnt = jnp.where(low_h, jnp.minimum(pos1, w_low), jnp.minimum(pos1, w_high))
        inv_w = jnp.where(low_r, 1.0 / w_low, 1.0 / w_high)
        return jnp.concatenate([total[:MAX_WINDOW] / cnt, total[MAX_WINDOW:] * inv_w], axis=0)

    e0 = ext[:, :LANES]
    s2 = e0 + pltpu.roll(e0, 1, 0)
    s4 = s2 + pltpu.roll(s2, 2, 0)
    sum0 = jnp.where(low, s2[MAX_WINDOW:], s4[MAX_WINDOW:])
    e1 = ext[:, LANES:]
    t2 = e1 + pltpu.roll(e1, 1, 0)
    t4 = t2 + pltpu.roll(t2, 2, 0)
    t8 = t4 + pltpu.roll(t4, 4, 0)
    t16 = t8 + pltpu.roll(t8, 8, 0)
    sum1 = jnp.where(low, t8[MAX_WINDOW:], t16[MAX_WINDOW:])
    pooled = jnp.concatenate([window_mean(sum0, 2.0, 4.0), window_mean(sum1, 8.0, 16.0)],
                             axis=1) - zb
    yb = _dot(pooled.astype(BF16), wpool_ref[...]) * pscale
    yb_ref[...] = yb.astype(BF16)

    u = _gelu(zu)
    v = _gelu(zv)
    mu = jnp.mean(v, axis=-1, keepdims=True)
    vc = v - mu
    var = jnp.mean(vc * vc, axis=-1, keepdims=True)
    vn = (vc * lax.rsqrt(var + EPS) * lng + lnb).astype(BF16)
    lane_c = lax.broadcasted_iota(jnp.int32, (CHUNK, LANES), 1)
    low_c = lane_c < SGU_HEAD_DIM
    zero = jnp.zeros((), BF16)
    for c in range(0, tm // CHUNK, 2):
        rows_a = slice(c * CHUNK, (c + 1) * CHUNK)
        rows_b = slice((c + 1) * CHUNK, (c + 2) * CHUNK)
        parts_a, parts_b = [], []
        for p in range(SGU_HEADS // 2):
            va = vn[rows_a, p * LANES:(p + 1) * LANES]
            vb = vn[rows_b, p * LANES:(p + 1) * LANES]
            top = jnp.concatenate([jnp.where(low_c, va, zero), jnp.where(low_c, vb, zero)], axis=1)
            bot = jnp.concatenate([jnp.where(low_c, zero, va), jnp.where(low_c, zero, vb)], axis=1)
            out = _dot(ws_ref[p], jnp.concatenate([top, bot], axis=0))
            parts_a.append(out[:, :LANES])
            parts_b.append(out[:, LANES:])
        for rows, parts in ((rows_a, parts_a), (rows_b, parts_b)):
            mixed = jnp.concatenate(parts, axis=1) + bsp_ref[...]
            yc_ref[rows, :] = (u[rows, :] * mixed).astype(BF16)


def _layer_spec(layer, shape):
    zeros = (0,) * len(shape)
    return pl.BlockSpec((None,) + tuple(shape), lambda *_: (layer,) + zeros)


def _mix_in(layer, x2d, vecs, win, wpool_bd, ws, bsp, seq_len, riders=()):
    m = x2d.shape[0]
    tm = MIX_IN_TILE
    n_steps = m // tm
    spec = functools.partial(_layer_spec, layer)
    r_specs = [_rider_specs(n_steps, lyr, w) for _, w, lyr in riders]
    return pl.pallas_call(
        functools.partial(_mix_in_kernel, seq_len // tm, tuple(k for k, _, _ in riders)),
        grid=(n_steps,),
        in_specs=[
            pl.BlockSpec((tm, D_MODEL), lambda i: (i, 0)),
            spec((VEC_ROWS, D_MODEL)),
            pl.BlockSpec((D_MODEL, D_IN), lambda i: (0, 0)),
            spec((D_POOL, D_POOL)),
            spec((SGU_HEADS // 2, CHUNK, 2 * CHUNK)),
            spec((CHUNK, D_SGU)),
        ] + [s for s, _, _ in r_specs],
        out_specs=[
            pl.BlockSpec((N_CLUSTERS, tm // SSM_BLOCK, FLAT), lambda i: (0, i, 0)),
            pl.BlockSpec((tm, D_POOL), lambda i: (i, 0)),
            pl.BlockSpec((tm, D_SGU), lambda i: (i, 0)),
        ] + [d for _, d, _ in r_specs],
        out_shape=[
            jax.ShapeDtypeStruct((N_CLUSTERS, m // SSM_BLOCK, FLAT), BF16),
            jax.ShapeDtypeStruct((m, D_POOL), BF16),
            jax.ShapeDtypeStruct((m, D_SGU), BF16),
        ] + [o for _, _, o in r_specs],
        scratch_shapes=[pltpu.VMEM((MAX_WINDOW, D_POOL), F32),
                        pltpu.VMEM((N_CLUSTERS, tm, LANES), F32)],
        compiler_params=pltpu.CompilerParams(
            dimension_semantics=("arbitrary",), vmem_limit_bytes=VMEM_MIX_IN),
        name="mix_in",
    )(x2d, vecs, win, wpool_bd, ws, bsp, *[w for _, w, _ in riders])


def _s5_build_operators(par_ref, bt_ref, ct_ref, wend_ref, wrd_ref, lag0_ref, lagk_ref, pw_ref):
    a_re = par_ref[0, 0:1, :]
    a_im = par_ref[0, 1:2, :]
    dt = jnp.exp(par_ref[0, 2:3, :])
    l_re = a_re * dt
    l_im = a_im * dt
    mag = jnp.exp(l_re)
    ar = mag * jnp.cos(l_im)
    ai = mag * jnp.sin(l_im)
    den = a_re * a_re + a_im * a_im
    f_re = ((ar - 1.0) * a_re + ai * a_im) / den
    f_im = (ai * a_re - (ar - 1.0) * a_im) / den
    bt_re, bt_im = bt_ref[0, 0], bt_ref[0, 1]
    bb_re = f_re * bt_re - f_im * bt_im
    bb_im = f_re * bt_im + f_im * bt_re
    ct_re, ct_im = ct_ref[0, 0], ct_ref[0, 1]

    lane_q = lax.broadcasted_iota(jnp.int32, (SSM_GROUP, LANES), 1) // SSM_STATE

    def power(l):
        m_l = jnp.exp(l * l_re)
        return m_l * jnp.cos(l * l_im), m_l * jnp.sin(l * l_im)

    def store_pairs(ref, slot, v_re, v_im):
        for p in range(GROUP_PAIRS):
            t_re = v_re[:, p * LANES:(p + 1) * LANES]
            t_im = v_im[:, p * LANES:(p + 1) * LANES]
            for q in range(2):
                rows = slice(q * GROUP_FLAT + slot * SSM_GROUP, q * GROUP_FLAT + (slot + 1) * SSM_GROUP)
                ref[p, rows, :LANES] = jnp.where(lane_q == q, t_re, 0.0).astype(BF16)
                ref[p, rows, LANES:] = jnp.where(lane_q == q, t_im, 0.0).astype(BF16)

    for s in range(SSM_BLOCK):
        p_re, p_im = power(float(SSM_BLOCK - 1 - s))
        store_pairs(wend_ref, s, p_re * bb_re - p_im * bb_im, p_re * bb_im + p_im * bb_re)
        q_re, q_im = power(float(s + 1))
        store_pairs(wrd_ref, s, q_re * ct_re - q_im * ct_im, -(q_re * ct_im + q_im * ct_re))
        r_re, r_im = power(float(s))
        store_pairs(lag0_ref, s, r_re * ct_re - r_im * ct_im, -(r_re * ct_im + r_im * ct_re))

    last = (SSM_BLOCK - 1) * SSM_GROUP
    col = lax.broadcasted_iota(jnp.int32, (SSM_GROUP, GROUP_FLAT), 1)
    for p in range(GROUP_PAIRS):
        bbar = jnp.concatenate([wend_ref[p, q * GROUP_FLAT + last:q * GROUP_FLAT + last + SSM_GROUP, :]
                                for q in range(2)], axis=0)
        lags = _dot_nt(bbar, lag0_ref[p])
        for q in range(2):
            k_row = lags[q * SSM_GROUP:(q + 1) * SSM_GROUP, q * GROUP_FLAT:(q + 1) * GROUP_FLAT]
            for s in range(SSM_BLOCK):
                shifted = pltpu.roll(k_row, s * SSM_GROUP, 1) if s else k_row
                lagk_ref[2 * p + q, s * SSM_GROUP:(s + 1) * SSM_GROUP, :] = jnp.where(
                    col >= s * SSM_GROUP, shifted, 0.0).astype(BF16)

    p_re, p_im = power(float(SSM_BLOCK))
    pw_ref[0:1, :] = jnp.concatenate([p_re, p_im], axis=1)
    for _ in range(SCAN_SEG_LEN.bit_length() - 1):
        p_re, p_im = p_re * p_re - p_im * p_im, 2.0 * (p_re * p_im)
    pw_ref[1:2, :] = jnp.concatenate([p_re, p_im], axis=1)


def _cmul_add(p_re, p_im, h_re, h_im, e_re, e_im):
    n_re = [pr * hr - pi * hi + er for pr, pi, hr, hi, er in zip(p_re, p_im, h_re, h_im, e_re)]
    n_im = [pr * hi + pi * hr + ei for pr, pi, hr, hi, ei in zip(p_re, p_im, h_re, h_im, e_im)]
    return n_re, n_im


def _s5_block_recurrence(st_ref, pw_ref, fillers):
    fillers = list(fillers)
    total_cost = sum(c for c, _ in fillers)
    issued_cost = [0.0]
    ticks = [0]

    def tick():
        while fillers and issued_cost[0] * 2 * SCAN_SEG_LEN <= ticks[0] * total_cost:
            cost, thunk = fillers.pop(0)
            issued_cost[0] += cost
            thunk()
        ticks[0] += 1

    half = SCAN_TILES // 2

    def bcast(row):
        return [jnp.broadcast_to(row[:, q * LANES:(q + 1) * LANES], (SCAN_SEGS, LANES))
                for q in range(half)]

    p_re, p_im = bcast(pw_ref[0:1, :CLUSTER_STATE]), bcast(pw_ref[0:1, CLUSTER_STATE:])
    seg_rows = lambda i: pl.ds(i, SCAN_SEGS, stride=SCAN_SEG_PITCH)

    def gather(i):
        e = [st_ref[q, seg_rows(i), :] for q in range(SCAN_TILES)]
        return e[:half], e[half:]

    zeros = [jnp.zeros((SCAN_SEGS, LANES), F32)] * half

    def end_step(i, h):
        e_re, e_im = gather(i)
        return _cmul_add(p_re, p_im, h[0], h[1], e_re, e_im)

    h = (zeros, zeros)
    for i in range(SCAN_SEG_LEN):
        tick()
        h = end_step(i, h)
    l_re, l_im = h

    s_re, s_im = bcast(pw_ref[1:2, :CLUSTER_STATE]), bcast(pw_ref[1:2, CLUSTER_STATE:])
    first = lax.broadcasted_iota(jnp.int32, (SCAN_SEGS, LANES), 0) == 0
    shift = lambda v: jnp.where(first, 0.0, pltpu.roll(v, 1, 0))
    c_re, c_im = zeros, zeros
    for _ in range(SCAN_SEGS - 1):
        n_re, n_im = _cmul_add(s_re, s_im, c_re, c_im, l_re, l_im)
        c_re, c_im = [shift(v) for v in n_re], [shift(v) for v in n_im]

    def scan_step(i, h):
        e_re, e_im = gather(i)
        for q in range(half):
            st_ref[q, seg_rows(i), :] = h[0][q]
            st_ref[half + q, seg_rows(i), :] = h[1][q]
        return _cmul_add(p_re, p_im, h[0], h[1], e_re, e_im)

    h = (c_re, c_im)
    for i in range(SCAN_SEG_LEN):
        tick()
        h = scan_step(i, h)
    while fillers:
        fillers.pop(0)[1]()


def _s5_kernel(u_ref, par_ref, bt_ref, ct_ref, dskip_ref, g_ref,
               wend_ref, wrd_ref, lag0_ref, lagk_ref, pw_ref, st_ref):
    r = pl.program_id(1)

    @pl.when(r == 0)
    def _():
        _s5_build_operators(par_ref, bt_ref, ct_ref, wend_ref, wrd_ref, lag0_ref, lagk_ref, pw_ref)

    u = u_ref[0]
    pair_cols = lambda p: slice(2 * p * GROUP_FLAT, 2 * (p + 1) * GROUP_FLAT)
    seg_rows = lambda j: slice(j * SCAN_SEG_PITCH, j * SCAN_SEG_PITCH + SCAN_SEG_LEN)

    for p in range(GROUP_PAIRS):
        e = _dot(u[:, pair_cols(p)], wend_ref[p])
        for tile, cols in ((p, slice(0, LANES)), (GROUP_PAIRS + p, slice(LANES, 2 * LANES))):
            for j in range(SCAN_SEGS):
                st_ref[tile, seg_rows(j), :] = e[j * SCAN_SEG_LEN:(j + 1) * SCAN_SEG_LEN, cols]

    ys = [None] * GROUPS_PER_CLUSTER

    def lag_product(g):
        ys[g] = _dot(u[:, g * GROUP_FLAT:(g + 1) * GROUP_FLAT], lagk_ref[g])

    _s5_block_recurrence(st_ref, pw_ref,
                         [(1, functools.partial(lag_product, g)) for g in range(len(ys))])
    y = jnp.concatenate(ys, axis=1)

    outs = []
    for p in range(GROUP_PAIRS):
        hprev = jnp.concatenate(
            [jnp.concatenate([st_ref[p, seg_rows(j), :], st_ref[GROUP_PAIRS + p, seg_rows(j), :]],
                             axis=1) for j in range(SCAN_SEGS)], axis=0)
        outs.append(_dot_nt(hprev.astype(BF16), wrd_ref[p]))
    y = y + jnp.concatenate(outs, axis=1)
    y = y + dskip_ref[0] * u.astype(F32)
    g_ref[0] = _gelu(y)


def _s5(layer, u_flat, par, bt, ct, dskip, blocks_per_seq):
    nb = u_flat.shape[1]
    rt = S5_ROW_TILE
    assert blocks_per_seq == rt and SCAN_SEG_LEN & (SCAN_SEG_LEN - 1) == 0
    return pl.pallas_call(
        _s5_kernel,
        grid=(N_CLUSTERS, nb // rt),
        in_specs=[
            pl.BlockSpec((1, rt, FLAT), lambda k, r: (k, r, 0)),
            pl.BlockSpec((None, 1, 3, CLUSTER_STATE), lambda k, r: (layer, k, 0, 0)),
            pl.BlockSpec((None, 1, 2, SSM_GROUP, CLUSTER_STATE), lambda k, r: (layer, k, 0, 0, 0)),
            pl.BlockSpec((None, 1, 2, SSM_GROUP, CLUSTER_STATE), lambda k, r: (layer, k, 0, 0, 0)),
            pl.BlockSpec((None, 1, 1, FLAT), lambda k, r: (layer, k, 0, 0)),
        ],
        out_specs=pl.BlockSpec((1, rt, FLAT), lambda k, r: (k, r, 0)),
        out_shape=jax.ShapeDtypeStruct((N_CLUSTERS, nb, FLAT), F32),
        scratch_shapes=[pltpu.VMEM((GROUP_PAIRS, 2 * GROUP_FLAT, 2 * LANES), BF16),
                        pltpu.VMEM((GROUP_PAIRS, 2 * GROUP_FLAT, 2 * LANES), BF16),
                        pltpu.VMEM((GROUP_PAIRS, 2 * GROUP_FLAT, 2 * LANES), BF16),
                        pltpu.VMEM((GROUPS_PER_CLUSTER, GROUP_FLAT, GROUP_FLAT), BF16),
                        pltpu.VMEM((2, 2 * CLUSTER_STATE), F32),
                        pltpu.VMEM((SCAN_TILES, SCAN_SEGS * SCAN_SEG_PITCH, LANES), F32)],
        compiler_params=pltpu.CompilerParams(
            dimension_semantics=("arbitrary", "arbitrary"), vmem_limit_bytes=VMEM_S5),
        name="s5",
    )(u_flat, par, bt, ct, dskip)


def _s5_params(A_re, A_im, log_dt, B_re, B_im, C_re, C_im, D_skip):
    depth = A_re.shape[0]
    k, gpc, n, c = N_CLUSTERS, GROUPS_PER_CLUSTER, SSM_STATE, SSM_GROUP
    rows = lambda a: a.reshape(depth, k, 1, CLUSTER_STATE)
    ldt = jnp.broadcast_to(log_dt[..., None], A_re.shape)
    par = jnp.concatenate([rows(A_re), rows(A_im), rows(ldt)], axis=2)
    b = jnp.stack([B_re, B_im], axis=1).reshape(depth, 2, k, gpc, n, c)
    bt = b.transpose(0, 2, 1, 5, 3, 4).reshape(depth, k, 2, c, CLUSTER_STATE)
    cc = jnp.stack([C_re, C_im], axis=1).reshape(depth, 2, k, gpc, c, n)
    ct = cc.transpose(0, 2, 1, 4, 3, 5).reshape(depth, k, 2, c, CLUSTER_STATE)
    dskip = jnp.broadcast_to(D_skip.reshape(depth, k, gpc, 1, c), (depth, k, gpc, SSM_BLOCK, c))
    return par, bt, ct, dskip.reshape(depth, k, 1, FLAT)


def _mix_out_kernel(final, rider_kinds, *refs):
    n_r = len(rider_kinds)
    (x_ref, g0_ref, gnext_ref, yb_ref, yc_ref, vec_ref, wglu_ref, wout_ref, wg_ref, wu_ref,
     wd_ref) = refs[:11]
    rider_src = refs[11:11 + n_r]
    o_ref = refs[11 + n_r]
    rider_dst = refs[12 + n_r:12 + 2 * n_r]
    gs_ref, ya_ref = refs[12 + 2 * n_r:]
    _cast_riders(rider_kinds, rider_src, rider_dst)

    bglu = vec_ref[VEC_GLU:VEC_GLU + 1, :D_SSM]
    gffn = vec_ref[VEC_GFFN:VEC_GFFN + 1, :]
    tm = x_ref.shape[0]
    half_block = SSM_BLOCK // 2

    def glu_branch(src_ref):
        for k in range(N_CLUSTERS):
            for hf in range(2):
                groups = [src_ref[k, :, g * GROUP_FLAT + hf * LANES:g * GROUP_FLAT + (hf + 1) * LANES]
                          for g in range(GROUPS_PER_CLUSTER)]
                for i, tile in enumerate(_chunk_transpose(groups)):
                    rows = pl.ds(hf * half_block + i, tm // SSM_BLOCK, stride=SSM_BLOCK)
                    gs_ref[k, rows, :] = tile
        g = jnp.concatenate([gs_ref[k] for k in range(N_CLUSTERS)], axis=1)
        ya = g * jax.nn.sigmoid(_dot(g.astype(BF16), wglu_ref[...]) + bglu)
        ya_ref[...] = ya.astype(BF16)

    @pl.when(pl.program_id(0) == 0)
    def _():
        glu_branch(g0_ref)

    ymix = jnp.concatenate([ya_ref[...], yb_ref[...], yc_ref[...]], axis=1)
    x1 = x_ref[...] + _dot(ymix, wout_ref[...])
    h = _rms(x1, gffn).astype(BF16)
    acc = x1
    for n, (c0, c1) in enumerate(FF_CHUNKS):
        gate = _dot(h, wg_ref[:, c0:c1])
        up = _dot(h, wu_ref[:, c0:c1])
        act = (gate * jax.nn.sigmoid(gate) * up).astype(BF16)
        acc = acc + _dot(act, wd_ref[c0:c1, :])
        if n == 0:
            glu_branch(gnext_ref)
    if final:
        acc = _rms(acc, vec_ref[VEC_GFINAL:VEC_GFINAL + 1, :])
    o_ref[...] = acc


def _mix_out(layer, x2d, g3, yb, yc, vecs, wglu, wout, wg, wu, wd, final, riders=()):
    m = x2d.shape[0]
    tm = TOKEN_TILE if riders else TOKEN_TILE_NO_RIDERS
    n_steps = m // tm
    d_ff = wg.shape[-1]
    resident = lambda shape: pl.BlockSpec(shape, lambda i: (0, 0), pipeline_mode=pl.Buffered(1))
    r_specs = [_rider_specs(n_steps, lyr, w) for _, w, lyr in riders]
    return pl.pallas_call(
        functools.partial(_mix_out_kernel, final, tuple(k for k, _, _ in riders)),
        grid=(n_steps,),
        in_specs=[
            pl.BlockSpec((tm, D_MODEL), lambda i: (i, 0)),
            pl.BlockSpec((N_CLUSTERS, tm // SSM_BLOCK, FLAT), lambda i: (0, 0, 0),
                         pipeline_mode=pl.Buffered(1)),
            pl.BlockSpec((N_CLUSTERS, tm // SSM_BLOCK, FLAT),
                         lambda i: (0, jnp.minimum(i + 1, n_steps - 1), 0)),
            pl.BlockSpec((tm, D_POOL), lambda i: (i, 0)),
            pl.BlockSpec((tm, D_SGU), lambda i: (i, 0)),
            _layer_spec(layer, (VEC_ROWS, D_MODEL)),
            _layer_spec(layer, (D_SSM, D_SSM)),
            resident((D_MODEL, D_MODEL)),
            resident((D_MODEL, d_ff)),
            resident((D_MODEL, d_ff)),
            resident((d_ff, D_MODEL)),
        ] + [s for s, _, _ in r_specs],
        out_specs=[pl.BlockSpec((tm, D_MODEL), lambda i: (i, 0))] + [d for _, d, _ in r_specs],
        out_shape=[jax.ShapeDtypeStruct((m, D_MODEL), F32)] + [o for _, _, o in r_specs],
        scratch_shapes=[pltpu.VMEM((N_CLUSTERS, tm, LANES), F32),
                        pltpu.VMEM((tm, D_SSM), BF16)],
        compiler_params=pltpu.CompilerParams(
            dimension_semantics=("arbitrary",), vmem_limit_bytes=VMEM_MIX_OUT),
        name="mix_out",
    )(x2d, g3, g3, yb, yc, vecs, wglu, wout, wg, wu, wd, *[w for _, w, _ in riders])


def kernel(x, g_mix, w_in, A_re, A_im, log_dt, B_re, B_im, C_re, C_im, D_skip, w_glu, b_glu,
           w_pool, pool_scale, sgu_ln_g, sgu_ln_b, w_spatial, b_spatial, w_out, g_ffn,
           w_gate, w_up, w_down, g_final):
    bsz, seq, d = x.shape
    depth = w_in.shape[0]
    m = bsz * seq
    assert d == D_MODEL and seq % MIX_IN_TILE == 0 and MIX_IN_TILE % (2 * CHUNK) == 0
    assert m % TOKEN_TILE == 0 and m % TOKEN_TILE_NO_RIDERS == 0
    assert (seq // SSM_BLOCK) % S5_ROW_TILE == 0
    x2d = x.reshape(m, D_MODEL)
    tril = jnp.tril(jnp.ones((CHUNK, CHUNK), dtype=bool))
    eye = jnp.eye(len(POOL_WINDOWS), dtype=F32)
    wpool_bd = jnp.einsum('dgij,gh->dgihj', w_pool, eye).reshape(depth, D_POOL, D_POOL).astype(BF16)
    ws = jnp.where(tril, w_spatial, 0.0).astype(BF16)
    ws = jnp.concatenate([ws[:, 0::2], ws[:, 1::2]], axis=-1)
    bsp = jnp.repeat(jnp.swapaxes(b_spatial, 1, 2), SGU_HEAD_DIM, axis=2)
    par, bt, ct, dskip = _s5_params(A_re, A_im, log_dt, B_re, B_im, C_re, C_im, D_skip)
    vec_rows = [None] * VEC_ROWS
    vec_rows[VEC_GMIX] = g_mix
    vec_rows[VEC_GFFN] = g_ffn
    vec_rows[VEC_SGU] = jnp.concatenate([pool_scale, sgu_ln_g, sgu_ln_b], axis=-1)
    vec_rows[VEC_GLU] = jnp.pad(b_glu, ((0, 0), (0, D_MODEL - D_SSM)))
    vec_rows[VEC_GFINAL] = jnp.broadcast_to(g_final, (depth, D_MODEL))
    zero_row = jnp.zeros((depth, D_MODEL), F32)
    vecs = jnp.stack([zero_row if r is None else r for r in vec_rows], axis=1)
    wglu = w_glu.astype(BF16)
    b_end, u_end = D_SSM + D_POOL, D_SSM + D_POOL + D_SGU
    win = jnp.concatenate([w_in[0, :, u_end:], w_in[0, :, b_end:u_end], w_in[0, :, :b_end]],
                          axis=-1).astype(BF16)
    ffn_stacks = (w_out, w_gate, w_up, w_down)
    ffn = None
    for l in range(depth):
        riders = tuple(("plain", w, 0) for w in ffn_stacks) if l == 0 else ()
        u_flat, yb, yc, *cast = _mix_in(l, x2d, vecs, win, wpool_bd, ws, bsp, seq, riders)
        if l == 0:
            ffn = cast
        g3 = _s5(l, u_flat, par, bt, ct, dskip, seq // SSM_BLOCK)
        last = l == depth - 1
        riders = () if last else (("w_in", w_in, l + 1),) + tuple(
            ("plain", w, l + 1) for w in ffn_stacks)
        x2d, *cast = _mix_out(l, x2d, g3, yb, yc, vecs, wglu, *ffn, final=last, riders=riders)
        if not last:
            win, ffn = cast[0], cast[1:]
    return x2d.reshape(bsz, seq, D_MODEL)
```

```python
import functools
import math

import jax
import jax.numpy as jnp
from jax import lax
from jax.experimental import pallas as pl
from jax.experimental.pallas import tpu as pltpu

D_MODEL = 1024
D_SSM = 384
SSM_GROUP = 16
N_SSM_GROUPS = D_SSM // SSM_GROUP
SSM_STATE = 64
POOL_WINDOWS = (2, 4, 8, 16)
POOL_GROUP = 64
D_POOL = len(POOL_WINDOWS) * POOL_GROUP
MAX_WINDOW = max(POOL_WINDOWS)
SGU_HEADS = 6
SGU_HEAD_DIM = 64
D_SGU = SGU_HEADS * SGU_HEAD_DIM
CHUNK = 128
D_IN = D_SSM + D_POOL + 2 * D_SGU
EPS = 1e-6

LANES = 128
SSM_BLOCK = 16
N_CLUSTERS = D_SSM // LANES
GROUPS_PER_CLUSTER = LANES // SSM_GROUP
CLUSTER_STATE = GROUPS_PER_CLUSTER * SSM_STATE
FLAT = SSM_BLOCK * LANES
GROUP_FLAT = SSM_BLOCK * SSM_GROUP
GROUP_PAIRS = GROUPS_PER_CLUSTER // 2

MXU_WIDTH = 256
BF16_SUBLANES = 16
VEC_ROWS = 8
VEC_GMIX, VEC_GFFN, VEC_SGU, VEC_GLU, VEC_GFINAL = 0, 1, 2, 3, 4
TOKEN_TILE = 512
MIX_IN_TILE = 1024
S5_ROW_TILE = 512
SUBLANES = 8
SCAN_SEGS = SUBLANES
SCAN_SEG_LEN = S5_ROW_TILE // SCAN_SEGS
SCAN_SEG_PITCH = SCAN_SEG_LEN + SUBLANES
SCAN_TILES = 2 * CLUSTER_STATE // LANES
FF_CHUNK = 3 * MXU_WIDTH
V7X_VMEM_BYTES = 64 * 1024 * 1024
VMEM_MIX_IN = VMEM_S5 = VMEM_MIX_OUT = V7X_VMEM_BYTES

F32 = jnp.float32
BF16 = jnp.bfloat16
NT_DIMS = (((1,), (1,)), ((), ()))


def _gelu(x):
    c = math.sqrt(2.0 / math.pi)
    return x * (0.5 + 0.5 * jnp.tanh(x * (c + (c * 0.044715) * (x * x))))


def _rms(x, g):
    ms = jnp.mean(x * x, axis=-1, keepdims=True)
    return x * lax.rsqrt(ms + EPS) * g


def _dot(a, b):
    return jnp.dot(a, b, preferred_element_type=F32)


def _dot_nt(a, b):
    return lax.dot_general(a, b, NT_DIMS, preferred_element_type=F32)


def _chunk_transpose(tiles):
    n = len(tiles)
    chunk = LANES // n
    v = list(tiles)
    cidx = lax.broadcasted_iota(jnp.int32, v[0].shape, 1) // chunk
    k = 1
    while k < n:
        keep = (cidx & k) == 0
        nv = list(v)
        for a in range(n):
            if a & k:
                continue
            lo, hi = v[a], v[a + k]
            nv[a] = jnp.where(keep, lo, pltpu.roll(hi, k * chunk, 1))
            nv[a + k] = jnp.where(keep, pltpu.roll(lo, LANES - k * chunk, 1), hi)
        v = nv
        k *= 2
    return v


def _cast_riders(kinds, src_refs, dst_refs):
    for kind, src, dst in zip(kinds, src_refs, dst_refs):
        if kind == "w_in":
            b_end, u_end = D_SSM + D_POOL, D_SSM + D_POOL + D_SGU
            dst[:, :D_SGU] = src[:, u_end:].astype(BF16)
            dst[:, D_SGU:2 * D_SGU] = src[:, b_end:u_end].astype(BF16)
            dst[:, 2 * D_SGU:] = src[:, :b_end].astype(BF16)
        else:
            dst[...] = src[...].astype(BF16)


def _rider_specs(n_steps, layer, w):
    rows, cols = w.shape[1:]
    share = 1
    while (rows * share) % n_steps or (rows * share // n_steps) % BF16_SUBLANES:
        share *= 2
    slab = rows * share // n_steps
    which = lambda i: jnp.minimum(i, n_steps - 1) // share
    src = pl.BlockSpec((None, slab, cols), lambda i: (layer, which(i), 0))
    dst = pl.BlockSpec((slab, cols), lambda i: (which(i), 0))
    return src, dst, jax.ShapeDtypeStruct((rows, cols), BF16)


def _mix_in_kernel(tiles_per_seq, rider_kinds, *refs):
    n_r = len(rider_kinds)
    x_ref, vec_ref, win_ref, wpool_ref, ws_ref, bsp_ref = refs[:6]
    rider_src = refs[6:6 + n_r]
    za_ref, yb_ref, yc_ref = refs[6 + n_r:9 + n_r]
    rider_dst = refs[9 + n_r:9 + 2 * n_r]
    halo_ref, zs_ref = refs[9 + 2 * n_r:]
    _cast_riders(rider_kinds, rider_src, rider_dst)

    tm = x_ref.shape[0]
    i = pl.program_id(0)
    seq_tile = i % tiles_per_seq
    gmix = vec_ref[VEC_GMIX:VEC_GMIX + 1, :]
    pscale = vec_ref[VEC_SGU:VEC_SGU + 1, :D_POOL]
    lng = vec_ref[VEC_SGU:VEC_SGU + 1, D_POOL:D_POOL + D_SGU]
    lnb = vec_ref[VEC_SGU:VEC_SGU + 1, D_POOL + D_SGU:]

    h = _rms(x_ref[...], gmix).astype(BF16)
    z_sgu = _dot(h, win_ref[:, :2 * D_SGU])
    z_rest = _dot(h, win_ref[:, 2 * D_SGU:])
    zv = z_sgu[:, :D_SGU]
    zu = z_sgu[:, D_SGU:]
    zb = z_rest[:, D_SSM:]

    for k in range(N_CLUSTERS):
        zs_ref[k] = z_rest[:, k * LANES:(k + 1) * LANES]
    half_block = SSM_BLOCK // 2
    for k in range(N_CLUSTERS):
        for hf in range(2):
            slots = [zs_ref[k, pl.ds(hf * half_block + i, tm // SSM_BLOCK, stride=SSM_BLOCK), :]
                     for i in range(half_block)]
            for g, tile in enumerate(_chunk_transpose(slots)):
                col = g * GROUP_FLAT + hf * LANES
                za_ref[k, :, col:col + LANES] = tile.astype(BF16)

    halo = jnp.where(seq_tile == 0, 0.0, halo_ref[...])
    halo_ref[...] = zb[tm - MAX_WINDOW:, :]
    ext = jnp.concatenate([halo, zb], axis=0)
    lane = lax.broadcasted_iota(jnp.int32, (tm, LANES), 1)
    low = lane < POOL_GROUP
    low_h = lax.broadcasted_iota(jnp.int32, (MAX_WINDOW, LANES), 1) < POOL_GROUP
    low_r = lax.broadcasted_iota(jnp.int32, (1, LANES), 1) < POOL_GROUP
    pos1 = (seq_tile * tm + 1
            + lax.broadcasted_iota(jnp.int32, (MAX_WINDOW, LANES), 0)).astype(F32)

    def window_mean(total, w_low, w_high):
        cnt = jnp.where(low_h, jnp.minimum(pos1, w_low), jnp.minimum(pos1, w_high))
        inv_w = jnp.where(low_r, 1.0 / w_low, 1.0 / w_high)
        return jnp.concatenate([total[:MAX_WINDOW] / cnt, total[MAX_WINDOW:] * inv_w], axis=0)

    e0 = ext[:, :LANES]
    s2 = e0 + pltpu.roll(e0, 1, 0)
    s4 = s2 + pltpu.roll(s2, 2, 0)
    sum0 = jnp.where(low, s2[MAX_WINDOW:], s4[MAX_WINDOW:])
    e1 = ext[:, LANES:]
    t2 = e1 + pltpu.roll(e1, 1, 0)
    t4 = t2 + pltpu.roll(t2, 2, 0)
    t8 = t4 + pltpu.roll(t4, 4, 0)
    t16 = t8 + pltpu.roll(t8, 8, 0)
    sum1 = jnp.where(low, t8[MAX_WINDOW:], t16[MAX_WINDOW:])
    pooled = jnp.concatenate([window_mean(sum0, 2.0, 4.0), window_mean(sum1, 8.0, 16.0)],
                             axis=1) - zb
    yb = _dot(pooled.astype(BF16), wpool_ref[...]) * pscale
    yb_ref[...] = yb.astype(BF16)

    u = _gelu(zu)
    v = _gelu(zv)
    mu = jnp.mean(v, axis=-1, keepdims=True)
    vc = v - mu
    var = jnp.mean(vc * vc, axis=-1, keepdims=True)
    vn = (vc * lax.rsqrt(var + EPS) * lng + lnb).astype(BF16)
    lane_c = lax.broadcasted_iota(jnp.int32, (CHUNK, LANES), 1)
    low_c = lane_c < SGU_HEAD_DIM
    zero = jnp.zeros((), BF16)
    for c in range(0, tm // CHUNK, 2):
        rows_a = slice(c * CHUNK, (c + 1) * CHUNK)
        rows_b = slice((c + 1) * CHUNK, (c + 2) * CHUNK)
        parts_a, parts_b = [], []
        for p in range(SGU_HEADS // 2):
            va = vn[rows_a, p * LANES:(p + 1) * LANES]
            vb = vn[rows_b, p * LANES:(p + 1) * LANES]
            top = jnp.concatenate([jnp.where(low_c, va, zero), jnp.where(low_c, vb, zero)], axis=1)
            bot = jnp.concatenate([jnp.where(low_c, zero, va), jnp.where(low_c, zero, vb)], axis=1)
            out = _dot(ws_ref[p], jnp.concatenate([top, bot], axis=0))
            parts_a.append(out[:, :LANES])
            parts_b.append(out[:, LANES:])
        for rows, parts in ((rows_a, parts_a), (rows_b, parts_b)):
            mixed = jnp.concatenate(parts, axis=1) + bsp_ref[...]
            yc_ref[rows, :] = (u[rows, :] * mixed).astype(BF16)


def _layer_spec(layer, shape):
    zeros = (0,) * len(shape)
    return pl.BlockSpec((None,) + tuple(shape), lambda *_: (layer,) + zeros)


def _mix_in(layer, x2d, vecs, win, wpool_bd, ws, bsp, seq_len, riders=()):
    m = x2d.shape[0]
    tm = MIX_IN_TILE
    n_steps = m // tm
    spec = functools.partial(_layer_spec, layer)
    r_specs = [_rider_specs(n_steps, lyr, w) for _, w, lyr in riders]
    return pl.pallas_call(
        functools.partial(_mix_in_kernel, seq_len // tm, tuple(k for k, _, _ in riders)),
        grid=(n_steps,),
        in_specs=[
            pl.BlockSpec((tm, D_MODEL), lambda i: (i, 0)),
            spec((VEC_ROWS, D_MODEL)),
            pl.BlockSpec((D_MODEL, D_IN), lambda i: (0, 0)),
            spec((D_POOL, D_POOL)),
            spec((SGU_HEADS // 2, CHUNK, 2 * CHUNK)),
            spec((CHUNK, D_SGU)),
        ] + [s for s, _, _ in r_specs],
        out_specs=[
            pl.BlockSpec((N_CLUSTERS, tm // SSM_BLOCK, FLAT), lambda i: (0, i, 0)),
            pl.BlockSpec((tm, D_POOL), lambda i: (i, 0)),
            pl.BlockSpec((tm, D_SGU), lambda i: (i, 0)),
        ] + [d for _, d, _ in r_specs],
        out_shape=[
            jax.ShapeDtypeStruct((N_CLUSTERS, m // SSM_BLOCK, FLAT), BF16),
            jax.ShapeDtypeStruct((m, D_POOL), BF16),
            jax.ShapeDtypeStruct((m, D_SGU), BF16),
        ] + [o for _, _, o in r_specs],
        scratch_shapes=[pltpu.VMEM((MAX_WINDOW, D_POOL), F32),
                        pltpu.VMEM((N_CLUSTERS, tm, LANES), F32)],
        compiler_params=pltpu.CompilerParams(
            dimension_semantics=("arbitrary",), vmem_limit_bytes=VMEM_MIX_IN),
        name="mix_in",
    )(x2d, vecs, win, wpool_bd, ws, bsp, *[w for _, w, _ in riders])


def _s5_build_operators(par_ref, bt_ref, ct_ref, wend_ref, wrd_ref, lag0_ref, lagk_ref, pw_ref):
    a_re = par_ref[0, 0:1, :]
    a_im = par_ref[0, 1:2, :]
    dt = jnp.exp(par_ref[0, 2:3, :])
    l_re = a_re * dt
    l_im = a_im * dt
    mag = jnp.exp(l_re)
    ar = mag * jnp.cos(l_im)
    ai = mag * jnp.sin(l_im)
    den = a_re * a_re + a_im * a_im
    f_re = ((ar - 1.0) * a_re + ai * a_im) / den
    f_im = (ai * a_re - (ar - 1.0) * a_im) / den
    bt_re, bt_im = bt_ref[0, 0], bt_ref[0, 1]
    bb_re = f_re * bt_re - f_im * bt_im
    bb_im = f_re * bt_im + f_im * bt_re
    ct_re, ct_im = ct_ref[0, 0], ct_ref[0, 1]

    lane_q = lax.broadcasted_iota(jnp.int32, (SSM_GROUP, LANES), 1) // SSM_STATE

    def power(l):
        m_l = jnp.exp(l * l_re)
        return m_l * jnp.cos(l * l_im), m_l * jnp.sin(l * l_im)

    def store_pairs(ref, slot, v_re, v_im):
        for p in range(GROUP_PAIRS):
            t_re = v_re[:, p * LANES:(p + 1) * LANES]
            t_im = v_im[:, p * LANES:(p + 1) * LANES]
            for q in range(2):
                rows = slice(q * GROUP_FLAT + slot * SSM_GROUP, q * GROUP_FLAT + (slot + 1) * SSM_GROUP)
                ref[p, rows, :LANES] = jnp.where(lane_q == q, t_re, 0.0).astype(BF16)
                ref[p, rows, LANES:] = jnp.where(lane_q == q, t_im, 0.0).astype(BF16)

    for s in range(SSM_BLOCK):
        p_re, p_im = power(float(SSM_BLOCK - 1 - s))
        store_pairs(wend_ref, s, p_re * bb_re - p_im * bb_im, p_re * bb_im + p_im * bb_re)
        q_re, q_im = power(float(s + 1))
        store_pairs(wrd_ref, s, q_re * ct_re - q_im * ct_im, -(q_re * ct_im + q_im * ct_re))
        r_re, r_im = power(float(s))
        store_pairs(lag0_ref, s, r_re * ct_re - r_im * ct_im, -(r_re * ct_im + r_im * ct_re))

    last = (SSM_BLOCK - 1) * SSM_GROUP
    col = lax.broadcasted_iota(jnp.int32, (SSM_GROUP, GROUP_FLAT), 1)
    for p in range(GROUP_PAIRS):
        bbar = jnp.concatenate([wend_ref[p, q * GROUP_FLAT + last:q * GROUP_FLAT + last + SSM_GROUP, :]
                                for q in range(2)], axis=0)
        lags = _dot_nt(bbar, lag0_ref[p])
        for q in range(2):
            k_row = lags[q * SSM_GROUP:(q + 1) * SSM_GROUP, q * GROUP_FLAT:(q + 1) * GROUP_FLAT]
            for s in range(SSM_BLOCK):
                shifted = pltpu.roll(k_row, s * SSM_GROUP, 1) if s else k_row
                lagk_ref[2 * p + q, s * SSM_GROUP:(s + 1) * SSM_GROUP, :] = jnp.where(
                    col >= s * SSM_GROUP, shifted, 0.0).astype(BF16)

    p_re, p_im = power(float(SSM_BLOCK))
    pw_ref[0:1, :] = jnp.concatenate([p_re, p_im], axis=1)
    for _ in range(SCAN_SEG_LEN.bit_length() - 1):
        p_re, p_im = p_re * p_re - p_im * p_im, 2.0 * (p_re * p_im)
    pw_ref[1:2, :] = jnp.concatenate([p_re, p_im], axis=1)


def _cmul_add(p_re, p_im, h_re, h_im, e_re, e_im):
    n_re = [pr * hr - pi * hi + er for pr, pi, hr, hi, er in zip(p_re, p_im, h_re, h_im, e_re)]
    n_im = [pr * hi + pi * hr + ei for pr, pi, hr, hi, ei in zip(p_re, p_im, h_re, h_im, e_im)]
    return n_re, n_im


def _s5_block_recurrence(st_ref, pw_ref, fillers):
    fillers = list(fillers)
    total_cost = sum(c for c, _ in fillers)
    issued_cost = [0.0]
    ticks = [0]

    def tick():
        while fillers and issued_cost[0] * 2 * SCAN_SEG_LEN <= ticks[0] * total_cost:
            cost, thunk = fillers.pop(0)
            issued_cost[0] += cost
            thunk()
        ticks[0] += 1

    half = SCAN_TILES // 2

    def bcast(row):
        return [jnp.broadcast_to(row[:, q * LANES:(q + 1) * LANES], (SCAN_SEGS, LANES))
                for q in range(half)]

    p_re, p_im = bcast(pw_ref[0:1, :CLUSTER_STATE]), bcast(pw_ref[0:1, CLUSTER_STATE:])
    seg_rows = lambda i: pl.ds(i, SCAN_SEGS, stride=SCAN_SEG_PITCH)

    def gather(i):
        e = [st_ref[q, seg_rows(i), :] for q in range(SCAN_TILES)]
        return e[:half], e[half:]

    zeros = [jnp.zeros((SCAN_SEGS, LANES), F32)] * half

    def end_step(i, h):
        e_re, e_im = gather(i)
        return _cmul_add(p_re, p_im, h[0], h[1], e_re, e_im)

    h = (zeros, zeros)
    for i in range(SCAN_SEG_LEN):
        tick()
        h = end_step(i, h)
    l_re, l_im = h

    s_re, s_im = bcast(pw_ref[1:2, :CLUSTER_STATE]), bcast(pw_ref[1:2, CLUSTER_STATE:])
    first = lax.broadcasted_iota(jnp.int32, (SCAN_SEGS, LANES), 0) == 0
    shift = lambda v: jnp.where(first, 0.0, pltpu.roll(v, 1, 0))
    c_re, c_im = zeros, zeros
    for _ in range(SCAN_SEGS - 1):
        n_re, n_im = _cmul_add(s_re, s_im, c_re, c_im, l_re, l_im)
        c_re, c_im = [shift(v) for v in n_re], [shift(v) for v in n_im]

    def scan_step(i, h):
        e_re, e_im = gather(i)
        for q in range(half):
            st_ref[q, seg_rows(i), :] = h[0][q]
            st_ref[half + q, seg_rows(i), :] = h[1][q]
        return _cmul_add(p_re, p_im, h[0], h[1], e_re, e_im)

    h = (c_re, c_im)
    for i in range(SCAN_SEG_LEN):
        tick()
        h = scan_step(i, h)
    while fillers:
        fillers.pop(0)[1]()


def _s5_kernel(u_ref, par_ref, bt_ref, ct_ref, dskip_ref, g_ref,
               wend_ref, wrd_ref, lag0_ref, lagk_ref, pw_ref, st_ref):
    r = pl.program_id(1)

    @pl.when(r == 0)
    def _():
        _s5_build_operators(par_ref, bt_ref, ct_ref, wend_ref, wrd_ref, lag0_ref, lagk_ref, pw_ref)

    u = u_ref[0]
    pair_cols = lambda p: slice(2 * p * GROUP_FLAT, 2 * (p + 1) * GROUP_FLAT)
    seg_rows = lambda j: slice(j * SCAN_SEG_PITCH, j * SCAN_SEG_PITCH + SCAN_SEG_LEN)

    for p in range(GROUP_PAIRS):
        e = _dot(u[:, pair_cols(p)], wend_ref[p])
        for tile, cols in ((p, slice(0, LANES)), (GROUP_PAIRS + p, slice(LANES, 2 * LANES))):
            for j in range(SCAN_SEGS):
                st_ref[tile, seg_rows(j), :] = e[j * SCAN_SEG_LEN:(j + 1) * SCAN_SEG_LEN, cols]

    ys = [None] * GROUPS_PER_CLUSTER

    def lag_product(g):
        ys[g] = _dot(u[:, g * GROUP_FLAT:(g + 1) * GROUP_FLAT], lagk_ref[g])

    _s5_block_recurrence(st_ref, pw_ref,
                         [(1, functools.partial(lag_product, g)) for g in range(len(ys))])
    y = jnp.concatenate(ys, axis=1)

    outs = []
    for p in range(GROUP_PAIRS):
        hprev = jnp.concatenate(
            [jnp.concatenate([st_ref[p, seg_rows(j), :], st_ref[GROUP_PAIRS + p, seg_rows(j), :]],
                             axis=1) for j in range(SCAN_SEGS)], axis=0)
        outs.append(_dot_nt(hprev.astype(BF16), wrd_ref[p]))
    y = y + jnp.concatenate(outs, axis=1)
    y = y + dskip_ref[0] * u.astype(F32)
    g_ref[0] = _gelu(y)


def _s5(layer, u_flat, par, bt, ct, dskip, blocks_per_seq):
    nb = u_flat.shape[1]
    rt = S5_ROW_TILE
    assert blocks_per_seq == rt and SCAN_SEG_LEN & (SCAN_SEG_LEN - 1) == 0
    return pl.pallas_call(
        _s5_kernel,
        grid=(N_CLUSTERS, nb // rt),
        in_specs=[
            pl.BlockSpec((1, rt, FLAT), lambda k, r: (k, r, 0)),
            pl.BlockSpec((None, 1, 3, CLUSTER_STATE), lambda k, r: (layer, k, 0, 0)),
            pl.BlockSpec((None, 1, 2, SSM_GROUP, CLUSTER_STATE), lambda k, r: (layer, k, 0, 0, 0)),
            pl.BlockSpec((None, 1, 2, SSM_GROUP, CLUSTER_STATE), lambda k, r: (layer, k, 0, 0, 0)),
            pl.BlockSpec((None, 1, 1, FLAT), lambda k, r: (layer, k, 0, 0)),
        ],
        out_specs=pl.BlockSpec((1, rt, FLAT), lambda k, r: (k, r, 0)),
        out_shape=jax.ShapeDtypeStruct((N_CLUSTERS, nb, FLAT), F32),
        scratch_shapes=[pltpu.VMEM((GROUP_PAIRS, 2 * GROUP_FLAT, 2 * LANES), BF16),
                        pltpu.VMEM((GROUP_PAIRS, 2 * GROUP_FLAT, 2 * LANES), BF16),
                        pltpu.VMEM((GROUP_PAIRS, 2 * GROUP_FLAT, 2 * LANES), BF16),
                        pltpu.VMEM((GROUPS_PER_CLUSTER, GROUP_FLAT, GROUP_FLAT), BF16),
                        pltpu.VMEM((2, 2 * CLUSTER_STATE), F32),
                        pltpu.VMEM((SCAN_TILES, SCAN_SEGS * SCAN_SEG_PITCH, LANES), F32)],
        compiler_params=pltpu.CompilerParams(
            dimension_semantics=("arbitrary", "arbitrary"), vmem_limit_bytes=VMEM_S5),
        name="s5",
    )(u_flat, par, bt, ct, dskip)


def _s5_params(A_re, A_im, log_dt, B_re, B_im, C_re, C_im, D_skip):
    depth = A_re.shape[0]
    k, gpc, n, c = N_CLUSTERS, GROUPS_PER_CLUSTER, SSM_STATE, SSM_GROUP
    rows = lambda a: a.reshape(depth, k, 1, CLUSTER_STATE)
    ldt = jnp.broadcast_to(log_dt[..., None], A_re.shape)
    par = jnp.concatenate([rows(A_re), rows(A_im), rows(ldt)], axis=2)
    b = jnp.stack([B_re, B_im], axis=1).reshape(depth, 2, k, gpc, n, c)
    bt = b.transpose(0, 2, 1, 5, 3, 4).reshape(depth, k, 2, c, CLUSTER_STATE)
    cc = jnp.stack([C_re, C_im], axis=1).reshape(depth, 2, k, gpc, c, n)
    ct = cc.transpose(0, 2, 1, 4, 3, 5).reshape(depth, k, 2, c, CLUSTER_STATE)
    dskip = jnp.broadcast_to(D_skip.reshape(depth, k, gpc, 1, c), (depth, k, gpc, SSM_BLOCK, c))
    return par, bt, ct, dskip.reshape(depth, k, 1, FLAT)


def _mix_out_kernel(final, rider_kinds, *refs):
    n_r = len(rider_kinds)
    (x_ref, g0_ref, gnext_ref, yb_ref, yc_ref, vec_ref, wglu_ref, wout_ref, wg_ref, wu_ref,
     wd_ref) = refs[:11]
    rider_src = refs[11:11 + n_r]
    o_ref = refs[11 + n_r]
    rider_dst = refs[12 + n_r:12 + 2 * n_r]
    gs_ref, ya_ref = refs[12 + 2 * n_r:]
    _cast_riders(rider_kinds, rider_src, rider_dst)

    bglu = vec_ref[VEC_GLU:VEC_GLU + 1, :D_SSM]
    gffn = vec_ref[VEC_GFFN:VEC_GFFN + 1, :]
    tm = x_ref.shape[0]
    half_block = SSM_BLOCK // 2

    def glu_branch(src_ref):
        for k in range(N_CLUSTERS):
            for hf in range(2):
                groups = [src_ref[k, :, g * GROUP_FLAT + hf * LANES:g * GROUP_FLAT + (hf + 1) * LANES]
                          for g in range(GROUPS_PER_CLUSTER)]
                for i, tile in enumerate(_chunk_transpose(groups)):
                    rows = pl.ds(hf * half_block + i, tm // SSM_BLOCK, stride=SSM_BLOCK)
                    gs_ref[k, rows, :] = tile
        g = jnp.concatenate([gs_ref[k] for k in range(N_CLUSTERS)], axis=1)
        ya = g * jax.nn.sigmoid(_dot(g.astype(BF16), wglu_ref[...]) + bglu)
        ya_ref[...] = ya.astype(BF16)

    @pl.when(pl.program_id(0) == 0)
    def _():
        glu_branch(g0_ref)

    ymix = jnp.concatenate([ya_ref[...], yb_ref[...], yc_ref[...]], axis=1)
    x1 = x_ref[...] + _dot(ymix, wout_ref[...])
    h = _rms(x1, gffn).astype(BF16)
    acc = x1
    d_ff = wg_ref.shape[1]
    for n, c0 in enumerate(range(0, d_ff, FF_CHUNK)):
        c1 = min(c0 + FF_CHUNK, d_ff)
        gate = _dot(h, wg_ref[:, c0:c1])
        up = _dot(h, wu_ref[:, c0:c1])
        act = (gate * jax.nn.sigmoid(gate) * up).astype(BF16)
        acc = acc + _dot(act, wd_ref[c0:c1, :])
        if n == 0:
            glu_branch(gnext_ref)
    if final:
        acc = _rms(acc, vec_ref[VEC_GFINAL:VEC_GFINAL + 1, :])
    o_ref[...] = acc


def _mix_out(layer, x2d, g3, yb, yc, vecs, wglu, wout, wg, wu, wd, final, riders=()):
    m = x2d.shape[0]
    tm = TOKEN_TILE
    n_steps = m // tm
    d_ff = wg.shape[-1]
    resident = lambda shape: pl.BlockSpec(shape, lambda i: (0, 0), pipeline_mode=pl.Buffered(1))
    r_specs = [_rider_specs(n_steps, lyr, w) for _, w, lyr in riders]
    return pl.pallas_call(
        functools.partial(_mix_out_kernel, final, tuple(k for k, _, _ in riders)),
        grid=(n_steps,),
        in_specs=[
            pl.BlockSpec((tm, D_MODEL), lambda i: (i, 0)),
            pl.BlockSpec((N_CLUSTERS, tm // SSM_BLOCK, FLAT), lambda i: (0, 0, 0),
                         pipeline_mode=pl.Buffered(1)),
            pl.BlockSpec((N_CLUSTERS, tm // SSM_BLOCK, FLAT),
                         lambda i: (0, jnp.minimum(i + 1, n_steps - 1), 0)),
            pl.BlockSpec((tm, D_POOL), lambda i: (i, 0)),
            pl.BlockSpec((tm, D_SGU), lambda i: (i, 0)),
            _layer_spec(layer, (VEC_ROWS, D_MODEL)),
            _layer_spec(layer, (D_SSM, D_SSM)),
            resident((D_MODEL, D_MODEL)),
            resident((D_MODEL, d_ff)),
            resident((D_MODEL, d_ff)),
            resident((d_ff, D_MODEL)),
        ] + [s for s, _, _ in r_specs],
        out_specs=[pl.BlockSpec((tm, D_MODEL), lambda i: (i, 0))] + [d for _, d, _ in r_specs],
        out_shape=[jax.ShapeDtypeStruct((m, D_MODEL), F32)] + [o for _, _, o in r_specs],
        scratch_shapes=[pltpu.VMEM((N_CLUSTERS, tm, LANES), F32),
                        pltpu.VMEM((tm, D_SSM), BF16)],
        compiler_params=pltpu.CompilerParams(
            dimension_semantics=("arbitrary",), vmem_limit_bytes=VMEM_MIX_OUT),
        name="mix_out",
    )(x2d, g3, g3, yb, yc, vecs, wglu, wout, wg, wu, wd, *[w for _, w, _ in riders])


def kernel(x, g_mix, w_in, A_re, A_im, log_dt, B_re, B_im, C_re, C_im, D_skip, w_glu, b_glu,
           w_pool, pool_scale, sgu_ln_g, sgu_ln_b, w_spatial, b_spatial, w_out, g_ffn,
           w_gate, w_up, w_down, g_final):
    bsz, seq, d = x.shape
    depth = w_in.shape[0]
    m = bsz * seq
    assert d == D_MODEL and seq % MIX_IN_TILE == 0 and MIX_IN_TILE % (2 * CHUNK) == 0
    assert m % TOKEN_TILE == 0
    assert (seq // SSM_BLOCK) % S5_ROW_TILE == 0
    x2d = x.reshape(m, D_MODEL)
    tril = jnp.tril(jnp.ones((CHUNK, CHUNK), dtype=bool))
    eye = jnp.eye(len(POOL_WINDOWS), dtype=F32)
    wpool_bd = jnp.einsum('dgij,gh->dgihj', w_pool, eye).reshape(depth, D_POOL, D_POOL).astype(BF16)
    ws = jnp.where(tril, w_spatial, 0.0).astype(BF16)
    ws = jnp.concatenate([ws[:, 0::2], ws[:, 1::2]], axis=-1)
    bsp = jnp.repeat(jnp.swapaxes(b_spatial, 1, 2), SGU_HEAD_DIM, axis=2)
    par, bt, ct, dskip = _s5_params(A_re, A_im, log_dt, B_re, B_im, C_re, C_im, D_skip)
    vec_rows = [None] * VEC_ROWS
    vec_rows[VEC_GMIX] = g_mix
    vec_rows[VEC_GFFN] = g_ffn
    vec_rows[VEC_SGU] = jnp.concatenate([pool_scale, sgu_ln_g, sgu_ln_b], axis=-1)
    vec_rows[VEC_GLU] = jnp.pad(b_glu, ((0, 0), (0, D_MODEL - D_SSM)))
    vec_rows[VEC_GFINAL] = jnp.broadcast_to(g_final, (depth, D_MODEL))
    zero_row = jnp.zeros((depth, D_MODEL), F32)
    vecs = jnp.stack([zero_row if r is None else r for r in vec_rows], axis=1)
    wglu = w_glu.astype(BF16)
    b_end, u_end = D_SSM + D_POOL, D_SSM + D_POOL + D_SGU
    win = jnp.concatenate([w_in[0, :, u_end:], w_in[0, :, b_end:u_end], w_in[0, :, :b_end]],
                          axis=-1).astype(BF16)
    ffn_stacks = (w_out, w_gate, w_up, w_down)
    ffn = None
    for l in range(depth):
        riders = tuple(("plain", w, 0) for w in ffn_stacks) if l == 0 else ()
        u_flat, yb, yc, *cast = _mix_in(l, x2d, vecs, win, wpool_bd, ws, bsp, seq, riders)
        if l == 0:
            ffn = cast
        g3 = _s5(l, u_flat, par, bt, ct, dskip, seq // SSM_BLOCK)
        last = l == depth - 1
        riders = () if last else (("w_in", w_in, l + 1),) + tuple(
            ("plain", w, l + 1) for w in ffn_stacks)
        x2d, *cast = _mix_out(l, x2d, g3, yb, yc, vecs, wglu, *ffn, final=last, riders=riders)
        if not last:
            win, ffn = cast[0], cast[1:]
    return x2d.reshape(bsz, seq, D_MODEL)
```

```python
import functools
import math

import jax
import jax.numpy as jnp
from jax import lax
from jax.experimental import pallas as pl
from jax.experimental.pallas import tpu as pltpu

D_MODEL = 1024
D_SSM = 384
SSM_GROUP = 16
N_SSM_GROUPS = D_SSM // SSM_GROUP
SSM_STATE = 64
POOL_WINDOWS = (2, 4, 8, 16)
POOL_GROUP = 64
D_POOL = len(POOL_WINDOWS) * POOL_GROUP
MAX_WINDOW = max(POOL_WINDOWS)
SGU_HEADS = 6
SGU_HEAD_DIM = 64
D_SGU = SGU_HEADS * SGU_HEAD_DIM
CHUNK = 128
D_IN = D_SSM + D_POOL + 2 * D_SGU
EPS = 1e-6

LANES = 128
SSM_BLOCK = 16
N_CLUSTERS = D_SSM // LANES
GROUPS_PER_CLUSTER = LANES // SSM_GROUP
CLUSTER_STATE = GROUPS_PER_CLUSTER * SSM_STATE
FLAT = SSM_BLOCK * LANES
GROUP_FLAT = SSM_BLOCK * SSM_GROUP
GROUP_PAIRS = GROUPS_PER_CLUSTER // 2

MXU_WIDTH = 256
BF16_SUBLANES = 16
VEC_ROWS = 8
VEC_GMIX, VEC_GFFN, VEC_SGU, VEC_GLU, VEC_GFINAL = 0, 1, 2, 3, 4
TOKEN_TILE = 512
MIX_IN_TILE = 1024
S5_ROW_TILE = 512
SUBLANES = 8
SCAN_SEGS = SUBLANES
SCAN_SEG_LEN = S5_ROW_TILE // SCAN_SEGS
SCAN_SEG_PITCH = SCAN_SEG_LEN + SUBLANES
SCAN_TILES = 2 * CLUSTER_STATE // LANES
FF_CHUNK = 3 * MXU_WIDTH
V7X_VMEM_BYTES = 64 * 1024 * 1024
VMEM_MIX_IN = VMEM_S5 = VMEM_MIX_OUT = V7X_VMEM_BYTES

F32 = jnp.float32
BF16 = jnp.bfloat16
NT_DIMS = (((1,), (1,)), ((), ()))


def _gelu(x):
    c = math.sqrt(2.0 / math.pi)
    return x * (0.5 + 0.5 * jnp.tanh(x * (c + (c * 0.044715) * (x * x))))


def _rms(x, g):
    ms = jnp.mean(x * x, axis=-1, keepdims=True)
    return x * lax.rsqrt(ms + EPS) * g


def _dot(a, b):
    return jnp.dot(a, b, preferred_element_type=F32)


def _dot_nt(a, b):
    return lax.dot_general(a, b, NT_DIMS, preferred_element_type=F32)


def _chunk_transpose(tiles):
    n = len(tiles)
    chunk = LANES // n
    v = list(tiles)
    cidx = lax.broadcasted_iota(jnp.int32, v[0].shape, 1) // chunk
    k = 1
    while k < n:
        keep = (cidx & k) == 0
        nv = list(v)
        for a in range(n):
            if a & k:
                continue
            lo, hi = v[a], v[a + k]
            nv[a] = jnp.where(keep, lo, pltpu.roll(hi, k * chunk, 1))
            nv[a + k] = jnp.where(keep, pltpu.roll(lo, LANES - k * chunk, 1), hi)
        v = nv
        k *= 2
    return v


def _cast_riders(kinds, src_refs, dst_refs):
    for kind, src, dst in zip(kinds, src_refs, dst_refs):
        if kind == "w_in":
            b_end, u_end = D_SSM + D_POOL, D_SSM + D_POOL + D_SGU
            dst[:, :D_SGU] = src[:, u_end:].astype(BF16)
            dst[:, D_SGU:2 * D_SGU] = src[:, b_end:u_end].astype(BF16)
            dst[:, 2 * D_SGU:] = src[:, :b_end].astype(BF16)
        else:
            dst[...] = src[...].astype(BF16)


def _rider_specs(n_steps, layer, w):
    rows, cols = w.shape[1:]
    share = 1
    while (rows * share) % n_steps or (rows * share // n_steps) % BF16_SUBLANES:
        share *= 2
    slab = rows * share // n_steps
    which = lambda i: jnp.minimum(i, n_steps - 1) // share
    src = pl.BlockSpec((None, slab, cols), lambda i: (layer, which(i), 0))
    dst = pl.BlockSpec((slab, cols), lambda i: (which(i), 0))
    return src, dst, jax.ShapeDtypeStruct((rows, cols), BF16)


def _mix_in_kernel(tiles_per_seq, rider_kinds, *refs):
    n_r = len(rider_kinds)
    x_ref, vec_ref, win_ref, wpool_ref, ws_ref, bsp_ref = refs[:6]
    rider_src = refs[6:6 + n_r]
    za_ref, yb_ref, yc_ref = refs[6 + n_r:9 + n_r]
    rider_dst = refs[9 + n_r:9 + 2 * n_r]
    halo_ref, zs_ref = refs[9 + 2 * n_r:]
    _cast_riders(rider_kinds, rider_src, rider_dst)

    tm = x_ref.shape[0]
    i = pl.program_id(0)
    seq_tile = i % tiles_per_seq
    gmix = vec_ref[VEC_GMIX:VEC_GMIX + 1, :]
    pscale = vec_ref[VEC_SGU:VEC_SGU + 1, :D_POOL]
    lng = vec_ref[VEC_SGU:VEC_SGU + 1, D_POOL:D_POOL + D_SGU]
    lnb = vec_ref[VEC_SGU:VEC_SGU + 1, D_POOL + D_SGU:]

    h = _rms(x_ref[...], gmix).astype(BF16)
    z_sgu = _dot(h, win_ref[:, :2 * D_SGU])
    z_rest = _dot(h, win_ref[:, 2 * D_SGU:])
    zv = z_sgu[:, :D_SGU]
    zu = z_sgu[:, D_SGU:]
    zb = z_rest[:, D_SSM:]

    for k in range(N_CLUSTERS):
        zs_ref[k] = z_rest[:, k * LANES:(k + 1) * LANES]
    half_block = SSM_BLOCK // 2
    for k in range(N_CLUSTERS):
        for hf in range(2):
            slots = [zs_ref[k, pl.ds(hf * half_block + i, tm // SSM_BLOCK, stride=SSM_BLOCK), :]
                     for i in range(half_block)]
            for g, tile in enumerate(_chunk_transpose(slots)):
                col = g * GROUP_FLAT + hf * LANES
                za_ref[k, :, col:col + LANES] = tile.astype(BF16)

    halo = jnp.where(seq_tile == 0, 0.0, halo_ref[...])
    halo_ref[...] = zb[tm - MAX_WINDOW:, :]
    ext = jnp.concatenate([halo, zb], axis=0)
    lane = lax.broadcasted_iota(jnp.int32, (tm, LANES), 1)
    low = lane < POOL_GROUP
    low_h = lax.broadcasted_iota(jnp.int32, (MAX_WINDOW, LANES), 1) < POOL_GROUP
    low_r = lax.broadcasted_iota(jnp.int32, (1, LANES), 1) < POOL_GROUP
    pos1 = (seq_tile * tm + 1
            + lax.broadcasted_iota(jnp.int32, (MAX_WINDOW, LANES), 0)).astype(F32)

    def window_mean(total, w_low, w_high):
        cnt = jnp.where(low_h, jnp.minimum(pos1, w_low), jnp.minimum(pos1, w_high))
        inv_w = jnp.where(low_r, 1.0 / w_low, 1.0 / w_high)
        return jnp.concatenate([total[:MAX_WINDOW] / cnt, total[MAX_WINDOW:] * inv_w], axis=0)

    e0 = ext[:, :LANES]
    s2 = e0 + pltpu.roll(e0, 1, 0)
    s4 = s2 + pltpu.roll(s2, 2, 0)
    sum0 = jnp.where(low, s2[MAX_WINDOW:], s4[MAX_WINDOW:])
    e1 = ext[:, LANES:]
    t2 = e1 + pltpu.roll(e1, 1, 0)
    t4 = t2 + pltpu.roll(t2, 2, 0)
    t8 = t4 + pltpu.roll(t4, 4, 0)
    t16 = t8 + pltpu.roll(t8, 8, 0)
    sum1 = jnp.where(low, t8[MAX_WINDOW:], t16[MAX_WINDOW:])
    pooled = jnp.concatenate([window_mean(sum0, 2.0, 4.0), window_mean(sum1, 8.0, 16.0)],
                             axis=1) - zb
    yb = _dot(pooled.astype(BF16), wpool_ref[...]) * pscale
    yb_ref[...] = yb.astype(BF16)

    u = _gelu(zu)
    v = _gelu(zv)
    mu = jnp.mean(v, axis=-1, keepdims=True)
    vc = v - mu
    var = jnp.mean(vc * vc, axis=-1, keepdims=True)
    vn = (vc * lax.rsqrt(var + EPS) * lng + lnb).astype(BF16)
    lane_c = lax.broadcasted_iota(jnp.int32, (CHUNK, LANES), 1)
    low_c = lane_c < SGU_HEAD_DIM
    zero = jnp.zeros((), BF16)
    for c in range(0, tm // CHUNK, 2):
        rows_a = slice(c * CHUNK, (c + 1) * CHUNK)
        rows_b = slice((c + 1) * CHUNK, (c + 2) * CHUNK)
        parts_a, parts_b = [], []
        for p in range(SGU_HEADS // 2):
            va = vn[rows_a, p * LANES:(p + 1) * LANES]
            vb = vn[rows_b, p * LANES:(p + 1) * LANES]
            top = jnp.concatenate([jnp.where(low_c, va, zero), jnp.where(low_c, vb, zero)], axis=1)
            bot = jnp.concatenate([jnp.where(low_c, zero, va), jnp.where(low_c, zero, vb)], axis=1)
            out = _dot(ws_ref[p], jnp.concatenate([top, bot], axis=0))
            parts_a.append(out[:, :LANES])
            parts_b.append(out[:, LANES:])
        for rows, parts in ((rows_a, parts_a), (rows_b, parts_b)):
            mixed = jnp.concatenate(parts, axis=1) + bsp_ref[...]
            yc_ref[rows, :] = (u[rows, :] * mixed).astype(BF16)


def _layer_spec(layer, shape):
    zeros = (0,) * len(shape)
    return pl.BlockSpec((None,) + tuple(shape), lambda *_: (layer,) + zeros)


def _mix_in(layer, x2d, vecs, win, wpool_bd, ws, bsp, seq_len, riders=()):
    m = x2d.shape[0]
    tm = MIX_IN_TILE
    n_steps = m // tm
    spec = functools.partial(_layer_spec, layer)
    r_specs = [_rider_specs(n_steps, lyr, w) for _, w, lyr in riders]
    return pl.pallas_call(
        functools.partial(_mix_in_kernel, seq_len // tm, tuple(k for k, _, _ in riders)),
        grid=(n_steps,),
        in_specs=[
            pl.BlockSpec((tm, D_MODEL), lambda i: (i, 0)),
            spec((VEC_ROWS, D_MODEL)),
            pl.BlockSpec((D_MODEL, D_IN), lambda i: (0, 0)),
            spec((D_POOL, D_POOL)),
            spec((SGU_HEADS // 2, CHUNK, 2 * CHUNK)),
            spec((CHUNK, D_SGU)),
        ] + [s for s, _, _ in r_specs],
        out_specs=[
            pl.BlockSpec((N_CLUSTERS, tm // SSM_BLOCK, FLAT), lambda i: (0, i, 0)),
            pl.BlockSpec((tm, D_POOL), lambda i: (i, 0)),
            pl.BlockSpec((tm, D_SGU), lambda i: (i, 0)),
        ] + [d for _, d, _ in r_specs],
        out_shape=[
            jax.ShapeDtypeStruct((N_CLUSTERS, m // SSM_BLOCK, FLAT), BF16),
            jax.ShapeDtypeStruct((m, D_POOL), BF16),
            jax.ShapeDtypeStruct((m, D_SGU), BF16),
        ] + [o for _, _, o in r_specs],
        scratch_shapes=[pltpu.VMEM((MAX_WINDOW, D_POOL), F32),
                        pltpu.VMEM((N_CLUSTERS, tm, LANES), F32)],
        compiler_params=pltpu.CompilerParams(
            dimension_semantics=("arbitrary",), vmem_limit_bytes=VMEM_MIX_IN),
        name="mix_in",
    )(x2d, vecs, win, wpool_bd, ws, bsp, *[w for _, w, _ in riders])


def _s5_build_operators(par_ref, bt_ref, ct_ref, wend_ref, wrd_ref, lag0_ref, lagk_ref, pw_ref):
    a_re = par_ref[0, 0:1, :]
    a_im = par_ref[0, 1:2, :]
    dt = jnp.exp(par_ref[0, 2:3, :])
    l_re = a_re * dt
    l_im = a_im * dt
    mag = jnp.exp(l_re)
    ar = mag * jnp.cos(l_im)
    ai = mag * jnp.sin(l_im)
    den = a_re * a_re + a_im * a_im
    f_re = ((ar - 1.0) * a_re + ai * a_im) / den
    f_im = (ai * a_re - (ar - 1.0) * a_im) / den
    bt_re, bt_im = bt_ref[0, 0], bt_ref[0, 1]
    bb_re = f_re * bt_re - f_im * bt_im
    bb_im = f_re * bt_im + f_im * bt_re
    ct_re, ct_im = ct_ref[0, 0], ct_ref[0, 1]

    lane_q = lax.broadcasted_iota(jnp.int32, (SSM_GROUP, LANES), 1) // SSM_STATE

    def power(l):
        m_l = jnp.exp(l * l_re)
        return m_l * jnp.cos(l * l_im), m_l * jnp.sin(l * l_im)

    def store_pairs(ref, slot, v_re, v_im):
        for p in range(GROUP_PAIRS):
            t_re = v_re[:, p * LANES:(p + 1) * LANES]
            t_im = v_im[:, p * LANES:(p + 1) * LANES]
            for q in range(2):
                rows = slice(q * GROUP_FLAT + slot * SSM_GROUP, q * GROUP_FLAT + (slot + 1) * SSM_GROUP)
                ref[p, rows, :LANES] = jnp.where(lane_q == q, t_re, 0.0).astype(BF16)
                ref[p, rows, LANES:] = jnp.where(lane_q == q, t_im, 0.0).astype(BF16)

    for s in range(SSM_BLOCK):
        p_re, p_im = power(float(SSM_BLOCK - 1 - s))
        store_pairs(wend_ref, s, p_re * bb_re - p_im * bb_im, p_re * bb_im + p_im * bb_re)
        q_re, q_im = power(float(s + 1))
        store_pairs(wrd_ref, s, q_re * ct_re - q_im * ct_im, -(q_re * ct_im + q_im * ct_re))
        r_re, r_im = power(float(s))
        store_pairs(lag0_ref, s, r_re * ct_re - r_im * ct_im, -(r_re * ct_im + r_im * ct_re))

    last = (SSM_BLOCK - 1) * SSM_GROUP
    col = lax.broadcasted_iota(jnp.int32, (SSM_GROUP, GROUP_FLAT), 1)
    for p in range(GROUP_PAIRS):
        bbar = jnp.concatenate([wend_ref[p, q * GROUP_FLAT + last:q * GROUP_FLAT + last + SSM_GROUP, :]
                                for q in range(2)], axis=0)
        lags = _dot_nt(bbar, lag0_ref[p])
        for q in range(2):
            k_row = lags[q * SSM_GROUP:(q + 1) * SSM_GROUP, q * GROUP_FLAT:(q + 1) * GROUP_FLAT]
            for s in range(SSM_BLOCK):
                shifted = pltpu.roll(k_row, s * SSM_GROUP, 1) if s else k_row
                lagk_ref[2 * p + q, s * SSM_GROUP:(s + 1) * SSM_GROUP, :] = jnp.where(
                    col >= s * SSM_GROUP, shifted, 0.0).astype(BF16)

    p_re, p_im = power(float(SSM_BLOCK))
    pw_ref[0:1, :] = jnp.concatenate([p_re, p_im], axis=1)
    for _ in range(SCAN_SEG_LEN.bit_length() - 1):
        p_re, p_im = p_re * p_re - p_im * p_im, 2.0 * (p_re * p_im)
    pw_ref[1:2, :] = jnp.concatenate([p_re, p_im], axis=1)


def _cmul_add(p_re, p_im, h_re, h_im, e_re, e_im):
    n_re = [pr * hr - pi * hi + er for pr, pi, hr, hi, er in zip(p_re, p_im, h_re, h_im, e_re)]
    n_im = [pr * hi + pi * hr + ei for pr, pi, hr, hi, ei in zip(p_re, p_im, h_re, h_im, e_im)]
    return n_re, n_im


def _s5_block_recurrence(st_ref, pw_ref, fillers):
    fillers = list(fillers)
    total_cost = sum(c for c, _ in fillers)
    issued_cost = [0.0]
    ticks = [0]

    def tick():
        while fillers and issued_cost[0] * 2 * SCAN_SEG_LEN <= ticks[0] * total_cost:
            cost, thunk = fillers.pop(0)
            issued_cost[0] += cost
            thunk()
        ticks[0] += 1

    half = SCAN_TILES // 2

    def bcast(row):
        return [jnp.broadcast_to(row[:, q * LANES:(q + 1) * LANES], (SCAN_SEGS, LANES))
                for q in range(half)]

    p_re, p_im = bcast(pw_ref[0:1, :CLUSTER_STATE]), bcast(pw_ref[0:1, CLUSTER_STATE:])
    seg_rows = lambda i: pl.ds(i, SCAN_SEGS, stride=SCAN_SEG_PITCH)

    def gather(i):
        e = [st_ref[q, seg_rows(i), :] for q in range(SCAN_TILES)]
        return e[:half], e[half:]

    zeros = [jnp.zeros((SCAN_SEGS, LANES), F32)] * half

    def end_step(i, h):
        e_re, e_im = gather(i)
        return _cmul_add(p_re, p_im, h[0], h[1], e_re, e_im)

    h = (zeros, zeros)
    for i in range(SCAN_SEG_LEN):
        tick()
        h = end_step(i, h)
    l_re, l_im = h

    s_re, s_im = bcast(pw_ref[1:2, :CLUSTER_STATE]), bcast(pw_ref[1:2, CLUSTER_STATE:])
    first = lax.broadcasted_iota(jnp.int32, (SCAN_SEGS, LANES), 0) == 0
    shift = lambda v: jnp.where(first, 0.0, pltpu.roll(v, 1, 0))
    c_re, c_im = zeros, zeros
    for _ in range(SCAN_SEGS - 1):
        n_re, n_im = _cmul_add(s_re, s_im, c_re, c_im, l_re, l_im)
        c_re, c_im = [shift(v) for v in n_re], [shift(v) for v in n_im]

    def scan_step(i, h):
        e_re, e_im = gather(i)
        for q in range(half):
            st_ref[q, seg_rows(i), :] = h[0][q]
            st_ref[half + q, seg_rows(i), :] = h[1][q]
        return _cmul_add(p_re, p_im, h[0], h[1], e_re, e_im)

    h = (c_re, c_im)
    for i in range(SCAN_SEG_LEN):
        tick()
        h = scan_step(i, h)
    while fillers:
        fillers.pop(0)[1]()


def _s5_kernel(u_ref, par_ref, bt_ref, ct_ref, dskip_ref, g_ref,
               wend_ref, wrd_ref, lag0_ref, lagk_ref, pw_ref, st_ref):
    r = pl.program_id(1)

    @pl.when(r == 0)
    def _():
        _s5_build_operators(par_ref, bt_ref, ct_ref, wend_ref, wrd_ref, lag0_ref, lagk_ref, pw_ref)

    u = u_ref[0]
    pair_cols = lambda p: slice(2 * p * GROUP_FLAT, 2 * (p + 1) * GROUP_FLAT)
    seg_rows = lambda j: slice(j * SCAN_SEG_PITCH, j * SCAN_SEG_PITCH + SCAN_SEG_LEN)

    for p in range(GROUP_PAIRS):
        e = _dot(u[:, pair_cols(p)], wend_ref[p])
        for tile, cols in ((p, slice(0, LANES)), (GROUP_PAIRS + p, slice(LANES, 2 * LANES))):
            for j in range(SCAN_SEGS):
                st_ref[tile, seg_rows(j), :] = e[j * SCAN_SEG_LEN:(j + 1) * SCAN_SEG_LEN, cols]

    ys = [None] * GROUPS_PER_CLUSTER

    def lag_product(g):
        ys[g] = _dot(u[:, g * GROUP_FLAT:(g + 1) * GROUP_FLAT], lagk_ref[g])

    _s5_block_recurrence(st_ref, pw_ref,
                         [(1, functools.partial(lag_product, g)) for g in range(len(ys))])
    y = jnp.concatenate(ys, axis=1)

    outs = []
    for p in range(GROUP_PAIRS):
        hprev = jnp.concatenate(
            [jnp.concatenate([st_ref[p, seg_rows(j), :], st_ref[GROUP_PAIRS + p, seg_rows(j), :]],
                             axis=1) for j in range(SCAN_SEGS)], axis=0)
        outs.append(_dot_nt(hprev.astype(BF16), wrd_ref[p]))
    y = y + jnp.concatenate(outs, axis=1)
    y = y + dskip_ref[0] * u.astype(F32)
    g_ref[0] = _gelu(y)


def _s5(layer, u_flat, par, bt, ct, dskip, blocks_per_seq):
    nb = u_flat.shape[1]
    rt = S5_ROW_TILE
    assert blocks_per_seq == rt and SCAN_SEG_LEN & (SCAN_SEG_LEN - 1) == 0
    return pl.pallas_call(
        _s5_kernel,
        grid=(N_CLUSTERS, nb // rt),
        in_specs=[
            pl.BlockSpec((1, rt, FLAT), lambda k, r: (k, r, 0)),
            pl.BlockSpec((None, 1, 3, CLUSTER_STATE), lambda k, r: (layer, k, 0, 0)),
            pl.BlockSpec((None, 1, 2, SSM_GROUP, CLUSTER_STATE), lambda k, r: (layer, k, 0, 0, 0)),
            pl.BlockSpec((None, 1, 2, SSM_GROUP, CLUSTER_STATE), lambda k, r: (layer, k, 0, 0, 0)),
            pl.BlockSpec((None, 1, 1, FLAT), lambda k, r: (layer, k, 0, 0)),
        ],
        out_specs=pl.BlockSpec((1, rt, FLAT), lambda k, r: (k, r, 0)),
        out_shape=jax.ShapeDtypeStruct((N_CLUSTERS, nb, FLAT), F32),
        scratch_shapes=[pltpu.VMEM((GROUP_PAIRS, 2 * GROUP_FLAT, 2 * LANES), BF16),
                        pltpu.VMEM((GROUP_PAIRS, 2 * GROUP_FLAT, 2 * LANES), BF16),
                        pltpu.VMEM((GROUP_PAIRS, 2 * GROUP_FLAT, 2 * LANES), BF16),
                        pltpu.VMEM((GROUPS_PER_CLUSTER, GROUP_FLAT, GROUP_FLAT), BF16),
                        pltpu.VMEM((2, 2 * CLUSTER_STATE), F32),
                        pltpu.VMEM((SCAN_TILES, SCAN_SEGS * SCAN_SEG_PITCH, LANES), F32)],
        compiler_params=pltpu.CompilerParams(
            dimension_semantics=("arbitrary", "arbitrary"), vmem_limit_bytes=VMEM_S5),
        name="s5",
    )(u_flat, par, bt, ct, dskip)


def _s5_params(A_re, A_im, log_dt, B_re, B_im, C_re, C_im, D_skip):
    depth = A_re.shape[0]
    k, gpc, n, c = N_CLUSTERS, GROUPS_PER_CLUSTER, SSM_STATE, SSM_GROUP
    rows = lambda a: a.reshape(depth, k, 1, CLUSTER_STATE)
    ldt = jnp.broadcast_to(log_dt[..., None], A_re.shape)
    par = jnp.concatenate([rows(A_re), rows(A_im), rows(ldt)], axis=2)
    b = jnp.stack([B_re, B_im], axis=1).reshape(depth, 2, k, gpc, n, c)
    bt = b.transpose(0, 2, 1, 5, 3, 4).reshape(depth, k, 2, c, CLUSTER_STATE)
    cc = jnp.stack([C_re, C_im], axis=1).reshape(depth, 2, k, gpc, c, n)
    ct = cc.transpose(0, 2, 1, 4, 3, 5).reshape(depth, k, 2, c, CLUSTER_STATE)
    dskip = jnp.broadcast_to(D_skip.reshape(depth, k, gpc, 1, c), (depth, k, gpc, SSM_BLOCK, c))
    return par, bt, ct, dskip.reshape(depth, k, 1, FLAT)


def _mix_out_kernel(final, rider_kinds, *refs):
    n_r = len(rider_kinds)
    first = refs[0:4]
    ahead = refs[4:8]
    vec_ref, wglu_ref, wout_ref, wg_ref, wu_ref, wd_ref = refs[8:14]
    rider_src = refs[14:14 + n_r]
    o_ref = refs[14 + n_r]
    rider_dst = refs[15 + n_r:15 + 2 * n_r]
    gs_ref, x1_ref, h_ref = refs[15 + 2 * n_r:]
    _cast_riders(rider_kinds, rider_src, rider_dst)

    bglu = vec_ref[VEC_GLU:VEC_GLU + 1, :D_SSM]
    gffn = vec_ref[VEC_GFFN:VEC_GFFN + 1, :]
    tm = o_ref.shape[0]
    half_block = SSM_BLOCK // 2

    def mix_tail(x_ref, g_ref, yb_ref, yc_ref):
        for k in range(N_CLUSTERS):
            for hf in range(2):
                groups = [g_ref[k, :, g * GROUP_FLAT + hf * LANES:g * GROUP_FLAT + (hf + 1) * LANES]
                          for g in range(GROUPS_PER_CLUSTER)]
                for i, tile in enumerate(_chunk_transpose(groups)):
                    rows = pl.ds(hf * half_block + i, tm // SSM_BLOCK, stride=SSM_BLOCK)
                    gs_ref[k, rows, :] = tile
        g = jnp.concatenate([gs_ref[k] for k in range(N_CLUSTERS)], axis=1)
        ya = g * jax.nn.sigmoid(_dot(g.astype(BF16), wglu_ref[...]) + bglu)
        ymix = jnp.concatenate([ya.astype(BF16), yb_ref[...], yc_ref[...]], axis=1)
        x1 = x_ref[...] + _dot(ymix, wout_ref[...])
        x1_ref[...] = x1
        h_ref[...] = _rms(x1, gffn).astype(BF16)

    @pl.when(pl.program_id(0) == 0)
    def _():
        mix_tail(*first)

    acc = x1_ref[...]
    h = h_ref[...]
    d_ff = wg_ref.shape[1]
    for n, c0 in enumerate(range(0, d_ff, FF_CHUNK)):
        c1 = min(c0 + FF_CHUNK, d_ff)
        gate = _dot(h, wg_ref[:, c0:c1])
        up = _dot(h, wu_ref[:, c0:c1])
        act = (gate * jax.nn.sigmoid(gate) * up).astype(BF16)
        acc = acc + _dot(act, wd_ref[c0:c1, :])
        if n == 0:
            mix_tail(*ahead)
    if final:
        acc = _rms(acc, vec_ref[VEC_GFINAL:VEC_GFINAL + 1, :])
    o_ref[...] = acc


def _mix_out(layer, x2d, g3, yb, yc, vecs, wglu, wout, wg, wu, wd, final, riders=()):
    m = x2d.shape[0]
    tm = TOKEN_TILE
    n_steps = m // tm
    d_ff = wg.shape[-1]
    resident = lambda shape: pl.BlockSpec(shape, lambda i: (0, 0), pipeline_mode=pl.Buffered(1))

    def tile0(shape, axis=0):
        return pl.BlockSpec(shape, lambda i: (0,) * len(shape), pipeline_mode=pl.Buffered(1))

    def ahead(shape, axis=0):
        nxt = lambda i: jnp.minimum(i + 1, n_steps - 1)
        return pl.BlockSpec(shape, lambda i: tuple(nxt(i) if d == axis else 0
                                                   for d in range(len(shape))))

    r_specs = [_rider_specs(n_steps, lyr, w) for _, w, lyr in riders]
    return pl.pallas_call(
        functools.partial(_mix_out_kernel, final, tuple(k for k, _, _ in riders)),
        grid=(n_steps,),
        in_specs=[
            tile0((tm, D_MODEL)), tile0((N_CLUSTERS, tm // SSM_BLOCK, FLAT), axis=1),
            tile0((tm, D_POOL)), tile0((tm, D_SGU)),
            ahead((tm, D_MODEL)), ahead((N_CLUSTERS, tm // SSM_BLOCK, FLAT), axis=1),
            ahead((tm, D_POOL)), ahead((tm, D_SGU)),
            _layer_spec(layer, (VEC_ROWS, D_MODEL)),
            _layer_spec(layer, (D_SSM, D_SSM)),
            resident((D_MODEL, D_MODEL)),
            resident((D_MODEL, d_ff)),
            resident((D_MODEL, d_ff)),
            resident((d_ff, D_MODEL)),
        ] + [s for s, _, _ in r_specs],
        out_specs=[pl.BlockSpec((tm, D_MODEL), lambda i: (i, 0))] + [d for _, d, _ in r_specs],
        out_shape=[jax.ShapeDtypeStruct((m, D_MODEL), F32)] + [o for _, _, o in r_specs],
        scratch_shapes=[pltpu.VMEM((N_CLUSTERS, tm, LANES), F32),
                        pltpu.VMEM((tm, D_MODEL), F32),
                        pltpu.VMEM((tm, D_MODEL), BF16)],
        compiler_params=pltpu.CompilerParams(
            dimension_semantics=("arbitrary",), vmem_limit_bytes=VMEM_MIX_OUT),
        name="mix_out",
    )(x2d, g3, yb, yc, x2d, g3, yb, yc, vecs, wglu, wout, wg, wu, wd, *[w for _, w, _ in riders])


def kernel(x, g_mix, w_in, A_re, A_im, log_dt, B_re, B_im, C_re, C_im, D_skip, w_glu, b_glu,
           w_pool, pool_scale, sgu_ln_g, sgu_ln_b, w_spatial, b_spatial, w_out, g_ffn,
           w_gate, w_up, w_down, g_final):
    bsz, seq, d = x.shape
    depth = w_in.shape[0]
    m = bsz * seq
    assert d == D_MODEL and seq % MIX_IN_TILE == 0 and MIX_IN_TILE % (2 * CHUNK) == 0
    assert m % TOKEN_TILE == 0
    assert (seq // SSM_BLOCK) % S5_ROW_TILE == 0
    x2d = x.reshape(m, D_MODEL)
    tril = jnp.tril(jnp.ones((CHUNK, CHUNK), dtype=bool))
    eye = jnp.eye(len(POOL_WINDOWS), dtype=F32)
    wpool_bd = jnp.einsum('dgij,gh->dgihj', w_pool, eye).reshape(depth, D_POOL, D_POOL).astype(BF16)
    ws = jnp.where(tril, w_spatial, 0.0).astype(BF16)
    ws = jnp.concatenate([ws[:, 0::2], ws[:, 1::2]], axis=-1)
    bsp = jnp.repeat(jnp.swapaxes(b_spatial, 1, 2), SGU_HEAD_DIM, axis=2)
    par, bt, ct, dskip = _s5_params(A_re, A_im, log_dt, B_re, B_im, C_re, C_im, D_skip)
    vec_rows = [None] * VEC_ROWS
    vec_rows[VEC_GMIX] = g_mix
    vec_rows[VEC_GFFN] = g_ffn
    vec_rows[VEC_SGU] = jnp.concatenate([pool_scale, sgu_ln_g, sgu_ln_b], axis=-1)
    vec_rows[VEC_GLU] = jnp.pad(b_glu, ((0, 0), (0, D_MODEL - D_SSM)))
    vec_rows[VEC_GFINAL] = jnp.broadcast_to(g_final, (depth, D_MODEL))
    zero_row = jnp.zeros((depth, D_MODEL), F32)
    vecs = jnp.stack([zero_row if r is None else r for r in vec_rows], axis=1)
    wglu = w_glu.astype(BF16)
    b_end, u_end = D_SSM + D_POOL, D_SSM + D_POOL + D_SGU
    win = jnp.concatenate([w_in[0, :, u_end:], w_in[0, :, b_end:u_end], w_in[0, :, :b_end]],
                          axis=-1).astype(BF16)
    ffn_stacks = (w_out, w_gate, w_up, w_down)
    ffn = None
    for l in range(depth):
        riders = tuple(("plain", w, 0) for w in ffn_stacks) if l == 0 else ()
        u_flat, yb, yc, *cast = _mix_in(l, x2d, vecs, win, wpool_bd, ws, bsp, seq, riders)
        if l == 0:
            ffn = cast
        g3 = _s5(l, u_flat, par, bt, ct, dskip, seq // SSM_BLOCK)
        last = l == depth - 1
        riders = () if last else (("w_in", w_in, l + 1),) + tuple(
            ("plain", w, l + 1) for w in ffn_stacks)
        x2d, *cast = _mix_out(l, x2d, g3, yb, yc, vecs, wglu, *ffn, final=last, riders=riders)
        if not last:
            win, ffn = cast[0], cast[1:]
    return x2d.reshape(bsz, seq, D_MODEL)
```

```python
import functools
import math

import jax
import jax.numpy as jnp
from jax import lax
from jax.experimental import pallas as pl
from jax.experimental.pallas import tpu as pltpu

D_MODEL = 1024
D_SSM = 384
SSM_GROUP = 16
N_SSM_GROUPS = D_SSM // SSM_GROUP
SSM_STATE = 64
POOL_WINDOWS = (2, 4, 8, 16)
POOL_GROUP = 64
D_POOL = len(POOL_WINDOWS) * POOL_GROUP
MAX_WINDOW = max(POOL_WINDOWS)
SGU_HEADS = 6
SGU_HEAD_DIM = 64
D_SGU = SGU_HEADS * SGU_HEAD_DIM
CHUNK = 128
D_IN = D_SSM + D_POOL + 2 * D_SGU
EPS = 1e-6

LANES = 128
SSM_BLOCK = 16
N_CLUSTERS = D_SSM // LANES
GROUPS_PER_CLUSTER = LANES // SSM_GROUP
CLUSTER_STATE = GROUPS_PER_CLUSTER * SSM_STATE
FLAT = SSM_BLOCK * LANES
GROUP_FLAT = SSM_BLOCK * SSM_GROUP
GROUP_PAIRS = GROUPS_PER_CLUSTER // 2

MXU_WIDTH = 256
BF16_SUBLANES = 16
VEC_ROWS = 8
VEC_GMIX, VEC_GFFN, VEC_SGU, VEC_GLU, VEC_GFINAL = 0, 1, 2, 3, 4
TOKEN_TILE = 512
MIX_IN_TILE = 1024
S5_ROW_TILE = 512
SUBLANES = 8
SCAN_SEGS = SUBLANES
SCAN_SEG_LEN = S5_ROW_TILE // SCAN_SEGS
SCAN_SEG_PITCH = SCAN_SEG_LEN + SUBLANES
SCAN_TILES = 2 * CLUSTER_STATE // LANES
FF_CHUNK = 3 * MXU_WIDTH
V7X_VMEM_BYTES = 64 * 1024 * 1024
VMEM_MIX_IN = VMEM_S5 = VMEM_MIX_OUT = V7X_VMEM_BYTES

F32 = jnp.float32
BF16 = jnp.bfloat16
NT_DIMS = (((1,), (1,)), ((), ()))


def _gelu(x):
    c = math.sqrt(2.0 / math.pi)
    return x * (0.5 + 0.5 * jnp.tanh(x * (c + (c * 0.044715) * (x * x))))


def _rms(x, g):
    ms = jnp.mean(x * x, axis=-1, keepdims=True)
    return x * lax.rsqrt(ms + EPS) * g


def _dot(a, b):
    return jnp.dot(a, b, preferred_element_type=F32)


def _dot_nt(a, b):
    return lax.dot_general(a, b, NT_DIMS, preferred_element_type=F32)


def _chunk_transpose(tiles):
    n = len(tiles)
    chunk = LANES // n
    v = list(tiles)
    cidx = lax.broadcasted_iota(jnp.int32, v[0].shape, 1) // chunk
    k = 1
    while k < n:
        keep = (cidx & k) == 0
        nv = list(v)
        for a in range(n):
            if a & k:
                continue
            lo, hi = v[a], v[a + k]
            nv[a] = jnp.where(keep, lo, pltpu.roll(hi, k * chunk, 1))
            nv[a + k] = jnp.where(keep, pltpu.roll(lo, LANES - k * chunk, 1), hi)
        v = nv
        k *= 2
    return v


def _cast_riders(kinds, src_refs, dst_refs):
    for kind, src, dst in zip(kinds, src_refs, dst_refs):
        if kind == "w_in":
            b_end, u_end = D_SSM + D_POOL, D_SSM + D_POOL + D_SGU
            dst[:, :D_SGU] = src[:, u_end:].astype(BF16)
            dst[:, D_SGU:2 * D_SGU] = src[:, b_end:u_end].astype(BF16)
            dst[:, 2 * D_SGU:] = src[:, :b_end].astype(BF16)
        else:
            dst[...] = src[...].astype(BF16)


def _rider_specs(n_steps, layer, w):
    rows, cols = w.shape[1:]
    share = 1
    while (rows * share) % n_steps or (rows * share // n_steps) % BF16_SUBLANES:
        share *= 2
    slab = rows * share // n_steps
    which = lambda i: jnp.minimum(i, n_steps - 1) // share
    src = pl.BlockSpec((None, slab, cols), lambda i: (layer, which(i), 0))
    dst = pl.BlockSpec((slab, cols), lambda i: (which(i), 0))
    return src, dst, jax.ShapeDtypeStruct((rows, cols), BF16)


def _mix_in_kernel(tiles_per_seq, rider_kinds, *refs):
    n_r = len(rider_kinds)
    x_ref, vec_ref, win_ref, wpool_ref, ws_ref, bsp_ref = refs[:6]
    rider_src = refs[6:6 + n_r]
    za_ref, yb_ref, yc_ref = refs[6 + n_r:9 + n_r]
    rider_dst = refs[9 + n_r:9 + 2 * n_r]
    halo_ref, zs_ref = refs[9 + 2 * n_r:]
    _cast_riders(rider_kinds, rider_src, rider_dst)

    tm = x_ref.shape[0]
    i = pl.program_id(0)
    seq_tile = i % tiles_per_seq
    gmix = vec_ref[VEC_GMIX:VEC_GMIX + 1, :]
    pscale = vec_ref[VEC_SGU:VEC_SGU + 1, :D_POOL]
    lng = vec_ref[VEC_SGU:VEC_SGU + 1, D_POOL:D_POOL + D_SGU]
    lnb = vec_ref[VEC_SGU:VEC_SGU + 1, D_POOL + D_SGU:]

    h = _rms(x_ref[...], gmix).astype(BF16)
    z_sgu = _dot(h, win_ref[:, :2 * D_SGU])
    z_rest = _dot(h, win_ref[:, 2 * D_SGU:])
    zv = z_sgu[:, :D_SGU]
    zu = z_sgu[:, D_SGU:]
    zb = z_rest[:, D_SSM:]

    for k in range(N_CLUSTERS):
        zs_ref[k] = z_rest[:, k * LANES:(k + 1) * LANES]
    half_block = SSM_BLOCK // 2
    for k in range(N_CLUSTERS):
        for hf in range(2):
            slots = [zs_ref[k, pl.ds(hf * half_block + i, tm // SSM_BLOCK, stride=SSM_BLOCK), :]
                     for i in range(half_block)]
            for g, tile in enumerate(_chunk_transpose(slots)):
                col = g * GROUP_FLAT + hf * LANES
                za_ref[k, :, col:col + LANES] = tile.astype(BF16)

    halo = jnp.where(seq_tile == 0, 0.0, halo_ref[...])
    halo_ref[...] = zb[tm - MAX_WINDOW:, :]
    ext = jnp.concatenate([halo, zb], axis=0)
    lane = lax.broadcasted_iota(jnp.int32, (tm, LANES), 1)
    low = lane < POOL_GROUP
    low_h = lax.broadcasted_iota(jnp.int32, (MAX_WINDOW, LANES), 1) < POOL_GROUP
    low_r = lax.broadcasted_iota(jnp.int32, (1, LANES), 1) < POOL_GROUP
    pos1 = (seq_tile * tm + 1
            + lax.broadcasted_iota(jnp.int32, (MAX_WINDOW, LANES), 0)).astype(F32)

    def window_mean(total, w_low, w_high):
        cnt = jnp.where(low_h, jnp.minimum(pos1, w_low), jnp.minimum(pos1, w_high))
        inv_w = jnp.where(low_r, 1.0 / w_low, 1.0 / w_high)
        return jnp.concatenate([total[:MAX_WINDOW] / cnt, total[MAX_WINDOW:] * inv_w], axis=0)

    e0 = ext[:, :LANES]
    s2 = e0 + pltpu.roll(e0, 1, 0)
    s4 = s2 + pltpu.roll(s2, 2, 0)
    sum0 = jnp.where(low, s2[MAX_WINDOW:], s4[MAX_WINDOW:])
    e1 = ext[:, LANES:]
    t2 = e1 + pltpu.roll(e1, 1, 0)
    t4 = t2 + pltpu.roll(t2, 2, 0)
    t8 = t4 + pltpu.roll(t4, 4, 0)
    t16 = t8 + pltpu.roll(t8, 8, 0)
    sum1 = jnp.where(low, t8[MAX_WINDOW:], t16[MAX_WINDOW:])
    pooled = jnp.concatenate([window_mean(sum0, 2.0, 4.0), window_mean(sum1, 8.0, 16.0)],
                             axis=1) - zb
    yb = _dot(pooled.astype(BF16), wpool_ref[...]) * pscale
    yb_ref[...] = yb.astype(BF16)

    u = _gelu(zu)
    v = _gelu(zv)
    mu = jnp.mean(v, axis=-1, keepdims=True)
    vc = v - mu
    var = jnp.mean(vc * vc, axis=-1, keepdims=True)
    vn = (vc * lax.rsqrt(var + EPS) * lng + lnb).astype(BF16)
    lane_c = lax.broadcasted_iota(jnp.int32, (CHUNK, LANES), 1)
    low_c = lane_c < SGU_HEAD_DIM
    zero = jnp.zeros((), BF16)
    for c in range(0, tm // CHUNK, 2):
        rows_a = slice(c * CHUNK, (c + 1) * CHUNK)
        rows_b = slice((c + 1) * CHUNK, (c + 2) * CHUNK)
        parts_a, parts_b = [], []
        for p in range(SGU_HEADS // 2):
            va = vn[rows_a, p * LANES:(p + 1) * LANES]
            vb = vn[rows_b, p * LANES:(p + 1) * LANES]
            top = jnp.concatenate([jnp.where(low_c, va, zero), jnp.where(low_c, vb, zero)], axis=1)
            bot = jnp.concatenate([jnp.where(low_c, zero, va), jnp.where(low_c, zero, vb)], axis=1)
            out = _dot(ws_ref[p], jnp.concatenate([top, bot], axis=0))
            parts_a.append(out[:, :LANES])
            parts_b.append(out[:, LANES:])
        for rows, parts in ((rows_a, parts_a), (rows_b, parts_b)):
            mixed = jnp.concatenate(parts, axis=1) + bsp_ref[...]
            yc_ref[rows, :] = (u[rows, :] * mixed).astype(BF16)


def _layer_spec(layer, shape):
    zeros = (0,) * len(shape)
    return pl.BlockSpec((None,) + tuple(shape), lambda *_: (layer,) + zeros)


def _mix_in(layer, x2d, vecs, win, wpool_bd, ws, bsp, seq_len, riders=()):
    m = x2d.shape[0]
    tm = MIX_IN_TILE
    n_steps = m // tm
    spec = functools.partial(_layer_spec, layer)
    r_specs = [_rider_specs(n_steps, lyr, w) for _, w, lyr in riders]
    return pl.pallas_call(
        functools.partial(_mix_in_kernel, seq_len // tm, tuple(k for k, _, _ in riders)),
        grid=(n_steps,),
        in_specs=[
            pl.BlockSpec((tm, D_MODEL), lambda i: (i, 0)),
            spec((VEC_ROWS, D_MODEL)),
            pl.BlockSpec((D_MODEL, D_IN), lambda i: (0, 0)),
            spec((D_POOL, D_POOL)),
            spec((SGU_HEADS // 2, CHUNK, 2 * CHUNK)),
            spec((CHUNK, D_SGU)),
        ] + [s for s, _, _ in r_specs],
        out_specs=[
            pl.BlockSpec((N_CLUSTERS, tm // SSM_BLOCK, FLAT), lambda i: (0, i, 0)),
            pl.BlockSpec((tm, D_POOL), lambda i: (i, 0)),
            pl.BlockSpec((tm, D_SGU), lambda i: (i, 0)),
        ] + [d for _, d, _ in r_specs],
        out_shape=[
            jax.ShapeDtypeStruct((N_CLUSTERS, m // SSM_BLOCK, FLAT), BF16),
            jax.ShapeDtypeStruct((m, D_POOL), BF16),
            jax.ShapeDtypeStruct((m, D_SGU), BF16),
        ] + [o for _, _, o in r_specs],
        scratch_shapes=[pltpu.VMEM((MAX_WINDOW, D_POOL), F32),
                        pltpu.VMEM((N_CLUSTERS, tm, LANES), F32)],
        compiler_params=pltpu.CompilerParams(
            dimension_semantics=("arbitrary",), vmem_limit_bytes=VMEM_MIX_IN),
        name="mix_in",
    )(x2d, vecs, win, wpool_bd, ws, bsp, *[w for _, w, _ in riders])


def _s5_build_operators(par_ref, bt_ref, ct_ref, wend_ref, wrd_ref, lag0_ref, lagk_ref, pw_ref):
    a_re = par_ref[0, 0:1, :]
    a_im = par_ref[0, 1:2, :]
    dt = jnp.exp(par_ref[0, 2:3, :])
    l_re = a_re * dt
    l_im = a_im * dt
    mag = jnp.exp(l_re)
    ar = mag * jnp.cos(l_im)
    ai = mag * jnp.sin(l_im)
    den = a_re * a_re + a_im * a_im
    f_re = ((ar - 1.0) * a_re + ai * a_im) / den
    f_im = (ai * a_re - (ar - 1.0) * a_im) / den
    bt_re, bt_im = bt_ref[0, 0], bt_ref[0, 1]
    bb_re = f_re * bt_re - f_im * bt_im
    bb_im = f_re * bt_im + f_im * bt_re
    ct_re, ct_im = ct_ref[0, 0], ct_ref[0, 1]

    lane_q = lax.broadcasted_iota(jnp.int32, (SSM_GROUP, LANES), 1) // SSM_STATE

    def power(l):
        m_l = jnp.exp(l * l_re)
        return m_l * jnp.cos(l * l_im), m_l * jnp.sin(l * l_im)

    def store_pairs(ref, slot, v_re, v_im):
        for p in range(GROUP_PAIRS):
            t_re = v_re[:, p * LANES:(p + 1) * LANES]
            t_im = v_im[:, p * LANES:(p + 1) * LANES]
            for q in range(2):
                rows = slice(q * GROUP_FLAT + slot * SSM_GROUP, q * GROUP_FLAT + (slot + 1) * SSM_GROUP)
                ref[p, rows, :LANES] = jnp.where(lane_q == q, t_re, 0.0).astype(BF16)
                ref[p, rows, LANES:] = jnp.where(lane_q == q, t_im, 0.0).astype(BF16)

    for s in range(SSM_BLOCK):
        p_re, p_im = power(float(SSM_BLOCK - 1 - s))
        store_pairs(wend_ref, s, p_re * bb_re - p_im * bb_im, p_re * bb_im + p_im * bb_re)
        q_re, q_im = power(float(s + 1))
        store_pairs(wrd_ref, s, q_re * ct_re - q_im * ct_im, -(q_re * ct_im + q_im * ct_re))
        r_re, r_im = power(float(s))
        store_pairs(lag0_ref, s, r_re * ct_re - r_im * ct_im, -(r_re * ct_im + r_im * ct_re))

    last = (SSM_BLOCK - 1) * SSM_GROUP
    col = lax.broadcasted_iota(jnp.int32, (SSM_GROUP, GROUP_FLAT), 1)
    for p in range(GROUP_PAIRS):
        bbar = jnp.concatenate([wend_ref[p, q * GROUP_FLAT + last:q * GROUP_FLAT + last + SSM_GROUP, :]
                                for q in range(2)], axis=0)
        lags = _dot_nt(bbar, lag0_ref[p])
        for q in range(2):
            k_row = lags[q * SSM_GROUP:(q + 1) * SSM_GROUP, q * GROUP_FLAT:(q + 1) * GROUP_FLAT]
            for s in range(SSM_BLOCK):
                shifted = pltpu.roll(k_row, s * SSM_GROUP, 1) if s else k_row
                lagk_ref[2 * p + q, s * SSM_GROUP:(s + 1) * SSM_GROUP, :] = jnp.where(
                    col >= s * SSM_GROUP, shifted, 0.0).astype(BF16)

    p_re, p_im = power(float(SSM_BLOCK))
    pw_ref[0:1, :] = jnp.concatenate([p_re, p_im], axis=1)
    for _ in range(SCAN_SEG_LEN.bit_length() - 1):
        p_re, p_im = p_re * p_re - p_im * p_im, 2.0 * (p_re * p_im)
    pw_ref[1:2, :] = jnp.concatenate([p_re, p_im], axis=1)


def _cmul_add(p_re, p_im, h_re, h_im, e_re, e_im):
    n_re = [pr * hr - pi * hi + er for pr, pi, hr, hi, er in zip(p_re, p_im, h_re, h_im, e_re)]
    n_im = [pr * hi + pi * hr + ei for pr, pi, hr, hi, ei in zip(p_re, p_im, h_re, h_im, e_im)]
    return n_re, n_im


def _s5_block_recurrence(st_ref, pw_ref, fillers):
    fillers = list(fillers)
    total_cost = sum(c for c, _ in fillers)
    issued_cost = [0.0]
    ticks = [0]

    def tick():
        while fillers and issued_cost[0] * 2 * SCAN_SEG_LEN <= ticks[0] * total_cost:
            cost, thunk = fillers.pop(0)
            issued_cost[0] += cost
            thunk()
        ticks[0] += 1

    half = SCAN_TILES // 2

    def bcast(row):
        return [jnp.broadcast_to(row[:, q * LANES:(q + 1) * LANES], (SCAN_SEGS, LANES))
                for q in range(half)]

    p_re, p_im = bcast(pw_ref[0:1, :CLUSTER_STATE]), bcast(pw_ref[0:1, CLUSTER_STATE:])
    seg_rows = lambda i: pl.ds(i, SCAN_SEGS, stride=SCAN_SEG_PITCH)

    def gather(i):
        e = [st_ref[q, seg_rows(i), :] for q in range(SCAN_TILES)]
        return e[:half], e[half:]

    zeros = [jnp.zeros((SCAN_SEGS, LANES), F32)] * half

    def end_step(i, h):
        e_re, e_im = gather(i)
        return _cmul_add(p_re, p_im, h[0], h[1], e_re, e_im)

    h = (zeros, zeros)
    for i in range(SCAN_SEG_LEN):
        tick()
        h = end_step(i, h)
    l_re, l_im = h

    s_re, s_im = bcast(pw_ref[1:2, :CLUSTER_STATE]), bcast(pw_ref[1:2, CLUSTER_STATE:])
    first = lax.broadcasted_iota(jnp.int32, (SCAN_SEGS, LANES), 0) == 0
    shift = lambda v: jnp.where(first, 0.0, pltpu.roll(v, 1, 0))
    c_re, c_im = zeros, zeros
    for _ in range(SCAN_SEGS - 1):
        n_re, n_im = _cmul_add(s_re, s_im, c_re, c_im, l_re, l_im)
        c_re, c_im = [shift(v) for v in n_re], [shift(v) for v in n_im]

    def scan_step(i, h):
        e_re, e_im = gather(i)
        for q in range(half):
            st_ref[q, seg_rows(i), :] = h[0][q]
            st_ref[half + q, seg_rows(i), :] = h[1][q]
        return _cmul_add(p_re, p_im, h[0], h[1], e_re, e_im)

    h = (c_re, c_im)
    for i in range(SCAN_SEG_LEN):
        tick()
        h = scan_step(i, h)
    while fillers:
        fillers.pop(0)[1]()


def _s5_kernel(u_ref, par_ref, bt_ref, ct_ref, dskip_ref, g_ref,
               wend_ref, wrd_ref, lag0_ref, lagk_ref, pw_ref, st_ref):
    r = pl.program_id(1)

    @pl.when(r == 0)
    def _():
        _s5_build_operators(par_ref, bt_ref, ct_ref, wend_ref, wrd_ref, lag0_ref, lagk_ref, pw_ref)

    u = u_ref[0]
    pair_cols = lambda p: slice(2 * p * GROUP_FLAT, 2 * (p + 1) * GROUP_FLAT)
    seg_rows = lambda j: slice(j * SCAN_SEG_PITCH, j * SCAN_SEG_PITCH + SCAN_SEG_LEN)

    for p in range(GROUP_PAIRS):
        e = _dot(u[:, pair_cols(p)], wend_ref[p])
        for tile, cols in ((p, slice(0, LANES)), (GROUP_PAIRS + p, slice(LANES, 2 * LANES))):
            for j in range(SCAN_SEGS):
                st_ref[tile, seg_rows(j), :] = e[j * SCAN_SEG_LEN:(j + 1) * SCAN_SEG_LEN, cols]

    ys = [None] * GROUPS_PER_CLUSTER

    def lag_product(g):
        ys[g] = _dot(u[:, g * GROUP_FLAT:(g + 1) * GROUP_FLAT], lagk_ref[g])

    _s5_block_recurrence(st_ref, pw_ref,
                         [(1, functools.partial(lag_product, g)) for g in range(len(ys))])
    y = jnp.concatenate(ys, axis=1)

    outs = []
    for p in range(GROUP_PAIRS):
        hprev = jnp.concatenate(
            [jnp.concatenate([st_ref[p, seg_rows(j), :], st_ref[GROUP_PAIRS + p, seg_rows(j), :]],
                             axis=1) for j in range(SCAN_SEGS)], axis=0)
        outs.append(_dot_nt(hprev.astype(BF16), wrd_ref[p]))
    y = y + jnp.concatenate(outs, axis=1)
    y = y + dskip_ref[0] * u.astype(F32)
    g_ref[0] = _gelu(y)


def _s5(layer, u_flat, par, bt, ct, dskip, blocks_per_seq):
    nb = u_flat.shape[1]
    rt = S5_ROW_TILE
    assert blocks_per_seq == rt and SCAN_SEG_LEN & (SCAN_SEG_LEN - 1) == 0
    return pl.pallas_call(
        _s5_kernel,
        grid=(N_CLUSTERS, nb // rt),
        in_specs=[
            pl.BlockSpec((1, rt, FLAT), lambda k, r: (k, r, 0)),
            pl.BlockSpec((None, 1, 3, CLUSTER_STATE), lambda k, r: (layer, k, 0, 0)),
            pl.BlockSpec((None, 1, 2, SSM_GROUP, CLUSTER_STATE), lambda k, r: (layer, k, 0, 0, 0)),
            pl.BlockSpec((None, 1, 2, SSM_GROUP, CLUSTER_STATE), lambda k, r: (layer, k, 0, 0, 0)),
            pl.BlockSpec((None, 1, 1, FLAT), lambda k, r: (layer, k, 0, 0)),
        ],
        out_specs=pl.BlockSpec((1, rt, FLAT), lambda k, r: (k, r, 0)),
        out_shape=jax.ShapeDtypeStruct((N_CLUSTERS, nb, FLAT), F32),
        scratch_shapes=[pltpu.VMEM((GROUP_PAIRS, 2 * GROUP_FLAT, 2 * LANES), BF16),
                        pltpu.VMEM((GROUP_PAIRS, 2 * GROUP_FLAT, 2 * LANES), BF16),
                        pltpu.VMEM((GROUP_PAIRS, 2 * GROUP_FLAT, 2 * LANES), BF16),
                        pltpu.VMEM((GROUPS_PER_CLUSTER, GROUP_FLAT, GROUP_FLAT), BF16),
                        pltpu.VMEM((2, 2 * CLUSTER_STATE), F32),
                        pltpu.VMEM((SCAN_TILES, SCAN_SEGS * SCAN_SEG_PITCH, LANES), F32)],
        compiler_params=pltpu.CompilerParams(
            dimension_semantics=("arbitrary", "arbitrary"), vmem_limit_bytes=VMEM_S5),
        name="s5",
    )(u_flat, par, bt, ct, dskip)


def _s5_params(A_re, A_im, log_dt, B_re, B_im, C_re, C_im, D_skip):
    depth = A_re.shape[0]
    k, gpc, n, c = N_CLUSTERS, GROUPS_PER_CLUSTER, SSM_STATE, SSM_GROUP
    rows = lambda a: a.reshape(depth, k, 1, CLUSTER_STATE)
    ldt = jnp.broadcast_to(log_dt[..., None], A_re.shape)
    par = jnp.concatenate([rows(A_re), rows(A_im), rows(ldt)], axis=2)
    b = jnp.stack([B_re, B_im], axis=1).reshape(depth, 2, k, gpc, n, c)
    bt = b.transpose(0, 2, 1, 5, 3, 4).reshape(depth, k, 2, c, CLUSTER_STATE)
    cc = jnp.stack([C_re, C_im], axis=1).reshape(depth, 2, k, gpc, c, n)
    ct = cc.transpose(0, 2, 1, 4, 3, 5).reshape(depth, k, 2, c, CLUSTER_STATE)
    dskip = jnp.broadcast_to(D_skip.reshape(depth, k, gpc, 1, c), (depth, k, gpc, SSM_BLOCK, c))
    return par, bt, ct, dskip.reshape(depth, k, 1, FLAT)


def _mix_out_kernel(final, rider_kinds, *refs):
    n_r = len(rider_kinds)
    first = refs[0:4]
    ahead = refs[4:8]
    vec_ref, wglu_ref, wout_ref, wg_ref, wu_ref, wd_ref = refs[8:14]
    rider_src = refs[14:14 + n_r]
    o_ref = refs[14 + n_r]
    rider_dst = refs[15 + n_r:15 + 2 * n_r]
    gs_ref, x1_ref, h_ref = refs[15 + 2 * n_r:]
    _cast_riders(rider_kinds, rider_src, rider_dst)

    bglu = vec_ref[VEC_GLU:VEC_GLU + 1, :D_SSM]
    gffn = vec_ref[VEC_GFFN:VEC_GFFN + 1, :]
    tm = o_ref.shape[0]
    half_block = SSM_BLOCK // 2

    def mix_tail(x_ref, g_ref, yb_ref, yc_ref):
        for k in range(N_CLUSTERS):
            for hf in range(2):
                groups = [g_ref[k, :, g * GROUP_FLAT + hf * LANES:g * GROUP_FLAT + (hf + 1) * LANES]
                          for g in range(GROUPS_PER_CLUSTER)]
                for i, tile in enumerate(_chunk_transpose(groups)):
                    rows = pl.ds(hf * half_block + i, tm // SSM_BLOCK, stride=SSM_BLOCK)
                    gs_ref[k, rows, :] = tile
        g = jnp.concatenate([gs_ref[k] for k in range(N_CLUSTERS)], axis=1)
        ya = g * jax.nn.sigmoid(_dot(g.astype(BF16), wglu_ref[...]) + bglu)
        ymix = jnp.concatenate([ya.astype(BF16), yb_ref[...], yc_ref[...]], axis=1)
        x1 = x_ref[...] + _dot(ymix, wout_ref[...])
        x1_ref[...] = x1
        h_ref[...] = _rms(x1, gffn).astype(BF16)

    @pl.when(pl.program_id(0) == 0)
    def _():
        mix_tail(*first)

    acc = x1_ref[...]
    h = h_ref[...]
    d_ff = wg_ref.shape[1]
    for n, c0 in enumerate(range(0, d_ff, FF_CHUNK)):
        c1 = min(c0 + FF_CHUNK, d_ff)
        gate = _dot(h, wg_ref[:, c0:c1])
        up = _dot(h, wu_ref[:, c0:c1])
        act = (gate * jax.nn.sigmoid(gate) * up).astype(BF16)
        acc = acc + _dot(act, wd_ref[c0:c1, :])
        if n == 0:
            mix_tail(*ahead)
    if final:
        acc = _rms(acc, vec_ref[VEC_GFINAL:VEC_GFINAL + 1, :])
    o_ref[...] = acc


def _mix_out(layer, x2d, g3, yb, yc, vecs, wglu, wout, wg, wu, wd, final, riders=()):
    m = x2d.shape[0]
    tm = TOKEN_TILE
    n_steps = m // tm
    d_ff = wg.shape[-1]
    resident = lambda shape: pl.BlockSpec(shape, lambda i: (0, 0), pipeline_mode=pl.Buffered(1))

    def tile0(shape):
        return pl.BlockSpec(shape, lambda i: (0,) * len(shape), pipeline_mode=pl.Buffered(1))

    def ahead(shape, axis=0):
        nxt = lambda i: jnp.minimum(i + 1, n_steps - 1)
        return pl.BlockSpec(shape, lambda i: tuple(nxt(i) if d == axis else 0
                                                   for d in range(len(shape))))

    r_specs = [_rider_specs(n_steps, lyr, w) for _, w, lyr in riders]
    return pl.pallas_call(
        functools.partial(_mix_out_kernel, final, tuple(k for k, _, _ in riders)),
        grid=(n_steps,),
        in_specs=[
            tile0((tm, D_MODEL)), tile0((N_CLUSTERS, tm // SSM_BLOCK, FLAT)),
            tile0((tm, D_POOL)), tile0((tm, D_SGU)),
            ahead((tm, D_MODEL)), ahead((N_CLUSTERS, tm // SSM_BLOCK, FLAT), axis=1),
            ahead((tm, D_POOL)), ahead((tm, D_SGU)),
            _layer_spec(layer, (VEC_ROWS, D_MODEL)),
            _layer_spec(layer, (D_SSM, D_SSM)),
            resident((D_MODEL, D_MODEL)),
            resident((D_MODEL, d_ff)),
            resident((D_MODEL, d_ff)),
            resident((d_ff, D_MODEL)),
        ] + [s for s, _, _ in r_specs],
        out_specs=[pl.BlockSpec((tm, D_MODEL), lambda i: (i, 0))] + [d for _, d, _ in r_specs],
        out_shape=[jax.ShapeDtypeStruct((m, D_MODEL), F32)] + [o for _, _, o in r_specs],
        scratch_shapes=[pltpu.VMEM((N_CLUSTERS, tm, LANES), F32),
                        pltpu.VMEM((tm, D_MODEL), F32),
                        pltpu.VMEM((tm, D_MODEL), BF16)],
        compiler_params=pltpu.CompilerParams(
            dimension_semantics=("arbitrary",), vmem_limit_bytes=VMEM_MIX_OUT),
        name="mix_out",
    )(x2d, g3, yb, yc, x2d, g3, yb, yc, vecs, wglu, wout, wg, wu, wd, *[w for _, w, _ in riders])


def kernel(x, g_mix, w_in, A_re, A_im, log_dt, B_re, B_im, C_re, C_im, D_skip, w_glu, b_glu,
           w_pool, pool_scale, sgu_ln_g, sgu_ln_b, w_spatial, b_spatial, w_out, g_ffn,
           w_gate, w_up, w_down, g_final):
    bsz, seq, d = x.shape
    depth = w_in.shape[0]
    m = bsz * seq
    assert d == D_MODEL and seq % MIX_IN_TILE == 0 and MIX_IN_TILE % (2 * CHUNK) == 0
    assert m % TOKEN_TILE == 0
    assert (seq // SSM_BLOCK) % S5_ROW_TILE == 0
    x2d = x.reshape(m, D_MODEL)
    tril = jnp.tril(jnp.ones((CHUNK, CHUNK), dtype=bool))
    eye = jnp.eye(len(POOL_WINDOWS), dtype=F32)
    wpool_bd = jnp.einsum('dgij,gh->dgihj', w_pool, eye).reshape(depth, D_POOL, D_POOL).astype(BF16)
    ws = jnp.where(tril, w_spatial, 0.0).astype(BF16)
    ws = jnp.concatenate([ws[:, 0::2], ws[:, 1::2]], axis=-1)
    bsp = jnp.repeat(jnp.swapaxes(b_spatial, 1, 2), SGU_HEAD_DIM, axis=2)
    par, bt, ct, dskip = _s5_params(A_re, A_im, log_dt, B_re, B_im, C_re, C_im, D_skip)
    vec_rows = [None] * VEC_ROWS
    vec_rows[VEC_GMIX] = g_mix
    vec_rows[VEC_GFFN] = g_ffn
    vec_rows[VEC_SGU] = jnp.concatenate([pool_scale, sgu_ln_g, sgu_ln_b], axis=-1)
    vec_rows[VEC_GLU] = jnp.pad(b_glu, ((0, 0), (0, D_MODEL - D_SSM)))
    vec_rows[VEC_GFINAL] = jnp.broadcast_to(g_final, (depth, D_MODEL))
    zero_row = jnp.zeros((depth, D_MODEL), F32)
    vecs = jnp.stack([zero_row if r is None else r for r in vec_rows], axis=1)
    wglu = w_glu.astype(BF16)
    b_end, u_end = D_SSM + D_POOL, D_SSM + D_POOL + D_SGU
    win = jnp.concatenate([w_in[0, :, u_end:], w_in[0, :, b_end:u_end], w_in[0, :, :b_end]],
                          axis=-1).astype(BF16)
    ffn_stacks = (w_out, w_gate, w_up, w_down)
    ffn = None
    for l in range(depth):
        riders = tuple(("plain", w, 0) for w in ffn_stacks) if l == 0 else ()
        u_flat, yb, yc, *cast = _mix_in(l, x2d, vecs, win, wpool_bd, ws, bsp, seq, riders)
        if l == 0:
            ffn = cast
        g3 = _s5(l, u_flat, par, bt, ct, dskip, seq // SSM_BLOCK)
        last = l == depth - 1
        riders = () if last else (("w_in", w_in, l + 1),) + tuple(
            ("plain", w, l + 1) for w in ffn_stacks)
        x2d, *cast = _mix_out(l, x2d, g3, yb, yc, vecs, wglu, *ffn, final=last, riders=riders)
        if not last:
            win, ffn = cast[0], cast[1:]
    return x2d.reshape(bsz, seq, D_MODEL)
```

```python
import functools
import math

import jax
import jax.numpy as jnp
from jax import lax
from jax.experimental import pallas as pl
from jax.experimental.pallas import tpu as pltpu

D_MODEL = 1024
D_SSM = 384
SSM_GROUP = 16
N_SSM_GROUPS = D_SSM // SSM_GROUP
SSM_STATE = 64
POOL_WINDOWS = (2, 4, 8, 16)
POOL_GROUP = 64
D_POOL = len(POOL_WINDOWS) * POOL_GROUP
MAX_WINDOW = max(POOL_WINDOWS)
SGU_HEADS = 6
SGU_HEAD_DIM = 64
D_SGU = SGU_HEADS * SGU_HEAD_DIM
CHUNK = 128
D_IN = D_SSM + D_POOL + 2 * D_SGU
EPS = 1e-6

LANES = 128
SSM_BLOCK = 16
N_CLUSTERS = D_SSM // LANES
GROUPS_PER_CLUSTER = LANES // SSM_GROUP
CLUSTER_STATE = GROUPS_PER_CLUSTER * SSM_STATE
FLAT = SSM_BLOCK * LANES
GROUP_FLAT = SSM_BLOCK * SSM_GROUP
GROUP_PAIRS = GROUPS_PER_CLUSTER // 2

MXU_WIDTH = 256
BF16_SUBLANES = 16
VEC_ROWS = 8
VEC_GMIX, VEC_GFFN, VEC_SGU, VEC_GLU, VEC_GFINAL = 0, 1, 2, 3, 4
TOKEN_TILE = 512
MIX_IN_TILE = 1024
S5_ROW_TILE = 512
SUBLANES = 8
SCAN_SEGS = SUBLANES
SCAN_SEG_LEN = S5_ROW_TILE // SCAN_SEGS
SCAN_SEG_PITCH = SCAN_SEG_LEN + SUBLANES
SCAN_TILES = 2 * CLUSTER_STATE // LANES
FF_CHUNK = 3 * MXU_WIDTH
V7X_VMEM_BYTES = 64 * 1024 * 1024
VMEM_MIX_IN = VMEM_S5 = VMEM_MIX_OUT = V7X_VMEM_BYTES

F32 = jnp.float32
BF16 = jnp.bfloat16
NT_DIMS = (((1,), (1,)), ((), ()))


def _gelu(x):
    c = math.sqrt(2.0 / math.pi)
    return x * (0.5 + 0.5 * jnp.tanh(x * (c + (c * 0.044715) * (x * x))))


def _rms(x, g):
    ms = jnp.mean(x * x, axis=-1, keepdims=True)
    return x * lax.rsqrt(ms + EPS) * g


def _dot(a, b):
    return jnp.dot(a, b, preferred_element_type=F32)


def _dot_nt(a, b):
    return lax.dot_general(a, b, NT_DIMS, preferred_element_type=F32)


def _chunk_transpose(tiles):
    n = len(tiles)
    chunk = LANES // n
    v = list(tiles)
    cidx = lax.broadcasted_iota(jnp.int32, v[0].shape, 1) // chunk
    k = 1
    while k < n:
        keep = (cidx & k) == 0
        nv = list(v)
        for a in range(n):
            if a & k:
                continue
            lo, hi = v[a], v[a + k]
            nv[a] = jnp.where(keep, lo, pltpu.roll(hi, k * chunk, 1))
            nv[a + k] = jnp.where(keep, pltpu.roll(lo, LANES - k * chunk, 1), hi)
        v = nv
        k *= 2
    return v


def _cast_riders(kinds, src_refs, dst_refs):
    for kind, src, dst in zip(kinds, src_refs, dst_refs):
        if kind == "w_in":
            b_end, u_end = D_SSM + D_POOL, D_SSM + D_POOL + D_SGU
            dst[:, :D_SGU] = src[:, u_end:].astype(BF16)
            dst[:, D_SGU:2 * D_SGU] = src[:, b_end:u_end].astype(BF16)
            dst[:, 2 * D_SGU:] = src[:, :b_end].astype(BF16)
        else:
            dst[...] = src[...].astype(BF16)


def _rider_specs(n_steps, layer, w):
    rows, cols = w.shape[1:]
    share = 1
    while (rows * share) % n_steps or (rows * share // n_steps) % BF16_SUBLANES:
        share *= 2
    slab = rows * share // n_steps
    which = lambda i: jnp.minimum(i, n_steps - 1) // share
    src = pl.BlockSpec((None, slab, cols), lambda i: (layer, which(i), 0))
    dst = pl.BlockSpec((slab, cols), lambda i: (which(i), 0))
    return src, dst, jax.ShapeDtypeStruct((rows, cols), BF16)


def _mix_in_kernel(tiles_per_seq, rider_kinds, *refs):
    n_r = len(rider_kinds)
    x_ref, vec_ref, win_ref, wpool_ref, ws_ref, bsp_ref = refs[:6]
    rider_src = refs[6:6 + n_r]
    za_ref, yb_ref, yc_ref = refs[6 + n_r:9 + n_r]
    rider_dst = refs[9 + n_r:9 + 2 * n_r]
    halo_ref, zs_ref = refs[9 + 2 * n_r:]
    _cast_riders(rider_kinds, rider_src, rider_dst)

    tm = x_ref.shape[0]
    i = pl.program_id(0)
    seq_tile = i % tiles_per_seq
    gmix = vec_ref[VEC_GMIX:VEC_GMIX + 1, :]
    pscale = vec_ref[VEC_SGU:VEC_SGU + 1, :D_POOL]
    lng = vec_ref[VEC_SGU:VEC_SGU + 1, D_POOL:D_POOL + D_SGU]
    lnb = vec_ref[VEC_SGU:VEC_SGU + 1, D_POOL + D_SGU:]

    h = _rms(x_ref[...], gmix).astype(BF16)
    z_sgu = _dot(h, win_ref[:, :2 * D_SGU])
    z_rest = _dot(h, win_ref[:, 2 * D_SGU:])
    zv = z_sgu[:, :D_SGU]
    zu = z_sgu[:, D_SGU:]
    zb = z_rest[:, D_SSM:]

    for k in range(N_CLUSTERS):
        zs_ref[k] = z_rest[:, k * LANES:(k + 1) * LANES]
    half_block = SSM_BLOCK // 2
    for k in range(N_CLUSTERS):
        for hf in range(2):
            slots = [zs_ref[k, pl.ds(hf * half_block + i, tm // SSM_BLOCK, stride=SSM_BLOCK), :]
                     for i in range(half_block)]
            for g, tile in enumerate(_chunk_transpose(slots)):
                col = g * GROUP_FLAT + hf * LANES
                za_ref[k, :, col:col + LANES] = tile.astype(BF16)

    halo = jnp.where(seq_tile == 0, 0.0, halo_ref[...])
    halo_ref[...] = zb[tm - MAX_WINDOW:, :]
    ext = jnp.concatenate([halo, zb], axis=0)
    lane = lax.broadcasted_iota(jnp.int32, (tm, LANES), 1)
    low = lane < POOL_GROUP
    low_h = lax.broadcasted_iota(jnp.int32, (MAX_WINDOW, LANES), 1) < POOL_GROUP
    low_r = lax.broadcasted_iota(jnp.int32, (1, LANES), 1) < POOL_GROUP
    pos1 = (seq_tile * tm + 1
            + lax.broadcasted_iota(jnp.int32, (MAX_WINDOW, LANES), 0)).astype(F32)

    def window_mean(total, w_low, w_high):
        cnt = jnp.where(low_h, jnp.minimum(pos1, w_low), jnp.minimum(pos1, w_high))
        inv_w = jnp.where(low_r, 1.0 / w_low, 1.0 / w_high)
        return jnp.concatenate([total[:MAX_WINDOW] / cnt, total[MAX_WINDOW:] * inv_w], axis=0)

    e0 = ext[:, :LANES]
    s2 = e0 + pltpu.roll(e0, 1, 0)
    s4 = s2 + pltpu.roll(s2, 2, 0)
    sum0 = jnp.where(low, s2[MAX_WINDOW:], s4[MAX_WINDOW:])
    e1 = ext[:, LANES:]
    t2 = e1 + pltpu.roll(e1, 1, 0)
    t4 = t2 + pltpu.roll(t2, 2, 0)
    t8 = t4 + pltpu.roll(t4, 4, 0)
    t16 = t8 + pltpu.roll(t8, 8, 0)
    sum1 = jnp.where(low, t8[MAX_WINDOW:], t16[MAX_WINDOW:])
    pooled = jnp.concatenate([window_mean(sum0, 2.0, 4.0), window_mean(sum1, 8.0, 16.0)],
                             axis=1) - zb
    yb = _dot(pooled.astype(BF16), wpool_ref[...]) * pscale
    yb_ref[...] = yb.astype(BF16)

    u = _gelu(zu)
    v = _gelu(zv)
    mu = jnp.mean(v, axis=-1, keepdims=True)
    vc = v - mu
    var = jnp.mean(vc * vc, axis=-1, keepdims=True)
    vn = (vc * lax.rsqrt(var + EPS) * lng + lnb).astype(BF16)
    lane_c = lax.broadcasted_iota(jnp.int32, (CHUNK, LANES), 1)
    low_c = lane_c < SGU_HEAD_DIM
    zero = jnp.zeros((), BF16)
    for c in range(0, tm // CHUNK, 2):
        rows_a = slice(c * CHUNK, (c + 1) * CHUNK)
        rows_b = slice((c + 1) * CHUNK, (c + 2) * CHUNK)
        parts_a, parts_b = [], []
        for p in range(SGU_HEADS // 2):
            va = vn[rows_a, p * LANES:(p + 1) * LANES]
            vb = vn[rows_b, p * LANES:(p + 1) * LANES]
            top = jnp.concatenate([jnp.where(low_c, va, zero), jnp.where(low_c, vb, zero)], axis=1)
            bot = jnp.concatenate([jnp.where(low_c, zero, va), jnp.where(low_c, zero, vb)], axis=1)
            out = _dot(ws_ref[p], jnp.concatenate([top, bot], axis=0))
            parts_a.append(out[:, :LANES])
            parts_b.append(out[:, LANES:])
        for rows, parts in ((rows_a, parts_a), (rows_b, parts_b)):
            mixed = jnp.concatenate(parts, axis=1) + bsp_ref[...]
            yc_ref[rows, :] = (u[rows, :] * mixed).astype(BF16)


def _layer_spec(layer, shape):
    zeros = (0,) * len(shape)
    return pl.BlockSpec((None,) + tuple(shape), lambda *_: (layer,) + zeros)


def _mix_in(layer, x2d, vecs, win, wpool_bd, ws, bsp, seq_len, riders=()):
    m = x2d.shape[0]
    tm = MIX_IN_TILE
    n_steps = m // tm
    spec = functools.partial(_layer_spec, layer)
    r_specs = [_rider_specs(n_steps, lyr, w) for _, w, lyr in riders]
    return pl.pallas_call(
        functools.partial(_mix_in_kernel, seq_len // tm, tuple(k for k, _, _ in riders)),
        grid=(n_steps,),
        in_specs=[
            pl.BlockSpec((tm, D_MODEL), lambda i: (i, 0)),
            spec((VEC_ROWS, D_MODEL)),
            pl.BlockSpec((D_MODEL, D_IN), lambda i: (0, 0)),
            spec((D_POOL, D_POOL)),
            spec((SGU_HEADS // 2, CHUNK, 2 * CHUNK)),
            spec((CHUNK, D_SGU)),
        ] + [s for s, _, _ in r_specs],
        out_specs=[
            pl.BlockSpec((N_CLUSTERS, tm // SSM_BLOCK, FLAT), lambda i: (0, i, 0)),
            pl.BlockSpec((tm, D_POOL), lambda i: (i, 0)),
            pl.BlockSpec((tm, D_SGU), lambda i: (i, 0)),
        ] + [d for _, d, _ in r_specs],
        out_shape=[
            jax.ShapeDtypeStruct((N_CLUSTERS, m // SSM_BLOCK, FLAT), BF16),
            jax.ShapeDtypeStruct((m, D_POOL), BF16),
            jax.ShapeDtypeStruct((m, D_SGU), BF16),
        ] + [o for _, _, o in r_specs],
        scratch_shapes=[pltpu.VMEM((MAX_WINDOW, D_POOL), F32),
                        pltpu.VMEM((N_CLUSTERS, tm, LANES), F32)],
        compiler_params=pltpu.CompilerParams(
            dimension_semantics=("arbitrary",), vmem_limit_bytes=VMEM_MIX_IN),
        name="mix_in",
    )(x2d, vecs, win, wpool_bd, ws, bsp, *[w for _, w, _ in riders])


def _s5_build_operators(par_ref, bt_ref, ct_ref, dskip_ref, wend_ref, wrd_ref, lag0_ref, lagk_ref,
                        pw_ref):
    a_re = par_ref[0, 0:1, :]
    a_im = par_ref[0, 1:2, :]
    dt = jnp.exp(par_ref[0, 2:3, :])
    l_re = a_re * dt
    l_im = a_im * dt
    mag = jnp.exp(l_re)
    ar = mag * jnp.cos(l_im)
    ai = mag * jnp.sin(l_im)
    den = a_re * a_re + a_im * a_im
    f_re = ((ar - 1.0) * a_re + ai * a_im) / den
    f_im = (ai * a_re - (ar - 1.0) * a_im) / den
    bt_re, bt_im = bt_ref[0, 0], bt_ref[0, 1]
    bb_re = f_re * bt_re - f_im * bt_im
    bb_im = f_re * bt_im + f_im * bt_re
    ct_re, ct_im = ct_ref[0, 0], ct_ref[0, 1]

    lane_q = lax.broadcasted_iota(jnp.int32, (SSM_GROUP, LANES), 1) // SSM_STATE

    def power(l):
        m_l = jnp.exp(l * l_re)
        return m_l * jnp.cos(l * l_im), m_l * jnp.sin(l * l_im)

    def store_pairs(ref, slot, v_re, v_im):
        for p in range(GROUP_PAIRS):
            t_re = v_re[:, p * LANES:(p + 1) * LANES]
            t_im = v_im[:, p * LANES:(p + 1) * LANES]
            for q in range(2):
                rows = slice(q * GROUP_FLAT + slot * SSM_GROUP, q * GROUP_FLAT + (slot + 1) * SSM_GROUP)
                ref[p, rows, :LANES] = jnp.where(lane_q == q, t_re, 0.0).astype(BF16)
                ref[p, rows, LANES:] = jnp.where(lane_q == q, t_im, 0.0).astype(BF16)

    for s in range(SSM_BLOCK):
        p_re, p_im = power(float(SSM_BLOCK - 1 - s))
        store_pairs(wend_ref, s, p_re * bb_re - p_im * bb_im, p_re * bb_im + p_im * bb_re)
        q_re, q_im = power(float(s + 1))
        store_pairs(wrd_ref, s, q_re * ct_re - q_im * ct_im, -(q_re * ct_im + q_im * ct_re))
        r_re, r_im = power(float(s))
        store_pairs(lag0_ref, s, r_re * ct_re - r_im * ct_im, -(r_re * ct_im + r_im * ct_re))

    last = (SSM_BLOCK - 1) * SSM_GROUP
    col = lax.broadcasted_iota(jnp.int32, (SSM_GROUP, GROUP_FLAT), 1)
    row = lax.broadcasted_iota(jnp.int32, (SSM_GROUP, GROUP_FLAT), 0)
    for p in range(GROUP_PAIRS):
        bbar = jnp.concatenate([wend_ref[p, q * GROUP_FLAT + last:q * GROUP_FLAT + last + SSM_GROUP, :]
                                for q in range(2)], axis=0)
        lags = _dot_nt(bbar, lag0_ref[p])
        for q in range(2):
            g = 2 * p + q
            k_row = lags[q * SSM_GROUP:(q + 1) * SSM_GROUP, q * GROUP_FLAT:(q + 1) * GROUP_FLAT]
            d_g = dskip_ref[0, :, g * GROUP_FLAT:(g + 1) * GROUP_FLAT]
            k_row = k_row + jnp.where(col == row, d_g, 0.0)
            for s in range(SSM_BLOCK):
                shifted = pltpu.roll(k_row, s * SSM_GROUP, 1) if s else k_row
                lagk_ref[2 * p + q, s * SSM_GROUP:(s + 1) * SSM_GROUP, :] = jnp.where(
                    col >= s * SSM_GROUP, shifted, 0.0).astype(BF16)

    p_re, p_im = power(float(SSM_BLOCK))
    pw_ref[0:1, :] = jnp.concatenate([p_re, p_im], axis=1)
    for _ in range(SCAN_SEG_LEN.bit_length() - 1):
        p_re, p_im = p_re * p_re - p_im * p_im, 2.0 * (p_re * p_im)
    pw_ref[1:2, :] = jnp.concatenate([p_re, p_im], axis=1)


def _cmul_add(p_re, p_im, h_re, h_im, e_re, e_im):
    n_re = [pr * hr - pi * hi + er for pr, pi, hr, hi, er in zip(p_re, p_im, h_re, h_im, e_re)]
    n_im = [pr * hi + pi * hr + ei for pr, pi, hr, hi, ei in zip(p_re, p_im, h_re, h_im, e_im)]
    return n_re, n_im


def _s5_block_recurrence(st_ref, pw_ref, fillers):
    fillers = list(fillers)
    total_cost = sum(c for c, _ in fillers)
    issued_cost = [0.0]
    ticks = [0]

    def tick():
        while fillers and issued_cost[0] * 2 * SCAN_SEG_LEN <= ticks[0] * total_cost:
            cost, thunk = fillers.pop(0)
            issued_cost[0] += cost
            thunk()
        ticks[0] += 1

    half = SCAN_TILES // 2

    def bcast(row):
        return [jnp.broadcast_to(row[:, q * LANES:(q + 1) * LANES], (SCAN_SEGS, LANES))
                for q in range(half)]

    p_re, p_im = bcast(pw_ref[0:1, :CLUSTER_STATE]), bcast(pw_ref[0:1, CLUSTER_STATE:])
    seg_rows = lambda i: pl.ds(i, SCAN_SEGS, stride=SCAN_SEG_PITCH)

    def gather(i):
        e = [st_ref[q, seg_rows(i), :] for q in range(SCAN_TILES)]
        return e[:half], e[half:]

    zeros = [jnp.zeros((SCAN_SEGS, LANES), F32)] * half

    def end_step(i, h):
        e_re, e_im = gather(i)
        return _cmul_add(p_re, p_im, h[0], h[1], e_re, e_im)

    h = (zeros, zeros)
    for i in range(SCAN_SEG_LEN):
        tick()
        h = end_step(i, h)
    l_re, l_im = h

    s_re, s_im = bcast(pw_ref[1:2, :CLUSTER_STATE]), bcast(pw_ref[1:2, CLUSTER_STATE:])
    first = lax.broadcasted_iota(jnp.int32, (SCAN_SEGS, LANES), 0) == 0
    shift = lambda v: jnp.where(first, 0.0, pltpu.roll(v, 1, 0))
    c_re, c_im = zeros, zeros
    for _ in range(SCAN_SEGS - 1):
        n_re, n_im = _cmul_add(s_re, s_im, c_re, c_im, l_re, l_im)
        c_re, c_im = [shift(v) for v in n_re], [shift(v) for v in n_im]

    def scan_step(i, h):
        e_re, e_im = gather(i)
        for q in range(half):
            st_ref[q, seg_rows(i), :] = h[0][q]
            st_ref[half + q, seg_rows(i), :] = h[1][q]
        return _cmul_add(p_re, p_im, h[0], h[1], e_re, e_im)

    h = (c_re, c_im)
    for i in range(SCAN_SEG_LEN):
        tick()
        h = scan_step(i, h)
    while fillers:
        fillers.pop(0)[1]()


def _s5_kernel(u_ref, par_ref, bt_ref, ct_ref, dskip_ref, g_ref,
               wend_ref, wrd_ref, lag0_ref, lagk_ref, pw_ref, st_ref):
    r = pl.program_id(1)

    @pl.when(r == 0)
    def _():
        _s5_build_operators(par_ref, bt_ref, ct_ref, dskip_ref, wend_ref, wrd_ref, lag0_ref,
                            lagk_ref, pw_ref)

    u = u_ref[0]
    pair_cols = lambda p: slice(2 * p * GROUP_FLAT, 2 * (p + 1) * GROUP_FLAT)
    seg_rows = lambda j: slice(j * SCAN_SEG_PITCH, j * SCAN_SEG_PITCH + SCAN_SEG_LEN)

    for p in range(GROUP_PAIRS):
        e = _dot(u[:, pair_cols(p)], wend_ref[p])
        for tile, cols in ((p, slice(0, LANES)), (GROUP_PAIRS + p, slice(LANES, 2 * LANES))):
            for j in range(SCAN_SEGS):
                st_ref[tile, seg_rows(j), :] = e[j * SCAN_SEG_LEN:(j + 1) * SCAN_SEG_LEN, cols]

    ys = [None] * GROUPS_PER_CLUSTER

    def lag_product(g):
        ys[g] = _dot(u[:, g * GROUP_FLAT:(g + 1) * GROUP_FLAT], lagk_ref[g])

    _s5_block_recurrence(st_ref, pw_ref,
                         [(1, functools.partial(lag_product, g)) for g in range(len(ys))])
    y = jnp.concatenate(ys, axis=1)

    outs = []
    for p in range(GROUP_PAIRS):
        hprev = jnp.concatenate(
            [jnp.concatenate([st_ref[p, seg_rows(j), :], st_ref[GROUP_PAIRS + p, seg_rows(j), :]],
                             axis=1) for j in range(SCAN_SEGS)], axis=0)
        outs.append(_dot_nt(hprev.astype(BF16), wrd_ref[p]))
    g_ref[0] = _gelu(y + jnp.concatenate(outs, axis=1))


def _s5(layer, u_flat, par, bt, ct, dskip, blocks_per_seq):
    nb = u_flat.shape[1]
    rt = S5_ROW_TILE
    assert blocks_per_seq == rt and SCAN_SEG_LEN & (SCAN_SEG_LEN - 1) == 0
    return pl.pallas_call(
        _s5_kernel,
        grid=(N_CLUSTERS, nb // rt),
        in_specs=[
            pl.BlockSpec((1, rt, FLAT), lambda k, r: (k, r, 0)),
            pl.BlockSpec((None, 1, 3, CLUSTER_STATE), lambda k, r: (layer, k, 0, 0)),
            pl.BlockSpec((None, 1, 2, SSM_GROUP, CLUSTER_STATE), lambda k, r: (layer, k, 0, 0, 0)),
            pl.BlockSpec((None, 1, 2, SSM_GROUP, CLUSTER_STATE), lambda k, r: (layer, k, 0, 0, 0)),
            pl.BlockSpec((None, 1, 1, FLAT), lambda k, r: (layer, k, 0, 0)),
        ],
        out_specs=pl.BlockSpec((1, rt, FLAT), lambda k, r: (k, r, 0)),
        out_shape=jax.ShapeDtypeStruct((N_CLUSTERS, nb, FLAT), F32),
        scratch_shapes=[pltpu.VMEM((GROUP_PAIRS, 2 * GROUP_FLAT, 2 * LANES), BF16),
                        pltpu.VMEM((GROUP_PAIRS, 2 * GROUP_FLAT, 2 * LANES), BF16),
                        pltpu.VMEM((GROUP_PAIRS, 2 * GROUP_FLAT, 2 * LANES), BF16),
                        pltpu.VMEM((GROUPS_PER_CLUSTER, GROUP_FLAT, GROUP_FLAT), BF16),
                        pltpu.VMEM((2, 2 * CLUSTER_STATE), F32),
                        pltpu.VMEM((SCAN_TILES, SCAN_SEGS * SCAN_SEG_PITCH, LANES), F32)],
        compiler_params=pltpu.CompilerParams(
            dimension_semantics=("arbitrary", "arbitrary"), vmem_limit_bytes=VMEM_S5),
        name="s5",
    )(u_flat, par, bt, ct, dskip)


def _s5_params(A_re, A_im, log_dt, B_re, B_im, C_re, C_im, D_skip):
    depth = A_re.shape[0]
    k, gpc, n, c = N_CLUSTERS, GROUPS_PER_CLUSTER, SSM_STATE, SSM_GROUP
    rows = lambda a: a.reshape(depth, k, 1, CLUSTER_STATE)
    ldt = jnp.broadcast_to(log_dt[..., None], A_re.shape)
    par = jnp.concatenate([rows(A_re), rows(A_im), rows(ldt)], axis=2)
    b = jnp.stack([B_re, B_im], axis=1).reshape(depth, 2, k, gpc, n, c)
    bt = b.transpose(0, 2, 1, 5, 3, 4).reshape(depth, k, 2, c, CLUSTER_STATE)
    cc = jnp.stack([C_re, C_im], axis=1).reshape(depth, 2, k, gpc, c, n)
    ct = cc.transpose(0, 2, 1, 4, 3, 5).reshape(depth, k, 2, c, CLUSTER_STATE)
    dskip = jnp.broadcast_to(D_skip.reshape(depth, k, gpc, 1, c), (depth, k, gpc, SSM_BLOCK, c))
    return par, bt, ct, dskip.reshape(depth, k, 1, FLAT)


def _mix_out_kernel(final, rider_kinds, *refs):
    n_r = len(rider_kinds)
    first = refs[0:4]
    ahead = refs[4:8]
    vec_ref, wglu_ref, wout_ref, wg_ref, wu_ref, wd_ref = refs[8:14]
    rider_src = refs[14:14 + n_r]
    o_ref = refs[14 + n_r]
    rider_dst = refs[15 + n_r:15 + 2 * n_r]
    gs_ref, x1_ref, h_ref = refs[15 + 2 * n_r:]
    _cast_riders(rider_kinds, rider_src, rider_dst)

    bglu = vec_ref[VEC_GLU:VEC_GLU + 1, :D_SSM]
    gffn = vec_ref[VEC_GFFN:VEC_GFFN + 1, :]
    tm = o_ref.shape[0]
    half_block = SSM_BLOCK // 2

    def mix_tail(x_ref, g_ref, yb_ref, yc_ref):
        for k in range(N_CLUSTERS):
            for hf in range(2):
                groups = [g_ref[k, :, g * GROUP_FLAT + hf * LANES:g * GROUP_FLAT + (hf + 1) * LANES]
                          for g in range(GROUPS_PER_CLUSTER)]
                for i, tile in enumerate(_chunk_transpose(groups)):
                    rows = pl.ds(hf * half_block + i, tm // SSM_BLOCK, stride=SSM_BLOCK)
                    gs_ref[k, rows, :] = tile
        g = jnp.concatenate([gs_ref[k] for k in range(N_CLUSTERS)], axis=1)
        ya = g * jax.nn.sigmoid(_dot(g.astype(BF16), wglu_ref[...]) + bglu)
        ymix = jnp.concatenate([ya.astype(BF16), yb_ref[...], yc_ref[...]], axis=1)
        x1 = x_ref[...] + _dot(ymix, wout_ref[...])
        x1_ref[...] = x1
        h_ref[...] = _rms(x1, gffn).astype(BF16)

    @pl.when(pl.program_id(0) == 0)
    def _():
        mix_tail(*first)

    acc = x1_ref[...]
    h = h_ref[...]
    d_ff = wg_ref.shape[1]
    for n, c0 in enumerate(range(0, d_ff, FF_CHUNK)):
        c1 = min(c0 + FF_CHUNK, d_ff)
        gate = _dot(h, wg_ref[:, c0:c1])
        up = _dot(h, wu_ref[:, c0:c1])
        act = (gate * jax.nn.sigmoid(gate) * up).astype(BF16)
        acc = acc + _dot(act, wd_ref[c0:c1, :])
        if n == 0:
            mix_tail(*ahead)
    if final:
        acc = _rms(acc, vec_ref[VEC_GFINAL:VEC_GFINAL + 1, :])
    o_ref[...] = acc


def _mix_out(layer, x2d, g3, yb, yc, vecs, wglu, wout, wg, wu, wd, final, riders=()):
    m = x2d.shape[0]
    tm = TOKEN_TILE
    n_steps = m // tm
    d_ff = wg.shape[-1]
    resident = lambda shape: pl.BlockSpec(shape, lambda i: (0, 0), pipeline_mode=pl.Buffered(1))

    def tile0(shape):
        return pl.BlockSpec(shape, lambda i: (0,) * len(shape), pipeline_mode=pl.Buffered(1))

    def ahead(shape, axis=0):
        nxt = lambda i: jnp.minimum(i + 1, n_steps - 1)
        return pl.BlockSpec(shape, lambda i: tuple(nxt(i) if d == axis else 0
                                                   for d in range(len(shape))))

    r_specs = [_rider_specs(n_steps, lyr, w) for _, w, lyr in riders]
    return pl.pallas_call(
        functools.partial(_mix_out_kernel, final, tuple(k for k, _, _ in riders)),
        grid=(n_steps,),
        in_specs=[
            tile0((tm, D_MODEL)), tile0((N_CLUSTERS, tm // SSM_BLOCK, FLAT)),
            tile0((tm, D_POOL)), tile0((tm, D_SGU)),
            ahead((tm, D_MODEL)), ahead((N_CLUSTERS, tm // SSM_BLOCK, FLAT), axis=1),
            ahead((tm, D_POOL)), ahead((tm, D_SGU)),
            _layer_spec(layer, (VEC_ROWS, D_MODEL)),
            _layer_spec(layer, (D_SSM, D_SSM)),
            resident((D_MODEL, D_MODEL)),
            resident((D_MODEL, d_ff)),
            resident((D_MODEL, d_ff)),
            resident((d_ff, D_MODEL)),
        ] + [s for s, _, _ in r_specs],
        out_specs=[pl.BlockSpec((tm, D_MODEL), lambda i: (i, 0))] + [d for _, d, _ in r_specs],
        out_shape=[jax.ShapeDtypeStruct((m, D_MODEL), F32)] + [o for _, _, o in r_specs],
        scratch_shapes=[pltpu.VMEM((N_CLUSTERS, tm, LANES), F32),
                        pltpu.VMEM((tm, D_MODEL), F32),
                        pltpu.VMEM((tm, D_MODEL), BF16)],
        compiler_params=pltpu.CompilerParams(
            dimension_semantics=("arbitrary",), vmem_limit_bytes=VMEM_MIX_OUT),
        name="mix_out",
    )(x2d, g3, yb, yc, x2d, g3, yb, yc, vecs, wglu, wout, wg, wu, wd, *[w for _, w, _ in riders])


def kernel(x, g_mix, w_in, A_re, A_im, log_dt, B_re, B_im, C_re, C_im, D_skip, w_glu, b_glu,
           w_pool, pool_scale, sgu_ln_g, sgu_ln_b, w_spatial, b_spatial, w_out, g_ffn,
           w_gate, w_up, w_down, g_final):
    bsz, seq, d = x.shape
    depth = w_in.shape[0]
    m = bsz * seq
    assert d == D_MODEL and seq % MIX_IN_TILE == 0 and MIX_IN_TILE % (2 * CHUNK) == 0
    assert m % TOKEN_TILE == 0
    assert (seq // SSM_BLOCK) % S5_ROW_TILE == 0
    x2d = x.reshape(m, D_MODEL)
    tril = jnp.tril(jnp.ones((CHUNK, CHUNK), dtype=bool))
    eye = jnp.eye(len(POOL_WINDOWS), dtype=F32)
    wpool_bd = jnp.einsum('dgij,gh->dgihj', w_pool, eye).reshape(depth, D_POOL, D_POOL).astype(BF16)
    ws = jnp.where(tril, w_spatial, 0.0).astype(BF16)
    ws = jnp.concatenate([ws[:, 0::2], ws[:, 1::2]], axis=-1)
    bsp = jnp.repeat(jnp.swapaxes(b_spatial, 1, 2), SGU_HEAD_DIM, axis=2)
    par, bt, ct, dskip = _s5_params(A_re, A_im, log_dt, B_re, B_im, C_re, C_im, D_skip)
    vec_rows = [None] * VEC_ROWS
    vec_rows[VEC_GMIX] = g_mix
    vec_rows[VEC_GFFN] = g_ffn
    vec_rows[VEC_SGU] = jnp.concatenate([pool_scale, sgu_ln_g, sgu_ln_b], axis=-1)
    vec_rows[VEC_GLU] = jnp.pad(b_glu, ((0, 0), (0, D_MODEL - D_SSM)))
    vec_rows[VEC_GFINAL] = jnp.broadcast_to(g_final, (depth, D_MODEL))
    zero_row = jnp.zeros((depth, D_MODEL), F32)
    vecs = jnp.stack([zero_row if r is None else r for r in vec_rows], axis=1)
    wglu = w_glu.astype(BF16)
    b_end, u_end = D_SSM + D_POOL, D_SSM + D_POOL + D_SGU
    win = jnp.concatenate([w_in[0, :, u_end:], w_in[0, :, b_end:u_end], w_in[0, :, :b_end]],
                          axis=-1).astype(BF16)
    ffn_stacks = (w_out, w_gate, w_up, w_down)
    ffn = None
    for l in range(depth):
        riders = tuple(("plain", w, 0) for w in ffn_stacks) if l == 0 else ()
        u_flat, yb, yc, *cast = _mix_in(l, x2d, vecs, win, wpool_bd, ws, bsp, seq, riders)
        if l == 0:
            ffn = cast
        g3 = _s5(l, u_flat, par, bt, ct, dskip, seq // SSM_BLOCK)
        last = l == depth - 1
        riders = () if last else (("w_in", w_in, l + 1),) + tuple(
            ("plain", w, l + 1) for w in ffn_stacks)
        x2d, *cast = _mix_out(l, x2d, g3, yb, yc, vecs, wglu, *ffn, final=last, riders=riders)
        if not last:
            win, ffn = cast[0], cast[1:]
    return x2d.reshape(bsz, seq, D_MODEL)
```

```python
import functools
import math

import jax
import jax.numpy as jnp
from jax import lax
from jax.experimental import pallas as pl
from jax.experimental.pallas import tpu as pltpu

D_MODEL = 1024
D_SSM = 384
SSM_GROUP = 16
N_SSM_GROUPS = D_SSM // SSM_GROUP
SSM_STATE = 64
POOL_WINDOWS = (2, 4, 8, 16)
POOL_GROUP = 64
D_POOL = len(POOL_WINDOWS) * POOL_GROUP
MAX_WINDOW = max(POOL_WINDOWS)
SGU_HEADS = 6
SGU_HEAD_DIM = 64
D_SGU = SGU_HEADS * SGU_HEAD_DIM
CHUNK = 128
D_IN = D_SSM + D_POOL + 2 * D_SGU
EPS = 1e-6

LANES = 128
SSM_BLOCK = 16
N_CLUSTERS = D_SSM // LANES
GROUPS_PER_CLUSTER = LANES // SSM_GROUP
CLUSTER_STATE = GROUPS_PER_CLUSTER * SSM_STATE
FLAT = SSM_BLOCK * LANES
GROUP_FLAT = SSM_BLOCK * SSM_GROUP
GROUP_PAIRS = GROUPS_PER_CLUSTER // 2

MXU_WIDTH = 256
BF16_SUBLANES = 16
VEC_ROWS = 8
VEC_GMIX, VEC_GFFN, VEC_SGU, VEC_GLU, VEC_GFINAL = 0, 1, 2, 3, 4
TOKEN_TILE = 512
MIX_IN_TILE = 1024
S5_ROW_TILE = 512
SUBLANES = 8
SCAN_SEGS = SUBLANES
SCAN_SEG_LEN = S5_ROW_TILE // SCAN_SEGS
SCAN_SEG_PITCH = SCAN_SEG_LEN + SUBLANES
SCAN_TILES = 2 * CLUSTER_STATE // LANES
BLOCK_PITCH = SSM_BLOCK + SUBLANES
FF_CHUNK = 3 * MXU_WIDTH
V7X_VMEM_BYTES = 64 * 1024 * 1024
VMEM_MIX_IN = VMEM_S5 = VMEM_MIX_OUT = V7X_VMEM_BYTES

F32 = jnp.float32
BF16 = jnp.bfloat16
NT_DIMS = (((1,), (1,)), ((), ()))


def _gelu(x):
    c = math.sqrt(2.0 / math.pi)
    return x * (0.5 + 0.5 * jnp.tanh(x * (c + (c * 0.044715) * (x * x))))


def _rms(x, g):
    ms = jnp.mean(x * x, axis=-1, keepdims=True)
    return x * lax.rsqrt(ms + EPS) * g


def _dot(a, b):
    return jnp.dot(a, b, preferred_element_type=F32)


def _dot_nt(a, b):
    return lax.dot_general(a, b, NT_DIMS, preferred_element_type=F32)


def _chunk_transpose(tiles):
    n = len(tiles)
    chunk = LANES // n
    v = list(tiles)
    cidx = lax.broadcasted_iota(jnp.int32, v[0].shape, 1) // chunk
    k = 1
    while k < n:
        keep = (cidx & k) == 0
        nv = list(v)
        for a in range(n):
            if a & k:
                continue
            lo, hi = v[a], v[a + k]
            nv[a] = jnp.where(keep, lo, pltpu.roll(hi, k * chunk, 1))
            nv[a + k] = jnp.where(keep, pltpu.roll(lo, LANES - k * chunk, 1), hi)
        v = nv
        k *= 2
    return v


def _cast_riders(kinds, src_refs, dst_refs):
    for kind, src, dst in zip(kinds, src_refs, dst_refs):
        if kind == "w_in":
            b_end, u_end = D_SSM + D_POOL, D_SSM + D_POOL + D_SGU
            dst[:, :D_SGU] = src[:, u_end:].astype(BF16)
            dst[:, D_SGU:2 * D_SGU] = src[:, b_end:u_end].astype(BF16)
            dst[:, 2 * D_SGU:] = src[:, :b_end].astype(BF16)
        else:
            dst[...] = src[...].astype(BF16)


def _rider_specs(n_steps, layer, w):
    rows, cols = w.shape[1:]
    share = 1
    while (rows * share) % n_steps or (rows * share // n_steps) % BF16_SUBLANES:
        share *= 2
    slab = rows * share // n_steps
    which = lambda i: jnp.minimum(i, n_steps - 1) // share
    src = pl.BlockSpec((None, slab, cols), lambda i: (layer, which(i), 0))
    dst = pl.BlockSpec((slab, cols), lambda i: (which(i), 0))
    return src, dst, jax.ShapeDtypeStruct((rows, cols), BF16)


def _mix_in_kernel(tiles_per_seq, rider_kinds, *refs):
    n_r = len(rider_kinds)
    x_ref, vec_ref, win_ref, wpool_ref, ws_ref, bsp_ref = refs[:6]
    rider_src = refs[6:6 + n_r]
    za_ref, yb_ref, yc_ref = refs[6 + n_r:9 + n_r]
    rider_dst = refs[9 + n_r:9 + 2 * n_r]
    halo_ref, zs_ref = refs[9 + 2 * n_r:]
    _cast_riders(rider_kinds, rider_src, rider_dst)

    tm = x_ref.shape[0]
    i = pl.program_id(0)
    seq_tile = i % tiles_per_seq
    gmix = vec_ref[VEC_GMIX:VEC_GMIX + 1, :]
    pscale = vec_ref[VEC_SGU:VEC_SGU + 1, :D_POOL]
    lng = vec_ref[VEC_SGU:VEC_SGU + 1, D_POOL:D_POOL + D_SGU]
    lnb = vec_ref[VEC_SGU:VEC_SGU + 1, D_POOL + D_SGU:]

    h = _rms(x_ref[...], gmix).astype(BF16)
    z_sgu = _dot(h, win_ref[:, :2 * D_SGU])
    z_rest = _dot(h, win_ref[:, 2 * D_SGU:])
    zv = z_sgu[:, :D_SGU]
    zu = z_sgu[:, D_SGU:]
    zb = z_rest[:, D_SSM:]

    n_blocks = tm // SSM_BLOCK
    for k in range(N_CLUSTERS):
        for b in range(n_blocks):
            zs_ref[k, b * BLOCK_PITCH:b * BLOCK_PITCH + SSM_BLOCK, :] = (
                z_rest[b * SSM_BLOCK:(b + 1) * SSM_BLOCK, k * LANES:(k + 1) * LANES])
    half_block = SSM_BLOCK // 2
    for k in range(N_CLUSTERS):
        for hf in range(2):
            slots = [zs_ref[k, pl.ds(hf * half_block + i, n_blocks, stride=BLOCK_PITCH), :]
                     for i in range(half_block)]
            for g, tile in enumerate(_chunk_transpose(slots)):
                col = g * GROUP_FLAT + hf * LANES
                za_ref[k, :, col:col + LANES] = tile.astype(BF16)

    halo = jnp.where(seq_tile == 0, 0.0, halo_ref[...])
    halo_ref[...] = zb[tm - MAX_WINDOW:, :]
    ext = jnp.concatenate([halo, zb], axis=0)
    lane = lax.broadcasted_iota(jnp.int32, (tm, LANES), 1)
    low = lane < POOL_GROUP
    low_h = lax.broadcasted_iota(jnp.int32, (MAX_WINDOW, LANES), 1) < POOL_GROUP
    low_r = lax.broadcasted_iota(jnp.int32, (1, LANES), 1) < POOL_GROUP
    pos1 = (seq_tile * tm + 1
            + lax.broadcasted_iota(jnp.int32, (MAX_WINDOW, LANES), 0)).astype(F32)

    def window_mean(total, w_low, w_high):
        cnt = jnp.where(low_h, jnp.minimum(pos1, w_low), jnp.minimum(pos1, w_high))
        inv_w = jnp.where(low_r, 1.0 / w_low, 1.0 / w_high)
        return jnp.concatenate([total[:MAX_WINDOW] / cnt, total[MAX_WINDOW:] * inv_w], axis=0)

    e0 = ext[:, :LANES]
    s2 = e0 + pltpu.roll(e0, 1, 0)
    s4 = s2 + pltpu.roll(s2, 2, 0)
    sum0 = jnp.where(low, s2[MAX_WINDOW:], s4[MAX_WINDOW:])
    e1 = ext[:, LANES:]
    t2 = e1 + pltpu.roll(e1, 1, 0)
    t4 = t2 + pltpu.roll(t2, 2, 0)
    t8 = t4 + pltpu.roll(t4, 4, 0)
    t16 = t8 + pltpu.roll(t8, 8, 0)
    sum1 = jnp.where(low, t8[MAX_WINDOW:], t16[MAX_WINDOW:])
    pooled = jnp.concatenate([window_mean(sum0, 2.0, 4.0), window_mean(sum1, 8.0, 16.0)],
                             axis=1) - zb
    yb = _dot(pooled.astype(BF16), wpool_ref[...]) * pscale
    yb_ref[...] = yb.astype(BF16)

    u = _gelu(zu)
    v = _gelu(zv)
    mu = jnp.mean(v, axis=-1, keepdims=True)
    vc = v - mu
    var = jnp.mean(vc * vc, axis=-1, keepdims=True)
    vn = (vc * lax.rsqrt(var + EPS) * lng + lnb).astype(BF16)
    lane_c = lax.broadcasted_iota(jnp.int32, (CHUNK, LANES), 1)
    low_c = lane_c < SGU_HEAD_DIM
    zero = jnp.zeros((), BF16)
    for c in range(0, tm // CHUNK, 2):
        rows_a = slice(c * CHUNK, (c + 1) * CHUNK)
        rows_b = slice((c + 1) * CHUNK, (c + 2) * CHUNK)
        parts_a, parts_b = [], []
        for p in range(SGU_HEADS // 2):
            va = vn[rows_a, p * LANES:(p + 1) * LANES]
            vb = vn[rows_b, p * LANES:(p + 1) * LANES]
            top = jnp.concatenate([jnp.where(low_c, va, zero), jnp.where(low_c, vb, zero)], axis=1)
            bot = jnp.concatenate([jnp.where(low_c, zero, va), jnp.where(low_c, zero, vb)], axis=1)
            out = _dot(ws_ref[p], jnp.concatenate([top, bot], axis=0))
            parts_a.append(out[:, :LANES])
            parts_b.append(out[:, LANES:])
        for rows, parts in ((rows_a, parts_a), (rows_b, parts_b)):
            mixed = jnp.concatenate(parts, axis=1) + bsp_ref[...]
            yc_ref[rows, :] = (u[rows, :] * mixed).astype(BF16)


def _layer_spec(layer, shape):
    zeros = (0,) * len(shape)
    return pl.BlockSpec((None,) + tuple(shape), lambda *_: (layer,) + zeros)


def _mix_in(layer, x2d, vecs, win, wpool_bd, ws, bsp, seq_len, riders=()):
    m = x2d.shape[0]
    tm = MIX_IN_TILE
    n_steps = m // tm
    spec = functools.partial(_layer_spec, layer)
    r_specs = [_rider_specs(n_steps, lyr, w) for _, w, lyr in riders]
    return pl.pallas_call(
        functools.partial(_mix_in_kernel, seq_len // tm, tuple(k for k, _, _ in riders)),
        grid=(n_steps,),
        in_specs=[
            pl.BlockSpec((tm, D_MODEL), lambda i: (i, 0)),
            spec((VEC_ROWS, D_MODEL)),
            pl.BlockSpec((D_MODEL, D_IN), lambda i: (0, 0)),
            spec((D_POOL, D_POOL)),
            spec((SGU_HEADS // 2, CHUNK, 2 * CHUNK)),
            spec((CHUNK, D_SGU)),
        ] + [s for s, _, _ in r_specs],
        out_specs=[
            pl.BlockSpec((N_CLUSTERS, tm // SSM_BLOCK, FLAT), lambda i: (0, i, 0)),
            pl.BlockSpec((tm, D_POOL), lambda i: (i, 0)),
            pl.BlockSpec((tm, D_SGU), lambda i: (i, 0)),
        ] + [d for _, d, _ in r_specs],
        out_shape=[
            jax.ShapeDtypeStruct((N_CLUSTERS, m // SSM_BLOCK, FLAT), BF16),
            jax.ShapeDtypeStruct((m, D_POOL), BF16),
            jax.ShapeDtypeStruct((m, D_SGU), BF16),
        ] + [o for _, _, o in r_specs],
        scratch_shapes=[pltpu.VMEM((MAX_WINDOW, D_POOL), F32),
                        pltpu.VMEM((N_CLUSTERS, tm // SSM_BLOCK * BLOCK_PITCH, LANES), F32)],
        compiler_params=pltpu.CompilerParams(
            dimension_semantics=("arbitrary",), vmem_limit_bytes=VMEM_MIX_IN),
        name="mix_in",
    )(x2d, vecs, win, wpool_bd, ws, bsp, *[w for _, w, _ in riders])


def _s5_build_operators(par_ref, bt_ref, ct_ref, dskip_ref, wend_ref, wrd_ref, lag0_ref, lagk_ref,
                        pw_ref):
    a_re = par_ref[0, 0:1, :]
    a_im = par_ref[0, 1:2, :]
    dt = jnp.exp(par_ref[0, 2:3, :])
    l_re = a_re * dt
    l_im = a_im * dt
    mag = jnp.exp(l_re)
    ar = mag * jnp.cos(l_im)
    ai = mag * jnp.sin(l_im)
    den = a_re * a_re + a_im * a_im
    f_re = ((ar - 1.0) * a_re + ai * a_im) / den
    f_im = (ai * a_re - (ar - 1.0) * a_im) / den
    bt_re, bt_im = bt_ref[0, 0], bt_ref[0, 1]
    bb_re = f_re * bt_re - f_im * bt_im
    bb_im = f_re * bt_im + f_im * bt_re
    ct_re, ct_im = ct_ref[0, 0], ct_ref[0, 1]

    lane_q = lax.broadcasted_iota(jnp.int32, (SSM_GROUP, LANES), 1) // SSM_STATE

    def power(l):
        m_l = jnp.exp(l * l_re)
        return m_l * jnp.cos(l * l_im), m_l * jnp.sin(l * l_im)

    def store_pairs(ref, slot, v_re, v_im):
        for p in range(GROUP_PAIRS):
            t_re = v_re[:, p * LANES:(p + 1) * LANES]
            t_im = v_im[:, p * LANES:(p + 1) * LANES]
            for q in range(2):
                rows = slice(q * GROUP_FLAT + slot * SSM_GROUP, q * GROUP_FLAT + (slot + 1) * SSM_GROUP)
                ref[p, rows, :LANES] = jnp.where(lane_q == q, t_re, 0.0).astype(BF16)
                ref[p, rows, LANES:] = jnp.where(lane_q == q, t_im, 0.0).astype(BF16)

    for s in range(SSM_BLOCK):
        p_re, p_im = power(float(SSM_BLOCK - 1 - s))
        store_pairs(wend_ref, s, p_re * bb_re - p_im * bb_im, p_re * bb_im + p_im * bb_re)
        q_re, q_im = power(float(s + 1))
        store_pairs(wrd_ref, s, q_re * ct_re - q_im * ct_im, -(q_re * ct_im + q_im * ct_re))
        r_re, r_im = power(float(s))
        store_pairs(lag0_ref, s, r_re * ct_re - r_im * ct_im, -(r_re * ct_im + r_im * ct_re))

    last = (SSM_BLOCK - 1) * SSM_GROUP
    col = lax.broadcasted_iota(jnp.int32, (SSM_GROUP, GROUP_FLAT), 1)
    row = lax.broadcasted_iota(jnp.int32, (SSM_GROUP, GROUP_FLAT), 0)
    for p in range(GROUP_PAIRS):
        bbar = jnp.concatenate([wend_ref[p, q * GROUP_FLAT + last:q * GROUP_FLAT + last + SSM_GROUP, :]
                                for q in range(2)], axis=0)
        lags = _dot_nt(bbar, lag0_ref[p])
        for q in range(2):
            g = 2 * p + q
            k_row = lags[q * SSM_GROUP:(q + 1) * SSM_GROUP, q * GROUP_FLAT:(q + 1) * GROUP_FLAT]
            d_g = dskip_ref[0, :, g * GROUP_FLAT:(g + 1) * GROUP_FLAT]
            k_row = k_row + jnp.where(col == row, d_g, 0.0)
            for s in range(SSM_BLOCK):
                shifted = pltpu.roll(k_row, s * SSM_GROUP, 1) if s else k_row
                lagk_ref[2 * p + q, s * SSM_GROUP:(s + 1) * SSM_GROUP, :] = jnp.where(
                    col >= s * SSM_GROUP, shifted, 0.0).astype(BF16)

    p_re, p_im = power(float(SSM_BLOCK))
    pw_ref[0:1, :] = jnp.concatenate([p_re, p_im], axis=1)
    for _ in range(SCAN_SEG_LEN.bit_length() - 1):
        p_re, p_im = p_re * p_re - p_im * p_im, 2.0 * (p_re * p_im)
    pw_ref[1:2, :] = jnp.concatenate([p_re, p_im], axis=1)


def _cmul_add(p_re, p_im, h_re, h_im, e_re, e_im):
    n_re = [pr * hr - pi * hi + er for pr, pi, hr, hi, er in zip(p_re, p_im, h_re, h_im, e_re)]
    n_im = [pr * hi + pi * hr + ei for pr, pi, hr, hi, ei in zip(p_re, p_im, h_re, h_im, e_im)]
    return n_re, n_im


def _s5_block_recurrence(st_ref, pw_ref, fillers):
    fillers = list(fillers)
    total_cost = sum(c for c, _ in fillers)
    issued_cost = [0.0]
    ticks = [0]

    def tick():
        while fillers and issued_cost[0] * 2 * SCAN_SEG_LEN <= ticks[0] * total_cost:
            cost, thunk = fillers.pop(0)
            issued_cost[0] += cost
            thunk()
        ticks[0] += 1

    half = SCAN_TILES // 2

    def bcast(row):
        return [jnp.broadcast_to(row[:, q * LANES:(q + 1) * LANES], (SCAN_SEGS, LANES))
                for q in range(half)]

    p_re, p_im = bcast(pw_ref[0:1, :CLUSTER_STATE]), bcast(pw_ref[0:1, CLUSTER_STATE:])
    seg_rows = lambda i: pl.ds(i, SCAN_SEGS, stride=SCAN_SEG_PITCH)

    def gather(i):
        e = [st_ref[q, seg_rows(i), :] for q in range(SCAN_TILES)]
        return e[:half], e[half:]

    zeros = [jnp.zeros((SCAN_SEGS, LANES), F32)] * half

    def end_step(i, h):
        e_re, e_im = gather(i)
        return _cmul_add(p_re, p_im, h[0], h[1], e_re, e_im)

    h = (zeros, zeros)
    for i in range(SCAN_SEG_LEN):
        tick()
        h = end_step(i, h)
    l_re, l_im = h

    s_re, s_im = bcast(pw_ref[1:2, :CLUSTER_STATE]), bcast(pw_ref[1:2, CLUSTER_STATE:])
    first = lax.broadcasted_iota(jnp.int32, (SCAN_SEGS, LANES), 0) == 0
    shift = lambda v: jnp.where(first, 0.0, pltpu.roll(v, 1, 0))
    c_re, c_im = zeros, zeros
    for _ in range(SCAN_SEGS - 1):
        n_re, n_im = _cmul_add(s_re, s_im, c_re, c_im, l_re, l_im)
        c_re, c_im = [shift(v) for v in n_re], [shift(v) for v in n_im]

    def scan_step(i, h):
        e_re, e_im = gather(i)
        for q in range(half):
            st_ref[q, seg_rows(i), :] = h[0][q]
            st_ref[half + q, seg_rows(i), :] = h[1][q]
        return _cmul_add(p_re, p_im, h[0], h[1], e_re, e_im)

    h = (c_re, c_im)
    for i in range(SCAN_SEG_LEN):
        tick()
        h = scan_step(i, h)
    while fillers:
        fillers.pop(0)[1]()


def _s5_kernel(u_ref, par_ref, bt_ref, ct_ref, dskip_ref, g_ref,
               wend_ref, wrd_ref, lag0_ref, lagk_ref, pw_ref, st_ref):
    r = pl.program_id(1)

    @pl.when(r == 0)
    def _():
        _s5_build_operators(par_ref, bt_ref, ct_ref, dskip_ref, wend_ref, wrd_ref, lag0_ref,
                            lagk_ref, pw_ref)

    u = u_ref[0]
    pair_cols = lambda p: slice(2 * p * GROUP_FLAT, 2 * (p + 1) * GROUP_FLAT)
    seg_rows = lambda j: slice(j * SCAN_SEG_PITCH, j * SCAN_SEG_PITCH + SCAN_SEG_LEN)

    for p in range(GROUP_PAIRS):
        e = _dot(u[:, pair_cols(p)], wend_ref[p])
        for tile, cols in ((p, slice(0, LANES)), (GROUP_PAIRS + p, slice(LANES, 2 * LANES))):
            for j in range(SCAN_SEGS):
                st_ref[tile, seg_rows(j), :] = e[j * SCAN_SEG_LEN:(j + 1) * SCAN_SEG_LEN, cols]

    ys = [None] * GROUPS_PER_CLUSTER

    def lag_product(g):
        ys[g] = _dot(u[:, g * GROUP_FLAT:(g + 1) * GROUP_FLAT], lagk_ref[g])

    _s5_block_recurrence(st_ref, pw_ref,
                         [(1, functools.partial(lag_product, g)) for g in range(len(ys))])
    y = jnp.concatenate(ys, axis=1)

    outs = []
    for p in range(GROUP_PAIRS):
        hprev = jnp.concatenate(
            [jnp.concatenate([st_ref[p, seg_rows(j), :], st_ref[GROUP_PAIRS + p, seg_rows(j), :]],
                             axis=1) for j in range(SCAN_SEGS)], axis=0)
        outs.append(_dot_nt(hprev.astype(BF16), wrd_ref[p]))
    g_ref[0] = _gelu(y + jnp.concatenate(outs, axis=1))


def _s5(layer, u_flat, par, bt, ct, dskip, blocks_per_seq):
    nb = u_flat.shape[1]
    rt = S5_ROW_TILE
    assert blocks_per_seq == rt and SCAN_SEG_LEN & (SCAN_SEG_LEN - 1) == 0
    return pl.pallas_call(
        _s5_kernel,
        grid=(N_CLUSTERS, nb // rt),
        in_specs=[
            pl.BlockSpec((1, rt, FLAT), lambda k, r: (k, r, 0)),
            pl.BlockSpec((None, 1, 3, CLUSTER_STATE), lambda k, r: (layer, k, 0, 0)),
            pl.BlockSpec((None, 1, 2, SSM_GROUP, CLUSTER_STATE), lambda k, r: (layer, k, 0, 0, 0)),
            pl.BlockSpec((None, 1, 2, SSM_GROUP, CLUSTER_STATE), lambda k, r: (layer, k, 0, 0, 0)),
            pl.BlockSpec((None, 1, 1, FLAT), lambda k, r: (layer, k, 0, 0)),
        ],
        out_specs=pl.BlockSpec((1, rt, FLAT), lambda k, r: (k, r, 0)),
        out_shape=jax.ShapeDtypeStruct((N_CLUSTERS, nb, FLAT), F32),
        scratch_shapes=[pltpu.VMEM((GROUP_PAIRS, 2 * GROUP_FLAT, 2 * LANES), BF16),
                        pltpu.VMEM((GROUP_PAIRS, 2 * GROUP_FLAT, 2 * LANES), BF16),
                        pltpu.VMEM((GROUP_PAIRS, 2 * GROUP_FLAT, 2 * LANES), BF16),
                        pltpu.VMEM((GROUPS_PER_CLUSTER, GROUP_FLAT, GROUP_FLAT), BF16),
                        pltpu.VMEM((2, 2 * CLUSTER_STATE), F32),
                        pltpu.VMEM((SCAN_TILES, SCAN_SEGS * SCAN_SEG_PITCH, LANES), F32)],
        compiler_params=pltpu.CompilerParams(
            dimension_semantics=("arbitrary", "arbitrary"), vmem_limit_bytes=VMEM_S5),
        name="s5",
    )(u_flat, par, bt, ct, dskip)


def _s5_params(A_re, A_im, log_dt, B_re, B_im, C_re, C_im, D_skip):
    depth = A_re.shape[0]
    k, gpc, n, c = N_CLUSTERS, GROUPS_PER_CLUSTER, SSM_STATE, SSM_GROUP
    rows = lambda a: a.reshape(depth, k, 1, CLUSTER_STATE)
    ldt = jnp.broadcast_to(log_dt[..., None], A_re.shape)
    par = jnp.concatenate([rows(A_re), rows(A_im), rows(ldt)], axis=2)
    b = jnp.stack([B_re, B_im], axis=1).reshape(depth, 2, k, gpc, n, c)
    bt = b.transpose(0, 2, 1, 5, 3, 4).reshape(depth, k, 2, c, CLUSTER_STATE)
    cc = jnp.stack([C_re, C_im], axis=1).reshape(depth, 2, k, gpc, c, n)
    ct = cc.transpose(0, 2, 1, 4, 3, 5).reshape(depth, k, 2, c, CLUSTER_STATE)
    dskip = jnp.broadcast_to(D_skip.reshape(depth, k, gpc, 1, c), (depth, k, gpc, SSM_BLOCK, c))
    return par, bt, ct, dskip.reshape(depth, k, 1, FLAT)


def _mix_out_kernel(final, rider_kinds, *refs):
    n_r = len(rider_kinds)
    first = refs[0:4]
    ahead = refs[4:8]
    vec_ref, wglu_ref, wout_ref, wg_ref, wu_ref, wd_ref = refs[8:14]
    rider_src = refs[14:14 + n_r]
    o_ref = refs[14 + n_r]
    rider_dst = refs[15 + n_r:15 + 2 * n_r]
    gs_ref, x1_ref, h_ref = refs[15 + 2 * n_r:]
    _cast_riders(rider_kinds, rider_src, rider_dst)

    bglu = vec_ref[VEC_GLU:VEC_GLU + 1, :D_SSM]
    gffn = vec_ref[VEC_GFFN:VEC_GFFN + 1, :]
    tm = o_ref.shape[0]
    half_block = SSM_BLOCK // 2

    def mix_tail(x_ref, g_ref, yb_ref, yc_ref):
        for k in range(N_CLUSTERS):
            for hf in range(2):
                groups = [g_ref[k, :, g * GROUP_FLAT + hf * LANES:g * GROUP_FLAT + (hf + 1) * LANES]
                          for g in range(GROUPS_PER_CLUSTER)]
                for i, tile in enumerate(_chunk_transpose(groups)):
                    rows = pl.ds(hf * half_block + i, tm // SSM_BLOCK, stride=SSM_BLOCK)
                    gs_ref[k, rows, :] = tile
        g = jnp.concatenate([gs_ref[k] for k in range(N_CLUSTERS)], axis=1)
        ya = g * jax.nn.sigmoid(_dot(g.astype(BF16), wglu_ref[...]) + bglu)
        ymix = jnp.concatenate([ya.astype(BF16), yb_ref[...], yc_ref[...]], axis=1)
        x1 = x_ref[...] + _dot(ymix, wout_ref[...])
        x1_ref[...] = x1
        h_ref[...] = _rms(x1, gffn).astype(BF16)

    @pl.when(pl.program_id(0) == 0)
    def _():
        mix_tail(*first)

    acc = x1_ref[...]
    h = h_ref[...]
    d_ff = wg_ref.shape[1]
    for n, c0 in enumerate(range(0, d_ff, FF_CHUNK)):
        c1 = min(c0 + FF_CHUNK, d_ff)
        gate = _dot(h, wg_ref[:, c0:c1])
        up = _dot(h, wu_ref[:, c0:c1])
        act = (gate * jax.nn.sigmoid(gate) * up).astype(BF16)
        acc = acc + _dot(act, wd_ref[c0:c1, :])
        if n == 0:
            mix_tail(*ahead)
    if final:
        acc = _rms(acc, vec_ref[VEC_GFINAL:VEC_GFINAL + 1, :])
    o_ref[...] = acc


def _mix_out(layer, x2d, g3, yb, yc, vecs, wglu, wout, wg, wu, wd, final, riders=()):
    m = x2d.shape[0]
    tm = TOKEN_TILE
    n_steps = m // tm
    d_ff = wg.shape[-1]
    resident = lambda shape: pl.BlockSpec(shape, lambda i: (0, 0), pipeline_mode=pl.Buffered(1))

    def tile0(shape):
        return pl.BlockSpec(shape, lambda i: (0,) * len(shape), pipeline_mode=pl.Buffered(1))

    def ahead(shape, axis=0):
        nxt = lambda i: jnp.minimum(i + 1, n_steps - 1)
        return pl.BlockSpec(shape, lambda i: tuple(nxt(i) if d == axis else 0
                                                   for d in range(len(shape))))

    r_specs = [_rider_specs(n_steps, lyr, w) for _, w, lyr in riders]
    return pl.pallas_call(
        functools.partial(_mix_out_kernel, final, tuple(k for k, _, _ in riders)),
        grid=(n_steps,),
        in_specs=[
            tile0((tm, D_MODEL)), tile0((N_CLUSTERS, tm // SSM_BLOCK, FLAT)),
            tile0((tm, D_POOL)), tile0((tm, D_SGU)),
            ahead((tm, D_MODEL)), ahead((N_CLUSTERS, tm // SSM_BLOCK, FLAT), axis=1),
            ahead((tm, D_POOL)), ahead((tm, D_SGU)),
            _layer_spec(layer, (VEC_ROWS, D_MODEL)),
            _layer_spec(layer, (D_SSM, D_SSM)),
            resident((D_MODEL, D_MODEL)),
            resident((D_MODEL, d_ff)),
            resident((D_MODEL, d_ff)),
            resident((d_ff, D_MODEL)),
        ] + [s for s, _, _ in r_specs],
        out_specs=[pl.BlockSpec((tm, D_MODEL), lambda i: (i, 0))] + [d for _, d, _ in r_specs],
        out_shape=[jax.ShapeDtypeStruct((m, D_MODEL), F32)] + [o for _, _, o in r_specs],
        scratch_shapes=[pltpu.VMEM((N_CLUSTERS, tm, LANES), F32),
                        pltpu.VMEM((tm, D_MODEL), F32),
                        pltpu.VMEM((tm, D_MODEL), BF16)],
        compiler_params=pltpu.CompilerParams(
            dimension_semantics=("arbitrary",), vmem_limit_bytes=VMEM_MIX_OUT),
        name="mix_out",
    )(x2d, g3, yb, yc, x2d, g3, yb, yc, vecs, wglu, wout, wg, wu, wd, *[w for _, w, _ in riders])


def kernel(x, g_mix, w_in, A_re, A_im, log_dt, B_re, B_im, C_re, C_im, D_skip, w_glu, b_glu,
           w_pool, pool_scale, sgu_ln_g, sgu_ln_b, w_spatial, b_spatial, w_out, g_ffn,
           w_gate, w_up, w_down, g_final):
    bsz, seq, d = x.shape
    depth = w_in.shape[0]
    m = bsz * seq
    assert d == D_MODEL and seq % MIX_IN_TILE == 0 and MIX_IN_TILE % (2 * CHUNK) == 0
    assert m % TOKEN_TILE == 0
    assert (seq // SSM_BLOCK) % S5_ROW_TILE == 0
    x2d = x.reshape(m, D_MODEL)
    tril = jnp.tril(jnp.ones((CHUNK, CHUNK), dtype=bool))
    eye = jnp.eye(len(POOL_WINDOWS), dtype=F32)
    wpool_bd = jnp.einsum('dgij,gh->dgihj', w_pool, eye).reshape(depth, D_POOL, D_POOL).astype(BF16)
    ws = jnp.where(tril, w_spatial, 0.0).astype(BF16)
    ws = jnp.concatenate([ws[:, 0::2], ws[:, 1::2]], axis=-1)
    bsp = jnp.repeat(jnp.swapaxes(b_spatial, 1, 2), SGU_HEAD_DIM, axis=2)
    par, bt, ct, dskip = _s5_params(A_re, A_im, log_dt, B_re, B_im, C_re, C_im, D_skip)
    vec_rows = [None] * VEC_ROWS
    vec_rows[VEC_GMIX] = g_mix
    vec_rows[VEC_GFFN] = g_ffn
    vec_rows[VEC_SGU] = jnp.concatenate([pool_scale, sgu_ln_g, sgu_ln_b], axis=-1)
    vec_rows[VEC_GLU] = jnp.pad(b_glu, ((0, 0), (0, D_MODEL - D_SSM)))
    vec_rows[VEC_GFINAL] = jnp.broadcast_to(g_final, (depth, D_MODEL))
    zero_row = jnp.zeros((depth, D_MODEL), F32)
    vecs = jnp.stack([zero_row if r is None else r for r in vec_rows], axis=1)
    wglu = w_glu.astype(BF16)
    b_end, u_end = D_SSM + D_POOL, D_SSM + D_POOL + D_SGU
    win = jnp.concatenate([w_in[0, :, u_end:], w_in[0, :, b_end:u_end], w_in[0, :, :b_end]],
                          axis=-1).astype(BF16)
    ffn_stacks = (w_out, w_gate, w_up, w_down)
    ffn = None
    for l in range(depth):
        riders = tuple(("plain", w, 0) for w in ffn_stacks) if l == 0 else ()
        u_flat, yb, yc, *cast = _mix_in(l, x2d, vecs, win, wpool_bd, ws, bsp, seq, riders)
        if l == 0:
            ffn = cast
        g3 = _s5(l, u_flat, par, bt, ct, dskip, seq // SSM_BLOCK)
        last = l == depth - 1
        riders = () if last else (("w_in", w_in, l + 1),) + tuple(
            ("plain", w, l + 1) for w in ffn_stacks)
        x2d, *cast = _mix_out(l, x2d, g3, yb, yc, vecs, wglu, *ffn, final=last, riders=riders)
        if not last:
            win, ffn = cast[0], cast[1:]
    return x2d.reshape(bsz, seq, D_MODEL)
```

```python
import functools
import math

import jax
import jax.numpy as jnp
from jax import lax
from jax.experimental import pallas as pl
from jax.experimental.pallas import tpu as pltpu

D_MODEL = 1024
D_SSM = 384
SSM_GROUP = 16
N_SSM_GROUPS = D_SSM // SSM_GROUP
SSM_STATE = 64
POOL_WINDOWS = (2, 4, 8, 16)
POOL_GROUP = 64
D_POOL = len(POOL_WINDOWS) * POOL_GROUP
MAX_WINDOW = max(POOL_WINDOWS)
SGU_HEADS = 6
SGU_HEAD_DIM = 64
D_SGU = SGU_HEADS * SGU_HEAD_DIM
CHUNK = 128
D_IN = D_SSM + D_POOL + 2 * D_SGU
EPS = 1e-6

LANES = 128
SSM_BLOCK = 16
N_CLUSTERS = D_SSM // LANES
GROUPS_PER_CLUSTER = LANES // SSM_GROUP
CLUSTER_STATE = GROUPS_PER_CLUSTER * SSM_STATE
FLAT = SSM_BLOCK * LANES
GROUP_FLAT = SSM_BLOCK * SSM_GROUP
GROUP_PAIRS = GROUPS_PER_CLUSTER // 2

MXU_WIDTH = 256
BF16_SUBLANES = 16
VEC_ROWS = 8
VEC_GMIX, VEC_GFFN, VEC_SGU, VEC_GLU, VEC_GFINAL = 0, 1, 2, 3, 4
TOKEN_TILE = 512
MIX_IN_TILE = 1024
S5_ROW_TILE = 512
SUBLANES = 8
SCAN_SEGS = SUBLANES
SCAN_SEG_LEN = S5_ROW_TILE // SCAN_SEGS
SCAN_SEG_PITCH = SCAN_SEG_LEN + 4
SCAN_TILES = 2 * CLUSTER_STATE // LANES
BLOCK_PITCH = SSM_BLOCK + SUBLANES
FF_CHUNK = 3 * MXU_WIDTH
V7X_VMEM_BYTES = 64 * 1024 * 1024
VMEM_MIX_IN = VMEM_S5 = VMEM_MIX_OUT = V7X_VMEM_BYTES

F32 = jnp.float32
BF16 = jnp.bfloat16
NT_DIMS = (((1,), (1,)), ((), ()))


def _gelu(x):
    c = math.sqrt(2.0 / math.pi)
    return x * (0.5 + 0.5 * jnp.tanh(x * (c + (c * 0.044715) * (x * x))))


def _rms(x, g):
    ms = jnp.mean(x * x, axis=-1, keepdims=True)
    return x * lax.rsqrt(ms + EPS) * g


def _dot(a, b):
    return jnp.dot(a, b, preferred_element_type=F32)


def _dot_nt(a, b):
    return lax.dot_general(a, b, NT_DIMS, preferred_element_type=F32)


def _chunk_transpose(tiles):
    n = len(tiles)
    chunk = LANES // n
    v = list(tiles)
    cidx = lax.broadcasted_iota(jnp.int32, v[0].shape, 1) // chunk
    k = 1
    while k < n:
        keep = (cidx & k) == 0
        nv = list(v)
        for a in range(n):
            if a & k:
                continue
            lo, hi = v[a], v[a + k]
            nv[a] = jnp.where(keep, lo, pltpu.roll(hi, k * chunk, 1))
            nv[a + k] = jnp.where(keep, pltpu.roll(lo, LANES - k * chunk, 1), hi)
        v = nv
        k *= 2
    return v


def _cast_riders(kinds, src_refs, dst_refs):
    for kind, src, dst in zip(kinds, src_refs, dst_refs):
        if kind == "w_in":
            b_end, u_end = D_SSM + D_POOL, D_SSM + D_POOL + D_SGU
            dst[:, :D_SGU] = src[:, u_end:].astype(BF16)
            dst[:, D_SGU:2 * D_SGU] = src[:, b_end:u_end].astype(BF16)
            dst[:, 2 * D_SGU:] = src[:, :b_end].astype(BF16)
        else:
            dst[...] = src[...].astype(BF16)


def _rider_specs(n_steps, layer, w):
    rows, cols = w.shape[1:]
    share = 1
    while (rows * share) % n_steps or (rows * share // n_steps) % BF16_SUBLANES:
        share *= 2
    slab = rows * share // n_steps
    which = lambda i: jnp.minimum(i, n_steps - 1) // share
    src = pl.BlockSpec((None, slab, cols), lambda i: (layer, which(i), 0))
    dst = pl.BlockSpec((slab, cols), lambda i: (which(i), 0))
    return src, dst, jax.ShapeDtypeStruct((rows, cols), BF16)


def _mix_in_kernel(tiles_per_seq, rider_kinds, *refs):
    n_r = len(rider_kinds)
    x_ref, vec_ref, win_ref, wpool_ref, ws_ref, bsp_ref = refs[:6]
    rider_src = refs[6:6 + n_r]
    za_ref, yb_ref, yc_ref = refs[6 + n_r:9 + n_r]
    rider_dst = refs[9 + n_r:9 + 2 * n_r]
    halo_ref, zs_ref = refs[9 + 2 * n_r:]
    _cast_riders(rider_kinds, rider_src, rider_dst)

    tm = x_ref.shape[0]
    i = pl.program_id(0)
    seq_tile = i % tiles_per_seq
    gmix = vec_ref[VEC_GMIX:VEC_GMIX + 1, :]
    pscale = vec_ref[VEC_SGU:VEC_SGU + 1, :D_POOL]
    lng = vec_ref[VEC_SGU:VEC_SGU + 1, D_POOL:D_POOL + D_SGU]
    lnb = vec_ref[VEC_SGU:VEC_SGU + 1, D_POOL + D_SGU:]

    h = _rms(x_ref[...], gmix).astype(BF16)
    z_sgu = _dot(h, win_ref[:, :2 * D_SGU])
    z_rest = _dot(h, win_ref[:, 2 * D_SGU:])
    zv = z_sgu[:, :D_SGU]
    zu = z_sgu[:, D_SGU:]
    zb = z_rest[:, D_SSM:]

    n_blocks = tm // SSM_BLOCK
    for k in range(N_CLUSTERS):
        for b in range(n_blocks):
            zs_ref[k, b * BLOCK_PITCH:b * BLOCK_PITCH + SSM_BLOCK, :] = (
                z_rest[b * SSM_BLOCK:(b + 1) * SSM_BLOCK, k * LANES:(k + 1) * LANES])
    half_block = SSM_BLOCK // 2
    for k in range(N_CLUSTERS):
        for hf in range(2):
            slots = [zs_ref[k, pl.ds(hf * half_block + i, n_blocks, stride=BLOCK_PITCH), :]
                     for i in range(half_block)]
            for g, tile in enumerate(_chunk_transpose(slots)):
                col = g * GROUP_FLAT + hf * LANES
                za_ref[k, :, col:col + LANES] = tile.astype(BF16)

    halo = jnp.where(seq_tile == 0, 0.0, halo_ref[...])
    halo_ref[...] = zb[tm - MAX_WINDOW:, :]
    ext = jnp.concatenate([halo, zb], axis=0)
    lane = lax.broadcasted_iota(jnp.int32, (tm, LANES), 1)
    low = lane < POOL_GROUP
    low_h = lax.broadcasted_iota(jnp.int32, (MAX_WINDOW, LANES), 1) < POOL_GROUP
    low_r = lax.broadcasted_iota(jnp.int32, (1, LANES), 1) < POOL_GROUP
    pos1 = (seq_tile * tm + 1
            + lax.broadcasted_iota(jnp.int32, (MAX_WINDOW, LANES), 0)).astype(F32)

    def window_mean(total, w_low, w_high):
        cnt = jnp.where(low_h, jnp.minimum(pos1, w_low), jnp.minimum(pos1, w_high))
        inv_w = jnp.where(low_r, 1.0 / w_low, 1.0 / w_high)
        return jnp.concatenate([total[:MAX_WINDOW] / cnt, total[MAX_WINDOW:] * inv_w], axis=0)

    e0 = ext[:, :LANES]
    s2 = e0 + pltpu.roll(e0, 1, 0)
    s4 = s2 + pltpu.roll(s2, 2, 0)
    sum0 = jnp.where(low, s2[MAX_WINDOW:], s4[MAX_WINDOW:])
    e1 = ext[:, LANES:]
    t2 = e1 + pltpu.roll(e1, 1, 0)
    t4 = t2 + pltpu.roll(t2, 2, 0)
    t8 = t4 + pltpu.roll(t4, 4, 0)
    t16 = t8 + pltpu.roll(t8, 8, 0)
    sum1 = jnp.where(low, t8[MAX_WINDOW:], t16[MAX_WINDOW:])
    pooled = jnp.concatenate([window_mean(sum0, 2.0, 4.0), window_mean(sum1, 8.0, 16.0)],
                             axis=1) - zb
    yb = _dot(pooled.astype(BF16), wpool_ref[...]) * pscale
    yb_ref[...] = yb.astype(BF16)

    u = _gelu(zu)
    v = _gelu(zv)
    mu = jnp.mean(v, axis=-1, keepdims=True)
    vc = v - mu
    var = jnp.mean(vc * vc, axis=-1, keepdims=True)
    vn = (vc * lax.rsqrt(var + EPS) * lng + lnb).astype(BF16)
    lane_c = lax.broadcasted_iota(jnp.int32, (CHUNK, LANES), 1)
    low_c = lane_c < SGU_HEAD_DIM
    zero = jnp.zeros((), BF16)
    for c in range(0, tm // CHUNK, 2):
        rows_a = slice(c * CHUNK, (c + 1) * CHUNK)
        rows_b = slice((c + 1) * CHUNK, (c + 2) * CHUNK)
        parts_a, parts_b = [], []
        for p in range(SGU_HEADS // 2):
            va = vn[rows_a, p * LANES:(p + 1) * LANES]
            vb = vn[rows_b, p * LANES:(p + 1) * LANES]
            top = jnp.concatenate([jnp.where(low_c, va, zero), jnp.where(low_c, vb, zero)], axis=1)
            bot = jnp.concatenate([jnp.where(low_c, zero, va), jnp.where(low_c, zero, vb)], axis=1)
            out = _dot(ws_ref[p], jnp.concatenate([top, bot], axis=0))
            parts_a.append(out[:, :LANES])
            parts_b.append(out[:, LANES:])
        for rows, parts in ((rows_a, parts_a), (rows_b, parts_b)):
            mixed = jnp.concatenate(parts, axis=1) + bsp_ref[...]
            yc_ref[rows, :] = (u[rows, :] * mixed).astype(BF16)


def _layer_spec(layer, shape):
    zeros = (0,) * len(shape)
    return pl.BlockSpec((None,) + tuple(shape), lambda *_: (layer,) + zeros)


def _mix_in(layer, x2d, vecs, win, wpool_bd, ws, bsp, seq_len, riders=()):
    m = x2d.shape[0]
    tm = MIX_IN_TILE
    n_steps = m // tm
    spec = functools.partial(_layer_spec, layer)
    r_specs = [_rider_specs(n_steps, lyr, w) for _, w, lyr in riders]
    return pl.pallas_call(
        functools.partial(_mix_in_kernel, seq_len // tm, tuple(k for k, _, _ in riders)),
        grid=(n_steps,),
        in_specs=[
            pl.BlockSpec((tm, D_MODEL), lambda i: (i, 0)),
            spec((VEC_ROWS, D_MODEL)),
            pl.BlockSpec((D_MODEL, D_IN), lambda i: (0, 0)),
            spec((D_POOL, D_POOL)),
            spec((SGU_HEADS // 2, CHUNK, 2 * CHUNK)),
            spec((CHUNK, D_SGU)),
        ] + [s for s, _, _ in r_specs],
        out_specs=[
            pl.BlockSpec((N_CLUSTERS, tm // SSM_BLOCK, FLAT), lambda i: (0, i, 0)),
            pl.BlockSpec((tm, D_POOL), lambda i: (i, 0)),
            pl.BlockSpec((tm, D_SGU), lambda i: (i, 0)),
        ] + [d for _, d, _ in r_specs],
        out_shape=[
            jax.ShapeDtypeStruct((N_CLUSTERS, m // SSM_BLOCK, FLAT), BF16),
            jax.ShapeDtypeStruct((m, D_POOL), BF16),
            jax.ShapeDtypeStruct((m, D_SGU), BF16),
        ] + [o for _, _, o in r_specs],
        scratch_shapes=[pltpu.VMEM((MAX_WINDOW, D_POOL), F32),
                        pltpu.VMEM((N_CLUSTERS, tm // SSM_BLOCK * BLOCK_PITCH, LANES), F32)],
        compiler_params=pltpu.CompilerParams(
            dimension_semantics=("arbitrary",), vmem_limit_bytes=VMEM_MIX_IN),
        name="mix_in",
    )(x2d, vecs, win, wpool_bd, ws, bsp, *[w for _, w, _ in riders])


def _s5_build_operators(par_ref, bt_ref, ct_ref, dskip_ref, wend_ref, wrd_ref, lag0_ref, lagk_ref,
                        pw_ref):
    a_re = par_ref[0, 0:1, :]
    a_im = par_ref[0, 1:2, :]
    dt = jnp.exp(par_ref[0, 2:3, :])
    l_re = a_re * dt
    l_im = a_im * dt
    mag = jnp.exp(l_re)
    ar = mag * jnp.cos(l_im)
    ai = mag * jnp.sin(l_im)
    den = a_re * a_re + a_im * a_im
    f_re = ((ar - 1.0) * a_re + ai * a_im) / den
    f_im = (ai * a_re - (ar - 1.0) * a_im) / den
    bt_re, bt_im = bt_ref[0, 0], bt_ref[0, 1]
    bb_re = f_re * bt_re - f_im * bt_im
    bb_im = f_re * bt_im + f_im * bt_re
    ct_re, ct_im = ct_ref[0, 0], ct_ref[0, 1]

    lane_q = lax.broadcasted_iota(jnp.int32, (SSM_GROUP, LANES), 1) // SSM_STATE

    def power(l):
        m_l = jnp.exp(l * l_re)
        return m_l * jnp.cos(l * l_im), m_l * jnp.sin(l * l_im)

    def store_pairs(ref, slot, v_re, v_im):
        for p in range(GROUP_PAIRS):
            t_re = v_re[:, p * LANES:(p + 1) * LANES]
            t_im = v_im[:, p * LANES:(p + 1) * LANES]
            for q in range(2):
                rows = slice(q * GROUP_FLAT + slot * SSM_GROUP, q * GROUP_FLAT + (slot + 1) * SSM_GROUP)
                ref[p, rows, :LANES] = jnp.where(lane_q == q, t_re, 0.0).astype(BF16)
                ref[p, rows, LANES:] = jnp.where(lane_q == q, t_im, 0.0).astype(BF16)

    for s in range(SSM_BLOCK):
        p_re, p_im = power(float(SSM_BLOCK - 1 - s))
        store_pairs(wend_ref, s, p_re * bb_re - p_im * bb_im, p_re * bb_im + p_im * bb_re)
        q_re, q_im = power(float(s + 1))
        store_pairs(wrd_ref, s, q_re * ct_re - q_im * ct_im, -(q_re * ct_im + q_im * ct_re))
        r_re, r_im = power(float(s))
        store_pairs(lag0_ref, s, r_re * ct_re - r_im * ct_im, -(r_re * ct_im + r_im * ct_re))

    last = (SSM_BLOCK - 1) * SSM_GROUP
    col = lax.broadcasted_iota(jnp.int32, (SSM_GROUP, GROUP_FLAT), 1)
    row = lax.broadcasted_iota(jnp.int32, (SSM_GROUP, GROUP_FLAT), 0)
    for p in range(GROUP_PAIRS):
        bbar = jnp.concatenate([wend_ref[p, q * GROUP_FLAT + last:q * GROUP_FLAT + last + SSM_GROUP, :]
                                for q in range(2)], axis=0)
        lags = _dot_nt(bbar, lag0_ref[p])
        for q in range(2):
            g = 2 * p + q
            k_row = lags[q * SSM_GROUP:(q + 1) * SSM_GROUP, q * GROUP_FLAT:(q + 1) * GROUP_FLAT]
            d_g = dskip_ref[0, :, g * GROUP_FLAT:(g + 1) * GROUP_FLAT]
            k_row = k_row + jnp.where(col == row, d_g, 0.0)
            for s in range(SSM_BLOCK):
                shifted = pltpu.roll(k_row, s * SSM_GROUP, 1) if s else k_row
                lagk_ref[2 * p + q, s * SSM_GROUP:(s + 1) * SSM_GROUP, :] = jnp.where(
                    col >= s * SSM_GROUP, shifted, 0.0).astype(BF16)

    p_re, p_im = power(float(SSM_BLOCK))
    pw_ref[0:1, :] = jnp.concatenate([p_re, p_im], axis=1)
    for _ in range(SCAN_SEG_LEN.bit_length() - 1):
        p_re, p_im = p_re * p_re - p_im * p_im, 2.0 * (p_re * p_im)
    pw_ref[1:2, :] = jnp.concatenate([p_re, p_im], axis=1)


def _cmul_add(p_re, p_im, h_re, h_im, e_re, e_im):
    n_re = [pr * hr - pi * hi + er for pr, pi, hr, hi, er in zip(p_re, p_im, h_re, h_im, e_re)]
    n_im = [pr * hi + pi * hr + ei for pr, pi, hr, hi, ei in zip(p_re, p_im, h_re, h_im, e_im)]
    return n_re, n_im


def _s5_block_recurrence(st_ref, pw_ref, fillers):
    fillers = list(fillers)
    total_cost = sum(c for c, _ in fillers)
    issued_cost = [0.0]
    ticks = [0]

    def tick():
        while fillers and issued_cost[0] * 2 * SCAN_SEG_LEN <= ticks[0] * total_cost:
            cost, thunk = fillers.pop(0)
            issued_cost[0] += cost
            thunk()
        ticks[0] += 1

    half = SCAN_TILES // 2

    def bcast(row):
        return [jnp.broadcast_to(row[:, q * LANES:(q + 1) * LANES], (SCAN_SEGS, LANES))
                for q in range(half)]

    p_re, p_im = bcast(pw_ref[0:1, :CLUSTER_STATE]), bcast(pw_ref[0:1, CLUSTER_STATE:])
    seg_rows = lambda i: pl.ds(i, SCAN_SEGS, stride=SCAN_SEG_PITCH)

    def gather(i):
        e = [st_ref[q, seg_rows(i), :] for q in range(SCAN_TILES)]
        return e[:half], e[half:]

    zeros = [jnp.zeros((SCAN_SEGS, LANES), F32)] * half

    def end_step(i, h):
        e_re, e_im = gather(i)
        return _cmul_add(p_re, p_im, h[0], h[1], e_re, e_im)

    h = (zeros, zeros)
    for i in range(SCAN_SEG_LEN):
        tick()
        h = end_step(i, h)
    l_re, l_im = h

    s_re, s_im = bcast(pw_ref[1:2, :CLUSTER_STATE]), bcast(pw_ref[1:2, CLUSTER_STATE:])
    first = lax.broadcasted_iota(jnp.int32, (SCAN_SEGS, LANES), 0) == 0
    shift = lambda v: jnp.where(first, 0.0, pltpu.roll(v, 1, 0))
    c_re, c_im = zeros, zeros
    for _ in range(SCAN_SEGS - 1):
        n_re, n_im = _cmul_add(s_re, s_im, c_re, c_im, l_re, l_im)
        c_re, c_im = [shift(v) for v in n_re], [shift(v) for v in n_im]

    def scan_step(i, h):
        e_re, e_im = gather(i)
        for q in range(half):
            st_ref[q, seg_rows(i), :] = h[0][q]
            st_ref[half + q, seg_rows(i), :] = h[1][q]
        return _cmul_add(p_re, p_im, h[0], h[1], e_re, e_im)

    h = (c_re, c_im)
    for i in range(SCAN_SEG_LEN):
        tick()
        h = scan_step(i, h)
    while fillers:
        fillers.pop(0)[1]()


def _s5_kernel(u_ref, par_ref, bt_ref, ct_ref, dskip_ref, g_ref,
               wend_ref, wrd_ref, lag0_ref, lagk_ref, pw_ref, st_ref):
    r = pl.program_id(1)

    @pl.when(r == 0)
    def _():
        _s5_build_operators(par_ref, bt_ref, ct_ref, dskip_ref, wend_ref, wrd_ref, lag0_ref,
                            lagk_ref, pw_ref)

    u = u_ref[0]
    pair_cols = lambda p: slice(2 * p * GROUP_FLAT, 2 * (p + 1) * GROUP_FLAT)
    seg_rows = lambda j: slice(j * SCAN_SEG_PITCH, j * SCAN_SEG_PITCH + SCAN_SEG_LEN)

    for p in range(GROUP_PAIRS):
        e = _dot(u[:, pair_cols(p)], wend_ref[p])
        for tile, cols in ((p, slice(0, LANES)), (GROUP_PAIRS + p, slice(LANES, 2 * LANES))):
            for j in range(SCAN_SEGS):
                st_ref[tile, seg_rows(j), :] = e[j * SCAN_SEG_LEN:(j + 1) * SCAN_SEG_LEN, cols]

    ys = [None] * GROUPS_PER_CLUSTER

    def lag_product(g):
        ys[g] = _dot(u[:, g * GROUP_FLAT:(g + 1) * GROUP_FLAT], lagk_ref[g])

    _s5_block_recurrence(st_ref, pw_ref,
                         [(1, functools.partial(lag_product, g)) for g in range(len(ys))])
    y = jnp.concatenate(ys, axis=1)

    outs = []
    for p in range(GROUP_PAIRS):
        hprev = jnp.concatenate(
            [jnp.concatenate([st_ref[p, seg_rows(j), :], st_ref[GROUP_PAIRS + p, seg_rows(j), :]],
                             axis=1) for j in range(SCAN_SEGS)], axis=0)
        outs.append(_dot_nt(hprev.astype(BF16), wrd_ref[p]))
    g_ref[0] = _gelu(y + jnp.concatenate(outs, axis=1))


def _s5(layer, u_flat, par, bt, ct, dskip, blocks_per_seq):
    nb = u_flat.shape[1]
    rt = S5_ROW_TILE
    assert blocks_per_seq == rt and SCAN_SEG_LEN & (SCAN_SEG_LEN - 1) == 0
    return pl.pallas_call(
        _s5_kernel,
        grid=(N_CLUSTERS, nb // rt),
        in_specs=[
            pl.BlockSpec((1, rt, FLAT), lambda k, r: (k, r, 0)),
            pl.BlockSpec((None, 1, 3, CLUSTER_STATE), lambda k, r: (layer, k, 0, 0)),
            pl.BlockSpec((None, 1, 2, SSM_GROUP, CLUSTER_STATE), lambda k, r: (layer, k, 0, 0, 0)),
            pl.BlockSpec((None, 1, 2, SSM_GROUP, CLUSTER_STATE), lambda k, r: (layer, k, 0, 0, 0)),
            pl.BlockSpec((None, 1, 1, FLAT), lambda k, r: (layer, k, 0, 0)),
        ],
        out_specs=pl.BlockSpec((1, rt, FLAT), lambda k, r: (k, r, 0)),
        out_shape=jax.ShapeDtypeStruct((N_CLUSTERS, nb, FLAT), F32),
        scratch_shapes=[pltpu.VMEM((GROUP_PAIRS, 2 * GROUP_FLAT, 2 * LANES), BF16),
                        pltpu.VMEM((GROUP_PAIRS, 2 * GROUP_FLAT, 2 * LANES), BF16),
                        pltpu.VMEM((GROUP_PAIRS, 2 * GROUP_FLAT, 2 * LANES), BF16),
                        pltpu.VMEM((GROUPS_PER_CLUSTER, GROUP_FLAT, GROUP_FLAT), BF16),
                        pltpu.VMEM((2, 2 * CLUSTER_STATE), F32),
                        pltpu.VMEM((SCAN_TILES, SCAN_SEGS * SCAN_SEG_PITCH, LANES), F32)],
        compiler_params=pltpu.CompilerParams(
            dimension_semantics=("arbitrary", "arbitrary"), vmem_limit_bytes=VMEM_S5),
        name="s5",
    )(u_flat, par, bt, ct, dskip)


def _s5_params(A_re, A_im, log_dt, B_re, B_im, C_re, C_im, D_skip):
    depth = A_re.shape[0]
    k, gpc, n, c = N_CLUSTERS, GROUPS_PER_CLUSTER, SSM_STATE, SSM_GROUP
    rows = lambda a: a.reshape(depth, k, 1, CLUSTER_STATE)
    ldt = jnp.broadcast_to(log_dt[..., None], A_re.shape)
    par = jnp.concatenate([rows(A_re), rows(A_im), rows(ldt)], axis=2)
    b = jnp.stack([B_re, B_im], axis=1).reshape(depth, 2, k, gpc, n, c)
    bt = b.transpose(0, 2, 1, 5, 3, 4).reshape(depth, k, 2, c, CLUSTER_STATE)
    cc = jnp.stack([C_re, C_im], axis=1).reshape(depth, 2, k, gpc, c, n)
    ct = cc.transpose(0, 2, 1, 4, 3, 5).reshape(depth, k, 2, c, CLUSTER_STATE)
    dskip = jnp.broadcast_to(D_skip.reshape(depth, k, gpc, 1, c), (depth, k, gpc, SSM_BLOCK, c))
    return par, bt, ct, dskip.reshape(depth, k, 1, FLAT)


def _mix_out_kernel(final, rider_kinds, *refs):
    n_r = len(rider_kinds)
    first = refs[0:4]
    ahead = refs[4:8]
    vec_ref, wglu_ref, wout_ref, wg_ref, wu_ref, wd_ref = refs[8:14]
    rider_src = refs[14:14 + n_r]
    o_ref = refs[14 + n_r]
    rider_dst = refs[15 + n_r:15 + 2 * n_r]
    gs_ref, x1_ref, h_ref = refs[15 + 2 * n_r:]
    _cast_riders(rider_kinds, rider_src, rider_dst)

    bglu = vec_ref[VEC_GLU:VEC_GLU + 1, :D_SSM]
    gffn = vec_ref[VEC_GFFN:VEC_GFFN + 1, :]
    tm = o_ref.shape[0]
    half_block = SSM_BLOCK // 2

    def mix_tail(x_ref, g_ref, yb_ref, yc_ref):
        for k in range(N_CLUSTERS):
            for hf in range(2):
                groups = [g_ref[k, :, g * GROUP_FLAT + hf * LANES:g * GROUP_FLAT + (hf + 1) * LANES]
                          for g in range(GROUPS_PER_CLUSTER)]
                for i, tile in enumerate(_chunk_transpose(groups)):
                    rows = pl.ds(hf * half_block + i, tm // SSM_BLOCK, stride=SSM_BLOCK)
                    gs_ref[k, rows, :] = tile
        g = jnp.concatenate([gs_ref[k] for k in range(N_CLUSTERS)], axis=1)
        ya = g * jax.nn.sigmoid(_dot(g.astype(BF16), wglu_ref[...]) + bglu)
        ymix = jnp.concatenate([ya.astype(BF16), yb_ref[...], yc_ref[...]], axis=1)
        x1 = x_ref[...] + _dot(ymix, wout_ref[...])
        x1_ref[...] = x1
        h_ref[...] = _rms(x1, gffn).astype(BF16)

    @pl.when(pl.program_id(0) == 0)
    def _():
        mix_tail(*first)

    acc = x1_ref[...]
    h = h_ref[...]
    d_ff = wg_ref.shape[1]
    for n, c0 in enumerate(range(0, d_ff, FF_CHUNK)):
        c1 = min(c0 + FF_CHUNK, d_ff)
        gate = _dot(h, wg_ref[:, c0:c1])
        up = _dot(h, wu_ref[:, c0:c1])
        act = (gate * jax.nn.sigmoid(gate) * up).astype(BF16)
        acc = acc + _dot(act, wd_ref[c0:c1, :])
        if n == 0:
            mix_tail(*ahead)
    if final:
        acc = _rms(acc, vec_ref[VEC_GFINAL:VEC_GFINAL + 1, :])
    o_ref[...] = acc


def _mix_out(layer, x2d, g3, yb, yc, vecs, wglu, wout, wg, wu, wd, final, riders=()):
    m = x2d.shape[0]
    tm = TOKEN_TILE
    n_steps = m // tm
    d_ff = wg.shape[-1]
    resident = lambda shape: pl.BlockSpec(shape, lambda i: (0, 0), pipeline_mode=pl.Buffered(1))

    def tile0(shape):
        return pl.BlockSpec(shape, lambda i: (0,) * len(shape), pipeline_mode=pl.Buffered(1))

    def ahead(shape, axis=0):
        nxt = lambda i: jnp.minimum(i + 1, n_steps - 1)
        return pl.BlockSpec(shape, lambda i: tuple(nxt(i) if d == axis else 0
                                                   for d in range(len(shape))))

    r_specs = [_rider_specs(n_steps, lyr, w) for _, w, lyr in riders]
    return pl.pallas_call(
        functools.partial(_mix_out_kernel, final, tuple(k for k, _, _ in riders)),
        grid=(n_steps,),
        in_specs=[
            tile0((tm, D_MODEL)), tile0((N_CLUSTERS, tm // SSM_BLOCK, FLAT)),
            tile0((tm, D_POOL)), tile0((tm, D_SGU)),
            ahead((tm, D_MODEL)), ahead((N_CLUSTERS, tm // SSM_BLOCK, FLAT), axis=1),
            ahead((tm, D_POOL)), ahead((tm, D_SGU)),
            _layer_spec(layer, (VEC_ROWS, D_MODEL)),
            _layer_spec(layer, (D_SSM, D_SSM)),
            resident((D_MODEL, D_MODEL)),
            resident((D_MODEL, d_ff)),
            resident((D_MODEL, d_ff)),
            resident((d_ff, D_MODEL)),
        ] + [s for s, _, _ in r_specs],
        out_specs=[pl.BlockSpec((tm, D_MODEL), lambda i: (i, 0))] + [d for _, d, _ in r_specs],
        out_shape=[jax.ShapeDtypeStruct((m, D_MODEL), F32)] + [o for _, _, o in r_specs],
        scratch_shapes=[pltpu.VMEM((N_CLUSTERS, tm, LANES), F32),
                        pltpu.VMEM((tm, D_MODEL), F32),
                        pltpu.VMEM((tm, D_MODEL), BF16)],
        compiler_params=pltpu.CompilerParams(
            dimension_semantics=("arbitrary",), vmem_limit_bytes=VMEM_MIX_OUT),
        name="mix_out",
    )(x2d, g3, yb, yc, x2d, g3, yb, yc, vecs, wglu, wout, wg, wu, wd, *[w for _, w, _ in riders])


def kernel(x, g_mix, w_in, A_re, A_im, log_dt, B_re, B_im, C_re, C_im, D_skip, w_glu, b_glu,
           w_pool, pool_scale, sgu_ln_g, sgu_ln_b, w_spatial, b_spatial, w_out, g_ffn,
           w_gate, w_up, w_down, g_final):
    bsz, seq, d = x.shape
    depth = w_in.shape[0]
    m = bsz * seq
    assert d == D_MODEL and seq % MIX_IN_TILE == 0 and MIX_IN_TILE % (2 * CHUNK) == 0
    assert m % TOKEN_TILE == 0
    assert (seq // SSM_BLOCK) % S5_ROW_TILE == 0
    x2d = x.reshape(m, D_MODEL)
    tril = jnp.tril(jnp.ones((CHUNK, CHUNK), dtype=bool))
    eye = jnp.eye(len(POOL_WINDOWS), dtype=F32)
    wpool_bd = jnp.einsum('dgij,gh->dgihj', w_pool, eye).reshape(depth, D_POOL, D_POOL).astype(BF16)
    ws = jnp.where(tril, w_spatial, 0.0).astype(BF16)
    ws = jnp.concatenate([ws[:, 0::2], ws[:, 1::2]], axis=-1)
    bsp = jnp.repeat(jnp.swapaxes(b_spatial, 1, 2), SGU_HEAD_DIM, axis=2)
    par, bt, ct, dskip = _s5_params(A_re, A_im, log_dt, B_re, B_im, C_re, C_im, D_skip)
    vec_rows = [None] * VEC_ROWS
    vec_rows[VEC_GMIX] = g_mix
    vec_rows[VEC_GFFN] = g_ffn
    vec_rows[VEC_SGU] = jnp.concatenate([pool_scale, sgu_ln_g, sgu_ln_b], axis=-1)
    vec_rows[VEC_GLU] = jnp.pad(b_glu, ((0, 0), (0, D_MODEL - D_SSM)))
    vec_rows[VEC_GFINAL] = jnp.broadcast_to(g_final, (depth, D_MODEL))
    zero_row = jnp.zeros((depth, D_MODEL), F32)
    vecs = jnp.stack([zero_row if r is None else r for r in vec_rows], axis=1)
    wglu = w_glu.astype(BF16)
    b_end, u_end = D_SSM + D_POOL, D_SSM + D_POOL + D_SGU
    win = jnp.concatenate([w_in[0, :, u_end:], w_in[0, :, b_end:u_end], w_in[0, :, :b_end]],
                          axis=-1).astype(BF16)
    ffn_stacks = (w_out, w_gate, w_up, w_down)
    ffn = None
    for l in range(depth):
        riders = tuple(("plain", w, 0) for w in ffn_stacks) if l == 0 else ()
        u_flat, yb, yc, *cast = _mix_in(l, x2d, vecs, win, wpool_bd, ws, bsp, seq, riders)
        if l == 0:
            ffn = cast
        g3 = _s5(l, u_flat, par, bt, ct, dskip, seq // SSM_BLOCK)
        last = l == depth - 1
        riders = () if last else (("w_in", w_in, l + 1),) + tuple(
            ("plain", w, l + 1) for w in ffn_stacks)
        x2d, *cast = _mix_out(l, x2d, g3, yb, yc, vecs, wglu, *ffn, final=last, riders=riders)
        if not last:
            win, ffn = cast[0], cast[1:]
    return x2d.reshape(bsz, seq, D_MODEL)
```

```python
import functools
import math

import jax
import jax.numpy as jnp
from jax import lax
from jax.experimental import pallas as pl
from jax.experimental.pallas import tpu as pltpu

D_MODEL = 1024
D_SSM = 384
SSM_GROUP = 16
N_SSM_GROUPS = D_SSM // SSM_GROUP
SSM_STATE = 64
POOL_WINDOWS = (2, 4, 8, 16)
POOL_GROUP = 64
D_POOL = len(POOL_WINDOWS) * POOL_GROUP
MAX_WINDOW = max(POOL_WINDOWS)
SGU_HEADS = 6
SGU_HEAD_DIM = 64
D_SGU = SGU_HEADS * SGU_HEAD_DIM
CHUNK = 128
D_IN = D_SSM + D_POOL + 2 * D_SGU
EPS = 1e-6

LANES = 128
SSM_BLOCK = 16
N_CLUSTERS = D_SSM // LANES
GROUPS_PER_CLUSTER = LANES // SSM_GROUP
CLUSTER_STATE = GROUPS_PER_CLUSTER * SSM_STATE
FLAT = SSM_BLOCK * LANES
GROUP_FLAT = SSM_BLOCK * SSM_GROUP
GROUP_PAIRS = GROUPS_PER_CLUSTER // 2

MXU_WIDTH = 256
BF16_SUBLANES = 16
VEC_ROWS = 8
VEC_GMIX, VEC_GFFN, VEC_SGU, VEC_GLU, VEC_GFINAL = 0, 1, 2, 3, 4
TOKEN_TILE = 512
MIX_IN_TILE = 1024
S5_ROW_TILE = 512
SUBLANES = 8
SCAN_SEGS = SUBLANES
SCAN_SEG_LEN = S5_ROW_TILE // SCAN_SEGS
SCAN_SEG_PITCH = SCAN_SEG_LEN + 4
SCAN_TILES = 2 * CLUSTER_STATE // LANES
BLOCK_PITCH = SSM_BLOCK + 4
FF_CHUNK = 3 * MXU_WIDTH
V7X_VMEM_BYTES = 64 * 1024 * 1024
VMEM_MIX_IN = VMEM_S5 = VMEM_MIX_OUT = V7X_VMEM_BYTES

F32 = jnp.float32
BF16 = jnp.bfloat16
NT_DIMS = (((1,), (1,)), ((), ()))


def _gelu(x):
    c = math.sqrt(2.0 / math.pi)
    return x * (0.5 + 0.5 * jnp.tanh(x * (c + (c * 0.044715) * (x * x))))


def _rms(x, g):
    ms = jnp.mean(x * x, axis=-1, keepdims=True)
    return x * lax.rsqrt(ms + EPS) * g


def _dot(a, b):
    return jnp.dot(a, b, preferred_element_type=F32)


def _dot_nt(a, b):
    return lax.dot_general(a, b, NT_DIMS, preferred_element_type=F32)


def _chunk_transpose(tiles):
    n = len(tiles)
    chunk = LANES // n
    v = list(tiles)
    cidx = lax.broadcasted_iota(jnp.int32, v[0].shape, 1) // chunk
    k = 1
    while k < n:
        keep = (cidx & k) == 0
        nv = list(v)
        for a in range(n):
            if a & k:
                continue
            lo, hi = v[a], v[a + k]
            nv[a] = jnp.where(keep, lo, pltpu.roll(hi, k * chunk, 1))
            nv[a + k] = jnp.where(keep, pltpu.roll(lo, LANES - k * chunk, 1), hi)
        v = nv
        k *= 2
    return v


def _cast_riders(kinds, src_refs, dst_refs):
    for kind, src, dst in zip(kinds, src_refs, dst_refs):
        if kind == "w_in":
            b_end, u_end = D_SSM + D_POOL, D_SSM + D_POOL + D_SGU
            dst[:, :D_SGU] = src[:, u_end:].astype(BF16)
            dst[:, D_SGU:2 * D_SGU] = src[:, b_end:u_end].astype(BF16)
            dst[:, 2 * D_SGU:] = src[:, :b_end].astype(BF16)
        else:
            dst[...] = src[...].astype(BF16)


def _rider_specs(n_steps, layer, w):
    rows, cols = w.shape[1:]
    share = 1
    while (rows * share) % n_steps or (rows * share // n_steps) % BF16_SUBLANES:
        share *= 2
    slab = rows * share // n_steps
    which = lambda i: jnp.minimum(i, n_steps - 1) // share
    src = pl.BlockSpec((None, slab, cols), lambda i: (layer, which(i), 0))
    dst = pl.BlockSpec((slab, cols), lambda i: (which(i), 0))
    return src, dst, jax.ShapeDtypeStruct((rows, cols), BF16)


def _mix_in_kernel(tiles_per_seq, rider_kinds, *refs):
    n_r = len(rider_kinds)
    x_ref, vec_ref, win_ref, wpool_ref, ws_ref, bsp_ref = refs[:6]
    rider_src = refs[6:6 + n_r]
    za_ref, yb_ref, yc_ref = refs[6 + n_r:9 + n_r]
    rider_dst = refs[9 + n_r:9 + 2 * n_r]
    halo_ref, zs_ref = refs[9 + 2 * n_r:]
    _cast_riders(rider_kinds, rider_src, rider_dst)

    tm = x_ref.shape[0]
    i = pl.program_id(0)
    seq_tile = i % tiles_per_seq
    gmix = vec_ref[VEC_GMIX:VEC_GMIX + 1, :]
    pscale = vec_ref[VEC_SGU:VEC_SGU + 1, :D_POOL]
    lng = vec_ref[VEC_SGU:VEC_SGU + 1, D_POOL:D_POOL + D_SGU]
    lnb = vec_ref[VEC_SGU:VEC_SGU + 1, D_POOL + D_SGU:]

    h = _rms(x_ref[...], gmix).astype(BF16)
    z_sgu = _dot(h, win_ref[:, :2 * D_SGU])
    z_rest = _dot(h, win_ref[:, 2 * D_SGU:])
    zv = z_sgu[:, :D_SGU]
    zu = z_sgu[:, D_SGU:]
    zb = z_rest[:, D_SSM:]

    n_blocks = tm // SSM_BLOCK
    for k in range(N_CLUSTERS):
        for b in range(n_blocks):
            zs_ref[k, b * BLOCK_PITCH:b * BLOCK_PITCH + SSM_BLOCK, :] = (
                z_rest[b * SSM_BLOCK:(b + 1) * SSM_BLOCK, k * LANES:(k + 1) * LANES])
    half_block = SSM_BLOCK // 2
    for k in range(N_CLUSTERS):
        for hf in range(2):
            slots = [zs_ref[k, pl.ds(hf * half_block + i, n_blocks, stride=BLOCK_PITCH), :]
                     for i in range(half_block)]
            for g, tile in enumerate(_chunk_transpose(slots)):
                col = g * GROUP_FLAT + hf * LANES
                za_ref[k, :, col:col + LANES] = tile.astype(BF16)

    halo = jnp.where(seq_tile == 0, 0.0, halo_ref[...])
    halo_ref[...] = zb[tm - MAX_WINDOW:, :]
    ext = jnp.concatenate([halo, zb], axis=0)
    lane = lax.broadcasted_iota(jnp.int32, (tm, LANES), 1)
    low = lane < POOL_GROUP
    low_h = lax.broadcasted_iota(jnp.int32, (MAX_WINDOW, LANES), 1) < POOL_GROUP
    low_r = lax.broadcasted_iota(jnp.int32, (1, LANES), 1) < POOL_GROUP
    pos1 = (seq_tile * tm + 1
            + lax.broadcasted_iota(jnp.int32, (MAX_WINDOW, LANES), 0)).astype(F32)

    def window_mean(total, w_low, w_high):
        cnt = jnp.where(low_h, jnp.minimum(pos1, w_low), jnp.minimum(pos1, w_high))
        inv_w = jnp.where(low_r, 1.0 / w_low, 1.0 / w_high)
        return jnp.concatenate([total[:MAX_WINDOW] / cnt, total[MAX_WINDOW:] * inv_w], axis=0)

    e0 = ext[:, :LANES]
    s2 = e0 + pltpu.roll(e0, 1, 0)
    s4 = s2 + pltpu.roll(s2, 2, 0)
    sum0 = jnp.where(low, s2[MAX_WINDOW:], s4[MAX_WINDOW:])
    e1 = ext[:, LANES:]
    t2 = e1 + pltpu.roll(e1, 1, 0)
    t4 = t2 + pltpu.roll(t2, 2, 0)
    t8 = t4 + pltpu.roll(t4, 4, 0)
    t16 = t8 + pltpu.roll(t8, 8, 0)
    sum1 = jnp.where(low, t8[MAX_WINDOW:], t16[MAX_WINDOW:])
    pooled = jnp.concatenate([window_mean(sum0, 2.0, 4.0), window_mean(sum1, 8.0, 16.0)],
                             axis=1) - zb
    yb = _dot(pooled.astype(BF16), wpool_ref[...]) * pscale
    yb_ref[...] = yb.astype(BF16)

    u = _gelu(zu)
    v = _gelu(zv)
    mu = jnp.mean(v, axis=-1, keepdims=True)
    vc = v - mu
    var = jnp.mean(vc * vc, axis=-1, keepdims=True)
    vn = (vc * lax.rsqrt(var + EPS) * lng + lnb).astype(BF16)
    lane_c = lax.broadcasted_iota(jnp.int32, (CHUNK, LANES), 1)
    low_c = lane_c < SGU_HEAD_DIM
    zero = jnp.zeros((), BF16)
    for c in range(0, tm // CHUNK, 2):
        rows_a = slice(c * CHUNK, (c + 1) * CHUNK)
        rows_b = slice((c + 1) * CHUNK, (c + 2) * CHUNK)
        parts_a, parts_b = [], []
        for p in range(SGU_HEADS // 2):
            va = vn[rows_a, p * LANES:(p + 1) * LANES]
            vb = vn[rows_b, p * LANES:(p + 1) * LANES]
            top = jnp.concatenate([jnp.where(low_c, va, zero), jnp.where(low_c, vb, zero)], axis=1)
            bot = jnp.concatenate([jnp.where(low_c, zero, va), jnp.where(low_c, zero, vb)], axis=1)
            out = _dot(ws_ref[p], jnp.concatenate([top, bot], axis=0))
            parts_a.append(out[:, :LANES])
            parts_b.append(out[:, LANES:])
        for rows, parts in ((rows_a, parts_a), (rows_b, parts_b)):
            mixed = jnp.concatenate(parts, axis=1) + bsp_ref[...]
            yc_ref[rows, :] = (u[rows, :] * mixed).astype(BF16)


def _layer_spec(layer, shape):
    zeros = (0,) * len(shape)
    return pl.BlockSpec((None,) + tuple(shape), lambda *_: (layer,) + zeros)


def _mix_in(layer, x2d, vecs, win, wpool_bd, ws, bsp, seq_len, riders=()):
    m = x2d.shape[0]
    tm = MIX_IN_TILE
    n_steps = m // tm
    spec = functools.partial(_layer_spec, layer)
    r_specs = [_rider_specs(n_steps, lyr, w) for _, w, lyr in riders]
    return pl.pallas_call(
        functools.partial(_mix_in_kernel, seq_len // tm, tuple(k for k, _, _ in riders)),
        grid=(n_steps,),
        in_specs=[
            pl.BlockSpec((tm, D_MODEL), lambda i: (i, 0)),
            spec((VEC_ROWS, D_MODEL)),
            pl.BlockSpec((D_MODEL, D_IN), lambda i: (0, 0)),
            spec((D_POOL, D_POOL)),
            spec((SGU_HEADS // 2, CHUNK, 2 * CHUNK)),
            spec((CHUNK, D_SGU)),
        ] + [s for s, _, _ in r_specs],
        out_specs=[
            pl.BlockSpec((N_CLUSTERS, tm // SSM_BLOCK, FLAT), lambda i: (0, i, 0)),
            pl.BlockSpec((tm, D_POOL), lambda i: (i, 0)),
            pl.BlockSpec((tm, D_SGU), lambda i: (i, 0)),
        ] + [d for _, d, _ in r_specs],
        out_shape=[
            jax.ShapeDtypeStruct((N_CLUSTERS, m // SSM_BLOCK, FLAT), BF16),
            jax.ShapeDtypeStruct((m, D_POOL), BF16),
            jax.ShapeDtypeStruct((m, D_SGU), BF16),
        ] + [o for _, _, o in r_specs],
        scratch_shapes=[pltpu.VMEM((MAX_WINDOW, D_POOL), F32),
                        pltpu.VMEM((N_CLUSTERS, tm // SSM_BLOCK * BLOCK_PITCH, LANES), F32)],
        compiler_params=pltpu.CompilerParams(
            dimension_semantics=("arbitrary",), vmem_limit_bytes=VMEM_MIX_IN),
        name="mix_in",
    )(x2d, vecs, win, wpool_bd, ws, bsp, *[w for _, w, _ in riders])


def _s5_build_operators(par_ref, bt_ref, ct_ref, dskip_ref, wend_ref, wrd_ref, lag0_ref, lagk_ref,
                        pw_ref):
    a_re = par_ref[0, 0:1, :]
    a_im = par_ref[0, 1:2, :]
    dt = jnp.exp(par_ref[0, 2:3, :])
    l_re = a_re * dt
    l_im = a_im * dt
    mag = jnp.exp(l_re)
    ar = mag * jnp.cos(l_im)
    ai = mag * jnp.sin(l_im)
    den = a_re * a_re + a_im * a_im
    f_re = ((ar - 1.0) * a_re + ai * a_im) / den
    f_im = (ai * a_re - (ar - 1.0) * a_im) / den
    bt_re, bt_im = bt_ref[0, 0], bt_ref[0, 1]
    bb_re = f_re * bt_re - f_im * bt_im
    bb_im = f_re * bt_im + f_im * bt_re
    ct_re, ct_im = ct_ref[0, 0], ct_ref[0, 1]

    lane_q = lax.broadcasted_iota(jnp.int32, (SSM_GROUP, LANES), 1) // SSM_STATE

    def power(l):
        m_l = jnp.exp(l * l_re)
        return m_l * jnp.cos(l * l_im), m_l * jnp.sin(l * l_im)

    def store_pairs(ref, slot, v_re, v_im):
        for p in range(GROUP_PAIRS):
            t_re = v_re[:, p * LANES:(p + 1) * LANES]
            t_im = v_im[:, p * LANES:(p + 1) * LANES]
            for q in range(2):
                rows = slice(q * GROUP_FLAT + slot * SSM_GROUP, q * GROUP_FLAT + (slot + 1) * SSM_GROUP)
                ref[p, rows, :LANES] = jnp.where(lane_q == q, t_re, 0.0).astype(BF16)
                ref[p, rows, LANES:] = jnp.where(lane_q == q, t_im, 0.0).astype(BF16)

    for s in range(SSM_BLOCK):
        p_re, p_im = power(float(SSM_BLOCK - 1 - s))
        store_pairs(wend_ref, s, p_re * bb_re - p_im * bb_im, p_re * bb_im + p_im * bb_re)
        q_re, q_im = power(float(s + 1))
        store_pairs(wrd_ref, s, q_re * ct_re - q_im * ct_im, -(q_re * ct_im + q_im * ct_re))
        r_re, r_im = power(float(s))
        store_pairs(lag0_ref, s, r_re * ct_re - r_im * ct_im, -(r_re * ct_im + r_im * ct_re))

    last = (SSM_BLOCK - 1) * SSM_GROUP
    col = lax.broadcasted_iota(jnp.int32, (SSM_GROUP, GROUP_FLAT), 1)
    row = lax.broadcasted_iota(jnp.int32, (SSM_GROUP, GROUP_FLAT), 0)
    for p in range(GROUP_PAIRS):
        bbar = jnp.concatenate([wend_ref[p, q * GROUP_FLAT + last:q * GROUP_FLAT + last + SSM_GROUP, :]
                                for q in range(2)], axis=0)
        lags = _dot_nt(bbar, lag0_ref[p])
        for q in range(2):
            g = 2 * p + q
            k_row = lags[q * SSM_GROUP:(q + 1) * SSM_GROUP, q * GROUP_FLAT:(q + 1) * GROUP_FLAT]
            d_g = dskip_ref[0, :, g * GROUP_FLAT:(g + 1) * GROUP_FLAT]
            k_row = k_row + jnp.where(col == row, d_g, 0.0)
            for s in range(SSM_BLOCK):
                shifted = pltpu.roll(k_row, s * SSM_GROUP, 1) if s else k_row
                lagk_ref[2 * p + q, s * SSM_GROUP:(s + 1) * SSM_GROUP, :] = jnp.where(
                    col >= s * SSM_GROUP, shifted, 0.0).astype(BF16)

    p_re, p_im = power(float(SSM_BLOCK))
    pw_ref[0:1, :] = jnp.concatenate([p_re, p_im], axis=1)
    for _ in range(SCAN_SEG_LEN.bit_length() - 1):
        p_re, p_im = p_re * p_re - p_im * p_im, 2.0 * (p_re * p_im)
    pw_ref[1:2, :] = jnp.concatenate([p_re, p_im], axis=1)


def _cmul_add(p_re, p_im, h_re, h_im, e_re, e_im):
    n_re = [pr * hr - pi * hi + er for pr, pi, hr, hi, er in zip(p_re, p_im, h_re, h_im, e_re)]
    n_im = [pr * hi + pi * hr + ei for pr, pi, hr, hi, ei in zip(p_re, p_im, h_re, h_im, e_im)]
    return n_re, n_im


def _s5_block_recurrence(st_ref, pw_ref, fillers):
    fillers = list(fillers)
    total_cost = sum(c for c, _ in fillers)
    issued_cost = [0.0]
    ticks = [0]

    def tick():
        while fillers and issued_cost[0] * 2 * SCAN_SEG_LEN <= ticks[0] * total_cost:
            cost, thunk = fillers.pop(0)
            issued_cost[0] += cost
            thunk()
        ticks[0] += 1

    half = SCAN_TILES // 2

    def bcast(row):
        return [jnp.broadcast_to(row[:, q * LANES:(q + 1) * LANES], (SCAN_SEGS, LANES))
                for q in range(half)]

    p_re, p_im = bcast(pw_ref[0:1, :CLUSTER_STATE]), bcast(pw_ref[0:1, CLUSTER_STATE:])
    seg_rows = lambda i: pl.ds(i, SCAN_SEGS, stride=SCAN_SEG_PITCH)

    def gather(i):
        e = [st_ref[q, seg_rows(i), :] for q in range(SCAN_TILES)]
        return e[:half], e[half:]

    zeros = [jnp.zeros((SCAN_SEGS, LANES), F32)] * half

    def end_step(i, h):
        e_re, e_im = gather(i)
        return _cmul_add(p_re, p_im, h[0], h[1], e_re, e_im)

    h = (zeros, zeros)
    for i in range(SCAN_SEG_LEN):
        tick()
        h = end_step(i, h)
    l_re, l_im = h

    s_re, s_im = bcast(pw_ref[1:2, :CLUSTER_STATE]), bcast(pw_ref[1:2, CLUSTER_STATE:])
    first = lax.broadcasted_iota(jnp.int32, (SCAN_SEGS, LANES), 0) == 0
    shift = lambda v: jnp.where(first, 0.0, pltpu.roll(v, 1, 0))
    c_re, c_im = zeros, zeros
    for _ in range(SCAN_SEGS - 1):
        n_re, n_im = _cmul_add(s_re, s_im, c_re, c_im, l_re, l_im)
        c_re, c_im = [shift(v) for v in n_re], [shift(v) for v in n_im]

    def scan_step(i, h):
        e_re, e_im = gather(i)
        for q in range(half):
            st_ref[q, seg_rows(i), :] = h[0][q]
            st_ref[half + q, seg_rows(i), :] = h[1][q]
        return _cmul_add(p_re, p_im, h[0], h[1], e_re, e_im)

    h = (c_re, c_im)
    for i in range(SCAN_SEG_LEN):
        tick()
        h = scan_step(i, h)
    while fillers:
        fillers.pop(0)[1]()


def _s5_kernel(u_ref, par_ref, bt_ref, ct_ref, dskip_ref, g_ref,
               wend_ref, wrd_ref, lag0_ref, lagk_ref, pw_ref, st_ref):
    r = pl.program_id(1)

    @pl.when(r == 0)
    def _():
        _s5_build_operators(par_ref, bt_ref, ct_ref, dskip_ref, wend_ref, wrd_ref, lag0_ref,
                            lagk_ref, pw_ref)

    u = u_ref[0]
    pair_cols = lambda p: slice(2 * p * GROUP_FLAT, 2 * (p + 1) * GROUP_FLAT)
    seg_rows = lambda j: slice(j * SCAN_SEG_PITCH, j * SCAN_SEG_PITCH + SCAN_SEG_LEN)

    for p in range(GROUP_PAIRS):
        e = _dot(u[:, pair_cols(p)], wend_ref[p])
        for tile, cols in ((p, slice(0, LANES)), (GROUP_PAIRS + p, slice(LANES, 2 * LANES))):
            for j in range(SCAN_SEGS):
                st_ref[tile, seg_rows(j), :] = e[j * SCAN_SEG_LEN:(j + 1) * SCAN_SEG_LEN, cols]

    ys = [None] * GROUPS_PER_CLUSTER

    def lag_product(g):
        ys[g] = _dot(u[:, g * GROUP_FLAT:(g + 1) * GROUP_FLAT], lagk_ref[g])

    _s5_block_recurrence(st_ref, pw_ref,
                         [(1, functools.partial(lag_product, g)) for g in range(len(ys))])
    y = jnp.concatenate(ys, axis=1)

    outs = []
    for p in range(GROUP_PAIRS):
        hprev = jnp.concatenate(
            [jnp.concatenate([st_ref[p, seg_rows(j), :], st_ref[GROUP_PAIRS + p, seg_rows(j), :]],
                             axis=1) for j in range(SCAN_SEGS)], axis=0)
        outs.append(_dot_nt(hprev.astype(BF16), wrd_ref[p]))
    g_ref[0] = _gelu(y + jnp.concatenate(outs, axis=1))


def _s5(layer, u_flat, par, bt, ct, dskip, blocks_per_seq):
    nb = u_flat.shape[1]
    rt = S5_ROW_TILE
    assert blocks_per_seq == rt and SCAN_SEG_LEN & (SCAN_SEG_LEN - 1) == 0
    return pl.pallas_call(
        _s5_kernel,
        grid=(N_CLUSTERS, nb // rt),
        in_specs=[
            pl.BlockSpec((1, rt, FLAT), lambda k, r: (k, r, 0)),
            pl.BlockSpec((None, 1, 3, CLUSTER_STATE), lambda k, r: (layer, k, 0, 0)),
            pl.BlockSpec((None, 1, 2, SSM_GROUP, CLUSTER_STATE), lambda k, r: (layer, k, 0, 0, 0)),
            pl.BlockSpec((None, 1, 2, SSM_GROUP, CLUSTER_STATE), lambda k, r: (layer, k, 0, 0, 0)),
            pl.BlockSpec((None, 1, 1, FLAT), lambda k, r: (layer, k, 0, 0)),
        ],
        out_specs=pl.BlockSpec((1, rt, FLAT), lambda k, r: (k, r, 0)),
        out_shape=jax.ShapeDtypeStruct((N_CLUSTERS, nb, FLAT), F32),
        scratch_shapes=[pltpu.VMEM((GROUP_PAIRS, 2 * GROUP_FLAT, 2 * LANES), BF16),
                        pltpu.VMEM((GROUP_PAIRS, 2 * GROUP_FLAT, 2 * LANES), BF16),
                        pltpu.VMEM((GROUP_PAIRS, 2 * GROUP_FLAT, 2 * LANES), BF16),
                        pltpu.VMEM((GROUPS_PER_CLUSTER, GROUP_FLAT, GROUP_FLAT), BF16),
                        pltpu.VMEM((2, 2 * CLUSTER_STATE), F32),
                        pltpu.VMEM((SCAN_TILES, SCAN_SEGS * SCAN_SEG_PITCH, LANES), F32)],
        compiler_params=pltpu.CompilerParams(
            dimension_semantics=("arbitrary", "arbitrary"), vmem_limit_bytes=VMEM_S5),
        name="s5",
    )(u_flat, par, bt, ct, dskip)


def _s5_params(A_re, A_im, log_dt, B_re, B_im, C_re, C_im, D_skip):
    depth = A_re.shape[0]
    k, gpc, n, c = N_CLUSTERS, GROUPS_PER_CLUSTER, SSM_STATE, SSM_GROUP
    rows = lambda a: a.reshape(depth, k, 1, CLUSTER_STATE)
    ldt = jnp.broadcast_to(log_dt[..., None], A_re.shape)
    par = jnp.concatenate([rows(A_re), rows(A_im), rows(ldt)], axis=2)
    b = jnp.stack([B_re, B_im], axis=1).reshape(depth, 2, k, gpc, n, c)
    bt = b.transpose(0, 2, 1, 5, 3, 4).reshape(depth, k, 2, c, CLUSTER_STATE)
    cc = jnp.stack([C_re, C_im], axis=1).reshape(depth, 2, k, gpc, c, n)
    ct = cc.transpose(0, 2, 1, 4, 3, 5).reshape(depth, k, 2, c, CLUSTER_STATE)
    dskip = jnp.broadcast_to(D_skip.reshape(depth, k, gpc, 1, c), (depth, k, gpc, SSM_BLOCK, c))
    return par, bt, ct, dskip.reshape(depth, k, 1, FLAT)


def _mix_out_kernel(final, rider_kinds, *refs):
    n_r = len(rider_kinds)
    first = refs[0:4]
    ahead = refs[4:8]
    vec_ref, wglu_ref, wout_ref, wg_ref, wu_ref, wd_ref = refs[8:14]
    rider_src = refs[14:14 + n_r]
    o_ref = refs[14 + n_r]
    rider_dst = refs[15 + n_r:15 + 2 * n_r]
    gs_ref, x1_ref, h_ref = refs[15 + 2 * n_r:]
    _cast_riders(rider_kinds, rider_src, rider_dst)

    bglu = vec_ref[VEC_GLU:VEC_GLU + 1, :D_SSM]
    gffn = vec_ref[VEC_GFFN:VEC_GFFN + 1, :]
    tm = o_ref.shape[0]
    half_block = SSM_BLOCK // 2

    def mix_tail(x_ref, g_ref, yb_ref, yc_ref):
        for k in range(N_CLUSTERS):
            for hf in range(2):
                groups = [g_ref[k, :, g * GROUP_FLAT + hf * LANES:g * GROUP_FLAT + (hf + 1) * LANES]
                          for g in range(GROUPS_PER_CLUSTER)]
                for i, tile in enumerate(_chunk_transpose(groups)):
                    rows = pl.ds(hf * half_block + i, tm // SSM_BLOCK, stride=BLOCK_PITCH)
                    gs_ref[k, rows, :] = tile
        g = jnp.concatenate(
            [jnp.concatenate([gs_ref[k, b * BLOCK_PITCH:b * BLOCK_PITCH + SSM_BLOCK, :]
                              for b in range(tm // SSM_BLOCK)], axis=0)
             for k in range(N_CLUSTERS)], axis=1)
        ya = g * jax.nn.sigmoid(_dot(g.astype(BF16), wglu_ref[...]) + bglu)
        ymix = jnp.concatenate([ya.astype(BF16), yb_ref[...], yc_ref[...]], axis=1)
        x1 = x_ref[...] + _dot(ymix, wout_ref[...])
        x1_ref[...] = x1
        h_ref[...] = _rms(x1, gffn).astype(BF16)

    @pl.when(pl.program_id(0) == 0)
    def _():
        mix_tail(*first)

    acc = x1_ref[...]
    h = h_ref[...]
    d_ff = wg_ref.shape[1]
    for n, c0 in enumerate(range(0, d_ff, FF_CHUNK)):
        c1 = min(c0 + FF_CHUNK, d_ff)
        gate = _dot(h, wg_ref[:, c0:c1])
        up = _dot(h, wu_ref[:, c0:c1])
        act = (gate * jax.nn.sigmoid(gate) * up).astype(BF16)
        acc = acc + _dot(act, wd_ref[c0:c1, :])
        if n == 0:
            mix_tail(*ahead)
    if final:
        acc = _rms(acc, vec_ref[VEC_GFINAL:VEC_GFINAL + 1, :])
    o_ref[...] = acc


def _mix_out(layer, x2d, g3, yb, yc, vecs, wglu, wout, wg, wu, wd, final, riders=()):
    m = x2d.shape[0]
    tm = TOKEN_TILE
    n_steps = m // tm
    d_ff = wg.shape[-1]
    resident = lambda shape: pl.BlockSpec(shape, lambda i: (0, 0), pipeline_mode=pl.Buffered(1))

    def tile0(shape):
        return pl.BlockSpec(shape, lambda i: (0,) * len(shape), pipeline_mode=pl.Buffered(1))

    def ahead(shape, axis=0):
        nxt = lambda i: jnp.minimum(i + 1, n_steps - 1)
        return pl.BlockSpec(shape, lambda i: tuple(nxt(i) if d == axis else 0
                                                   for d in range(len(shape))))

    r_specs = [_rider_specs(n_steps, lyr, w) for _, w, lyr in riders]
    return pl.pallas_call(
        functools.partial(_mix_out_kernel, final, tuple(k for k, _, _ in riders)),
        grid=(n_steps,),
        in_specs=[
            tile0((tm, D_MODEL)), tile0((N_CLUSTERS, tm // SSM_BLOCK, FLAT)),
            tile0((tm, D_POOL)), tile0((tm, D_SGU)),
            ahead((tm, D_MODEL)), ahead((N_CLUSTERS, tm // SSM_BLOCK, FLAT), axis=1),
            ahead((tm, D_POOL)), ahead((tm, D_SGU)),
            _layer_spec(layer, (VEC_ROWS, D_MODEL)),
            _layer_spec(layer, (D_SSM, D_SSM)),
            resident((D_MODEL, D_MODEL)),
            resident((D_MODEL, d_ff)),
            resident((D_MODEL, d_ff)),
            resident((d_ff, D_MODEL)),
        ] + [s for s, _, _ in r_specs],
        out_specs=[pl.BlockSpec((tm, D_MODEL), lambda i: (i, 0))] + [d for _, d, _ in r_specs],
        out_shape=[jax.ShapeDtypeStruct((m, D_MODEL), F32)] + [o for _, _, o in r_specs],
        scratch_shapes=[pltpu.VMEM((N_CLUSTERS, tm // SSM_BLOCK * BLOCK_PITCH, LANES), F32),
                        pltpu.VMEM((tm, D_MODEL), F32),
                        pltpu.VMEM((tm, D_MODEL), BF16)],
        compiler_params=pltpu.CompilerParams(
            dimension_semantics=("arbitrary",), vmem_limit_bytes=VMEM_MIX_OUT),
        name="mix_out",
    )(x2d, g3, yb, yc, x2d, g3, yb, yc, vecs, wglu, wout, wg, wu, wd, *[w for _, w, _ in riders])


def kernel(x, g_mix, w_in, A_re, A_im, log_dt, B_re, B_im, C_re, C_im, D_skip, w_glu, b_glu,
           w_pool, pool_scale, sgu_ln_g, sgu_ln_b, w_spatial, b_spatial, w_out, g_ffn,
           w_gate, w_up, w_down, g_final):
    bsz, seq, d = x.shape
    depth = w_in.shape[0]
    m = bsz * seq
    assert d == D_MODEL and seq % MIX_IN_TILE == 0 and MIX_IN_TILE % (2 * CHUNK) == 0
    assert m % TOKEN_TILE == 0
    assert (seq // SSM_BLOCK) % S5_ROW_TILE == 0
    x2d = x.reshape(m, D_MODEL)
    tril = jnp.tril(jnp.ones((CHUNK, CHUNK), dtype=bool))
    eye = jnp.eye(len(POOL_WINDOWS), dtype=F32)
    wpool_bd = jnp.einsum('dgij,gh->dgihj', w_pool, eye).reshape(depth, D_POOL, D_POOL).astype(BF16)
    ws = jnp.where(tril, w_spatial, 0.0).astype(BF16)
    ws = jnp.concatenate([ws[:, 0::2], ws[:, 1::2]], axis=-1)
    bsp = jnp.repeat(jnp.swapaxes(b_spatial, 1, 2), SGU_HEAD_DIM, axis=2)
    par, bt, ct, dskip = _s5_params(A_re, A_im, log_dt, B_re, B_im, C_re, C_im, D_skip)
    vec_rows = [None] * VEC_ROWS
    vec_rows[VEC_GMIX] = g_mix
    vec_rows[VEC_GFFN] = g_ffn
    vec_rows[VEC_SGU] = jnp.concatenate([pool_scale, sgu_ln_g, sgu_ln_b], axis=-1)
    vec_rows[VEC_GLU] = jnp.pad(b_glu, ((0, 0), (0, D_MODEL - D_SSM)))
    vec_rows[VEC_GFINAL] = jnp.broadcast_to(g_final, (depth, D_MODEL))
    zero_row = jnp.zeros((depth, D_MODEL), F32)
    vecs = jnp.stack([zero_row if r is None else r for r in vec_rows], axis=1)
    wglu = w_glu.astype(BF16)
    b_end, u_end = D_SSM + D_POOL, D_SSM + D_POOL + D_SGU
    win = jnp.concatenate([w_in[0, :, u_end:], w_in[0, :, b_end:u_end], w_in[0, :, :b_end]],
                          axis=-1).astype(BF16)
    ffn_stacks = (w_out, w_gate, w_up, w_down)
    ffn = None
    for l in range(depth):
        riders = tuple(("plain", w, 0) for w in ffn_stacks) if l == 0 else ()
        u_flat, yb, yc, *cast = _mix_in(l, x2d, vecs, win, wpool_bd, ws, bsp, seq, riders)
        if l == 0:
            ffn = cast
        g3 = _s5(l, u_flat, par, bt, ct, dskip, seq // SSM_BLOCK)
        last = l == depth - 1
        riders = () if last else (("w_in", w_in, l + 1),) + tuple(
            ("plain", w, l + 1) for w in ffn_stacks)
        x2d, *cast = _mix_out(l, x2d, g3, yb, yc, vecs, wglu, *ffn, final=last, riders=riders)
        if not last:
            win, ffn = cast[0], cast[1:]
    return x2d.reshape(bsz, seq, D_MODEL)
```

```python
import functools
import math

import jax
import jax.numpy as jnp
from jax import lax
from jax.experimental import pallas as pl
from jax.experimental.pallas import tpu as pltpu

D_MODEL = 1024
D_SSM = 384
SSM_GROUP = 16
N_SSM_GROUPS = D_SSM // SSM_GROUP
SSM_STATE = 64
POOL_WINDOWS = (2, 4, 8, 16)
POOL_GROUP = 64
D_POOL = len(POOL_WINDOWS) * POOL_GROUP
MAX_WINDOW = max(POOL_WINDOWS)
SGU_HEADS = 6
SGU_HEAD_DIM = 64
D_SGU = SGU_HEADS * SGU_HEAD_DIM
CHUNK = 128
D_IN = D_SSM + D_POOL + 2 * D_SGU
EPS = 1e-6

LANES = 128
SSM_BLOCK = 16
N_CLUSTERS = D_SSM // LANES
GROUPS_PER_CLUSTER = LANES // SSM_GROUP
CLUSTER_STATE = GROUPS_PER_CLUSTER * SSM_STATE
FLAT = SSM_BLOCK * LANES
GROUP_FLAT = SSM_BLOCK * SSM_GROUP
GROUP_PAIRS = GROUPS_PER_CLUSTER // 2

MXU_WIDTH = 256
BF16_SUBLANES = 16
VEC_ROWS = 8
VEC_GMIX, VEC_GFFN, VEC_SGU, VEC_GLU, VEC_GFINAL = 0, 1, 2, 3, 4
TOKEN_TILE = 512
MIX_IN_TILE = 1024
S5_ROW_TILE = 512
SUBLANES = 8
SCAN_SEGS = SUBLANES
SCAN_SEG_LEN = S5_ROW_TILE // SCAN_SEGS
PITCH_PAD = 4
SCAN_SEG_PITCH = SCAN_SEG_LEN + PITCH_PAD
SCAN_TILES = 2 * CLUSTER_STATE // LANES
BLOCK_PITCH = SSM_BLOCK + PITCH_PAD
FF_CHUNK = 3 * MXU_WIDTH
V7X_VMEM_BYTES = 64 * 1024 * 1024
VMEM_MIX_IN = VMEM_S5 = VMEM_MIX_OUT = V7X_VMEM_BYTES

F32 = jnp.float32
BF16 = jnp.bfloat16
NT_DIMS = (((1,), (1,)), ((), ()))


def _gelu(x):
    c = math.sqrt(2.0 / math.pi)
    return x * (0.5 + 0.5 * jnp.tanh(x * (c + (c * 0.044715) * (x * x))))


def _rms(x, g):
    ms = jnp.mean(x * x, axis=-1, keepdims=True)
    return x * lax.rsqrt(ms + EPS) * g


def _dot(a, b):
    return jnp.dot(a, b, preferred_element_type=F32)


def _dot_nt(a, b):
    return lax.dot_general(a, b, NT_DIMS, preferred_element_type=F32)


def _chunk_transpose(tiles):
    n = len(tiles)
    chunk = LANES // n
    v = list(tiles)
    cidx = lax.broadcasted_iota(jnp.int32, v[0].shape, 1) // chunk
    k = 1
    while k < n:
        keep = (cidx & k) == 0
        nv = list(v)
        for a in range(n):
            if a & k:
                continue
            lo, hi = v[a], v[a + k]
            nv[a] = jnp.where(keep, lo, pltpu.roll(hi, k * chunk, 1))
            nv[a + k] = jnp.where(keep, pltpu.roll(lo, LANES - k * chunk, 1), hi)
        v = nv
        k *= 2
    return v


def _cast_riders(kinds, src_refs, dst_refs):
    for kind, src, dst in zip(kinds, src_refs, dst_refs):
        if kind == "w_in":
            b_end, u_end = D_SSM + D_POOL, D_SSM + D_POOL + D_SGU
            dst[:, :D_SGU] = src[:, u_end:].astype(BF16)
            dst[:, D_SGU:2 * D_SGU] = src[:, b_end:u_end].astype(BF16)
            dst[:, 2 * D_SGU:] = src[:, :b_end].astype(BF16)
        else:
            dst[...] = src[...].astype(BF16)


def _rider_specs(n_steps, layer, w):
    rows, cols = w.shape[1:]
    share = 1
    while (rows * share) % n_steps or (rows * share // n_steps) % BF16_SUBLANES:
        share *= 2
    slab = rows * share // n_steps
    which = lambda i: jnp.minimum(i, n_steps - 1) // share
    src = pl.BlockSpec((None, slab, cols), lambda i: (layer, which(i), 0))
    dst = pl.BlockSpec((slab, cols), lambda i: (which(i), 0))
    return src, dst, jax.ShapeDtypeStruct((rows, cols), BF16)


def _mix_in_kernel(tiles_per_seq, rider_kinds, *refs):
    n_r = len(rider_kinds)
    x_ref, vec_ref, win_ref, wpool_ref, ws_ref, bsp_ref = refs[:6]
    rider_src = refs[6:6 + n_r]
    za_ref, yb_ref, yc_ref = refs[6 + n_r:9 + n_r]
    rider_dst = refs[9 + n_r:9 + 2 * n_r]
    halo_ref, zs_ref = refs[9 + 2 * n_r:]
    _cast_riders(rider_kinds, rider_src, rider_dst)

    tm = x_ref.shape[0]
    i = pl.program_id(0)
    seq_tile = i % tiles_per_seq
    gmix = vec_ref[VEC_GMIX:VEC_GMIX + 1, :]
    pscale = vec_ref[VEC_SGU:VEC_SGU + 1, :D_POOL]
    lng = vec_ref[VEC_SGU:VEC_SGU + 1, D_POOL:D_POOL + D_SGU]
    lnb = vec_ref[VEC_SGU:VEC_SGU + 1, D_POOL + D_SGU:]

    h = _rms(x_ref[...], gmix).astype(BF16)
    z_sgu = _dot(h, win_ref[:, :2 * D_SGU])
    z_rest = _dot(h, win_ref[:, 2 * D_SGU:])
    zv = z_sgu[:, :D_SGU]
    zu = z_sgu[:, D_SGU:]
    zb = z_rest[:, D_SSM:]

    n_blocks = tm // SSM_BLOCK
    for k in range(N_CLUSTERS):
        for b in range(n_blocks):
            zs_ref[k, b * BLOCK_PITCH:b * BLOCK_PITCH + SSM_BLOCK, :] = (
                z_rest[b * SSM_BLOCK:(b + 1) * SSM_BLOCK, k * LANES:(k + 1) * LANES])
    half_block = SSM_BLOCK // 2
    for k in range(N_CLUSTERS):
        for hf in range(2):
            slots = [zs_ref[k, pl.ds(hf * half_block + i, n_blocks, stride=BLOCK_PITCH), :]
                     for i in range(half_block)]
            for g, tile in enumerate(_chunk_transpose(slots)):
                col = g * GROUP_FLAT + hf * LANES
                za_ref[k, :, col:col + LANES] = tile.astype(BF16)

    halo = jnp.where(seq_tile == 0, 0.0, halo_ref[...])
    halo_ref[...] = zb[tm - MAX_WINDOW:, :]
    ext = jnp.concatenate([halo, zb], axis=0)
    lane = lax.broadcasted_iota(jnp.int32, (tm, LANES), 1)
    low = lane < POOL_GROUP
    low_h = lax.broadcasted_iota(jnp.int32, (MAX_WINDOW, LANES), 1) < POOL_GROUP
    low_r = lax.broadcasted_iota(jnp.int32, (1, LANES), 1) < POOL_GROUP
    pos1 = (seq_tile * tm + 1
            + lax.broadcasted_iota(jnp.int32, (MAX_WINDOW, LANES), 0)).astype(F32)

    def window_mean(total, w_low, w_high):
        cnt = jnp.where(low_h, jnp.minimum(pos1, w_low), jnp.minimum(pos1, w_high))
        inv_w = jnp.where(low_r, 1.0 / w_low, 1.0 / w_high)
        return jnp.concatenate([total[:MAX_WINDOW] / cnt, total[MAX_WINDOW:] * inv_w], axis=0)

    e0 = ext[:, :LANES]
    s2 = e0 + pltpu.roll(e0, 1, 0)
    s4 = s2 + pltpu.roll(s2, 2, 0)
    sum0 = jnp.where(low, s2[MAX_WINDOW:], s4[MAX_WINDOW:])
    e1 = ext[:, LANES:]
    t2 = e1 + pltpu.roll(e1, 1, 0)
    t4 = t2 + pltpu.roll(t2, 2, 0)
    t8 = t4 + pltpu.roll(t4, 4, 0)
    t16 = t8 + pltpu.roll(t8, 8, 0)
    sum1 = jnp.where(low, t8[MAX_WINDOW:], t16[MAX_WINDOW:])
    pooled = jnp.concatenate([window_mean(sum0, 2.0, 4.0), window_mean(sum1, 8.0, 16.0)],
                             axis=1) - zb
    yb = _dot(pooled.astype(BF16), wpool_ref[...]) * pscale
    yb_ref[...] = yb.astype(BF16)

    u = _gelu(zu)
    v = _gelu(zv)
    mu = jnp.mean(v, axis=-1, keepdims=True)
    vc = v - mu
    var = jnp.mean(vc * vc, axis=-1, keepdims=True)
    vn = (vc * lax.rsqrt(var + EPS) * lng + lnb).astype(BF16)
    lane_c = lax.broadcasted_iota(jnp.int32, (CHUNK, LANES), 1)
    low_c = lane_c < SGU_HEAD_DIM
    zero = jnp.zeros((), BF16)
    for c in range(0, tm // CHUNK, 2):
        rows_a = slice(c * CHUNK, (c + 1) * CHUNK)
        rows_b = slice((c + 1) * CHUNK, (c + 2) * CHUNK)
        parts_a, parts_b = [], []
        for p in range(SGU_HEADS // 2):
            va = vn[rows_a, p * LANES:(p + 1) * LANES]
            vb = vn[rows_b, p * LANES:(p + 1) * LANES]
            top = jnp.concatenate([jnp.where(low_c, va, zero), jnp.where(low_c, vb, zero)], axis=1)
            bot = jnp.concatenate([jnp.where(low_c, zero, va), jnp.where(low_c, zero, vb)], axis=1)
            out = _dot(ws_ref[p], jnp.concatenate([top, bot], axis=0))
            parts_a.append(out[:, :LANES])
            parts_b.append(out[:, LANES:])
        for rows, parts in ((rows_a, parts_a), (rows_b, parts_b)):
            mixed = jnp.concatenate(parts, axis=1) + bsp_ref[...]
            yc_ref[rows, :] = (u[rows, :] * mixed).astype(BF16)


def _layer_spec(layer, shape):
    zeros = (0,) * len(shape)
    return pl.BlockSpec((None,) + tuple(shape), lambda *_: (layer,) + zeros)


def _mix_in(layer, x2d, vecs, win, wpool_bd, ws, bsp, seq_len, riders=()):
    m = x2d.shape[0]
    tm = MIX_IN_TILE
    n_steps = m // tm
    spec = functools.partial(_layer_spec, layer)
    r_specs = [_rider_specs(n_steps, lyr, w) for _, w, lyr in riders]
    return pl.pallas_call(
        functools.partial(_mix_in_kernel, seq_len // tm, tuple(k for k, _, _ in riders)),
        grid=(n_steps,),
        in_specs=[
            pl.BlockSpec((tm, D_MODEL), lambda i: (i, 0)),
            spec((VEC_ROWS, D_MODEL)),
            pl.BlockSpec((D_MODEL, D_IN), lambda i: (0, 0)),
            spec((D_POOL, D_POOL)),
            spec((SGU_HEADS // 2, CHUNK, 2 * CHUNK)),
            spec((CHUNK, D_SGU)),
        ] + [s for s, _, _ in r_specs],
        out_specs=[
            pl.BlockSpec((N_CLUSTERS, tm // SSM_BLOCK, FLAT), lambda i: (0, i, 0)),
            pl.BlockSpec((tm, D_POOL), lambda i: (i, 0)),
            pl.BlockSpec((tm, D_SGU), lambda i: (i, 0)),
        ] + [d for _, d, _ in r_specs],
        out_shape=[
            jax.ShapeDtypeStruct((N_CLUSTERS, m // SSM_BLOCK, FLAT), BF16),
            jax.ShapeDtypeStruct((m, D_POOL), BF16),
            jax.ShapeDtypeStruct((m, D_SGU), BF16),
        ] + [o for _, _, o in r_specs],
        scratch_shapes=[pltpu.VMEM((MAX_WINDOW, D_POOL), F32),
                        pltpu.VMEM((N_CLUSTERS, tm // SSM_BLOCK * BLOCK_PITCH, LANES), F32)],
        compiler_params=pltpu.CompilerParams(
            dimension_semantics=("arbitrary",), vmem_limit_bytes=VMEM_MIX_IN),
        name="mix_in",
    )(x2d, vecs, win, wpool_bd, ws, bsp, *[w for _, w, _ in riders])


def _s5_build_operators(par_ref, bt_ref, ct_ref, dskip_ref, wend_ref, wrd_ref, lag0_ref, lagk_ref,
                        pw_ref):
    a_re = par_ref[0, 0:1, :]
    a_im = par_ref[0, 1:2, :]
    dt = jnp.exp(par_ref[0, 2:3, :])
    l_re = a_re * dt
    l_im = a_im * dt
    mag = jnp.exp(l_re)
    ar = mag * jnp.cos(l_im)
    ai = mag * jnp.sin(l_im)
    den = a_re * a_re + a_im * a_im
    f_re = ((ar - 1.0) * a_re + ai * a_im) / den
    f_im = (ai * a_re - (ar - 1.0) * a_im) / den
    bt_re, bt_im = bt_ref[0, 0], bt_ref[0, 1]
    bb_re = f_re * bt_re - f_im * bt_im
    bb_im = f_re * bt_im + f_im * bt_re
    ct_re, ct_im = ct_ref[0, 0], ct_ref[0, 1]

    lane_q = lax.broadcasted_iota(jnp.int32, (SSM_GROUP, LANES), 1) // SSM_STATE

    def power(l):
        m_l = jnp.exp(l * l_re)
        return m_l * jnp.cos(l * l_im), m_l * jnp.sin(l * l_im)

    def store_pairs(ref, slot, v_re, v_im):
        for p in range(GROUP_PAIRS):
            t_re = v_re[:, p * LANES:(p + 1) * LANES]
            t_im = v_im[:, p * LANES:(p + 1) * LANES]
            for q in range(2):
                rows = slice(q * GROUP_FLAT + slot * SSM_GROUP, q * GROUP_FLAT + (slot + 1) * SSM_GROUP)
                ref[p, rows, :LANES] = jnp.where(lane_q == q, t_re, 0.0).astype(BF16)
                ref[p, rows, LANES:] = jnp.where(lane_q == q, t_im, 0.0).astype(BF16)

    for s in range(SSM_BLOCK):
        p_re, p_im = power(float(SSM_BLOCK - 1 - s))
        store_pairs(wend_ref, s, p_re * bb_re - p_im * bb_im, p_re * bb_im + p_im * bb_re)
        q_re, q_im = power(float(s + 1))
        store_pairs(wrd_ref, s, q_re * ct_re - q_im * ct_im, -(q_re * ct_im + q_im * ct_re))
        r_re, r_im = power(float(s))
        store_pairs(lag0_ref, s, r_re * ct_re - r_im * ct_im, -(r_re * ct_im + r_im * ct_re))

    last = (SSM_BLOCK - 1) * SSM_GROUP
    col = lax.broadcasted_iota(jnp.int32, (SSM_GROUP, GROUP_FLAT), 1)
    row = lax.broadcasted_iota(jnp.int32, (SSM_GROUP, GROUP_FLAT), 0)
    for p in range(GROUP_PAIRS):
        bbar = jnp.concatenate([wend_ref[p, q * GROUP_FLAT + last:q * GROUP_FLAT + last + SSM_GROUP, :]
                                for q in range(2)], axis=0)
        lags = _dot_nt(bbar, lag0_ref[p])
        for q in range(2):
            g = 2 * p + q
            k_row = lags[q * SSM_GROUP:(q + 1) * SSM_GROUP, q * GROUP_FLAT:(q + 1) * GROUP_FLAT]
            d_g = dskip_ref[0, :, g * GROUP_FLAT:(g + 1) * GROUP_FLAT]
            k_row = k_row + jnp.where(col == row, d_g, 0.0)
            for s in range(SSM_BLOCK):
                shifted = pltpu.roll(k_row, s * SSM_GROUP, 1) if s else k_row
                lagk_ref[2 * p + q, s * SSM_GROUP:(s + 1) * SSM_GROUP, :] = jnp.where(
                    col >= s * SSM_GROUP, shifted, 0.0).astype(BF16)

    p_re, p_im = power(float(SSM_BLOCK))
    pw_ref[0:1, :] = jnp.concatenate([p_re, p_im], axis=1)
    for _ in range(SCAN_SEG_LEN.bit_length() - 1):
        p_re, p_im = p_re * p_re - p_im * p_im, 2.0 * (p_re * p_im)
    pw_ref[1:2, :] = jnp.concatenate([p_re, p_im], axis=1)


def _cmul_add(p_re, p_im, h_re, h_im, e_re, e_im):
    n_re = [pr * hr - pi * hi + er for pr, pi, hr, hi, er in zip(p_re, p_im, h_re, h_im, e_re)]
    n_im = [pr * hi + pi * hr + ei for pr, pi, hr, hi, ei in zip(p_re, p_im, h_re, h_im, e_im)]
    return n_re, n_im


def _s5_block_recurrence(st_ref, pw_ref, fillers):
    fillers = list(fillers)
    total_cost = sum(c for c, _ in fillers)
    issued_cost = [0.0]
    ticks = [0]

    def tick():
        while fillers and issued_cost[0] * 2 * SCAN_SEG_LEN <= ticks[0] * total_cost:
            cost, thunk = fillers.pop(0)
            issued_cost[0] += cost
            thunk()
        ticks[0] += 1

    half = SCAN_TILES // 2

    def bcast(row):
        return [jnp.broadcast_to(row[:, q * LANES:(q + 1) * LANES], (SCAN_SEGS, LANES))
                for q in range(half)]

    p_re, p_im = bcast(pw_ref[0:1, :CLUSTER_STATE]), bcast(pw_ref[0:1, CLUSTER_STATE:])
    seg_rows = lambda i: pl.ds(i, SCAN_SEGS, stride=SCAN_SEG_PITCH)

    def gather(i):
        e = [st_ref[q, seg_rows(i), :] for q in range(SCAN_TILES)]
        return e[:half], e[half:]

    zeros = [jnp.zeros((SCAN_SEGS, LANES), F32)] * half

    def end_step(i, h):
        e_re, e_im = gather(i)
        return _cmul_add(p_re, p_im, h[0], h[1], e_re, e_im)

    h = (zeros, zeros)
    for i in range(SCAN_SEG_LEN):
        tick()
        h = end_step(i, h)
    l_re, l_im = h

    s_re, s_im = bcast(pw_ref[1:2, :CLUSTER_STATE]), bcast(pw_ref[1:2, CLUSTER_STATE:])
    first = lax.broadcasted_iota(jnp.int32, (SCAN_SEGS, LANES), 0) == 0
    shift = lambda v: jnp.where(first, 0.0, pltpu.roll(v, 1, 0))
    c_re, c_im = zeros, zeros
    for _ in range(SCAN_SEGS - 1):
        n_re, n_im = _cmul_add(s_re, s_im, c_re, c_im, l_re, l_im)
        c_re, c_im = [shift(v) for v in n_re], [shift(v) for v in n_im]

    def scan_step(i, h):
        e_re, e_im = gather(i)
        for q in range(half):
            st_ref[q, seg_rows(i), :] = h[0][q]
            st_ref[half + q, seg_rows(i), :] = h[1][q]
        return _cmul_add(p_re, p_im, h[0], h[1], e_re, e_im)

    h = (c_re, c_im)
    for i in range(SCAN_SEG_LEN):
        tick()
        h = scan_step(i, h)
    while fillers:
        fillers.pop(0)[1]()


def _s5_kernel(u_ref, par_ref, bt_ref, ct_ref, dskip_ref, g_ref,
               wend_ref, wrd_ref, lag0_ref, lagk_ref, pw_ref, st_ref):
    r = pl.program_id(1)

    @pl.when(r == 0)
    def _():
        _s5_build_operators(par_ref, bt_ref, ct_ref, dskip_ref, wend_ref, wrd_ref, lag0_ref,
                            lagk_ref, pw_ref)

    u = u_ref[0]
    pair_cols = lambda p: slice(2 * p * GROUP_FLAT, 2 * (p + 1) * GROUP_FLAT)
    seg_rows = lambda j: slice(j * SCAN_SEG_PITCH, j * SCAN_SEG_PITCH + SCAN_SEG_LEN)

    for p in range(GROUP_PAIRS):
        e = _dot(u[:, pair_cols(p)], wend_ref[p])
        for tile, cols in ((p, slice(0, LANES)), (GROUP_PAIRS + p, slice(LANES, 2 * LANES))):
            for j in range(SCAN_SEGS):
                st_ref[tile, seg_rows(j), :] = e[j * SCAN_SEG_LEN:(j + 1) * SCAN_SEG_LEN, cols]

    ys = [None] * GROUPS_PER_CLUSTER

    def lag_product(g):
        ys[g] = _dot(u[:, g * GROUP_FLAT:(g + 1) * GROUP_FLAT], lagk_ref[g])

    _s5_block_recurrence(st_ref, pw_ref,
                         [(1, functools.partial(lag_product, g)) for g in range(len(ys))])
    y = jnp.concatenate(ys, axis=1)

    outs = []
    for p in range(GROUP_PAIRS):
        hprev = jnp.concatenate(
            [jnp.concatenate([st_ref[p, seg_rows(j), :], st_ref[GROUP_PAIRS + p, seg_rows(j), :]],
                             axis=1) for j in range(SCAN_SEGS)], axis=0)
        outs.append(_dot_nt(hprev.astype(BF16), wrd_ref[p]))
    g_ref[0] = _gelu(y + jnp.concatenate(outs, axis=1))


def _s5(layer, u_flat, par, bt, ct, dskip, blocks_per_seq):
    nb = u_flat.shape[1]
    rt = S5_ROW_TILE
    assert blocks_per_seq == rt and SCAN_SEG_LEN & (SCAN_SEG_LEN - 1) == 0
    return pl.pallas_call(
        _s5_kernel,
        grid=(N_CLUSTERS, nb // rt),
        in_specs=[
            pl.BlockSpec((1, rt, FLAT), lambda k, r: (k, r, 0)),
            pl.BlockSpec((None, 1, 3, CLUSTER_STATE), lambda k, r: (layer, k, 0, 0)),
            pl.BlockSpec((None, 1, 2, SSM_GROUP, CLUSTER_STATE), lambda k, r: (layer, k, 0, 0, 0)),
            pl.BlockSpec((None, 1, 2, SSM_GROUP, CLUSTER_STATE), lambda k, r: (layer, k, 0, 0, 0)),
            pl.BlockSpec((None, 1, 1, FLAT), lambda k, r: (layer, k, 0, 0)),
        ],
        out_specs=pl.BlockSpec((1, rt, FLAT), lambda k, r: (k, r, 0)),
        out_shape=jax.ShapeDtypeStruct((N_CLUSTERS, nb, FLAT), F32),
        scratch_shapes=[pltpu.VMEM((GROUP_PAIRS, 2 * GROUP_FLAT, 2 * LANES), BF16),
                        pltpu.VMEM((GROUP_PAIRS, 2 * GROUP_FLAT, 2 * LANES), BF16),
                        pltpu.VMEM((GROUP_PAIRS, 2 * GROUP_FLAT, 2 * LANES), BF16),
                        pltpu.VMEM((GROUPS_PER_CLUSTER, GROUP_FLAT, GROUP_FLAT), BF16),
                        pltpu.VMEM((2, 2 * CLUSTER_STATE), F32),
                        pltpu.VMEM((SCAN_TILES, SCAN_SEGS * SCAN_SEG_PITCH, LANES), F32)],
        compiler_params=pltpu.CompilerParams(
            dimension_semantics=("arbitrary", "arbitrary"), vmem_limit_bytes=VMEM_S5),
        name="s5",
    )(u_flat, par, bt, ct, dskip)


def _s5_params(A_re, A_im, log_dt, B_re, B_im, C_re, C_im, D_skip):
    depth = A_re.shape[0]
    k, gpc, n, c = N_CLUSTERS, GROUPS_PER_CLUSTER, SSM_STATE, SSM_GROUP
    rows = lambda a: a.reshape(depth, k, 1, CLUSTER_STATE)
    ldt = jnp.broadcast_to(log_dt[..., None], A_re.shape)
    par = jnp.concatenate([rows(A_re), rows(A_im), rows(ldt)], axis=2)
    b = jnp.stack([B_re, B_im], axis=1).reshape(depth, 2, k, gpc, n, c)
    bt = b.transpose(0, 2, 1, 5, 3, 4).reshape(depth, k, 2, c, CLUSTER_STATE)
    cc = jnp.stack([C_re, C_im], axis=1).reshape(depth, 2, k, gpc, c, n)
    ct = cc.transpose(0, 2, 1, 4, 3, 5).reshape(depth, k, 2, c, CLUSTER_STATE)
    dskip = jnp.broadcast_to(D_skip.reshape(depth, k, gpc, 1, c), (depth, k, gpc, SSM_BLOCK, c))
    return par, bt, ct, dskip.reshape(depth, k, 1, FLAT)


def _mix_out_kernel(final, rider_kinds, *refs):
    n_r = len(rider_kinds)
    first = refs[0:4]
    ahead = refs[4:8]
    vec_ref, wglu_ref, wout_ref, wg_ref, wu_ref, wd_ref = refs[8:14]
    rider_src = refs[14:14 + n_r]
    o_ref = refs[14 + n_r]
    rider_dst = refs[15 + n_r:15 + 2 * n_r]
    gs_ref, x1_ref, h_ref = refs[15 + 2 * n_r:]
    _cast_riders(rider_kinds, rider_src, rider_dst)

    bglu = vec_ref[VEC_GLU:VEC_GLU + 1, :D_SSM]
    gffn = vec_ref[VEC_GFFN:VEC_GFFN + 1, :]
    tm = o_ref.shape[0]
    half_block = SSM_BLOCK // 2

    def mix_tail(x_ref, g_ref, yb_ref, yc_ref):
        for k in range(N_CLUSTERS):
            for hf in range(2):
                groups = [g_ref[k, :, g * GROUP_FLAT + hf * LANES:g * GROUP_FLAT + (hf + 1) * LANES]
                          for g in range(GROUPS_PER_CLUSTER)]
                for i, tile in enumerate(_chunk_transpose(groups)):
                    rows = pl.ds(hf * half_block + i, tm // SSM_BLOCK, stride=BLOCK_PITCH)
                    gs_ref[k, rows, :] = tile
        g = jnp.concatenate(
            [jnp.concatenate([gs_ref[k, b * BLOCK_PITCH:b * BLOCK_PITCH + SSM_BLOCK, :]
                              for b in range(tm // SSM_BLOCK)], axis=0)
             for k in range(N_CLUSTERS)], axis=1)
        ya = g * jax.nn.sigmoid(_dot(g.astype(BF16), wglu_ref[...]) + bglu)
        ymix = jnp.concatenate([ya.astype(BF16), yb_ref[...], yc_ref[...]], axis=1)
        x1 = x_ref[...] + _dot(ymix, wout_ref[...])
        x1_ref[...] = x1
        h_ref[...] = _rms(x1, gffn).astype(BF16)

    @pl.when(pl.program_id(0) == 0)
    def _():
        mix_tail(*first)

    acc = x1_ref[...]
    h = h_ref[...]
    d_ff = wg_ref.shape[1]
    for n, c0 in enumerate(range(0, d_ff, FF_CHUNK)):
        c1 = min(c0 + FF_CHUNK, d_ff)
        gate = _dot(h, wg_ref[:, c0:c1])
        up = _dot(h, wu_ref[:, c0:c1])
        act = (gate * jax.nn.sigmoid(gate) * up).astype(BF16)
        acc = acc + _dot(act, wd_ref[c0:c1, :])
        if n == 0:
            mix_tail(*ahead)
    if final:
        acc = _rms(acc, vec_ref[VEC_GFINAL:VEC_GFINAL + 1, :])
    o_ref[...] = acc


def _mix_out(layer, x2d, g3, yb, yc, vecs, wglu, wout, wg, wu, wd, final, riders=()):
    m = x2d.shape[0]
    tm = TOKEN_TILE
    n_steps = m // tm
    d_ff = wg.shape[-1]
    resident = lambda shape: pl.BlockSpec(shape, lambda i: (0, 0), pipeline_mode=pl.Buffered(1))

    def tile0(shape):
        return pl.BlockSpec(shape, lambda i: (0,) * len(shape), pipeline_mode=pl.Buffered(1))

    def ahead(shape, axis=0):
        nxt = lambda i: jnp.minimum(i + 1, n_steps - 1)
        return pl.BlockSpec(shape, lambda i: tuple(nxt(i) if d == axis else 0
                                                   for d in range(len(shape))))

    r_specs = [_rider_specs(n_steps, lyr, w) for _, w, lyr in riders]
    return pl.pallas_call(
        functools.partial(_mix_out_kernel, final, tuple(k for k, _, _ in riders)),
        grid=(n_steps,),
        in_specs=[
            tile0((tm, D_MODEL)), tile0((N_CLUSTERS, tm // SSM_BLOCK, FLAT)),
            tile0((tm, D_POOL)), tile0((tm, D_SGU)),
            ahead((tm, D_MODEL)), ahead((N_CLUSTERS, tm // SSM_BLOCK, FLAT), axis=1),
            ahead((tm, D_POOL)), ahead((tm, D_SGU)),
            _layer_spec(layer, (VEC_ROWS, D_MODEL)),
            _layer_spec(layer, (D_SSM, D_SSM)),
            resident((D_MODEL, D_MODEL)),
            resident((D_MODEL, d_ff)),
            resident((D_MODEL, d_ff)),
            resident((d_ff, D_MODEL)),
        ] + [s for s, _, _ in r_specs],
        out_specs=[pl.BlockSpec((tm, D_MODEL), lambda i: (i, 0))] + [d for _, d, _ in r_specs],
        out_shape=[jax.ShapeDtypeStruct((m, D_MODEL), F32)] + [o for _, _, o in r_specs],
        scratch_shapes=[pltpu.VMEM((N_CLUSTERS, tm // SSM_BLOCK * BLOCK_PITCH, LANES), F32),
                        pltpu.VMEM((tm, D_MODEL), F32),
                        pltpu.VMEM((tm, D_MODEL), BF16)],
        compiler_params=pltpu.CompilerParams(
            dimension_semantics=("arbitrary",), vmem_limit_bytes=VMEM_MIX_OUT),
        name="mix_out",
    )(x2d, g3, yb, yc, x2d, g3, yb, yc, vecs, wglu, wout, wg, wu, wd, *[w for _, w, _ in riders])


def kernel(x, g_mix, w_in, A_re, A_im, log_dt, B_re, B_im, C_re, C_im, D_skip, w_glu, b_glu,
           w_pool, pool_scale, sgu_ln_g, sgu_ln_b, w_spatial, b_spatial, w_out, g_ffn,
           w_gate, w_up, w_down, g_final):
    bsz, seq, d = x.shape
    depth = w_in.shape[0]
    m = bsz * seq
    assert d == D_MODEL and seq % MIX_IN_TILE == 0 and MIX_IN_TILE % (2 * CHUNK) == 0
    assert m % TOKEN_TILE == 0
    assert (seq // SSM_BLOCK) % S5_ROW_TILE == 0
    x2d = x.reshape(m, D_MODEL)
    tril = jnp.tril(jnp.ones((CHUNK, CHUNK), dtype=bool))
    eye = jnp.eye(len(POOL_WINDOWS), dtype=F32)
    wpool_bd = jnp.einsum('dgij,gh->dgihj', w_pool, eye).reshape(depth, D_POOL, D_POOL).astype(BF16)
    ws = jnp.where(tril, w_spatial, 0.0).astype(BF16)
    ws = jnp.concatenate([ws[:, 0::2], ws[:, 1::2]], axis=-1)
    bsp = jnp.repeat(jnp.swapaxes(b_spatial, 1, 2), SGU_HEAD_DIM, axis=2)
    par, bt, ct, dskip = _s5_params(A_re, A_im, log_dt, B_re, B_im, C_re, C_im, D_skip)
    vec_rows = [None] * VEC_ROWS
    vec_rows[VEC_GMIX] = g_mix
    vec_rows[VEC_GFFN] = g_ffn
    vec_rows[VEC_SGU] = jnp.concatenate([pool_scale, sgu_ln_g, sgu_ln_b], axis=-1)
    vec_rows[VEC_GLU] = jnp.pad(b_glu, ((0, 0), (0, D_MODEL - D_SSM)))
    vec_rows[VEC_GFINAL] = jnp.broadcast_to(g_final, (depth, D_MODEL))
    zero_row = jnp.zeros((depth, D_MODEL), F32)
    vecs = jnp.stack([zero_row if r is None else r for r in vec_rows], axis=1)
    wglu = w_glu.astype(BF16)
    b_end, u_end = D_SSM + D_POOL, D_SSM + D_POOL + D_SGU
    win = jnp.concatenate([w_in[0, :, u_end:], w_in[0, :, b_end:u_end], w_in[0, :, :b_end]],
                          axis=-1).astype(BF16)
    ffn_stacks = (w_out, w_gate, w_up, w_down)
    ffn = None
    for l in range(depth):
        riders = tuple(("plain", w, 0) for w in ffn_stacks) if l == 0 else ()
        u_flat, yb, yc, *cast = _mix_in(l, x2d, vecs, win, wpool_bd, ws, bsp, seq, riders)
        if l == 0:
            ffn = cast
        g3 = _s5(l, u_flat, par, bt, ct, dskip, seq // SSM_BLOCK)
        last = l == depth - 1
        riders = () if last else (("w_in", w_in, l + 1),) + tuple(
            ("plain", w, l + 1) for w in ffn_stacks)
        x2d, *cast = _mix_out(l, x2d, g3, yb, yc, vecs, wglu, *ffn, final=last, riders=riders)
        if not last:
            win, ffn = cast[0], cast[1:]
    return x2d.reshape(bsz, seq, D_MODEL)
```

```python
import functools
import math

import jax
import jax.numpy as jnp
from jax import lax
from jax.experimental import pallas as pl
from jax.experimental.pallas import tpu as pltpu

D_MODEL = 1024
D_SSM = 384
SSM_GROUP = 16
N_SSM_GROUPS = D_SSM // SSM_GROUP
SSM_STATE = 64
POOL_WINDOWS = (2, 4, 8, 16)
POOL_GROUP = 64
D_POOL = len(POOL_WINDOWS) * POOL_GROUP
MAX_WINDOW = max(POOL_WINDOWS)
SGU_HEADS = 6
SGU_HEAD_DIM = 64
D_SGU = SGU_HEADS * SGU_HEAD_DIM
CHUNK = 128
D_IN = D_SSM + D_POOL + 2 * D_SGU
EPS = 1e-6

LANES = 128
SSM_BLOCK = 16
N_CLUSTERS = D_SSM // LANES
GROUPS_PER_CLUSTER = LANES // SSM_GROUP
CLUSTER_STATE = GROUPS_PER_CLUSTER * SSM_STATE
FLAT = SSM_BLOCK * LANES
GROUP_FLAT = SSM_BLOCK * SSM_GROUP
GROUP_PAIRS = GROUPS_PER_CLUSTER // 2

MXU_WIDTH = 256
BF16_SUBLANES = 16
VEC_ROWS = 8
VEC_GMIX, VEC_GFFN, VEC_SGU, VEC_GLU, VEC_GFINAL = 0, 1, 2, 3, 4
TOKEN_TILE = 512
MIX_IN_TILE = 1024
S5_ROW_TILE = 512
SUBLANES = 8
SCAN_SEGS = SUBLANES
SCAN_SEG_LEN = S5_ROW_TILE // SCAN_SEGS
PITCH_PAD = 4
SCAN_SEG_PITCH = SCAN_SEG_LEN + PITCH_PAD
SCAN_TILES = 2 * CLUSTER_STATE // LANES
BLOCK_PITCH = SSM_BLOCK + PITCH_PAD
FF_CHUNK = 3 * MXU_WIDTH
V7X_VMEM_BYTES = 64 * 1024 * 1024
VMEM_MIX_IN = VMEM_S5 = VMEM_MIX_OUT = V7X_VMEM_BYTES

F32 = jnp.float32
BF16 = jnp.bfloat16
NT_DIMS = (((1,), (1,)), ((), ()))


def _gelu(x):
    c = math.sqrt(2.0 / math.pi)
    return x * (0.5 + 0.5 * jnp.tanh(x * (c + (c * 0.044715) * (x * x))))


def _rms(x, g):
    ms = jnp.mean(x * x, axis=-1, keepdims=True)
    return x * lax.rsqrt(ms + EPS) * g


def _dot(a, b):
    return jnp.dot(a, b, preferred_element_type=F32)


def _dot_nt(a, b):
    return lax.dot_general(a, b, NT_DIMS, preferred_element_type=F32)


def _chunk_transpose(tiles):
    n = len(tiles)
    chunk = LANES // n
    v = list(tiles)
    cidx = lax.broadcasted_iota(jnp.int32, v[0].shape, 1) // chunk
    k = 1
    while k < n:
        keep = (cidx & k) == 0
        nv = list(v)
        for a in range(n):
            if a & k:
                continue
            lo, hi = v[a], v[a + k]
            nv[a] = jnp.where(keep, lo, pltpu.roll(hi, k * chunk, 1))
            nv[a + k] = jnp.where(keep, pltpu.roll(lo, LANES - k * chunk, 1), hi)
        v = nv
        k *= 2
    return v


def _cast_riders(kinds, src_refs, dst_refs):
    for kind, src, dst in zip(kinds, src_refs, dst_refs):
        if kind == "w_in":
            b_end, u_end = D_SSM + D_POOL, D_SSM + D_POOL + D_SGU
            dst[:, :D_SGU] = src[:, u_end:].astype(BF16)
            dst[:, D_SGU:2 * D_SGU] = src[:, b_end:u_end].astype(BF16)
            dst[:, 2 * D_SGU:] = src[:, :b_end].astype(BF16)
        else:
            dst[...] = src[...].astype(BF16)


def _rider_specs(n_steps, layer, w):
    rows, cols = w.shape[1:]
    share = 1
    while (rows * share) % n_steps or (rows * share // n_steps) % BF16_SUBLANES:
        share *= 2
    slab = rows * share // n_steps
    which = lambda i: jnp.minimum(i, n_steps - 1) // share
    src = pl.BlockSpec((None, slab, cols), lambda i: (layer, which(i), 0))
    dst = pl.BlockSpec((slab, cols), lambda i: (which(i), 0))
    return src, dst, jax.ShapeDtypeStruct((rows, cols), BF16)


def _mix_in_kernel(tiles_per_seq, rider_kinds, *refs):
    n_r = len(rider_kinds)
    x_ref, vec_ref, win_ref, wpool_ref, ws_ref, bsp_ref = refs[:6]
    rider_src = refs[6:6 + n_r]
    za_ref, yb_ref, yc_ref = refs[6 + n_r:9 + n_r]
    rider_dst = refs[9 + n_r:9 + 2 * n_r]
    halo_ref, zs_ref = refs[9 + 2 * n_r:]
    _cast_riders(rider_kinds, rider_src, rider_dst)

    tm = x_ref.shape[0]
    i = pl.program_id(0)
    seq_tile = i % tiles_per_seq
    gmix = vec_ref[VEC_GMIX:VEC_GMIX + 1, :]
    pscale = vec_ref[VEC_SGU:VEC_SGU + 1, :D_POOL]
    lng = vec_ref[VEC_SGU:VEC_SGU + 1, D_POOL:D_POOL + D_SGU]
    lnb = vec_ref[VEC_SGU:VEC_SGU + 1, D_POOL + D_SGU:]

    h = _rms(x_ref[...], gmix).astype(BF16)
    z_sgu = _dot(h, win_ref[:, :2 * D_SGU])
    z_rest = _dot(h, win_ref[:, 2 * D_SGU:])
    zv = z_sgu[:, :D_SGU]
    zu = z_sgu[:, D_SGU:]
    zb = z_rest[:, D_SSM:]

    n_blocks = tm // SSM_BLOCK
    for k in range(N_CLUSTERS):
        for b in range(n_blocks):
            zs_ref[k, b * BLOCK_PITCH:b * BLOCK_PITCH + SSM_BLOCK, :] = (
                z_rest[b * SSM_BLOCK:(b + 1) * SSM_BLOCK, k * LANES:(k + 1) * LANES])
    half_block = SSM_BLOCK // 2
    for k in range(N_CLUSTERS):
        for hf in range(2):
            slots = [zs_ref[k, pl.ds(hf * half_block + i, n_blocks, stride=BLOCK_PITCH), :]
                     for i in range(half_block)]
            for g, tile in enumerate(_chunk_transpose(slots)):
                col = g * GROUP_FLAT + hf * LANES
                za_ref[k, :, col:col + LANES] = tile.astype(BF16)

    halo = jnp.where(seq_tile == 0, 0.0, halo_ref[...])
    halo_ref[...] = zb[tm - MAX_WINDOW:, :]
    ext = jnp.concatenate([halo, zb], axis=0)
    lane = lax.broadcasted_iota(jnp.int32, (tm, LANES), 1)
    low = lane < POOL_GROUP
    low_h = lax.broadcasted_iota(jnp.int32, (MAX_WINDOW, LANES), 1) < POOL_GROUP
    low_r = lax.broadcasted_iota(jnp.int32, (1, LANES), 1) < POOL_GROUP
    pos1 = (seq_tile * tm + 1
            + lax.broadcasted_iota(jnp.int32, (MAX_WINDOW, LANES), 0)).astype(F32)

    def window_mean(total, w_low, w_high):
        cnt = jnp.where(low_h, jnp.minimum(pos1, w_low), jnp.minimum(pos1, w_high))
        inv_w = jnp.where(low_r, 1.0 / w_low, 1.0 / w_high)
        return jnp.concatenate([total[:MAX_WINDOW] / cnt, total[MAX_WINDOW:] * inv_w], axis=0)

    e0 = ext[:, :LANES]
    s2 = e0 + pltpu.roll(e0, 1, 0)
    s4 = s2 + pltpu.roll(s2, 2, 0)
    sum0 = jnp.where(low, s2[MAX_WINDOW:], s4[MAX_WINDOW:])
    e1 = ext[:, LANES:]
    t2 = e1 + pltpu.roll(e1, 1, 0)
    t4 = t2 + pltpu.roll(t2, 2, 0)
    t8 = t4 + pltpu.roll(t4, 4, 0)
    t16 = t8 + pltpu.roll(t8, 8, 0)
    sum1 = jnp.where(low, t8[MAX_WINDOW:], t16[MAX_WINDOW:])
    pooled = jnp.concatenate([window_mean(sum0, 2.0, 4.0), window_mean(sum1, 8.0, 16.0)],
                             axis=1) - zb
    yb = _dot(pooled.astype(BF16), wpool_ref[...]) * pscale
    yb_ref[...] = yb.astype(BF16)

    u = _gelu(zu)
    v = _gelu(zv)
    mu = jnp.mean(v, axis=-1, keepdims=True)
    vc = v - mu
    var = jnp.mean(vc * vc, axis=-1, keepdims=True)
    vn = (vc * lax.rsqrt(var + EPS) * lng + lnb).astype(BF16)
    lane_c = lax.broadcasted_iota(jnp.int32, (CHUNK, LANES), 1)
    low_c = lane_c < SGU_HEAD_DIM
    zero = jnp.zeros((), BF16)
    for c in range(0, tm // CHUNK, 2):
        rows_a = slice(c * CHUNK, (c + 1) * CHUNK)
        rows_b = slice((c + 1) * CHUNK, (c + 2) * CHUNK)
        parts_a, parts_b = [], []
        for p in range(SGU_HEADS // 2):
            va = vn[rows_a, p * LANES:(p + 1) * LANES]
            vb = vn[rows_b, p * LANES:(p + 1) * LANES]
            top = jnp.concatenate([jnp.where(low_c, va, zero), jnp.where(low_c, vb, zero)], axis=1)
            bot = jnp.concatenate([jnp.where(low_c, zero, va), jnp.where(low_c, zero, vb)], axis=1)
            out = _dot(ws_ref[p], jnp.concatenate([top, bot], axis=0))
            parts_a.append(out[:, :LANES])
            parts_b.append(out[:, LANES:])
        for rows, parts in ((rows_a, parts_a), (rows_b, parts_b)):
            mixed = jnp.concatenate(parts, axis=1) + bsp_ref[...]
            yc_ref[rows, :] = (u[rows, :] * mixed).astype(BF16)


def _layer_spec(layer, shape):
    zeros = (0,) * len(shape)
    return pl.BlockSpec((None,) + tuple(shape), lambda *_: (layer,) + zeros)


def _mix_in(layer, x2d, vecs, win, wpool_bd, ws, bsp, seq_len, riders=()):
    m = x2d.shape[0]
    tm = MIX_IN_TILE
    n_steps = m // tm
    spec = functools.partial(_layer_spec, layer)
    r_specs = [_rider_specs(n_steps, lyr, w) for _, w, lyr in riders]
    return pl.pallas_call(
        functools.partial(_mix_in_kernel, seq_len // tm, tuple(k for k, _, _ in riders)),
        grid=(n_steps,),
        in_specs=[
            pl.BlockSpec((tm, D_MODEL), lambda i: (i, 0)),
            spec((VEC_ROWS, D_MODEL)),
            pl.BlockSpec((D_MODEL, D_IN), lambda i: (0, 0)),
            spec((D_POOL, D_POOL)),
            spec((SGU_HEADS // 2, CHUNK, 2 * CHUNK)),
            spec((CHUNK, D_SGU)),
        ] + [s for s, _, _ in r_specs],
        out_specs=[
            pl.BlockSpec((N_CLUSTERS, tm // SSM_BLOCK, FLAT), lambda i: (0, i, 0)),
            pl.BlockSpec((tm, D_POOL), lambda i: (i, 0)),
            pl.BlockSpec((tm, D_SGU), lambda i: (i, 0)),
        ] + [d for _, d, _ in r_specs],
        out_shape=[
            jax.ShapeDtypeStruct((N_CLUSTERS, m // SSM_BLOCK, FLAT), BF16),
            jax.ShapeDtypeStruct((m, D_POOL), BF16),
            jax.ShapeDtypeStruct((m, D_SGU), BF16),
        ] + [o for _, _, o in r_specs],
        scratch_shapes=[pltpu.VMEM((MAX_WINDOW, D_POOL), F32),
                        pltpu.VMEM((N_CLUSTERS, tm // SSM_BLOCK * BLOCK_PITCH, LANES), F32)],
        compiler_params=pltpu.CompilerParams(
            dimension_semantics=("arbitrary",), vmem_limit_bytes=VMEM_MIX_IN),
        name="mix_in",
    )(x2d, vecs, win, wpool_bd, ws, bsp, *[w for _, w, _ in riders])


def _s5_build_operators(par_ref, bt_ref, ct_ref, dskip_ref, wend_ref, wrd_ref, lag0_ref, lagk_ref,
                        pw_ref):
    a_re = par_ref[0, 0:1, :]
    a_im = par_ref[0, 1:2, :]
    dt = jnp.exp(par_ref[0, 2:3, :])
    l_re = a_re * dt
    l_im = a_im * dt
    mag = jnp.exp(l_re)
    ar = mag * jnp.cos(l_im)
    ai = mag * jnp.sin(l_im)
    den = a_re * a_re + a_im * a_im
    f_re = ((ar - 1.0) * a_re + ai * a_im) / den
    f_im = (ai * a_re - (ar - 1.0) * a_im) / den
    bt_re, bt_im = bt_ref[0, 0], bt_ref[0, 1]
    bb_re = f_re * bt_re - f_im * bt_im
    bb_im = f_re * bt_im + f_im * bt_re
    ct_re, ct_im = ct_ref[0, 0], ct_ref[0, 1]

    lane_q = lax.broadcasted_iota(jnp.int32, (SSM_GROUP, LANES), 1) // SSM_STATE

    def power(l):
        m_l = jnp.exp(l * l_re)
        return m_l * jnp.cos(l * l_im), m_l * jnp.sin(l * l_im)

    def store_pairs(ref, slot, v_re, v_im):
        for p in range(GROUP_PAIRS):
            t_re = v_re[:, p * LANES:(p + 1) * LANES]
            t_im = v_im[:, p * LANES:(p + 1) * LANES]
            for q in range(2):
                rows = slice(q * GROUP_FLAT + slot * SSM_GROUP, q * GROUP_FLAT + (slot + 1) * SSM_GROUP)
                ref[p, rows, :LANES] = jnp.where(lane_q == q, t_re, 0.0).astype(BF16)
                ref[p, rows, LANES:] = jnp.where(lane_q == q, t_im, 0.0).astype(BF16)

    for s in range(SSM_BLOCK):
        p_re, p_im = power(float(SSM_BLOCK - 1 - s))
        store_pairs(wend_ref, s, p_re * bb_re - p_im * bb_im, p_re * bb_im + p_im * bb_re)
        q_re, q_im = power(float(s + 1))
        store_pairs(wrd_ref, s, q_re * ct_re - q_im * ct_im, -(q_re * ct_im + q_im * ct_re))
        r_re, r_im = power(float(s))
        store_pairs(lag0_ref, s, r_re * ct_re - r_im * ct_im, -(r_re * ct_im + r_im * ct_re))

    last = (SSM_BLOCK - 1) * SSM_GROUP
    col = lax.broadcasted_iota(jnp.int32, (SSM_GROUP, GROUP_FLAT), 1)
    row = lax.broadcasted_iota(jnp.int32, (SSM_GROUP, GROUP_FLAT), 0)
    for p in range(GROUP_PAIRS):
        bbar = jnp.concatenate([wend_ref[p, q * GROUP_FLAT + last:q * GROUP_FLAT + last + SSM_GROUP, :]
                                for q in range(2)], axis=0)
        lags = _dot_nt(bbar, lag0_ref[p])
        for q in range(2):
            g = 2 * p + q
            k_row = lags[q * SSM_GROUP:(q + 1) * SSM_GROUP, q * GROUP_FLAT:(q + 1) * GROUP_FLAT]
            d_g = dskip_ref[0, :, g * GROUP_FLAT:(g + 1) * GROUP_FLAT]
            k_row = k_row + jnp.where(col == row, d_g, 0.0)
            for s in range(SSM_BLOCK):
                shifted = pltpu.roll(k_row, s * SSM_GROUP, 1) if s else k_row
                lagk_ref[2 * p + q, s * SSM_GROUP:(s + 1) * SSM_GROUP, :] = jnp.where(
                    col >= s * SSM_GROUP, shifted, 0.0).astype(BF16)

    p_re, p_im = power(float(SSM_BLOCK))
    pw_ref[0:1, :] = jnp.concatenate([p_re, p_im], axis=1)
    for _ in range(SCAN_SEG_LEN.bit_length() - 1):
        p_re, p_im = p_re * p_re - p_im * p_im, 2.0 * (p_re * p_im)
    pw_ref[1:2, :] = jnp.concatenate([p_re, p_im], axis=1)


def _cmul_add(p_re, p_im, h_re, h_im, e_re, e_im):
    n_re = [pr * hr - pi * hi + er for pr, pi, hr, hi, er in zip(p_re, p_im, h_re, h_im, e_re)]
    n_im = [pr * hi + pi * hr + ei for pr, pi, hr, hi, ei in zip(p_re, p_im, h_re, h_im, e_im)]
    return n_re, n_im


def _s5_block_recurrence(st_ref, pw_ref, fillers):
    fillers = list(fillers)
    total_cost = sum(c for c, _ in fillers)
    issued_cost = [0.0]
    ticks = [0]

    def tick():
        while fillers and issued_cost[0] * 2 * SCAN_SEG_LEN <= ticks[0] * total_cost:
            cost, thunk = fillers.pop(0)
            issued_cost[0] += cost
            thunk()
        ticks[0] += 1

    half = SCAN_TILES // 2

    def bcast(row):
        return [jnp.broadcast_to(row[:, q * LANES:(q + 1) * LANES], (SCAN_SEGS, LANES))
                for q in range(half)]

    p_re, p_im = bcast(pw_ref[0:1, :CLUSTER_STATE]), bcast(pw_ref[0:1, CLUSTER_STATE:])
    seg_rows = lambda i: pl.ds(i, SCAN_SEGS, stride=SCAN_SEG_PITCH)

    def gather(i):
        e = [st_ref[q, seg_rows(i), :] for q in range(SCAN_TILES)]
        return e[:half], e[half:]

    zeros = [jnp.zeros((SCAN_SEGS, LANES), F32)] * half

    def end_step(i, h):
        e_re, e_im = gather(i)
        return _cmul_add(p_re, p_im, h[0], h[1], e_re, e_im)

    h = (zeros, zeros)
    for i in range(SCAN_SEG_LEN):
        tick()
        h = end_step(i, h)
    l_re, l_im = h

    s_re, s_im = bcast(pw_ref[1:2, :CLUSTER_STATE]), bcast(pw_ref[1:2, CLUSTER_STATE:])
    first = lax.broadcasted_iota(jnp.int32, (SCAN_SEGS, LANES), 0) == 0
    shift = lambda v: jnp.where(first, 0.0, pltpu.roll(v, 1, 0))
    c_re, c_im = zeros, zeros
    for _ in range(SCAN_SEGS - 1):
        n_re, n_im = _cmul_add(s_re, s_im, c_re, c_im, l_re, l_im)
        c_re, c_im = [shift(v) for v in n_re], [shift(v) for v in n_im]

    def scan_step(i, h):
        e_re, e_im = gather(i)
        for q in range(half):
            st_ref[q, seg_rows(i), :] = h[0][q]
            st_ref[half + q, seg_rows(i), :] = h[1][q]
        return _cmul_add(p_re, p_im, h[0], h[1], e_re, e_im)

    h = (c_re, c_im)
    for i in range(SCAN_SEG_LEN):
        tick()
        h = scan_step(i, h)
    while fillers:
        fillers.pop(0)[1]()


def _s5_kernel(u_ref, par_ref, bt_ref, ct_ref, dskip_ref, g_ref,
               wend_ref, wrd_ref, lag0_ref, lagk_ref, pw_ref, st_ref):
    r = pl.program_id(1)

    @pl.when(r == 0)
    def _():
        _s5_build_operators(par_ref, bt_ref, ct_ref, dskip_ref, wend_ref, wrd_ref, lag0_ref,
                            lagk_ref, pw_ref)

    u = u_ref[0]
    pair_cols = lambda p: slice(2 * p * GROUP_FLAT, 2 * (p + 1) * GROUP_FLAT)
    seg_rows = lambda j: slice(j * SCAN_SEG_PITCH, j * SCAN_SEG_PITCH + SCAN_SEG_LEN)

    for p in range(GROUP_PAIRS):
        e = _dot(u[:, pair_cols(p)], wend_ref[p])
        for tile, cols in ((p, slice(0, LANES)), (GROUP_PAIRS + p, slice(LANES, 2 * LANES))):
            for j in range(SCAN_SEGS):
                st_ref[tile, seg_rows(j), :] = e[j * SCAN_SEG_LEN:(j + 1) * SCAN_SEG_LEN, cols]

    ys = [None] * GROUPS_PER_CLUSTER

    def lag_product(g):
        ys[g] = _dot(u[:, g * GROUP_FLAT:(g + 1) * GROUP_FLAT], lagk_ref[g])

    _s5_block_recurrence(st_ref, pw_ref,
                         [(1, functools.partial(lag_product, g)) for g in range(len(ys))])
    y = jnp.concatenate(ys, axis=1)

    outs = []
    for p in range(GROUP_PAIRS):
        hprev = jnp.concatenate(
            [jnp.concatenate([st_ref[p, seg_rows(j), :], st_ref[GROUP_PAIRS + p, seg_rows(j), :]],
                             axis=1) for j in range(SCAN_SEGS)], axis=0)
        outs.append(_dot_nt(hprev.astype(BF16), wrd_ref[p]))
    g_ref[0] = y + jnp.concatenate(outs, axis=1)


def _s5(layer, u_flat, par, bt, ct, dskip, blocks_per_seq):
    nb = u_flat.shape[1]
    rt = S5_ROW_TILE
    assert blocks_per_seq == rt and SCAN_SEG_LEN & (SCAN_SEG_LEN - 1) == 0
    return pl.pallas_call(
        _s5_kernel,
        grid=(N_CLUSTERS, nb // rt),
        in_specs=[
            pl.BlockSpec((1, rt, FLAT), lambda k, r: (k, r, 0)),
            pl.BlockSpec((None, 1, 3, CLUSTER_STATE), lambda k, r: (layer, k, 0, 0)),
            pl.BlockSpec((None, 1, 2, SSM_GROUP, CLUSTER_STATE), lambda k, r: (layer, k, 0, 0, 0)),
            pl.BlockSpec((None, 1, 2, SSM_GROUP, CLUSTER_STATE), lambda k, r: (layer, k, 0, 0, 0)),
            pl.BlockSpec((None, 1, 1, FLAT), lambda k, r: (layer, k, 0, 0)),
        ],
        out_specs=pl.BlockSpec((1, rt, FLAT), lambda k, r: (k, r, 0)),
        out_shape=jax.ShapeDtypeStruct((N_CLUSTERS, nb, FLAT), F32),
        scratch_shapes=[pltpu.VMEM((GROUP_PAIRS, 2 * GROUP_FLAT, 2 * LANES), BF16),
                        pltpu.VMEM((GROUP_PAIRS, 2 * GROUP_FLAT, 2 * LANES), BF16),
                        pltpu.VMEM((GROUP_PAIRS, 2 * GROUP_FLAT, 2 * LANES), BF16),
                        pltpu.VMEM((GROUPS_PER_CLUSTER, GROUP_FLAT, GROUP_FLAT), BF16),
                        pltpu.VMEM((2, 2 * CLUSTER_STATE), F32),
                        pltpu.VMEM((SCAN_TILES, SCAN_SEGS * SCAN_SEG_PITCH, LANES), F32)],
        compiler_params=pltpu.CompilerParams(
            dimension_semantics=("arbitrary", "arbitrary"), vmem_limit_bytes=VMEM_S5),
        name="s5",
    )(u_flat, par, bt, ct, dskip)


def _s5_params(A_re, A_im, log_dt, B_re, B_im, C_re, C_im, D_skip):
    depth = A_re.shape[0]
    k, gpc, n, c = N_CLUSTERS, GROUPS_PER_CLUSTER, SSM_STATE, SSM_GROUP
    rows = lambda a: a.reshape(depth, k, 1, CLUSTER_STATE)
    ldt = jnp.broadcast_to(log_dt[..., None], A_re.shape)
    par = jnp.concatenate([rows(A_re), rows(A_im), rows(ldt)], axis=2)
    b = jnp.stack([B_re, B_im], axis=1).reshape(depth, 2, k, gpc, n, c)
    bt = b.transpose(0, 2, 1, 5, 3, 4).reshape(depth, k, 2, c, CLUSTER_STATE)
    cc = jnp.stack([C_re, C_im], axis=1).reshape(depth, 2, k, gpc, c, n)
    ct = cc.transpose(0, 2, 1, 4, 3, 5).reshape(depth, k, 2, c, CLUSTER_STATE)
    dskip = jnp.broadcast_to(D_skip.reshape(depth, k, gpc, 1, c), (depth, k, gpc, SSM_BLOCK, c))
    return par, bt, ct, dskip.reshape(depth, k, 1, FLAT)


def _mix_out_kernel(final, rider_kinds, *refs):
    n_r = len(rider_kinds)
    first = refs[0:4]
    ahead = refs[4:8]
    vec_ref, wglu_ref, wout_ref, wg_ref, wu_ref, wd_ref = refs[8:14]
    rider_src = refs[14:14 + n_r]
    o_ref = refs[14 + n_r]
    rider_dst = refs[15 + n_r:15 + 2 * n_r]
    gs_ref, x1_ref, h_ref = refs[15 + 2 * n_r:]
    _cast_riders(rider_kinds, rider_src, rider_dst)

    bglu = vec_ref[VEC_GLU:VEC_GLU + 1, :D_SSM]
    gffn = vec_ref[VEC_GFFN:VEC_GFFN + 1, :]
    tm = o_ref.shape[0]
    half_block = SSM_BLOCK // 2

    def mix_tail(x_ref, g_ref, yb_ref, yc_ref):
        for k in range(N_CLUSTERS):
            for hf in range(2):
                groups = [g_ref[k, :, g * GROUP_FLAT + hf * LANES:g * GROUP_FLAT + (hf + 1) * LANES]
                          for g in range(GROUPS_PER_CLUSTER)]
                for i, tile in enumerate(_chunk_transpose(groups)):
                    rows = pl.ds(hf * half_block + i, tm // SSM_BLOCK, stride=BLOCK_PITCH)
                    gs_ref[k, rows, :] = tile
        g = jnp.concatenate(
            [jnp.concatenate([gs_ref[k, b * BLOCK_PITCH:b * BLOCK_PITCH + SSM_BLOCK, :]
                              for b in range(tm // SSM_BLOCK)], axis=0)
             for k in range(N_CLUSTERS)], axis=1)
        g = _gelu(g)
        ya = g * jax.nn.sigmoid(_dot(g.astype(BF16), wglu_ref[...]) + bglu)
        ymix = jnp.concatenate([ya.astype(BF16), yb_ref[...], yc_ref[...]], axis=1)
        x1 = x_ref[...] + _dot(ymix, wout_ref[...])
        x1_ref[...] = x1
        h_ref[...] = _rms(x1, gffn).astype(BF16)

    @pl.when(pl.program_id(0) == 0)
    def _():
        mix_tail(*first)

    acc = x1_ref[...]
    h = h_ref[...]
    d_ff = wg_ref.shape[1]
    for n, c0 in enumerate(range(0, d_ff, FF_CHUNK)):
        c1 = min(c0 + FF_CHUNK, d_ff)
        gate = _dot(h, wg_ref[:, c0:c1])
        up = _dot(h, wu_ref[:, c0:c1])
        act = (gate * jax.nn.sigmoid(gate) * up).astype(BF16)
        acc = acc + _dot(act, wd_ref[c0:c1, :])
        if n == 0:
            mix_tail(*ahead)
    if final:
        acc = _rms(acc, vec_ref[VEC_GFINAL:VEC_GFINAL + 1, :])
    o_ref[...] = acc


def _mix_out(layer, x2d, g3, yb, yc, vecs, wglu, wout, wg, wu, wd, final, riders=()):
    m = x2d.shape[0]
    tm = TOKEN_TILE
    n_steps = m // tm
    d_ff = wg.shape[-1]
    resident = lambda shape: pl.BlockSpec(shape, lambda i: (0, 0), pipeline_mode=pl.Buffered(1))

    def tile0(shape):
        return pl.BlockSpec(shape, lambda i: (0,) * len(shape), pipeline_mode=pl.Buffered(1))

    def ahead(shape, axis=0):
        nxt = lambda i: jnp.minimum(i + 1, n_steps - 1)
        return pl.BlockSpec(shape, lambda i: tuple(nxt(i) if d == axis else 0
                                                   for d in range(len(shape))))

    r_specs = [_rider_specs(n_steps, lyr, w) for _, w, lyr in riders]
    return pl.pallas_call(
        functools.partial(_mix_out_kernel, final, tuple(k for k, _, _ in riders)),
        grid=(n_steps,),
        in_specs=[
            tile0((tm, D_MODEL)), tile0((N_CLUSTERS, tm // SSM_BLOCK, FLAT)),
            tile0((tm, D_POOL)), tile0((tm, D_SGU)),
            ahead((tm, D_MODEL)), ahead((N_CLUSTERS, tm // SSM_BLOCK, FLAT), axis=1),
            ahead((tm, D_POOL)), ahead((tm, D_SGU)),
            _layer_spec(layer, (VEC_ROWS, D_MODEL)),
            _layer_spec(layer, (D_SSM, D_SSM)),
            resident((D_MODEL, D_MODEL)),
            resident((D_MODEL, d_ff)),
            resident((D_MODEL, d_ff)),
            resident((d_ff, D_MODEL)),
        ] + [s for s, _, _ in r_specs],
        out_specs=[pl.BlockSpec((tm, D_MODEL), lambda i: (i, 0))] + [d for _, d, _ in r_specs],
        out_shape=[jax.ShapeDtypeStruct((m, D_MODEL), F32)] + [o for _, _, o in r_specs],
        scratch_shapes=[pltpu.VMEM((N_CLUSTERS, tm // SSM_BLOCK * BLOCK_PITCH, LANES), F32),
                        pltpu.VMEM((tm, D_MODEL), F32),
                        pltpu.VMEM((tm, D_MODEL), BF16)],
        compiler_params=pltpu.CompilerParams(
            dimension_semantics=("arbitrary",), vmem_limit_bytes=VMEM_MIX_OUT),
        name="mix_out",
    )(x2d, g3, yb, yc, x2d, g3, yb, yc, vecs, wglu, wout, wg, wu, wd, *[w for _, w, _ in riders])


def kernel(x, g_mix, w_in, A_re, A_im, log_dt, B_re, B_im, C_re, C_im, D_skip, w_glu, b_glu,
           w_pool, pool_scale, sgu_ln_g, sgu_ln_b, w_spatial, b_spatial, w_out, g_ffn,
           w_gate, w_up, w_down, g_final):
    bsz, seq, d = x.shape
    depth = w_in.shape[0]
    m = bsz * seq
    assert d == D_MODEL and seq % MIX_IN_TILE == 0 and MIX_IN_TILE % (2 * CHUNK) == 0
    assert m % TOKEN_TILE == 0
    assert (seq // SSM_BLOCK) % S5_ROW_TILE == 0
    x2d = x.reshape(m, D_MODEL)
    tril = jnp.tril(jnp.ones((CHUNK, CHUNK), dtype=bool))
    eye = jnp.eye(len(POOL_WINDOWS), dtype=F32)
    wpool_bd = jnp.einsum('dgij,gh->dgihj', w_pool, eye).reshape(depth, D_POOL, D_POOL).astype(BF16)
    ws = jnp.where(tril, w_spatial, 0.0).astype(BF16)
    ws = jnp.concatenate([ws[:, 0::2], ws[:, 1::2]], axis=-1)
    bsp = jnp.repeat(jnp.swapaxes(b_spatial, 1, 2), SGU_HEAD_DIM, axis=2)
    par, bt, ct, dskip = _s5_params(A_re, A_im, log_dt, B_re, B_im, C_re, C_im, D_skip)
    vec_rows = [None] * VEC_ROWS
    vec_rows[VEC_GMIX] = g_mix
    vec_rows[VEC_GFFN] = g_ffn
    vec_rows[VEC_SGU] = jnp.concatenate([pool_scale, sgu_ln_g, sgu_ln_b], axis=-1)
    vec_rows[VEC_GLU] = jnp.pad(b_glu, ((0, 0), (0, D_MODEL - D_SSM)))
    vec_rows[VEC_GFINAL] = jnp.broadcast_to(g_final, (depth, D_MODEL))
    zero_row = jnp.zeros((depth, D_MODEL), F32)
    vecs = jnp.stack([zero_row if r is None else r for r in vec_rows], axis=1)
    wglu = w_glu.astype(BF16)
    b_end, u_end = D_SSM + D_POOL, D_SSM + D_POOL + D_SGU
    win = jnp.concatenate([w_in[0, :, u_end:], w_in[0, :, b_end:u_end], w_in[0, :, :b_end]],
                          axis=-1).astype(BF16)
    ffn_stacks = (w_out, w_gate, w_up, w_down)
    ffn = None
    for l in range(depth):
        riders = tuple(("plain", w, 0) for w in ffn_stacks) if l == 0 else ()
        u_flat, yb, yc, *cast = _mix_in(l, x2d, vecs, win, wpool_bd, ws, bsp, seq, riders)
        if l == 0:
            ffn = cast
        g3 = _s5(l, u_flat, par, bt, ct, dskip, seq // SSM_BLOCK)
        last = l == depth - 1
        riders = () if last else (("w_in", w_in, l + 1),) + tuple(
            ("plain", w, l + 1) for w in ffn_stacks)
        x2d, *cast = _mix_out(l, x2d, g3, yb, yc, vecs, wglu, *ffn, final=last, riders=riders)
        if not last:
            win, ffn = cast[0], cast[1:]
    return x2d.reshape(bsz, seq, D_MODEL)
```

```python
import functools
import math

import jax
import jax.numpy as jnp
from jax import lax
from jax.experimental import pallas as pl
from jax.experimental.pallas import tpu as pltpu

D_MODEL = 1024
D_SSM = 384
SSM_GROUP = 16
N_SSM_GROUPS = D_SSM // SSM_GROUP
SSM_STATE = 64
POOL_WINDOWS = (2, 4, 8, 16)
POOL_GROUP = 64
D_POOL = len(POOL_WINDOWS) * POOL_GROUP
MAX_WINDOW = max(POOL_WINDOWS)
SGU_HEADS = 6
SGU_HEAD_DIM = 64
D_SGU = SGU_HEADS * SGU_HEAD_DIM
CHUNK = 128
D_IN = D_SSM + D_POOL + 2 * D_SGU
EPS = 1e-6

LANES = 128
SSM_BLOCK = 16
N_CLUSTERS = D_SSM // LANES
GROUPS_PER_CLUSTER = LANES // SSM_GROUP
CLUSTER_STATE = GROUPS_PER_CLUSTER * SSM_STATE
FLAT = SSM_BLOCK * LANES
GROUP_FLAT = SSM_BLOCK * SSM_GROUP
GROUP_PAIRS = GROUPS_PER_CLUSTER // 2

MXU_WIDTH = 256
BF16_SUBLANES = 16
VEC_ROWS = 8
VEC_GMIX, VEC_GFFN, VEC_SGU, VEC_GLU, VEC_GFINAL = 0, 1, 2, 3, 4
TOKEN_TILE = 512
MIX_IN_TILE = 1024
S5_ROW_TILE = 512
SUBLANES = 8
SCAN_SEGS = SUBLANES
SCAN_SEG_LEN = S5_ROW_TILE // SCAN_SEGS
PITCH_PAD = 4
SCAN_SEG_PITCH = SCAN_SEG_LEN + PITCH_PAD
SCAN_TILES = 2 * CLUSTER_STATE // LANES
BLOCK_PITCH = SSM_BLOCK + PITCH_PAD
FF_CHUNK = 3 * MXU_WIDTH
V7X_VMEM_BYTES = 64 * 1024 * 1024
VMEM_MIX_IN = VMEM_S5 = VMEM_MIX_OUT = V7X_VMEM_BYTES

F32 = jnp.float32
BF16 = jnp.bfloat16
NT_DIMS = (((1,), (1,)), ((), ()))


def _gelu(x):
    c = math.sqrt(2.0 / math.pi)
    return x * (0.5 + 0.5 * jnp.tanh(x * (c + (c * 0.044715) * (x * x))))


def _rms(x, g):
    ms = jnp.mean(x * x, axis=-1, keepdims=True)
    return x * lax.rsqrt(ms + EPS) * g


def _dot(a, b):
    return jnp.dot(a, b, preferred_element_type=F32)


def _dot_nt(a, b):
    return lax.dot_general(a, b, NT_DIMS, preferred_element_type=F32)


def _chunk_transpose(tiles):
    n = len(tiles)
    chunk = LANES // n
    v = list(tiles)
    cidx = lax.broadcasted_iota(jnp.int32, v[0].shape, 1) // chunk
    k = 1
    while k < n:
        keep = (cidx & k) == 0
        nv = list(v)
        for a in range(n):
            if a & k:
                continue
            lo, hi = v[a], v[a + k]
            nv[a] = jnp.where(keep, lo, pltpu.roll(hi, k * chunk, 1))
            nv[a + k] = jnp.where(keep, pltpu.roll(lo, LANES - k * chunk, 1), hi)
        v = nv
        k *= 2
    return v


def _cast_riders(kinds, src_refs, dst_refs):
    for kind, src, dst in zip(kinds, src_refs, dst_refs):
        if kind == "w_in":
            b_end, u_end = D_SSM + D_POOL, D_SSM + D_POOL + D_SGU
            dst[:, :D_SGU] = src[:, u_end:].astype(BF16)
            dst[:, D_SGU:2 * D_SGU] = src[:, b_end:u_end].astype(BF16)
            dst[:, 2 * D_SGU:] = src[:, :b_end].astype(BF16)
        else:
            dst[...] = src[...].astype(BF16)


def _rider_specs(n_steps, layer, w):
    rows, cols = w.shape[1:]
    share = 1
    while (rows * share) % n_steps or (rows * share // n_steps) % BF16_SUBLANES:
        share *= 2
    slab = rows * share // n_steps
    which = lambda i: jnp.minimum(i, n_steps - 1) // share
    src = pl.BlockSpec((None, slab, cols), lambda i: (layer, which(i), 0))
    dst = pl.BlockSpec((slab, cols), lambda i: (which(i), 0))
    return src, dst, jax.ShapeDtypeStruct((rows, cols), BF16)


def _mix_in_kernel(tiles_per_seq, rider_kinds, cast_win, *refs):
    n_r = len(rider_kinds)
    x_ref, vec_ref, win_ref, wpool_ref, ws_ref, bsp_ref = refs[:6]
    rider_src = refs[6:6 + n_r]
    za_ref, yb_ref, yc_ref = refs[6 + n_r:9 + n_r]
    rider_dst = refs[9 + n_r:9 + 2 * n_r]
    halo_ref, zs_ref = refs[9 + 2 * n_r:11 + 2 * n_r]
    _cast_riders(rider_kinds, rider_src, rider_dst)

    tm = x_ref.shape[0]
    i = pl.program_id(0)
    if cast_win:
        win_f32_ref, win_ref = win_ref, refs[11 + 2 * n_r]

        @pl.when(i == 0)
        def _():
            _cast_riders(("w_in",), (win_f32_ref,), (win_ref,))

    seq_tile = i % tiles_per_seq
    gmix = vec_ref[VEC_GMIX:VEC_GMIX + 1, :]
    pscale = vec_ref[VEC_SGU:VEC_SGU + 1, :D_POOL]
    lng = vec_ref[VEC_SGU:VEC_SGU + 1, D_POOL:D_POOL + D_SGU]
    lnb = vec_ref[VEC_SGU:VEC_SGU + 1, D_POOL + D_SGU:]

    h = _rms(x_ref[...], gmix).astype(BF16)
    z_sgu = _dot(h, win_ref[:, :2 * D_SGU])
    z_rest = _dot(h, win_ref[:, 2 * D_SGU:])
    zv = z_sgu[:, :D_SGU]
    zu = z_sgu[:, D_SGU:]
    zb = z_rest[:, D_SSM:]

    n_blocks = tm // SSM_BLOCK
    for k in range(N_CLUSTERS):
        for b in range(n_blocks):
            zs_ref[k, b * BLOCK_PITCH:b * BLOCK_PITCH + SSM_BLOCK, :] = (
                z_rest[b * SSM_BLOCK:(b + 1) * SSM_BLOCK, k * LANES:(k + 1) * LANES])
    half_block = SSM_BLOCK // 2
    for k in range(N_CLUSTERS):
        for hf in range(2):
            slots = [zs_ref[k, pl.ds(hf * half_block + i, n_blocks, stride=BLOCK_PITCH), :]
                     for i in range(half_block)]
            for g, tile in enumerate(_chunk_transpose(slots)):
                col = g * GROUP_FLAT + hf * LANES
                za_ref[k, :, col:col + LANES] = tile.astype(BF16)

    halo = jnp.where(seq_tile == 0, 0.0, halo_ref[...])
    halo_ref[...] = zb[tm - MAX_WINDOW:, :]
    ext = jnp.concatenate([halo, zb], axis=0)
    lane = lax.broadcasted_iota(jnp.int32, (tm, LANES), 1)
    low = lane < POOL_GROUP
    low_h = lax.broadcasted_iota(jnp.int32, (MAX_WINDOW, LANES), 1) < POOL_GROUP
    low_r = lax.broadcasted_iota(jnp.int32, (1, LANES), 1) < POOL_GROUP
    pos1 = (seq_tile * tm + 1
            + lax.broadcasted_iota(jnp.int32, (MAX_WINDOW, LANES), 0)).astype(F32)

    def window_mean(total, w_low, w_high):
        cnt = jnp.where(low_h, jnp.minimum(pos1, w_low), jnp.minimum(pos1, w_high))
        inv_w = jnp.where(low_r, 1.0 / w_low, 1.0 / w_high)
        return jnp.concatenate([total[:MAX_WINDOW] / cnt, total[MAX_WINDOW:] * inv_w], axis=0)

    e0 = ext[:, :LANES]
    s2 = e0 + pltpu.roll(e0, 1, 0)
    s4 = s2 + pltpu.roll(s2, 2, 0)
    sum0 = jnp.where(low, s2[MAX_WINDOW:], s4[MAX_WINDOW:])
    e1 = ext[:, LANES:]
    t2 = e1 + pltpu.roll(e1, 1, 0)
    t4 = t2 + pltpu.roll(t2, 2, 0)
    t8 = t4 + pltpu.roll(t4, 4, 0)
    t16 = t8 + pltpu.roll(t8, 8, 0)
    sum1 = jnp.where(low, t8[MAX_WINDOW:], t16[MAX_WINDOW:])
    pooled = jnp.concatenate([window_mean(sum0, 2.0, 4.0), window_mean(sum1, 8.0, 16.0)],
                             axis=1) - zb
    yb = _dot(pooled.astype(BF16), wpool_ref[...]) * pscale
    yb_ref[...] = yb.astype(BF16)

    u = _gelu(zu)
    v = _gelu(zv)
    mu = jnp.mean(v, axis=-1, keepdims=True)
    vc = v - mu
    var = jnp.mean(vc * vc, axis=-1, keepdims=True)
    vn = (vc * lax.rsqrt(var + EPS) * lng + lnb).astype(BF16)
    lane_c = lax.broadcasted_iota(jnp.int32, (CHUNK, LANES), 1)
    low_c = lane_c < SGU_HEAD_DIM
    zero = jnp.zeros((), BF16)
    for c in range(0, tm // CHUNK, 2):
        rows_a = slice(c * CHUNK, (c + 1) * CHUNK)
        rows_b = slice((c + 1) * CHUNK, (c + 2) * CHUNK)
        parts_a, parts_b = [], []
        for p in range(SGU_HEADS // 2):
            va = vn[rows_a, p * LANES:(p + 1) * LANES]
            vb = vn[rows_b, p * LANES:(p + 1) * LANES]
            top = jnp.concatenate([jnp.where(low_c, va, zero), jnp.where(low_c, vb, zero)], axis=1)
            bot = jnp.concatenate([jnp.where(low_c, zero, va), jnp.where(low_c, zero, vb)], axis=1)
            out = _dot(ws_ref[p], jnp.concatenate([top, bot], axis=0))
            parts_a.append(out[:, :LANES])
            parts_b.append(out[:, LANES:])
        for rows, parts in ((rows_a, parts_a), (rows_b, parts_b)):
            mixed = jnp.concatenate(parts, axis=1) + bsp_ref[...]
            yc_ref[rows, :] = (u[rows, :] * mixed).astype(BF16)


def _layer_spec(layer, shape):
    zeros = (0,) * len(shape)
    return pl.BlockSpec((None,) + tuple(shape), lambda *_: (layer,) + zeros)


def _mix_in(layer, x2d, vecs, win, wpool_bd, ws, bsp, seq_len, riders=()):
    m = x2d.shape[0]
    tm = MIX_IN_TILE
    n_steps = m // tm
    spec = functools.partial(_layer_spec, layer)
    r_specs = [_rider_specs(n_steps, lyr, w) for _, w, lyr in riders]
    cast_win = win.dtype == F32
    if cast_win:
        win_spec = pl.BlockSpec((None, D_MODEL, D_IN), lambda i: (layer, 0, 0),
                                pipeline_mode=pl.Buffered(1))
        win_scratch = [pltpu.VMEM((D_MODEL, D_IN), BF16)]
    else:
        win_spec = pl.BlockSpec((D_MODEL, D_IN), lambda i: (0, 0))
        win_scratch = []
    return pl.pallas_call(
        functools.partial(_mix_in_kernel, seq_len // tm, tuple(k for k, _, _ in riders), cast_win),
        grid=(n_steps,),
        in_specs=[
            pl.BlockSpec((tm, D_MODEL), lambda i: (i, 0)),
            spec((VEC_ROWS, D_MODEL)),
            win_spec,
            spec((D_POOL, D_POOL)),
            spec((SGU_HEADS // 2, CHUNK, 2 * CHUNK)),
            spec((CHUNK, D_SGU)),
        ] + [s for s, _, _ in r_specs],
        out_specs=[
            pl.BlockSpec((N_CLUSTERS, tm // SSM_BLOCK, FLAT), lambda i: (0, i, 0)),
            pl.BlockSpec((tm, D_POOL), lambda i: (i, 0)),
            pl.BlockSpec((tm, D_SGU), lambda i: (i, 0)),
        ] + [d for _, d, _ in r_specs],
        out_shape=[
            jax.ShapeDtypeStruct((N_CLUSTERS, m // SSM_BLOCK, FLAT), BF16),
            jax.ShapeDtypeStruct((m, D_POOL), BF16),
            jax.ShapeDtypeStruct((m, D_SGU), BF16),
        ] + [o for _, _, o in r_specs],
        scratch_shapes=[pltpu.VMEM((MAX_WINDOW, D_POOL), F32),
                        pltpu.VMEM((N_CLUSTERS, tm // SSM_BLOCK * BLOCK_PITCH, LANES), F32)]
        + win_scratch,
        compiler_params=pltpu.CompilerParams(
            dimension_semantics=("arbitrary",), vmem_limit_bytes=VMEM_MIX_IN),
        name="mix_in",
    )(x2d, vecs, win, wpool_bd, ws, bsp, *[w for _, w, _ in riders])


def _s5_build_operators(par_ref, bt_ref, ct_ref, dskip_ref, wend_ref, wrd_ref, lag0_ref, lagk_ref,
                        pw_ref):
    a_re = par_ref[0, 0:1, :]
    a_im = par_ref[0, 1:2, :]
    dt = jnp.exp(par_ref[0, 2:3, :])
    l_re = a_re * dt
    l_im = a_im * dt
    mag = jnp.exp(l_re)
    ar = mag * jnp.cos(l_im)
    ai = mag * jnp.sin(l_im)
    den = a_re * a_re + a_im * a_im
    f_re = ((ar - 1.0) * a_re + ai * a_im) / den
    f_im = (ai * a_re - (ar - 1.0) * a_im) / den
    bt_re, bt_im = bt_ref[0, 0], bt_ref[0, 1]
    bb_re = f_re * bt_re - f_im * bt_im
    bb_im = f_re * bt_im + f_im * bt_re
    ct_re, ct_im = ct_ref[0, 0], ct_ref[0, 1]

    lane_q = lax.broadcasted_iota(jnp.int32, (SSM_GROUP, LANES), 1) // SSM_STATE

    def power(l):
        m_l = jnp.exp(l * l_re)
        return m_l * jnp.cos(l * l_im), m_l * jnp.sin(l * l_im)

    def store_pairs(ref, slot, v_re, v_im):
        for p in range(GROUP_PAIRS):
            t_re = v_re[:, p * LANES:(p + 1) * LANES]
            t_im = v_im[:, p * LANES:(p + 1) * LANES]
            for q in range(2):
                rows = slice(q * GROUP_FLAT + slot * SSM_GROUP, q * GROUP_FLAT + (slot + 1) * SSM_GROUP)
                ref[p, rows, :LANES] = jnp.where(lane_q == q, t_re, 0.0).astype(BF16)
                ref[p, rows, LANES:] = jnp.where(lane_q == q, t_im, 0.0).astype(BF16)

    for s in range(SSM_BLOCK):
        p_re, p_im = power(float(SSM_BLOCK - 1 - s))
        store_pairs(wend_ref, s, p_re * bb_re - p_im * bb_im, p_re * bb_im + p_im * bb_re)
        q_re, q_im = power(float(s + 1))
        store_pairs(wrd_ref, s, q_re * ct_re - q_im * ct_im, -(q_re * ct_im + q_im * ct_re))
        r_re, r_im = power(float(s))
        store_pairs(lag0_ref, s, r_re * ct_re - r_im * ct_im, -(r_re * ct_im + r_im * ct_re))

    last = (SSM_BLOCK - 1) * SSM_GROUP
    col = lax.broadcasted_iota(jnp.int32, (SSM_GROUP, GROUP_FLAT), 1)
    row = lax.broadcasted_iota(jnp.int32, (SSM_GROUP, GROUP_FLAT), 0)
    for p in range(GROUP_PAIRS):
        bbar = jnp.concatenate([wend_ref[p, q * GROUP_FLAT + last:q * GROUP_FLAT + last + SSM_GROUP, :]
                                for q in range(2)], axis=0)
        lags = _dot_nt(bbar, lag0_ref[p])
        for q in range(2):
            g = 2 * p + q
            k_row = lags[q * SSM_GROUP:(q + 1) * SSM_GROUP, q * GROUP_FLAT:(q + 1) * GROUP_FLAT]
            d_g = dskip_ref[0, :, g * GROUP_FLAT:(g + 1) * GROUP_FLAT]
            k_row = k_row + jnp.where(col == row, d_g, 0.0)
            for s in range(SSM_BLOCK):
                shifted = pltpu.roll(k_row, s * SSM_GROUP, 1) if s else k_row
                lagk_ref[2 * p + q, s * SSM_GROUP:(s + 1) * SSM_GROUP, :] = jnp.where(
                    col >= s * SSM_GROUP, shifted, 0.0).astype(BF16)

    p_re, p_im = power(float(SSM_BLOCK))
    pw_ref[0:1, :] = jnp.concatenate([p_re, p_im], axis=1)
    for _ in range(SCAN_SEG_LEN.bit_length() - 1):
        p_re, p_im = p_re * p_re - p_im * p_im, 2.0 * (p_re * p_im)
    pw_ref[1:2, :] = jnp.concatenate([p_re, p_im], axis=1)


def _cmul_add(p_re, p_im, h_re, h_im, e_re, e_im):
    n_re = [pr * hr - pi * hi + er for pr, pi, hr, hi, er in zip(p_re, p_im, h_re, h_im, e_re)]
    n_im = [pr * hi + pi * hr + ei for pr, pi, hr, hi, ei in zip(p_re, p_im, h_re, h_im, e_im)]
    return n_re, n_im


def _s5_block_recurrence(st_ref, pw_ref, fillers):
    fillers = list(fillers)
    total_cost = sum(c for c, _ in fillers)
    issued_cost = [0.0]
    ticks = [0]

    def tick():
        while fillers and issued_cost[0] * 2 * SCAN_SEG_LEN <= ticks[0] * total_cost:
            cost, thunk = fillers.pop(0)
            issued_cost[0] += cost
            thunk()
        ticks[0] += 1

    half = SCAN_TILES // 2

    def bcast(row):
        return [jnp.broadcast_to(row[:, q * LANES:(q + 1) * LANES], (SCAN_SEGS, LANES))
                for q in range(half)]

    p_re, p_im = bcast(pw_ref[0:1, :CLUSTER_STATE]), bcast(pw_ref[0:1, CLUSTER_STATE:])
    seg_rows = lambda i: pl.ds(i, SCAN_SEGS, stride=SCAN_SEG_PITCH)

    def gather(i):
        e = [st_ref[q, seg_rows(i), :] for q in range(SCAN_TILES)]
        return e[:half], e[half:]

    zeros = [jnp.zeros((SCAN_SEGS, LANES), F32)] * half

    def end_step(i, h):
        e_re, e_im = gather(i)
        return _cmul_add(p_re, p_im, h[0], h[1], e_re, e_im)

    h = (zeros, zeros)
    for i in range(SCAN_SEG_LEN):
        tick()
        h = end_step(i, h)
    l_re, l_im = h

    s_re, s_im = bcast(pw_ref[1:2, :CLUSTER_STATE]), bcast(pw_ref[1:2, CLUSTER_STATE:])
    first = lax.broadcasted_iota(jnp.int32, (SCAN_SEGS, LANES), 0) == 0
    shift = lambda v: jnp.where(first, 0.0, pltpu.roll(v, 1, 0))
    c_re, c_im = zeros, zeros
    for _ in range(SCAN_SEGS - 1):
        n_re, n_im = _cmul_add(s_re, s_im, c_re, c_im, l_re, l_im)
        c_re, c_im = [shift(v) for v in n_re], [shift(v) for v in n_im]

    def scan_step(i, h):
        e_re, e_im = gather(i)
        for q in range(half):
            st_ref[q, seg_rows(i), :] = h[0][q]
            st_ref[half + q, seg_rows(i), :] = h[1][q]
        return _cmul_add(p_re, p_im, h[0], h[1], e_re, e_im)

    h = (c_re, c_im)
    for i in range(SCAN_SEG_LEN):
        tick()
        h = scan_step(i, h)
    while fillers:
        fillers.pop(0)[1]()


def _s5_kernel(u_ref, par_ref, bt_ref, ct_ref, dskip_ref, g_ref,
               wend_ref, wrd_ref, lag0_ref, lagk_ref, pw_ref, st_ref):
    r = pl.program_id(1)

    @pl.when(r == 0)
    def _():
        _s5_build_operators(par_ref, bt_ref, ct_ref, dskip_ref, wend_ref, wrd_ref, lag0_ref,
                            lagk_ref, pw_ref)

    u = u_ref[0]
    pair_cols = lambda p: slice(2 * p * GROUP_FLAT, 2 * (p + 1) * GROUP_FLAT)
    seg_rows = lambda j: slice(j * SCAN_SEG_PITCH, j * SCAN_SEG_PITCH + SCAN_SEG_LEN)

    for p in range(GROUP_PAIRS):
        e = _dot(u[:, pair_cols(p)], wend_ref[p])
        for tile, cols in ((p, slice(0, LANES)), (GROUP_PAIRS + p, slice(LANES, 2 * LANES))):
            for j in range(SCAN_SEGS):
                st_ref[tile, seg_rows(j), :] = e[j * SCAN_SEG_LEN:(j + 1) * SCAN_SEG_LEN, cols]

    ys = [None] * GROUPS_PER_CLUSTER

    def lag_product(g):
        ys[g] = _dot(u[:, g * GROUP_FLAT:(g + 1) * GROUP_FLAT], lagk_ref[g])

    _s5_block_recurrence(st_ref, pw_ref,
                         [(1, functools.partial(lag_product, g)) for g in range(len(ys))])
    y = jnp.concatenate(ys, axis=1)

    outs = []
    for p in range(GROUP_PAIRS):
        hprev = jnp.concatenate(
            [jnp.concatenate([st_ref[p, seg_rows(j), :], st_ref[GROUP_PAIRS + p, seg_rows(j), :]],
                             axis=1) for j in range(SCAN_SEGS)], axis=0)
        outs.append(_dot_nt(hprev.astype(BF16), wrd_ref[p]))
    g_ref[0] = y + jnp.concatenate(outs, axis=1)


def _s5(layer, u_flat, par, bt, ct, dskip, blocks_per_seq):
    nb = u_flat.shape[1]
    rt = S5_ROW_TILE
    assert blocks_per_seq == rt and SCAN_SEG_LEN & (SCAN_SEG_LEN - 1) == 0
    return pl.pallas_call(
        _s5_kernel,
        grid=(N_CLUSTERS, nb // rt),
        in_specs=[
            pl.BlockSpec((1, rt, FLAT), lambda k, r: (k, r, 0)),
            pl.BlockSpec((None, 1, 3, CLUSTER_STATE), lambda k, r: (layer, k, 0, 0)),
            pl.BlockSpec((None, 1, 2, SSM_GROUP, CLUSTER_STATE), lambda k, r: (layer, k, 0, 0, 0)),
            pl.BlockSpec((None, 1, 2, SSM_GROUP, CLUSTER_STATE), lambda k, r: (layer, k, 0, 0, 0)),
            pl.BlockSpec((None, 1, 1, FLAT), lambda k, r: (layer, k, 0, 0)),
        ],
        out_specs=pl.BlockSpec((1, rt, FLAT), lambda k, r: (k, r, 0)),
        out_shape=jax.ShapeDtypeStruct((N_CLUSTERS, nb, FLAT), F32),
        scratch_shapes=[pltpu.VMEM((GROUP_PAIRS, 2 * GROUP_FLAT, 2 * LANES), BF16),
                        pltpu.VMEM((GROUP_PAIRS, 2 * GROUP_FLAT, 2 * LANES), BF16),
                        pltpu.VMEM((GROUP_PAIRS, 2 * GROUP_FLAT, 2 * LANES), BF16),
                        pltpu.VMEM((GROUPS_PER_CLUSTER, GROUP_FLAT, GROUP_FLAT), BF16),
                        pltpu.VMEM((2, 2 * CLUSTER_STATE), F32),
                        pltpu.VMEM((SCAN_TILES, SCAN_SEGS * SCAN_SEG_PITCH, LANES), F32)],
        compiler_params=pltpu.CompilerParams(
            dimension_semantics=("arbitrary", "arbitrary"), vmem_limit_bytes=VMEM_S5),
        name="s5",
    )(u_flat, par, bt, ct, dskip)


def _s5_params(A_re, A_im, log_dt, B_re, B_im, C_re, C_im, D_skip):
    depth = A_re.shape[0]
    k, gpc, n, c = N_CLUSTERS, GROUPS_PER_CLUSTER, SSM_STATE, SSM_GROUP
    rows = lambda a: a.reshape(depth, k, 1, CLUSTER_STATE)
    ldt = jnp.broadcast_to(log_dt[..., None], A_re.shape)
    par = jnp.concatenate([rows(A_re), rows(A_im), rows(ldt)], axis=2)
    b = jnp.stack([B_re, B_im], axis=1).reshape(depth, 2, k, gpc, n, c)
    bt = b.transpose(0, 2, 1, 5, 3, 4).reshape(depth, k, 2, c, CLUSTER_STATE)
    cc = jnp.stack([C_re, C_im], axis=1).reshape(depth, 2, k, gpc, c, n)
    ct = cc.transpose(0, 2, 1, 4, 3, 5).reshape(depth, k, 2, c, CLUSTER_STATE)
    dskip = jnp.broadcast_to(D_skip.reshape(depth, k, gpc, 1, c), (depth, k, gpc, SSM_BLOCK, c))
    return par, bt, ct, dskip.reshape(depth, k, 1, FLAT)


def _mix_out_kernel(final, rider_kinds, *refs):
    n_r = len(rider_kinds)
    first = refs[0:4]
    ahead = refs[4:8]
    vec_ref, wglu_ref, wout_ref, wg_ref, wu_ref, wd_ref = refs[8:14]
    rider_src = refs[14:14 + n_r]
    o_ref = refs[14 + n_r]
    rider_dst = refs[15 + n_r:15 + 2 * n_r]
    gs_ref, x1_ref, h_ref = refs[15 + 2 * n_r:]
    _cast_riders(rider_kinds, rider_src, rider_dst)

    bglu = vec_ref[VEC_GLU:VEC_GLU + 1, :D_SSM]
    gffn = vec_ref[VEC_GFFN:VEC_GFFN + 1, :]
    tm = o_ref.shape[0]
    half_block = SSM_BLOCK // 2

    def mix_tail(x_ref, g_ref, yb_ref, yc_ref):
        for k in range(N_CLUSTERS):
            for hf in range(2):
                groups = [g_ref[k, :, g * GROUP_FLAT + hf * LANES:g * GROUP_FLAT + (hf + 1) * LANES]
                          for g in range(GROUPS_PER_CLUSTER)]
                for i, tile in enumerate(_chunk_transpose(groups)):
                    rows = pl.ds(hf * half_block + i, tm // SSM_BLOCK, stride=BLOCK_PITCH)
                    gs_ref[k, rows, :] = tile
        g = jnp.concatenate(
            [jnp.concatenate([gs_ref[k, b * BLOCK_PITCH:b * BLOCK_PITCH + SSM_BLOCK, :]
                              for b in range(tm // SSM_BLOCK)], axis=0)
             for k in range(N_CLUSTERS)], axis=1)
        g = _gelu(g)
        ya = g * jax.nn.sigmoid(_dot(g.astype(BF16), wglu_ref[...]) + bglu)
        ymix = jnp.concatenate([ya.astype(BF16), yb_ref[...], yc_ref[...]], axis=1)
        x1 = x_ref[...] + _dot(ymix, wout_ref[...])
        x1_ref[...] = x1
        h_ref[...] = _rms(x1, gffn).astype(BF16)

    @pl.when(pl.program_id(0) == 0)
    def _():
        mix_tail(*first)

    acc = x1_ref[...]
    h = h_ref[...]
    d_ff = wg_ref.shape[1]
    for n, c0 in enumerate(range(0, d_ff, FF_CHUNK)):
        c1 = min(c0 + FF_CHUNK, d_ff)
        gate = _dot(h, wg_ref[:, c0:c1])
        up = _dot(h, wu_ref[:, c0:c1])
        act = (gate * jax.nn.sigmoid(gate) * up).astype(BF16)
        acc = acc + _dot(act, wd_ref[c0:c1, :])
        if n == 0:
            mix_tail(*ahead)
    if final:
        acc = _rms(acc, vec_ref[VEC_GFINAL:VEC_GFINAL + 1, :])
    o_ref[...] = acc


def _mix_out(layer, x2d, g3, yb, yc, vecs, wglu, wout, wg, wu, wd, final, riders=()):
    m = x2d.shape[0]
    tm = TOKEN_TILE
    n_steps = m // tm
    d_ff = wg.shape[-1]
    resident = lambda shape: pl.BlockSpec(shape, lambda i: (0, 0), pipeline_mode=pl.Buffered(1))

    def tile0(shape):
        return pl.BlockSpec(shape, lambda i: (0,) * len(shape), pipeline_mode=pl.Buffered(1))

    def ahead(shape, axis=0):
        nxt = lambda i: jnp.minimum(i + 1, n_steps - 1)
        return pl.BlockSpec(shape, lambda i: tuple(nxt(i) if d == axis else 0
                                                   for d in range(len(shape))))

    r_specs = [_rider_specs(n_steps, lyr, w) for _, w, lyr in riders]
    return pl.pallas_call(
        functools.partial(_mix_out_kernel, final, tuple(k for k, _, _ in riders)),
        grid=(n_steps,),
        in_specs=[
            tile0((tm, D_MODEL)), tile0((N_CLUSTERS, tm // SSM_BLOCK, FLAT)),
            tile0((tm, D_POOL)), tile0((tm, D_SGU)),
            ahead((tm, D_MODEL)), ahead((N_CLUSTERS, tm // SSM_BLOCK, FLAT), axis=1),
            ahead((tm, D_POOL)), ahead((tm, D_SGU)),
            _layer_spec(layer, (VEC_ROWS, D_MODEL)),
            _layer_spec(layer, (D_SSM, D_SSM)),
            resident((D_MODEL, D_MODEL)),
            resident((D_MODEL, d_ff)),
            resident((D_MODEL, d_ff)),
            resident((d_ff, D_MODEL)),
        ] + [s for s, _, _ in r_specs],
        out_specs=[pl.BlockSpec((tm, D_MODEL), lambda i: (i, 0))] + [d for _, d, _ in r_specs],
        out_shape=[jax.ShapeDtypeStruct((m, D_MODEL), F32)] + [o for _, _, o in r_specs],
        scratch_shapes=[pltpu.VMEM((N_CLUSTERS, tm // SSM_BLOCK * BLOCK_PITCH, LANES), F32),
                        pltpu.VMEM((tm, D_MODEL), F32),
                        pltpu.VMEM((tm, D_MODEL), BF16)],
        compiler_params=pltpu.CompilerParams(
            dimension_semantics=("arbitrary",), vmem_limit_bytes=VMEM_MIX_OUT),
        name="mix_out",
    )(x2d, g3, yb, yc, x2d, g3, yb, yc, vecs, wglu, wout, wg, wu, wd, *[w for _, w, _ in riders])


def kernel(x, g_mix, w_in, A_re, A_im, log_dt, B_re, B_im, C_re, C_im, D_skip, w_glu, b_glu,
           w_pool, pool_scale, sgu_ln_g, sgu_ln_b, w_spatial, b_spatial, w_out, g_ffn,
           w_gate, w_up, w_down, g_final):
    bsz, seq, d = x.shape
    depth = w_in.shape[0]
    m = bsz * seq
    assert d == D_MODEL and seq % MIX_IN_TILE == 0 and MIX_IN_TILE % (2 * CHUNK) == 0
    assert m % TOKEN_TILE == 0
    assert (seq // SSM_BLOCK) % S5_ROW_TILE == 0
    x2d = x.reshape(m, D_MODEL)
    tril = jnp.tril(jnp.ones((CHUNK, CHUNK), dtype=bool))
    eye = jnp.eye(len(POOL_WINDOWS), dtype=F32)
    wpool_bd = jnp.einsum('dgij,gh->dgihj', w_pool, eye).reshape(depth, D_POOL, D_POOL).astype(BF16)
    ws = jnp.where(tril, w_spatial, 0.0).astype(BF16)
    ws = jnp.concatenate([ws[:, 0::2], ws[:, 1::2]], axis=-1)
    bsp = jnp.repeat(jnp.swapaxes(b_spatial, 1, 2), SGU_HEAD_DIM, axis=2)
    par, bt, ct, dskip = _s5_params(A_re, A_im, log_dt, B_re, B_im, C_re, C_im, D_skip)
    vec_rows = [None] * VEC_ROWS
    vec_rows[VEC_GMIX] = g_mix
    vec_rows[VEC_GFFN] = g_ffn
    vec_rows[VEC_SGU] = jnp.concatenate([pool_scale, sgu_ln_g, sgu_ln_b], axis=-1)
    vec_rows[VEC_GLU] = jnp.pad(b_glu, ((0, 0), (0, D_MODEL - D_SSM)))
    vec_rows[VEC_GFINAL] = jnp.broadcast_to(g_final, (depth, D_MODEL))
    zero_row = jnp.zeros((depth, D_MODEL), F32)
    vecs = jnp.stack([zero_row if r is None else r for r in vec_rows], axis=1)
    wglu = w_glu.astype(BF16)
    win = w_in
    ffn_stacks = (w_out, w_gate, w_up, w_down)
    ffn = None
    for l in range(depth):
        riders = tuple(("plain", w, 0) for w in ffn_stacks) if l == 0 else ()
        u_flat, yb, yc, *cast = _mix_in(l, x2d, vecs, win, wpool_bd, ws, bsp, seq, riders)
        if l == 0:
            ffn = cast
        g3 = _s5(l, u_flat, par, bt, ct, dskip, seq // SSM_BLOCK)
        last = l == depth - 1
        riders = () if last else (("w_in", w_in, l + 1),) + tuple(
            ("plain", w, l + 1) for w in ffn_stacks)
        x2d, *cast = _mix_out(l, x2d, g3, yb, yc, vecs, wglu, *ffn, final=last, riders=riders)
        if not last:
            win, ffn = cast[0], cast[1:]
    return x2d.reshape(bsz, seq, D_MODEL)
```
